```python
import math
import jax, jax.numpy as jnp
from jax import lax
import numpy as np

D_MODEL = 2048
BATCH = 1
SEQ = 8192
DEPTH = 4

GRID_W = 64
Q_BLOCK = 128
HEAD_DIM = 128
N_BRANCH = 4
BRANCH_W = D_MODEL // 4
DA_QK = HEAD_DIM // 2
DA_V = HEAD_DIM
HA = BRANCH_W // DA_V
HB = BRANCH_W // HEAD_DIM
NB_ROWS_MAX = 8
NB_COLS = 16
HC = BRANCH_W // HEAD_DIM
KVC = HC // 2
AXIAL_THETA = 10000.0
DIL_WINDOWS = (128, 512, 2048)
DIL_RATES = (1, 4, 16)
N_DIL = 3
HD_PER = BRANCH_W // HEAD_DIM
HD = N_DIL * HD_PER
DIL_SIDE = 64
ROPE_THETA = 500000.0
ROPE_FRAC = 4
DN_ALPHA = (2 * DEPTH) ** 0.25
DN_BETA = (8 * DEPTH) ** -0.25
LN_EPS = 1e-5
RMS_EPS = 1e-6
NEG_INF = -1e30

IN_SIZES = (
    HA * 2 * DA_QK, HA * 2 * DA_QK, HA * DA_V, BRANCH_W,
    HB * HEAD_DIM, HB * HEAD_DIM, HB * HEAD_DIM, BRANCH_W,
    HC * HEAD_DIM, KVC * HEAD_DIM, KVC * HEAD_DIM, BRANCH_W,
    HD * HEAD_DIM, HD * HEAD_DIM, HD * HEAD_DIM, BRANCH_W,
    N_BRANCH * D_MODEL,
)
D_IN = sum(IN_SIZES)

kernel_name = 'hybrid_gated_multi_mixer_encoder'


def _split(h, sizes):
    outs, start = [], 0
    for n in sizes:
        outs.append(h[..., start:start + n])
        start += n
    return outs


def _layer_norm(x, g, b):
    xf = x.astype(jnp.float32)
    mu = jnp.mean(xf, -1, keepdims=True)
    var = jnp.mean(jnp.square(xf - mu), -1, keepdims=True)
    y = (xf - mu) * lax.rsqrt(var + LN_EPS)
    return (y * g.astype(jnp.float32) + b.astype(jnp.float32)).astype(x.dtype)


def _rms_norm(x, g):
    xf = x.astype(jnp.float32)
    y = xf * lax.rsqrt(jnp.mean(xf * xf, -1, keepdims=True) + RMS_EPS)
    return (y * g.astype(jnp.float32)).astype(x.dtype)


def _rope(x, pos, theta):
    dim = x.shape[-1]
    half = dim // 2
    inv = jnp.power(jnp.float32(theta), -jnp.arange(half, dtype=jnp.float32) * 2.0 / dim)
    ang = pos.astype(jnp.float32)[:, None] * inv[None, :]
    cos = jnp.cos(ang)[None, :, None, :].astype(x.dtype)
    sin = jnp.sin(ang)[None, :, None, :].astype(x.dtype)
    x1, x2 = x[..., :half], x[..., half:]
    return jnp.concatenate([x1 * cos - x2 * sin, x2 * cos + x1 * sin], -1)


def _partial_rope(x, pos):
    r = x.shape[-1] // ROPE_FRAC
    return jnp.concatenate([_rope(x[..., :r], pos, ROPE_THETA), x[..., r:]], -1)


def _to_blocks(t):
    b, s = t.shape[:2]
    t = t.reshape((b, s // Q_BLOCK, Q_BLOCK) + t.shape[2:])
    return jnp.moveaxis(t, 1, 0)


def _from_blocks(t):
    t = jnp.moveaxis(t, 0, 1)
    return t.reshape((t.shape[0], t.shape[1] * t.shape[2]) + t.shape[3:])


def _diff_attention(q, k, v, lam_params, subln_g, lam_init, pos):
    b, s = q.shape[:2]
    q = _partial_rope(q.reshape(b, s, HA * 2, DA_QK), pos).reshape(b, s, HA, 2, DA_QK) * (DA_QK ** -0.5)
    k = _partial_rope(k.reshape(b, s, HA * 2, DA_QK), pos).reshape(b, s, HA, 2, DA_QK)
    lp = lam_params.astype(jnp.float32)
    lam = jnp.exp(jnp.sum(lp[0] * lp[1])) - jnp.exp(jnp.sum(lp[2] * lp[3])) + lam_init

    def block(qb):
        sc = jnp.einsum('bqhcd,bshcd->bhcqs', qb, k).astype(jnp.float32)
        p = jax.nn.softmax(sc, axis=-1)
        a = p[:, :, 0] - lam * p[:, :, 1]
        return jnp.einsum('bhqs,bshd->bqhd', a.astype(v.dtype), v)

    o = _from_blocks(lax.map(block, _to_blocks(q)))
    o = _rms_norm(o, subln_g) * (1.0 - lam_init)
    return o.reshape(b, s, HA * DA_V)


def _neighbourhood_attention(q, k, v, rpb):
    b, s = q.shape[:2]
    rows = s // GRID_W
    kr = min(NB_ROWS_MAX, rows)
    kc = NB_COLS
    qg = jnp.moveaxis((q * HEAD_DIM ** -0.5).reshape(b, rows, GRID_W, HB, HEAD_DIM), 1, 0)
    kg = k.reshape(b, rows, GRID_W, HB, HEAD_DIM)
    vg = v.reshape(b, rows, GRID_W, HB, HEAD_DIM)
    cols = jnp.arange(GRID_W)
    col_idx = jnp.clip(cols - kc // 2, 0, GRID_W - kc)[:, None] + jnp.arange(kc)[None, :]
    dc = col_idx - cols[:, None] + (NB_COLS - 1)

    def row_block(args):
        r, qr = args
        r0 = jnp.clip(r - kr // 2, 0, rows - kr)
        k_nb = lax.dynamic_slice_in_dim(kg, r0, kr, axis=1)[:, :, col_idx]
        v_nb = lax.dynamic_slice_in_dim(vg, r0, kr, axis=1)[:, :, col_idx]
        dr = r0 + jnp.arange(kr) - r + (NB_ROWS_MAX - 1)
        bias = rpb[:, dr[None, :, None], dc[:, None, :]]
        sc = jnp.einsum('bchd,brcjhd->bhcrj', qr, k_nb).astype(jnp.float32) + bias.astype(jnp.float32)[None]
        p = jax.nn.softmax(sc.reshape(b, HB, GRID_W, kr * kc), axis=-1).reshape(sc.shape)
        return jnp.einsum('bhcrj,brcjhd->bchd', p.astype(v.dtype), v_nb)

    o = lax.map(row_block, (jnp.arange(rows), qg))
    return jnp.moveaxis(o, 0, 1).reshape(b, s, HB * HEAD_DIM)


def _axial_gqa(q, k, v, qn_g, kn_g):
    b, s = q.shape[:2]
    t = jnp.arange(s)
    row, col = t // GRID_W, t % GRID_W
    half = HEAD_DIM // 2

    def axial(z):
        return jnp.concatenate([_rope(z[..., :half], row, AXIAL_THETA), _rope(z[..., half:], col, AXIAL_THETA)], -1)

    q = (axial(_rms_norm(q, qn_g)) * HEAD_DIM ** -0.5).reshape(b, s, KVC, HC // KVC, HEAD_DIM)
    k = axial(_rms_norm(k, kn_g))

    def block(qb):
        sc = jnp.einsum('bqkgd,bskd->bkgqs', qb, k).astype(jnp.float32)
        p = jax.nn.softmax(sc, axis=-1)
        return jnp.einsum('bkgqs,bskd->bqkgd', p.astype(v.dtype), v)

    o = _from_blocks(lax.map(block, _to_blocks(q)))
    return o.reshape(b, s, HC * HEAD_DIM)


def _dilated_attention(q, k, v, pos):
    b, s = q.shape[:2]
    shp = (b, s, N_DIL, HD_PER, HEAD_DIM)
    q = (_partial_rope(q, pos) * HEAD_DIM ** -0.5).reshape(shp)
    k = _partial_rope(k, pos).reshape(shp)
    v = v.reshape(shp)
    offsets = jnp.array(DIL_RATES, dtype=jnp.int32)[:, None] * jnp.arange(-DIL_SIDE, DIL_SIDE + 1, dtype=jnp.int32)[None, :]
    g_idx = jnp.arange(N_DIL)[None, :, None]

    def block(args):
        i, qb = args
        t = i * Q_BLOCK + jnp.arange(Q_BLOCK, dtype=jnp.int32)
        idx = t[:, None, None] + offsets[None]
        valid = (idx >= 0) & (idx < s)
        idx = jnp.clip(idx, 0, s - 1)
        k_nb = k[:, idx, g_idx]
        v_nb = v[:, idx, g_idx]
        sc = jnp.einsum('bqghd,bqgjhd->bghqj', qb, k_nb).astype(jnp.float32)
        sc = jnp.where(jnp.moveaxis(valid, 0, 1)[None, :, None], sc, NEG_INF)
        m = jnp.max(sc, -1, keepdims=True)
        e = jnp.exp(sc - m)
        den = jnp.sum(e, -1, keepdims=True)
        o = jnp.einsum('bghqj,bqgjhd->bqghd', (e / den).astype(v.dtype), v_nb)
        lse = (m + jnp.log(den))[..., 0]
        w = jnp.transpose(jax.nn.softmax(lse, axis=1), (0, 3, 1, 2))[..., None]
        return jnp.sum(o * w.astype(o.dtype), axis=2)

    o = _from_blocks(lax.map(block, (jnp.arange(s // Q_BLOCK, dtype=jnp.int32), _to_blocks(q))))
    return o.reshape(b, s, HD_PER * HEAD_DIM)


def _hybrid_layer(x, w_in, b_gate, lam_params, subln_g, rpb, qn_g, kn_g, w_branch, w_out, ln_g, ln_b, lam_init):
    b, s, _ = x.shape
    pos = jnp.arange(s, dtype=jnp.int32)
    h = jnp.einsum('bsd,de->bse', x, w_in)
    (aq, ak, av, az, bq, bk, bv, bz, cq, ck, cv, cz, dq, dk, dv, dz, gl) = _split(h, IN_SIZES)
    ya = _diff_attention(aq.reshape(b, s, HA, 2, DA_QK), ak.reshape(b, s, HA, 2, DA_QK),
                         av.reshape(b, s, HA, DA_V), lam_params, subln_g, lam_init, pos)
    yb = _neighbourhood_attention(bq.reshape(b, s, HB, HEAD_DIM), bk.reshape(b, s, HB, HEAD_DIM),
                                  bv.reshape(b, s, HB, HEAD_DIM), rpb)
    yc = _axial_gqa(cq.reshape(b, s, HC, HEAD_DIM), ck.reshape(b, s, KVC, HEAD_DIM),
                    cv.reshape(b, s, KVC, HEAD_DIM), qn_g, kn_g)
    yd = _dilated_attention(dq.reshape(b, s, HD, HEAD_DIM), dk.reshape(b, s, HD, HEAD_DIM),
                            dv.reshape(b, s, HD, HEAD_DIM), pos)
    ys = jnp.stack([ya * jax.nn.silu(az), yb * jax.nn.silu(bz),
                    yc * jax.nn.silu(cz), yd * jax.nn.silu(dz)], axis=2)
    proj = jnp.einsum('bsnw,nwd->bsnd', ys, w_branch)
    gates = jax.nn.sigmoid((gl + b_gate).astype(jnp.float32)).astype(x.dtype).reshape(b, s, N_BRANCH, D_MODEL)
    merged = jnp.sum(proj * gates, axis=2)
    out = jnp.einsum('bsd,de->bse', merged, w_out)
    return _layer_norm(DN_ALPHA * x + out, ln_g, ln_b)


def setup_inputs(seed: int = 0) -> dict:
    key = jax.random.key(seed)
    ks = jax.random.split(key, 16)
    L, D = DEPTH, D_MODEL
    nrm = jax.random.normal
    return {
        'x': nrm(ks[0], (BATCH, SEQ, D), jnp.float32),
        'emb_ln_g': 1.0 + 0.01 * nrm(ks[1], (D,), jnp.float32),
        'emb_ln_b': 0.01 * nrm(ks[2], (D,), jnp.float32),
        'w_in': nrm(ks[3], (L, D, D_IN), jnp.float32) * D ** -0.5,
        'b_gate': 0.01 * nrm(ks[4], (L, N_BRANCH * D), jnp.float32),
        'diff_lambda': 0.1 * nrm(ks[5], (L, 4, DA_QK), jnp.float32),
        'diff_subln_g': 1.0 + 0.01 * nrm(ks[6], (L, DA_V), jnp.float32),
        'nat_rpb': 0.1 * nrm(ks[7], (L, HB, 2 * NB_ROWS_MAX - 1, 2 * NB_COLS - 1), jnp.float32),
        'gqa_q_norm_g': 1.0 + 0.01 * nrm(ks[8], (L, HEAD_DIM), jnp.float32),
        'gqa_k_norm_g': 1.0 + 0.01 * nrm(ks[9], (L, HEAD_DIM), jnp.float32),
        'w_branch': nrm(ks[10], (L, N_BRANCH, BRANCH_W, D), jnp.float32) * (BRANCH_W ** -0.5) * DN_BETA,
        'w_out': nrm(ks[11], (L, D, D), jnp.float32) * (D ** -0.5) * DN_BETA,
        'ln_g': 1.0 + 0.01 * nrm(ks[12], (L, D), jnp.float32),
        'ln_b': 0.01 * nrm(ks[13], (L, D), jnp.float32),
    }


def reference(x, emb_ln_g, emb_ln_b, w_in, b_gate, diff_lambda, diff_subln_g, nat_rpb,
              gqa_q_norm_g, gqa_k_norm_g, w_branch, w_out, ln_g, ln_b):
    x = _layer_norm(x, emb_ln_g, emb_ln_b)
    for l in range(DEPTH):
        lam_init = 0.8 - 0.6 * math.exp(-0.3 * l)
        x = _hybrid_layer(x, w_in[l], b_gate[l], diff_lambda[l], diff_subln_g[l], nat_rpb[l],
                          gqa_q_norm_g[l], gqa_k_norm_g[l], w_branch[l], w_out[l], ln_g[l], ln_b[l], lam_init)
    return x
```

```python
import functools
import math

import jax
import jax.numpy as jnp
from jax import lax
from jax.experimental import pallas as pl
from jax.experimental.pallas import tpu as pltpu

F32 = jnp.float32
BF16 = jnp.bfloat16

D_MODEL = 2048
GRID_W = 64
HEAD_DIM = 128
N_BRANCH = 4
BRANCH_W = D_MODEL // 4
DA_QK = HEAD_DIM // 2
NB_ROWS = 8
NB_COLS = 16
AXIAL_THETA = 10000.0
DIL_RATES = (1, 4, 16)
DIL_SIDE = 64
ROPE_THETA = 500000.0
LN_EPS = 1e-5
RMS_EPS = 1e-6
NEG_INF = -1e30

LANES = 128
TN = 512
D_IN = 18944
N_TILES = D_IN // TN
CB_AQ, CB_AK, CB_AV, CB_AZ = 0, 4, 8, 12
CB_BQ, CB_BK, CB_BV, CB_BZ = 16, 20, 24, 28
CB_CQ, CB_CK, CB_CV, CB_CZ = 32, 36, 38, 40
CB_DQ, CB_DK, CB_DV, CB_DZ = 44, 56, 68, 80
TILE_GATE0 = 21

VMEM_LIMIT = 56 * 1024 * 1024


def _cparams(sem):
    return pltpu.CompilerParams(dimension_semantics=sem, vmem_limit_bytes=VMEM_LIMIT)


def _ln_rows(x, g, b):
    mu = jnp.mean(x, axis=-1, keepdims=True)
    xc = x - mu
    var = jnp.mean(xc * xc, axis=-1, keepdims=True)
    return xc * lax.rsqrt(var + LN_EPS) * g + b


def _embed_ln_kernel(x_ref, g_ref, b_ref, y_ref, ybf_ref):
    y = _ln_rows(x_ref[...], g_ref[...], b_ref[...])
    y_ref[...] = y
    ybf_ref[...] = y.astype(BF16)


def _embed_ln(x, g, b, tm=512):
    s, d = x.shape
    return pl.pallas_call(
        _embed_ln_kernel,
        grid=(s // tm,),
        in_specs=[pl.BlockSpec((tm, d), lambda i: (i, 0)),
                  pl.BlockSpec((1, d), lambda i: (0, 0)),
                  pl.BlockSpec((1, d), lambda i: (0, 0))],
        out_specs=[pl.BlockSpec((tm, d), lambda i: (i, 0)),
                   pl.BlockSpec((tm, d), lambda i: (i, 0))],
        out_shape=[jax.ShapeDtypeStruct((s, d), F32), jax.ShapeDtypeStruct((s, d), BF16)],
        compiler_params=_cparams(("arbitrary",)),
        name="embed_ln",
    )(x, g.reshape(1, d), b.reshape(1, d))


def _rope_tables(seq):
    t = jnp.arange(seq)

    def cs(pos, dim, theta):
        half = dim // 2
        inv = jnp.power(jnp.float32(theta), -jnp.arange(half, dtype=jnp.float32) * 2.0 / dim)
        ang = pos.astype(jnp.float32)[:, None] * inv[None, :]
        return jnp.cos(ang), jnp.sin(ang)

    def group(cos, sin, width):
        half = cos.shape[1]
        pad = width - 2 * half
        one = jnp.ones((seq, pad), F32)
        zero = jnp.zeros((seq, pad), F32)
        zh = jnp.zeros((seq, half), F32)
        return (jnp.concatenate([cos, cos, one], 1),
                jnp.concatenate([zh, sin, zero], 1),
                jnp.concatenate([-sin, zh, zero], 1))

    ca, sa = cs(t, DA_QK // 4, ROPE_THETA)
    va = [jnp.concatenate([p, p], 1) for p in group(ca, sa, DA_QK)]
    cr, sr = cs(t // GRID_W, HEAD_DIM // 2, AXIAL_THETA)
    cc, sc = cs(t % GRID_W, HEAD_DIM // 2, AXIAL_THETA)
    vc = [jnp.concatenate([p, q], 1)
          for p, q in zip(group(cr, sr, HEAD_DIM // 2), group(cc, sc, HEAD_DIM // 2))]
    cd, sd = cs(t, HEAD_DIM // 4, ROPE_THETA)
    vd = list(group(cd, sd, HEAD_DIM))
    return jnp.stack([jnp.stack(va), jnp.stack(vc), jnp.stack(vd)])


ROT_HALF = (DA_QK // 8, HEAD_DIM // 4, HEAD_DIM // 8)


def _rot(a, tab_ref, half):
    return (a * tab_ref[0] + pltpu.roll(a, half, 1) * tab_ref[1]
            + pltpu.roll(a, LANES - half, 1) * tab_ref[2])


_TILES_ROT_A = (0, 1)
_TILES_PLAIN = (2, 5, 6, 17, 18, 19)
_TILES_SILU = (3, 7, 10, 20)
_TILE_BQ, _TILE_CQ, _TILE_CKV = 4, 8, 9
_TILES_DQ = (11, 12, 13)
_TILES_DK = (14, 15, 16)


def _any_of(j, tiles):
    c = j == tiles[0]
    for t in tiles[1:]:
        c = c | (j == t)
    return c


def _inproj_kernel(x_ref, w_ref, bg_ref, tab_ref, gn_ref, o_ref, wbf_ref):
    j = pl.program_id(0)
    i = pl.program_id(1)

    @pl.when(i == 0)
    def _():
        wbf_ref[...] = w_ref[...].astype(BF16)

    acc = jnp.dot(x_ref[...], wbf_ref[...], preferred_element_type=F32)
    groups = [acc[:, g * LANES:(g + 1) * LANES] for g in range(TN // LANES)]

    def store(vals):
        for g, v in enumerate(vals):
            o_ref[:, g * LANES:(g + 1) * LANES] = v.astype(BF16)

    def rms(a, gain):
        ms = jnp.mean(a * a, axis=-1, keepdims=True)
        return a * lax.rsqrt(ms + RMS_EPS) * gain

    @pl.when(j == 0)
    def _():
        store([_rot(a, tab_ref, ROT_HALF[0]) * (DA_QK ** -0.5) for a in groups])

    @pl.when(j == 1)
    def _():
        store([_rot(a, tab_ref, ROT_HALF[0]) for a in groups])

    @pl.when(_any_of(j, _TILES_PLAIN))
    def _():
        store(groups)

    @pl.when(_any_of(j, _TILES_SILU))
    def _():
        store([a * jax.nn.sigmoid(a) for a in groups])

    @pl.when(j == _TILE_BQ)
    def _():
        store([a * (HEAD_DIM ** -0.5) for a in groups])

    @pl.when(j == _TILE_CQ)
    def _():
        store([_rot(rms(a, gn_ref[0]), tab_ref, ROT_HALF[1]) * (HEAD_DIM ** -0.5) for a in groups])

    @pl.when(j == _TILE_CKV)
    def _():
        store([_rot(rms(a, gn_ref[1]), tab_ref, ROT_HALF[1]) for a in groups[:2]] + groups[2:])

    @pl.when(_any_of(j, _TILES_DQ))
    def _():
        store([_rot(a, tab_ref, ROT_HALF[2]) * (HEAD_DIM ** -0.5) for a in groups])

    @pl.when(_any_of(j, _TILES_DK))
    def _():
        store([_rot(a, tab_ref, ROT_HALF[2]) for a in groups])

    @pl.when(j >= TILE_GATE0)
    def _():
        store([jax.nn.sigmoid(a + bg_ref[:, g * LANES:(g + 1) * LANES]) for g, a in enumerate(groups)])


def _tab_index(j, i):
    is_a = j <= 1
    is_c = (j == _TILE_CQ) | (j == _TILE_CKV)
    is_d = (j >= _TILES_DQ[0]) & (j <= _TILES_DK[-1])
    variant = jnp.where(is_a, 0, jnp.where(is_c, 1, 2))
    row = jnp.where(is_a | is_c | is_d, i, 0)
    return variant, 0, row, 0


def _inproj(xbf, w_in_l, b_gate_l, tabs, gains, tm=1024):
    s, d = xbf.shape
    tm = min(tm, s)
    return pl.pallas_call(
        _inproj_kernel,
        grid=(N_TILES, s // tm),
        in_specs=[pl.BlockSpec((tm, d), lambda j, i: (i, 0)),
                  pl.BlockSpec((d, TN), lambda j, i: (0, j)),
                  pl.BlockSpec((1, TN), lambda j, i: (0, jnp.maximum(j - TILE_GATE0, 0))),
                  pl.BlockSpec((None, 3, tm, LANES), _tab_index),
                  pl.BlockSpec((2, 1, LANES), lambda j, i: (0, 0, 0))],
        out_specs=pl.BlockSpec((tm, TN), lambda j, i: (i, j)),
        out_shape=jax.ShapeDtypeStruct((s, D_IN), BF16),
        scratch_shapes=[pltpu.VMEM((d, TN), BF16)],
        compiler_params=_cparams(("arbitrary", "arbitrary")),
        name="inproj",
    )(xbf, w_in_l, b_gate_l.reshape(1, -1), tabs, gains)


def _pair_attn_kernel(diff, tq, tk, lam_init, *refs):
    if diff:
        q_ref, k_ref, v_ref, z_ref, lp_ref, sg_ref, o_ref, q2_ref, m_ref, l_ref, acc_ref = refs
    else:
        q_ref, k_ref, v_ref, z_ref, o_ref, q2_ref, m_ref, l_ref, acc_ref = refs
    seq = k_ref.shape[0]

    if diff:
        q = q_ref[...]
        lane = lax.broadcasted_iota(jnp.int32, q.shape, 1)
        zero = jnp.zeros_like(q)
        q2_ref[:tq, :] = jnp.where(lane < DA_QK, q, zero)
        q2_ref[tq:, :] = jnp.where(lane >= DA_QK, q, zero)
    else:
        q2_ref[:tq, :] = q_ref[:, :LANES]
        q2_ref[tq:, :] = q_ref[:, LANES:]
    m_ref[...] = jnp.full(m_ref.shape, NEG_INF, F32)
    l_ref[...] = jnp.zeros(l_ref.shape, F32)
    acc_ref[...] = jnp.zeros(acc_ref.shape, F32)

    def body(c, carry):
        start = pl.multiple_of(c * tk, tk)
        kc = k_ref[pl.ds(start, tk), :]
        vc = v_ref[pl.ds(start, tk), :]
        s = lax.dot_general(q2_ref[...], kc, (((1,), (1,)), ((), ())),
                            preferred_element_type=F32)
        m_prev = m_ref[...]
        m_new = jnp.maximum(m_prev, jnp.max(s, axis=-1, keepdims=True))
        alpha = jnp.exp(m_prev - m_new)
        p = jnp.exp(s - m_new)
        l_ref[...] = alpha * l_ref[...] + jnp.sum(p, axis=-1, keepdims=True)
        acc_ref[...] = alpha * acc_ref[...] + jnp.dot(p.astype(BF16), vc, preferred_element_type=F32)
        m_ref[...] = m_new
        return carry

    lax.fori_loop(0, seq // tk, body, 0)

    o = acc_ref[...] / l_ref[...]
    if diff:
        lp = lp_ref[...]
        lam = (jnp.exp(jnp.sum(lp[0:1] * lp[1:2], keepdims=True))
               - jnp.exp(jnp.sum(lp[2:3] * lp[3:4], keepdims=True)) + lam_init)
        dlt = o[:tq] - lam * o[tq:]
        ms = jnp.mean(dlt * dlt, axis=-1, keepdims=True)
        y = dlt * lax.rsqrt(ms + RMS_EPS) * sg_ref[...] * (1.0 - lam_init)
        o_ref[...] = (y * z_ref[...].astype(F32)).astype(BF16)
    else:
        o_ref[:, :LANES] = (o[:tq] * z_ref[:, :LANES].astype(F32)).astype(BF16)
        o_ref[:, LANES:] = (o[tq:] * z_ref[:, LANES:].astype(F32)).astype(BF16)


def _pair_attn(h, diff, lam_init=0.0, lam_params=None, subln_g=None, tq=256, tk=512):
    s = h.shape[0]
    if diff:
        n_kv, qw = 4, LANES
        q_map = lambda hh, i: (i, CB_AQ + hh)
        k_map = lambda hh, i: (0, CB_AK + hh)
        v_map = lambda hh, i: (0, CB_AV + hh)
        z_map = lambda hh, i: (i, CB_AZ + hh)
    else:
        n_kv, qw = 2, 2 * LANES
        q_map = lambda hh, i: (i, CB_CQ // 2 + hh)
        k_map = lambda hh, i: (0, CB_CK + hh)
        v_map = lambda hh, i: (0, CB_CV + hh)
        z_map = lambda hh, i: (i, CB_CZ // 2 + hh)
    in_specs = [pl.BlockSpec((tq, qw), q_map),
                pl.BlockSpec((s, LANES), k_map),
                pl.BlockSpec((s, LANES), v_map),
                pl.BlockSpec((tq, qw), z_map)]
    args = [h, h, h, h]
    if diff:
        in_specs += [pl.BlockSpec((4, DA_QK), lambda hh, i: (0, 0)),
                     pl.BlockSpec((1, LANES), lambda hh, i: (0, 0))]
        args += [lam_params, subln_g.reshape(1, LANES)]
    return pl.pallas_call(
        functools.partial(_pair_attn_kernel, diff, tq, tk, lam_init),
        grid=(n_kv, s // tq),
        in_specs=in_specs,
        out_specs=pl.BlockSpec((tq, qw), lambda hh, i: (i, hh)),
        out_shape=jax.ShapeDtypeStruct((s, BRANCH_W), BF16),
        scratch_shapes=[pltpu.VMEM((2 * tq, LANES), BF16),
                        pltpu.VMEM((2 * tq, 1), F32),
                        pltpu.VMEM((2 * tq, 1), F32),
                        pltpu.VMEM((2 * tq, LANES), F32)],
        compiler_params=_cparams(("arbitrary", "arbitrary")),
        name="diff_attn" if diff else "axial_gqa",
    )(*args)


def _nat_bias(rpb):
    cols = jnp.arange(GRID_W)
    c0 = jnp.clip(cols - NB_COLS // 2, 0, GRID_W - NB_COLS)
    kc = jnp.arange(GRID_W)
    valid = (kc[None, :] >= c0[:, None]) & (kc[None, :] < c0[:, None] + NB_COLS)
    dc = jnp.clip(kc[None, :] - cols[:, None] + (NB_COLS - 1), 0, 2 * NB_COLS - 2)
    t = jnp.where(valid[None, None], rpb[:, :, dc].astype(F32), NEG_INF)
    out = []
    for ds in range(NB_ROWS):
        w = t[:, ds:ds + NB_ROWS]
        out.append(jnp.transpose(w, (0, 2, 1, 3)).reshape(rpb.shape[0], GRID_W, NB_ROWS * GRID_W))
    return jnp.stack(out)


def _nat_kernel(rows, q_ref, k_ref, v_ref, z_ref, bias_ref, o_ref):
    r = pl.program_id(1)
    r0 = jnp.clip(r - NB_ROWS // 2, 0, rows - NB_ROWS)
    start = pl.multiple_of(r0 * GRID_W, GRID_W)
    win = NB_ROWS * GRID_W
    kw = k_ref[pl.ds(start, win), :]
    vw = v_ref[pl.ds(start, win), :]
    s = lax.dot_general(q_ref[...], kw, (((1,), (1,)), ((), ())), preferred_element_type=F32)
    s = s + bias_ref[...]
    m = jnp.max(s, axis=-1, keepdims=True)
    p = jnp.exp(s - m)
    l = jnp.sum(p, axis=-1, keepdims=True)
    o = jnp.dot(p.astype(BF16), vw, preferred_element_type=F32) / l
    o_ref[...] = (o * z_ref[...].astype(F32)).astype(BF16)


def _nat_attn(h, bias):
    s = h.shape[0]
    rows = s // GRID_W

    def bias_map(hh, r):
        r0 = jnp.clip(r - NB_ROWS // 2, 0, rows - NB_ROWS)
        return r0 - r + (NB_ROWS - 1), hh, 0, 0

    return pl.pallas_call(
        functools.partial(_nat_kernel, rows),
        grid=(4, rows),
        in_specs=[pl.BlockSpec((GRID_W, LANES), lambda hh, r: (r, CB_BQ + hh)),
                  pl.BlockSpec((s, LANES), lambda hh, r: (0, CB_BK + hh)),
                  pl.BlockSpec((s, LANES), lambda hh, r: (0, CB_BV + hh)),
                  pl.BlockSpec((GRID_W, LANES), lambda hh, r: (r, CB_BZ + hh)),
                  pl.BlockSpec((None, None, GRID_W, NB_ROWS * GRID_W), bias_map)],
        out_specs=pl.BlockSpec((GRID_W, LANES), lambda hh, r: (r, hh)),
        out_shape=jax.ShapeDtypeStruct((s, BRANCH_W), BF16),
        compiler_params=_cparams(("arbitrary", "arbitrary")),
        name="nat_attn",
    )(h, h, h, h, bias)


def _dil_kernel(tq, q0_ref, q1_ref, q2_ref, k0_ref, k1_ref, k2_ref, v0_ref, v1_ref, v2_ref,
                z_ref, o_ref):
    seq = k0_ref.shape[0]
    t0 = pl.program_id(1) * tq
    qpos = t0 + lax.broadcasted_iota(jnp.int32, (tq, 1), 0)
    scores, wins = [], []
    for rate, q_ref, k_ref in zip(DIL_RATES, (q0_ref, q1_ref, q2_ref), (k0_ref, k1_ref, k2_ref)):
        reach = DIL_SIDE * rate
        width = tq + 2 * reach
        start = pl.multiple_of(jnp.clip(t0 - reach, 0, seq - width), DIL_SIDE)
        kw = k_ref[pl.ds(start, width), :]
        s = lax.dot_general(q_ref[...], kw, (((1,), (1,)), ((), ())), preferred_element_type=F32)
        dist = start + lax.broadcasted_iota(jnp.int32, (1, width), 1) - qpos
        valid = (jnp.abs(dist) <= reach) & ((dist & (rate - 1)) == 0)
        scores.append(jnp.where(valid, s, NEG_INF))
        wins.append((start, width))
    m = scores[0].max(axis=-1, keepdims=True)
    for s in scores[1:]:
        m = jnp.maximum(m, s.max(axis=-1, keepdims=True))
    l = jnp.zeros((tq, 1), F32)
    o = jnp.zeros((tq, LANES), F32)
    for s, (start, width), v_ref in zip(scores, wins, (v0_ref, v1_ref, v2_ref)):
        p = jnp.exp(s - m)
        l = l + jnp.sum(p, axis=-1, keepdims=True)
        o = o + jnp.dot(p.astype(BF16), v_ref[pl.ds(start, width), :], preferred_element_type=F32)
    o_ref[...] = (o / l * z_ref[...].astype(F32)).astype(BF16)


def _dil_attn(h, tq=128):
    s = h.shape[0]
    qs = [pl.BlockSpec((tq, LANES), functools.partial(lambda g, hh, i: (i, CB_DQ + 4 * g + hh), g))
          for g in range(3)]
    ks = [pl.BlockSpec((s, LANES), functools.partial(lambda g, hh, i: (0, CB_DK + 4 * g + hh), g))
          for g in range(3)]
    vs = [pl.BlockSpec((s, LANES), functools.partial(lambda g, hh, i: (0, CB_DV + 4 * g + hh), g))
          for g in range(3)]
    return pl.pallas_call(
        functools.partial(_dil_kernel, tq),
        grid=(4, s // tq),
        in_specs=qs + ks + vs + [pl.BlockSpec((tq, LANES), lambda hh, i: (i, CB_DZ + hh))],
        out_specs=pl.BlockSpec((tq, LANES), lambda hh, i: (i, hh)),
        out_shape=jax.ShapeDtypeStruct((s, BRANCH_W), BF16),
        compiler_params=_cparams(("arbitrary", "arbitrary")),
        name="dil_attn",
    )(*([h] * 10))


def _merge_kernel(ya_ref, yb_ref, yc_ref, yd_ref, ga_ref, gb_ref, gc_ref, gd_ref, wb_ref, o_ref):
    acc = None
    for n, (y_ref, g_ref) in enumerate(zip((ya_ref, yb_ref, yc_ref, yd_ref),
                                           (ga_ref, gb_ref, gc_ref, gd_ref))):
        proj = jnp.dot(y_ref[...], wb_ref[n], preferred_element_type=F32)
        term = proj * g_ref[...].astype(F32)
        acc = term if acc is None else acc + term
    o_ref[...] = acc.astype(BF16)


def _merge(ys, h, wb_bf, tm=1024):
    s = h.shape[0]
    tm = min(tm, s)
    n_col = D_MODEL // TN
    y_specs = [pl.BlockSpec((tm, BRANCH_W), lambda j, i: (i, 0)) for _ in range(N_BRANCH)]
    g_specs = [pl.BlockSpec((tm, TN), functools.partial(lambda n, j, i: (i, TILE_GATE0 + n_col * n + j), n))
               for n in range(N_BRANCH)]
    return pl.pallas_call(
        _merge_kernel,
        grid=(n_col, s // tm),
        in_specs=y_specs + g_specs + [pl.BlockSpec((N_BRANCH, BRANCH_W, TN), lambda j, i: (0, 0, j))],
        out_specs=pl.BlockSpec((tm, TN), lambda j, i: (i, j)),
        out_shape=jax.ShapeDtypeStruct((s, D_MODEL), BF16),
        compiler_params=_cparams(("arbitrary", "arbitrary")),
        name="branch_merge",
    )(*ys, h, h, h, h, wb_bf)


def _out_ln_kernel(alpha, m_ref, w_ref, x_ref, g_ref, b_ref, y_ref, ybf_ref):
    out = jnp.dot(m_ref[...], w_ref[...], preferred_element_type=F32)
    y = _ln_rows(alpha * x_ref[...] + out, g_ref[...], b_ref[...])
    y_ref[...] = y
    ybf_ref[...] = y.astype(BF16)


def _out_ln(merged, wo_bf, x, g, b, alpha, tm=512):
    s, d = x.shape
    return pl.pallas_call(
        functools.partial(_out_ln_kernel, alpha),
        grid=(s // tm,),
        in_specs=[pl.BlockSpec((tm, d), lambda i: (i, 0)),
                  pl.BlockSpec((d, d), lambda i: (0, 0)),
                  pl.BlockSpec((tm, d), lambda i: (i, 0)),
                  pl.BlockSpec((1, d), lambda i: (0, 0)),
                  pl.BlockSpec((1, d), lambda i: (0, 0))],
        out_specs=[pl.BlockSpec((tm, d), lambda i: (i, 0)),
                   pl.BlockSpec((tm, d), lambda i: (i, 0))],
        out_shape=[jax.ShapeDtypeStruct((s, d), F32), jax.ShapeDtypeStruct((s, d), BF16)],
        compiler_params=_cparams(("arbitrary",)),
        name="out_ln",
    )(merged, wo_bf, x, g.reshape(1, d), b.reshape(1, d))


def kernel(x, emb_ln_g, emb_ln_b, w_in, b_gate, diff_lambda, diff_subln_g, nat_rpb,
           gqa_q_norm_g, gqa_k_norm_g, w_branch, w_out, ln_g, ln_b):
    batch, seq, d = x.shape
    assert batch == 1 and d == D_MODEL and w_in.shape[-1] == D_IN
    depth = w_in.shape[0]
    alpha = (2 * depth) ** 0.25
    tabs = _rope_tables(seq)
    xf, xbf = _embed_ln(x[0], emb_ln_g, emb_ln_b)
    for l in range(depth):
        lam_init = 0.8 - 0.6 * math.exp(-0.3 * l)
        gains = jnp.stack([gqa_q_norm_g[l], gqa_k_norm_g[l]]).reshape(2, 1, LANES)
        h = _inproj(xbf, w_in[l], b_gate[l], tabs, gains)
        ya = _pair_attn(h, True, lam_init, diff_lambda[l], diff_subln_g[l])
        yb = _nat_attn(h, _nat_bias(nat_rpb[l]))
        yc = _pair_attn(h, False)
        yd = _dil_attn(h)
        merged = _merge((ya, yb, yc, yd), h, w_branch[l].astype(BF16))
        xf, xbf = _out_ln(merged, w_out[l].astype(BF16), xf, ln_g[l], ln_b[l], alpha)
    return xf[None]
```

```python
import functools
import math

import jax
import jax.numpy as jnp
from jax import lax
from jax.experimental import pallas as pl
from jax.experimental.pallas import tpu as pltpu

F32 = jnp.float32
BF16 = jnp.bfloat16

D_MODEL = 2048
GRID_W = 64
HEAD_DIM = 128
N_BRANCH = 4
BRANCH_W = D_MODEL // 4
DA_QK = HEAD_DIM // 2
NB_ROWS = 8
NB_COLS = 16
AXIAL_THETA = 10000.0
DIL_RATES = (1, 4, 16)
DIL_SIDE = 64
ROPE_THETA = 500000.0
LN_EPS = 1e-5
RMS_EPS = 1e-6
NEG_INF = -1e30
LOG2E = math.log2(math.e)

LANES = 128
VT_PAD = 16
TN = 512
D_IN = 18944
N_TILES = D_IN // TN
CB_AQ, CB_AK, CB_AV, CB_AZ = 0, 4, 8, 12
CB_BQ, CB_BK, CB_BV, CB_BZ = 16, 20, 24, 28
CB_CQ, CB_CK, CB_CV, CB_CZ = 32, 36, 38, 40
CB_DQ, CB_DK, CB_DV, CB_DZ = 44, 56, 68, 80
TILE_GATE0 = 21

VMEM_LIMIT = 56 * 1024 * 1024


def _cparams(sem):
    return pltpu.CompilerParams(dimension_semantics=sem, vmem_limit_bytes=VMEM_LIMIT)


def _ln_rows(x, g, b):
    mu = jnp.mean(x, axis=-1, keepdims=True)
    xc = x - mu
    var = jnp.mean(xc * xc, axis=-1, keepdims=True)
    return xc * lax.rsqrt(var + LN_EPS) * g + b


def _embed_ln_kernel(x_ref, g_ref, b_ref, y_ref, ybf_ref):
    y = _ln_rows(x_ref[...], g_ref[...], b_ref[...])
    y_ref[...] = y
    ybf_ref[...] = y.astype(BF16)


def _embed_ln(x, g, b, tm=512):
    s, d = x.shape
    return pl.pallas_call(
        _embed_ln_kernel,
        grid=(s // tm,),
        in_specs=[pl.BlockSpec((tm, d), lambda i: (i, 0)),
                  pl.BlockSpec((1, d), lambda i: (0, 0)),
                  pl.BlockSpec((1, d), lambda i: (0, 0))],
        out_specs=[pl.BlockSpec((tm, d), lambda i: (i, 0)),
                   pl.BlockSpec((tm, d), lambda i: (i, 0))],
        out_shape=[jax.ShapeDtypeStruct((s, d), F32), jax.ShapeDtypeStruct((s, d), BF16)],
        compiler_params=_cparams(("arbitrary",)),
        name="embed_ln",
    )(x, g.reshape(1, d), b.reshape(1, d))


def _rope_tables(seq):
    t = jnp.arange(seq)

    def cs(pos, dim, theta):
        half = dim // 2
        inv = jnp.power(jnp.float32(theta), -jnp.arange(half, dtype=jnp.float32) * 2.0 / dim)
        ang = pos.astype(jnp.float32)[:, None] * inv[None, :]
        return jnp.cos(ang), jnp.sin(ang)

    def group(cos, sin, width):
        half = cos.shape[1]
        pad = width - 2 * half
        one = jnp.ones((seq, pad), F32)
        zero = jnp.zeros((seq, pad), F32)
        zh = jnp.zeros((seq, half), F32)
        return (jnp.concatenate([cos, cos, one], 1),
                jnp.concatenate([zh, sin, zero], 1),
                jnp.concatenate([-sin, zh, zero], 1))

    ca, sa = cs(t, DA_QK // 4, ROPE_THETA)
    va = [jnp.concatenate([p, p], 1) for p in group(ca, sa, DA_QK)]
    cr, sr = cs(t // GRID_W, HEAD_DIM // 2, AXIAL_THETA)
    cc, sc = cs(t % GRID_W, HEAD_DIM // 2, AXIAL_THETA)
    vc = [jnp.concatenate([p, q], 1)
          for p, q in zip(group(cr, sr, HEAD_DIM // 2), group(cc, sc, HEAD_DIM // 2))]
    cd, sd = cs(t, HEAD_DIM // 4, ROPE_THETA)
    vd = list(group(cd, sd, HEAD_DIM))
    return jnp.stack([jnp.stack(va), jnp.stack(vc), jnp.stack(vd)])


ROT_HALF = (DA_QK // 8, HEAD_DIM // 4, HEAD_DIM // 8)


def _rot(a, tab_ref, half):
    return (a * tab_ref[0] + pltpu.roll(a, half, 1) * tab_ref[1]
            + pltpu.roll(a, LANES - half, 1) * tab_ref[2])


_TILES_ROT_A = (0, 1)
_TILES_PLAIN = (2, 5, 6, 17, 18, 19)
_TILES_SILU = (3, 7, 10, 20)
_TILE_BQ, _TILE_CQ, _TILE_CKV = 4, 8, 9
_TILES_DQ = (11, 12, 13)
_TILES_DK = (14, 15, 16)


def _any_of(j, tiles):
    c = j == tiles[0]
    for t in tiles[1:]:
        c = c | (j == t)
    return c


def _inproj_kernel(x_ref, w_ref, bg_ref, tab_ref, gn_ref, o_ref, wbf_ref):
    j = pl.program_id(0)
    i = pl.program_id(1)

    @pl.when(i == 0)
    def _():
        wbf_ref[...] = w_ref[...].astype(BF16)

    acc = jnp.dot(x_ref[...], wbf_ref[...], preferred_element_type=F32)
    groups = [acc[:, g * LANES:(g + 1) * LANES] for g in range(TN // LANES)]

    def store(vals):
        for g, v in enumerate(vals):
            o_ref[:, g * LANES:(g + 1) * LANES] = v.astype(BF16)

    def rms(a, gain):
        ms = jnp.mean(a * a, axis=-1, keepdims=True)
        return a * lax.rsqrt(ms + RMS_EPS) * gain

    @pl.when(j == 0)
    def _():
        store([_rot(a, tab_ref, ROT_HALF[0]) * (DA_QK ** -0.5 * LOG2E) for a in groups])

    @pl.when(j == 1)
    def _():
        store([_rot(a, tab_ref, ROT_HALF[0]) for a in groups])

    @pl.when(_any_of(j, _TILES_PLAIN))
    def _():
        store(groups)

    @pl.when(_any_of(j, _TILES_SILU))
    def _():
        store([a * jax.nn.sigmoid(a) for a in groups])

    @pl.when(j == _TILE_BQ)
    def _():
        store([a * (HEAD_DIM ** -0.5) for a in groups])

    @pl.when(j == _TILE_CQ)
    def _():
        store([_rot(rms(a, gn_ref[0]), tab_ref, ROT_HALF[1]) * (HEAD_DIM ** -0.5 * LOG2E) for a in groups])

    @pl.when(j == _TILE_CKV)
    def _():
        store([_rot(rms(a, gn_ref[1]), tab_ref, ROT_HALF[1]) for a in groups[:2]] + groups[2:])

    @pl.when(_any_of(j, _TILES_DQ))
    def _():
        store([_rot(a, tab_ref, ROT_HALF[2]) * (HEAD_DIM ** -0.5) for a in groups])

    @pl.when(_any_of(j, _TILES_DK))
    def _():
        store([_rot(a, tab_ref, ROT_HALF[2]) for a in groups])

    @pl.when(j >= TILE_GATE0)
    def _():
        store([jax.nn.sigmoid(a + bg_ref[:, g * LANES:(g + 1) * LANES]) for g, a in enumerate(groups)])


def _tab_index(j, i):
    is_a = j <= 1
    is_c = (j == _TILE_CQ) | (j == _TILE_CKV)
    is_d = (j >= _TILES_DQ[0]) & (j <= _TILES_DK[-1])
    variant = jnp.where(is_a, 0, jnp.where(is_c, 1, 2))
    row = jnp.where(is_a | is_c | is_d, i, 0)
    return variant, 0, row, 0


def _inproj(xbf, w_in, b_gate, layer, tabs, gains, tm=1024):
    s, d = xbf.shape
    tm = min(tm, s)
    return pl.pallas_call(
        _inproj_kernel,
        grid=(N_TILES, s // tm),
        in_specs=[pl.BlockSpec((tm, d), lambda j, i: (i, 0)),
                  pl.BlockSpec((None, d, TN), lambda j, i: (layer, 0, j)),
                  pl.BlockSpec((None, 1, TN), lambda j, i: (layer, 0, jnp.maximum(j - TILE_GATE0, 0))),
                  pl.BlockSpec((None, 3, tm, LANES), _tab_index),
                  pl.BlockSpec((2, 1, LANES), lambda j, i: (0, 0, 0))],
        out_specs=pl.BlockSpec((tm, TN), lambda j, i: (i, j)),
        out_shape=jax.ShapeDtypeStruct((s, D_IN), BF16),
        scratch_shapes=[pltpu.VMEM((d, TN), BF16)],
        compiler_params=_cparams(("arbitrary", "arbitrary")),
        name="inproj",
    )(xbf, w_in, b_gate.reshape(b_gate.shape[0], 1, -1), tabs, gains)


def _pair_attn_kernel(diff, tq, tk, lam_init, *refs):
    if diff:
        q_ref, k_ref, v_ref, z_ref, lp_ref, sg_ref, o_ref, q2_ref, vt_ref, acc_ref, s_ref = refs
    else:
        q_ref, k_ref, v_ref, z_ref, o_ref, q2_ref, vt_ref, acc_ref, s_ref = refs
    seq = k_ref.shape[0]
    nq = 2 * tq

    @pl.when(pl.program_id(1) == 0)
    def _():
        for c in range(seq // tk):
            vt_ref[c, :LANES, :] = v_ref[c * tk:(c + 1) * tk, :].astype(F32).T.astype(BF16)
            vt_ref[c, LANES:, :] = jnp.ones((VT_PAD, tk), BF16)

    if diff:
        q = q_ref[...]
        lane = lax.broadcasted_iota(jnp.int32, q.shape, 1)
        zero = jnp.zeros_like(q)
        q2_ref[:tq, :] = jnp.where(lane < DA_QK, q, zero)
        q2_ref[tq:, :] = jnp.where(lane >= DA_QK, q, zero)
    else:
        q2_ref[:tq, :] = q_ref[:, :LANES]
        q2_ref[tq:, :] = q_ref[:, LANES:]
    acc_ref[...] = jnp.zeros(acc_ref.shape, F32)

    n_chunks = seq // tk

    def scores(c, slot):
        start = pl.multiple_of(c * tk, tk)
        s_ref[slot] = lax.dot_general(k_ref[pl.ds(start, tk), :], q2_ref[...], (((1,), (1,)), ((), ())),
                                      preferred_element_type=F32)

    def fold(c, slot, m_prev):
        s = s_ref[slot]
        m_new = jnp.maximum(m_prev, jnp.max(s, axis=0, keepdims=True))
        alpha = jnp.exp2(m_prev - m_new)
        p = jnp.exp2(s - m_new)
        acc_ref[...] = alpha * acc_ref[...] + jnp.dot(vt_ref[c], p.astype(BF16),
                                                      preferred_element_type=F32)
        return m_new

    unroll = 4 if n_chunks % 4 == 0 else 2

    def group(base, m, last):
        for u in range(unroll):
            if not (last and u == unroll - 1):
                scores(base + u + 1, (u + 1) % 2)
            m = fold(base + u, u % 2, m)
        return m

    scores(0, 0)
    m = jnp.full((1, nq), NEG_INF, F32)
    n_groups = n_chunks // unroll
    if n_groups > 1:
        m = lax.fori_loop(0, n_groups - 1, lambda j, mm: group(j * unroll, mm, False), m)
    group((n_groups - 1) * unroll, m, True)

    acc = acc_ref[...]
    o = (acc[:LANES] / acc[LANES:LANES + 1]).T
    if diff:
        lp = lp_ref[...]
        lam = (jnp.exp(jnp.sum(lp[0:1] * lp[1:2], keepdims=True))
               - jnp.exp(jnp.sum(lp[2:3] * lp[3:4], keepdims=True)) + lam_init)
        dlt = o[:tq] - lam * o[tq:]
        ms = jnp.mean(dlt * dlt, axis=-1, keepdims=True)
        y = dlt * lax.rsqrt(ms + RMS_EPS) * sg_ref[...] * (1.0 - lam_init)
        o_ref[...] = (y * z_ref[...].astype(F32)).astype(BF16)
    else:
        o_ref[:, :LANES] = (o[:tq] * z_ref[:, :LANES].astype(F32)).astype(BF16)
        o_ref[:, LANES:] = (o[tq:] * z_ref[:, LANES:].astype(F32)).astype(BF16)


def _pair_attn(h, diff, lam_init=0.0, lam_params=None, subln_g=None, tq=256, tk=512):
    s = h.shape[0]
    if diff:
        n_kv, qw = 4, LANES
        q_map = lambda hh, i: (i, CB_AQ + hh)
        k_map = lambda hh, i: (0, CB_AK + hh)
        v_map = lambda hh, i: (0, CB_AV + hh)
        z_map = lambda hh, i: (i, CB_AZ + hh)
    else:
        n_kv, qw = 2, 2 * LANES
        q_map = lambda hh, i: (i, CB_CQ // 2 + hh)
        k_map = lambda hh, i: (0, CB_CK + hh)
        v_map = lambda hh, i: (0, CB_CV + hh)
        z_map = lambda hh, i: (i, CB_CZ // 2 + hh)
    in_specs = [pl.BlockSpec((tq, qw), q_map),
                pl.BlockSpec((s, LANES), k_map),
                pl.BlockSpec((s, LANES), v_map),
                pl.BlockSpec((tq, qw), z_map)]
    args = [h, h, h, h]
    if diff:
        in_specs += [pl.BlockSpec((4, DA_QK), lambda hh, i: (0, 0)),
                     pl.BlockSpec((1, LANES), lambda hh, i: (0, 0))]
        args += [lam_params, subln_g.reshape(1, LANES)]
    return pl.pallas_call(
        functools.partial(_pair_attn_kernel, diff, tq, tk, lam_init),
        grid=(n_kv, s // tq),
        in_specs=in_specs,
        out_specs=pl.BlockSpec((tq, qw), lambda hh, i: (i, hh)),
        out_shape=jax.ShapeDtypeStruct((s, BRANCH_W), BF16),
        scratch_shapes=[pltpu.VMEM((2 * tq, LANES), BF16),
                        pltpu.VMEM((s // tk, LANES + VT_PAD, tk), BF16),
                        pltpu.VMEM((LANES + VT_PAD, 2 * tq), F32),
                        pltpu.VMEM((2, tk, 2 * tq), F32)],
        compiler_params=_cparams(("arbitrary", "arbitrary")),
        name="diff_attn" if diff else "axial_gqa",
    )(*args)


def _nat_bias(rpb):
    cols = jnp.arange(GRID_W)
    c0 = jnp.clip(cols - NB_COLS // 2, 0, GRID_W - NB_COLS)
    kc = jnp.arange(GRID_W)
    valid = (kc[None, :] >= c0[:, None]) & (kc[None, :] < c0[:, None] + NB_COLS)
    dc = jnp.clip(kc[None, :] - cols[:, None] + (NB_COLS - 1), 0, 2 * NB_COLS - 2)
    t = jnp.where(valid[None, None], rpb[:, :, dc].astype(F32), NEG_INF)
    out = []
    for ds in range(NB_ROWS):
        w = t[:, ds:ds + NB_ROWS]
        out.append(jnp.transpose(w, (0, 2, 1, 3)).reshape(rpb.shape[0], GRID_W, NB_ROWS * GRID_W))
    return jnp.stack(out)


def _nat_kernel(rows, q_ref, k_ref, v_ref, z_ref, bias_ref, o_ref):
    r = pl.program_id(1)
    r0 = jnp.clip(r - NB_ROWS // 2, 0, rows - NB_ROWS)
    start = pl.multiple_of(r0 * GRID_W, GRID_W)
    win = NB_ROWS * GRID_W
    kw = k_ref[pl.ds(start, win), :]
    vw = v_ref[pl.ds(start, win), :]
    s = lax.dot_general(q_ref[...], kw, (((1,), (1,)), ((), ())), preferred_element_type=F32)
    s = s + bias_ref[...]
    m = jnp.max(s, axis=-1, keepdims=True)
    p = jnp.exp(s - m)
    l = jnp.sum(p, axis=-1, keepdims=True)
    o = jnp.dot(p.astype(BF16), vw, preferred_element_type=F32) / l
    o_ref[...] = (o * z_ref[...].astype(F32)).astype(BF16)


def _nat_attn(h, bias):
    s = h.shape[0]
    rows = s // GRID_W

    def bias_map(hh, r):
        r0 = jnp.clip(r - NB_ROWS // 2, 0, rows - NB_ROWS)
        return r0 - r + (NB_ROWS - 1), hh, 0, 0

    return pl.pallas_call(
        functools.partial(_nat_kernel, rows),
        grid=(4, rows),
        in_specs=[pl.BlockSpec((GRID_W, LANES), lambda hh, r: (r, CB_BQ + hh)),
                  pl.BlockSpec((s, LANES), lambda hh, r: (0, CB_BK + hh)),
                  pl.BlockSpec((s, LANES), lambda hh, r: (0, CB_BV + hh)),
                  pl.BlockSpec((GRID_W, LANES), lambda hh, r: (r, CB_BZ + hh)),
                  pl.BlockSpec((None, None, GRID_W, NB_ROWS * GRID_W), bias_map)],
        out_specs=pl.BlockSpec((GRID_W, LANES), lambda hh, r: (r, hh)),
        out_shape=jax.ShapeDtypeStruct((s, BRANCH_W), BF16),
        compiler_params=_cparams(("arbitrary", "arbitrary")),
        name="nat_attn",
    )(h, h, h, h, bias)


def _dil_kernel(tq, q0_ref, q1_ref, q2_ref, k0_ref, k1_ref, k2_ref, v0_ref, v1_ref, v2_ref,
                z_ref, o_ref):
    seq = k0_ref.shape[0]
    t0 = pl.program_id(1) * tq
    qpos = t0 + lax.broadcasted_iota(jnp.int32, (tq, 1), 0)
    scores, wins = [], []
    for rate, q_ref, k_ref in zip(DIL_RATES, (q0_ref, q1_ref, q2_ref), (k0_ref, k1_ref, k2_ref)):
        reach = DIL_SIDE * rate
        width = tq + 2 * reach
        start = pl.multiple_of(jnp.clip(t0 - reach, 0, seq - width), DIL_SIDE)
        kw = k_ref[pl.ds(start, width), :]
        s = lax.dot_general(q_ref[...], kw, (((1,), (1,)), ((), ())), preferred_element_type=F32)
        dist = start + lax.broadcasted_iota(jnp.int32, (1, width), 1) - qpos
        valid = (jnp.abs(dist) <= reach) & ((dist & (rate - 1)) == 0)
        scores.append(jnp.where(valid, s, NEG_INF))
        wins.append((start, width))
    m = scores[0].max(axis=-1, keepdims=True)
    for s in scores[1:]:
        m = jnp.maximum(m, s.max(axis=-1, keepdims=True))
    l = jnp.zeros((tq, 1), F32)
    o = jnp.zeros((tq, LANES), F32)
    for s, (start, width), v_ref in zip(scores, wins, (v0_ref, v1_ref, v2_ref)):
        p = jnp.exp(s - m)
        l = l + jnp.sum(p, axis=-1, keepdims=True)
        o = o + jnp.dot(p.astype(BF16), v_ref[pl.ds(start, width), :], preferred_element_type=F32)
    o_ref[...] = (o / l * z_ref[...].astype(F32)).astype(BF16)


def _dil_attn(h, tq=128):
    s = h.shape[0]
    qs = [pl.BlockSpec((tq, LANES), functools.partial(lambda g, hh, i: (i, CB_DQ + 4 * g + hh), g))
          for g in range(3)]
    ks = [pl.BlockSpec((s, LANES), functools.partial(lambda g, hh, i: (0, CB_DK + 4 * g + hh), g))
          for g in range(3)]
    vs = [pl.BlockSpec((s, LANES), functools.partial(lambda g, hh, i: (0, CB_DV + 4 * g + hh), g))
          for g in range(3)]
    return pl.pallas_call(
        functools.partial(_dil_kernel, tq),
        grid=(4, s // tq),
        in_specs=qs + ks + vs + [pl.BlockSpec((tq, LANES), lambda hh, i: (i, CB_DZ + hh))],
        out_specs=pl.BlockSpec((tq, LANES), lambda hh, i: (i, hh)),
        out_shape=jax.ShapeDtypeStruct((s, BRANCH_W), BF16),
        compiler_params=_cparams(("arbitrary", "arbitrary")),
        name="dil_attn",
    )(*([h] * 10))


def _merge_kernel(ya_ref, yb_ref, yc_ref, yd_ref, ga_ref, gb_ref, gc_ref, gd_ref, wb_ref, o_ref):
    acc = None
    for n, (y_ref, g_ref) in enumerate(zip((ya_ref, yb_ref, yc_ref, yd_ref),
                                           (ga_ref, gb_ref, gc_ref, gd_ref))):
        proj = jnp.dot(y_ref[...], wb_ref[n], preferred_element_type=F32)
        term = proj * g_ref[...].astype(F32)
        acc = term if acc is None else acc + term
    o_ref[...] = acc.astype(BF16)


def _merge(ys, h, wb_bf, tm=1024):
    s = h.shape[0]
    tm = min(tm, s)
    n_col = D_MODEL // TN
    y_specs = [pl.BlockSpec((tm, BRANCH_W), lambda j, i: (i, 0)) for _ in range(N_BRANCH)]
    g_specs = [pl.BlockSpec((tm, TN), functools.partial(lambda n, j, i: (i, TILE_GATE0 + n_col * n + j), n))
               for n in range(N_BRANCH)]
    return pl.pallas_call(
        _merge_kernel,
        grid=(n_col, s // tm),
        in_specs=y_specs + g_specs + [pl.BlockSpec((N_BRANCH, BRANCH_W, TN), lambda j, i: (0, 0, j))],
        out_specs=pl.BlockSpec((tm, TN), lambda j, i: (i, j)),
        out_shape=jax.ShapeDtypeStruct((s, D_MODEL), BF16),
        compiler_params=_cparams(("arbitrary", "arbitrary")),
        name="branch_merge",
    )(*ys, h, h, h, h, wb_bf)


def _out_ln_kernel(alpha, m_ref, w_ref, x_ref, g_ref, b_ref, y_ref, ybf_ref):
    out = jnp.dot(m_ref[...], w_ref[...], preferred_element_type=F32)
    y = _ln_rows(alpha * x_ref[...] + out, g_ref[...], b_ref[...])
    y_ref[...] = y
    ybf_ref[...] = y.astype(BF16)


def _out_ln(merged, wo_bf, x, g, b, alpha, tm=512):
    s, d = x.shape
    return pl.pallas_call(
        functools.partial(_out_ln_kernel, alpha),
        grid=(s // tm,),
        in_specs=[pl.BlockSpec((tm, d), lambda i: (i, 0)),
                  pl.BlockSpec((d, d), lambda i: (0, 0)),
                  pl.BlockSpec((tm, d), lambda i: (i, 0)),
                  pl.BlockSpec((1, d), lambda i: (0, 0)),
                  pl.BlockSpec((1, d), lambda i: (0, 0))],
        out_specs=[pl.BlockSpec((tm, d), lambda i: (i, 0)),
                   pl.BlockSpec((tm, d), lambda i: (i, 0))],
        out_shape=[jax.ShapeDtypeStruct((s, d), F32), jax.ShapeDtypeStruct((s, d), BF16)],
        compiler_params=_cparams(("arbitrary",)),
        name="out_ln",
    )(merged, wo_bf, x, g.reshape(1, d), b.reshape(1, d))


def kernel(x, emb_ln_g, emb_ln_b, w_in, b_gate, diff_lambda, diff_subln_g, nat_rpb,
           gqa_q_norm_g, gqa_k_norm_g, w_branch, w_out, ln_g, ln_b):
    batch, seq, d = x.shape
    assert batch == 1 and d == D_MODEL and w_in.shape[-1] == D_IN
    depth = w_in.shape[0]
    alpha = (2 * depth) ** 0.25
    tabs = _rope_tables(seq)
    xf, xbf = _embed_ln(x[0], emb_ln_g, emb_ln_b)
    for l in range(depth):
        lam_init = 0.8 - 0.6 * math.exp(-0.3 * l)
        gains = jnp.stack([gqa_q_norm_g[l], gqa_k_norm_g[l]]).reshape(2, 1, LANES)
        h = _inproj(xbf, w_in, b_gate, l, tabs, gains)
        ya = _pair_attn(h, True, lam_init, diff_lambda[l], diff_subln_g[l])
        yb = _nat_attn(h, _nat_bias(nat_rpb[l]))
        yc = _pair_attn(h, False)
        yd = _dil_attn(h)
        merged = _merge((ya, yb, yc, yd), h, w_branch[l].astype(BF16))
        xf, xbf = _out_ln(merged, w_out[l].astype(BF16), xf, ln_g[l], ln_b[l], alpha)
    return xf[None]
```

```python
import functools
import math

import numpy as np
import jax
import jax.numpy as jnp
from jax import lax
from jax.experimental import pallas as pl
from jax.experimental.pallas import tpu as pltpu

F32 = jnp.float32
BF16 = jnp.bfloat16

D_MODEL = 2048
GRID_W = 64
HEAD_DIM = 128
N_BRANCH = 4
BRANCH_W = D_MODEL // 4
DA_QK = HEAD_DIM // 2
NB_ROWS = 8
NB_COLS = 16
AXIAL_THETA = 10000.0
DIL_RATES = (1, 4, 16)
DIL_SIDE = 64
ROPE_THETA = 500000.0
LN_EPS = 1e-5
RMS_EPS = 1e-6
NEG_INF = -1e30
LOG2E = math.log2(math.e)

LANES = 128
VT_PAD = 16
TN = 512
D_IN = 18944
N_TILES = D_IN // TN
CB_AQ, CB_AK, CB_AV, CB_AZ = 0, 4, 8, 12
CB_BQ, CB_BK, CB_BV, CB_BZ = 16, 20, 24, 28
CB_CQ, CB_CK, CB_CV, CB_CZ = 32, 36, 38, 40
CB_DQ, CB_DK, CB_DV, CB_DZ = 44, 56, 68, 80
TILE_GATE0 = 21

VMEM_LIMIT = 56 * 1024 * 1024


def _cparams(sem):
    return pltpu.CompilerParams(dimension_semantics=sem, vmem_limit_bytes=VMEM_LIMIT)


def _ln_rows(x, g, b):
    mu = jnp.mean(x, axis=-1, keepdims=True)
    xc = x - mu
    var = jnp.mean(xc * xc, axis=-1, keepdims=True)
    return xc * lax.rsqrt(var + LN_EPS) * g + b


def _embed_ln_kernel(x_ref, g_ref, b_ref, y_ref, ybf_ref):
    y = _ln_rows(x_ref[...], g_ref[...], b_ref[...])
    y_ref[...] = y
    ybf_ref[...] = y.astype(BF16)


def _embed_ln(x, g, b, tm=512):
    s, d = x.shape
    return pl.pallas_call(
        _embed_ln_kernel,
        grid=(s // tm,),
        in_specs=[pl.BlockSpec((tm, d), lambda i: (i, 0)),
                  pl.BlockSpec((1, d), lambda i: (0, 0)),
                  pl.BlockSpec((1, d), lambda i: (0, 0))],
        out_specs=[pl.BlockSpec((tm, d), lambda i: (i, 0)),
                   pl.BlockSpec((tm, d), lambda i: (i, 0))],
        out_shape=[jax.ShapeDtypeStruct((s, d), F32), jax.ShapeDtypeStruct((s, d), BF16)],
        compiler_params=_cparams(("arbitrary",)),
        name="embed_ln",
    )(x, g.reshape(1, d), b.reshape(1, d))


def _rope_tables(seq):
    t = jnp.arange(seq)

    def cs(pos, dim, theta):
        half = dim // 2
        inv = jnp.power(jnp.float32(theta), -jnp.arange(half, dtype=jnp.float32) * 2.0 / dim)
        ang = pos.astype(jnp.float32)[:, None] * inv[None, :]
        return jnp.cos(ang), jnp.sin(ang)

    def group(cos, sin, width):
        half = cos.shape[1]
        pad = width - 2 * half
        one = jnp.ones((seq, pad), F32)
        zero = jnp.zeros((seq, pad), F32)
        zh = jnp.zeros((seq, half), F32)
        return (jnp.concatenate([cos, cos, one], 1),
                jnp.concatenate([zh, sin, zero], 1),
                jnp.concatenate([-sin, zh, zero], 1))

    ca, sa = cs(t, DA_QK // 4, ROPE_THETA)
    va = [jnp.concatenate([p, p], 1) for p in group(ca, sa, DA_QK)]
    cr, sr = cs(t // GRID_W, HEAD_DIM // 2, AXIAL_THETA)
    cc, sc = cs(t % GRID_W, HEAD_DIM // 2, AXIAL_THETA)
    vc = [jnp.concatenate([p, q], 1)
          for p, q in zip(group(cr, sr, HEAD_DIM // 2), group(cc, sc, HEAD_DIM // 2))]
    cd, sd = cs(t, HEAD_DIM // 4, ROPE_THETA)
    vd = list(group(cd, sd, HEAD_DIM))
    return jnp.stack([jnp.stack(va), jnp.stack(vc), jnp.stack(vd)])


ROT_HALF = (DA_QK // 8, HEAD_DIM // 4, HEAD_DIM // 8)


_TILES_ROT_A = (0, 1)
_TILES_PLAIN = (2, 5, 6, 17, 18, 19)
_TILES_SILU = (3, 7, 10, 20)
_TILE_BQ, _TILE_CQ, _TILE_CKV = 4, 8, 9
_TILES_DQ = (11, 12, 13)
_TILES_DK = (14, 15, 16)


def _any_of(j, tiles):
    c = j == tiles[0]
    for t in tiles[1:]:
        c = c | (j == t)
    return c


def _inproj_kernel(tsub, x_ref, w_ref, bg_ref, tab_ref, gn_ref, o_ref, wbf_ref, acc_ref):
    j = pl.program_id(0)
    i = pl.program_id(1)
    n_sub = x_ref.shape[0] // tsub

    @pl.when(i == 0)
    def _():
        wbf_ref[...] = w_ref[...].astype(BF16)

    def matmul(u):
        acc_ref[u % 2] = jnp.dot(x_ref[u * tsub:(u + 1) * tsub, :], wbf_ref[...],
                                 preferred_element_type=F32)

    def pipeline(epilogue):
        matmul(0)
        for u in range(n_sub):
            if u + 1 < n_sub:
                matmul(u + 1)
            rows = slice(u * tsub, (u + 1) * tsub)
            for g in range(TN // LANES):
                lanes = slice(g * LANES, (g + 1) * LANES)
                o_ref[rows, lanes] = epilogue(acc_ref[u % 2, :, lanes], rows, g).astype(BF16)

    def rot(a, rows, half):
        return (a * tab_ref[0, rows, :] + pltpu.roll(a, half, 1) * tab_ref[1, rows, :]
                + pltpu.roll(a, LANES - half, 1) * tab_ref[2, rows, :])

    def rms(a, gain):
        ms = jnp.mean(a * a, axis=-1, keepdims=True)
        return a * lax.rsqrt(ms + RMS_EPS) * gain

    @pl.when(j == 0)
    def _():
        pipeline(lambda a, rows, g: rot(a, rows, ROT_HALF[0]) * (DA_QK ** -0.5 * LOG2E))

    @pl.when(j == 1)
    def _():
        pipeline(lambda a, rows, g: rot(a, rows, ROT_HALF[0]))

    @pl.when(_any_of(j, _TILES_PLAIN))
    def _():
        pipeline(lambda a, rows, g: a)

    @pl.when(_any_of(j, _TILES_SILU))
    def _():
        pipeline(lambda a, rows, g: a * jax.nn.sigmoid(a))

    @pl.when(j == _TILE_BQ)
    def _():
        pipeline(lambda a, rows, g: a * (HEAD_DIM ** -0.5))

    @pl.when(j == _TILE_CQ)
    def _():
        pipeline(lambda a, rows, g: rot(rms(a, gn_ref[0]), rows, ROT_HALF[1]) * (HEAD_DIM ** -0.5 * LOG2E))

    @pl.when(j == _TILE_CKV)
    def _():
        pipeline(lambda a, rows, g: rot(rms(a, gn_ref[1]), rows, ROT_HALF[1]) if g < 2 else a)

    @pl.when(_any_of(j, _TILES_DQ))
    def _():
        pipeline(lambda a, rows, g: rot(a, rows, ROT_HALF[2]) * (HEAD_DIM ** -0.5 * LOG2E))

    @pl.when(_any_of(j, _TILES_DK))
    def _():
        pipeline(lambda a, rows, g: rot(a, rows, ROT_HALF[2]))

    @pl.when(j >= TILE_GATE0)
    def _():
        pipeline(lambda a, rows, g: jax.nn.sigmoid(a + bg_ref[:, g * LANES:(g + 1) * LANES]))


def _tab_index(j, i):
    is_a = j <= 1
    is_c = (j == _TILE_CQ) | (j == _TILE_CKV)
    is_d = (j >= _TILES_DQ[0]) & (j <= _TILES_DK[-1])
    variant = jnp.where(is_a, 0, jnp.where(is_c, 1, 2))
    row = jnp.where(is_a | is_c | is_d, i, 0)
    return variant, 0, row, 0


def _inproj(xbf, w_in, b_gate, layer, tabs, gains, tm=2048, tsub=256):
    s, d = xbf.shape
    tm = math.gcd(tm, s)
    return pl.pallas_call(
        functools.partial(_inproj_kernel, tsub),
        grid=(N_TILES, s // tm),
        in_specs=[pl.BlockSpec((tm, d), lambda j, i: (i, 0)),
                  pl.BlockSpec((None, d, TN), lambda j, i: (layer, 0, j)),
                  pl.BlockSpec((None, 1, TN), lambda j, i: (layer, 0, jnp.maximum(j - TILE_GATE0, 0))),
                  pl.BlockSpec((None, 3, tm, LANES), _tab_index),
                  pl.BlockSpec((2, 1, LANES), lambda j, i: (0, 0, 0))],
        out_specs=pl.BlockSpec((tm, TN), lambda j, i: (i, j)),
        out_shape=jax.ShapeDtypeStruct((s, D_IN), BF16),
        scratch_shapes=[pltpu.VMEM((d, TN), BF16), pltpu.VMEM((2, tsub, TN), F32)],
        compiler_params=_cparams(("arbitrary", "arbitrary")),
        name="inproj",
    )(xbf, w_in, b_gate.reshape(b_gate.shape[0], 1, -1), tabs, gains)


def _pair_attn_kernel(diff, tq, tk, lam_init, *refs):
    if diff:
        q_ref, k_ref, v_ref, z_ref, lp_ref, sg_ref, o_ref, q2_ref, vt_ref, acc_ref, s_ref = refs
    else:
        q_ref, k_ref, v_ref, z_ref, o_ref, q2_ref, vt_ref, acc_ref, s_ref = refs
    seq = k_ref.shape[0]
    nq = 2 * tq

    @pl.when(pl.program_id(1) == 0)
    def _():
        for c in range(seq // tk):
            vt_ref[c, :LANES, :] = v_ref[c * tk:(c + 1) * tk, :].astype(F32).T.astype(BF16)
            vt_ref[c, LANES:, :] = jnp.ones((VT_PAD, tk), BF16)

    if diff:
        q = q_ref[...]
        lane = lax.broadcasted_iota(jnp.int32, q.shape, 1)
        zero = jnp.zeros_like(q)
        q2_ref[:tq, :] = jnp.where(lane < DA_QK, q, zero)
        q2_ref[tq:, :] = jnp.where(lane >= DA_QK, q, zero)
    else:
        q2_ref[:tq, :] = q_ref[:, :LANES]
        q2_ref[tq:, :] = q_ref[:, LANES:]
    acc_ref[...] = jnp.zeros(acc_ref.shape, F32)

    n_chunks = seq // tk

    def scores(c, slot):
        start = pl.multiple_of(c * tk, tk)
        s_ref[slot] = lax.dot_general(k_ref[pl.ds(start, tk), :], q2_ref[...], (((1,), (1,)), ((), ())),
                                      preferred_element_type=F32)

    def fold(c, slot, m_prev):
        s = s_ref[slot]
        m_new = jnp.maximum(m_prev, jnp.max(s, axis=0, keepdims=True))
        alpha = jnp.exp2(m_prev - m_new)
        p = jnp.exp2(s - m_new)
        acc_ref[...] = alpha * acc_ref[...] + jnp.dot(vt_ref[c], p.astype(BF16),
                                                      preferred_element_type=F32)
        return m_new

    unroll = 4 if n_chunks % 4 == 0 else 2

    def group(base, m, last):
        for u in range(unroll):
            if not (last and u == unroll - 1):
                scores(base + u + 1, (u + 1) % 2)
            m = fold(base + u, u % 2, m)
        return m

    scores(0, 0)
    m = jnp.full((1, nq), NEG_INF, F32)
    n_groups = n_chunks // unroll
    if n_groups > 1:
        m = lax.fori_loop(0, n_groups - 1, lambda j, mm: group(j * unroll, mm, False), m)
    group((n_groups - 1) * unroll, m, True)

    acc = acc_ref[...]
    o = (acc[:LANES] / acc[LANES:LANES + 1]).T
    if diff:
        lp = lp_ref[...]
        lam = (jnp.exp(jnp.sum(lp[0:1] * lp[1:2], keepdims=True))
               - jnp.exp(jnp.sum(lp[2:3] * lp[3:4], keepdims=True)) + lam_init)
        dlt = o[:tq] - lam * o[tq:]
        ms = jnp.mean(dlt * dlt, axis=-1, keepdims=True)
        y = dlt * lax.rsqrt(ms + RMS_EPS) * sg_ref[...] * (1.0 - lam_init)
        o_ref[...] = (y * z_ref[...].astype(F32)).astype(BF16)
    else:
        o_ref[:, :LANES] = (o[:tq] * z_ref[:, :LANES].astype(F32)).astype(BF16)
        o_ref[:, LANES:] = (o[tq:] * z_ref[:, LANES:].astype(F32)).astype(BF16)


def _pair_attn(h, diff, lam_init=0.0, lam_params=None, subln_g=None, tq=256, tk=512):
    s = h.shape[0]
    if diff:
        n_kv, qw = 4, LANES
        q_map = lambda hh, i: (i, CB_AQ + hh)
        k_map = lambda hh, i: (0, CB_AK + hh)
        v_map = lambda hh, i: (0, CB_AV + hh)
        z_map = lambda hh, i: (i, CB_AZ + hh)
    else:
        n_kv, qw = 2, 2 * LANES
        q_map = lambda hh, i: (i, CB_CQ // 2 + hh)
        k_map = lambda hh, i: (0, CB_CK + hh)
        v_map = lambda hh, i: (0, CB_CV + hh)
        z_map = lambda hh, i: (i, CB_CZ // 2 + hh)
    in_specs = [pl.BlockSpec((tq, qw), q_map),
                pl.BlockSpec((s, LANES), k_map),
                pl.BlockSpec((s, LANES), v_map),
                pl.BlockSpec((tq, qw), z_map)]
    args = [h, h, h, h]
    if diff:
        in_specs += [pl.BlockSpec((4, DA_QK), lambda hh, i: (0, 0)),
                     pl.BlockSpec((1, LANES), lambda hh, i: (0, 0))]
        args += [lam_params, subln_g.reshape(1, LANES)]
    return pl.pallas_call(
        functools.partial(_pair_attn_kernel, diff, tq, tk, lam_init),
        grid=(n_kv, s // tq),
        in_specs=in_specs,
        out_specs=pl.BlockSpec((tq, qw), lambda hh, i: (i, hh)),
        out_shape=jax.ShapeDtypeStruct((s, BRANCH_W), BF16),
        scratch_shapes=[pltpu.VMEM((2 * tq, LANES), BF16),
                        pltpu.VMEM((s // tk, LANES + VT_PAD, tk), BF16),
                        pltpu.VMEM((LANES + VT_PAD, 2 * tq), F32),
                        pltpu.VMEM((2, tk, 2 * tq), F32)],
        compiler_params=_cparams(("arbitrary", "arbitrary")),
        name="diff_attn" if diff else "axial_gqa",
    )(*args)


def _nat_bias(rpb):
    cols = jnp.arange(GRID_W)
    c0 = jnp.clip(cols - NB_COLS // 2, 0, GRID_W - NB_COLS)
    kc = jnp.arange(GRID_W)
    valid = (kc[None, :] >= c0[:, None]) & (kc[None, :] < c0[:, None] + NB_COLS)
    dc = jnp.clip(kc[None, :] - cols[:, None] + (NB_COLS - 1), 0, 2 * NB_COLS - 2)
    t = jnp.where(valid[None, None], rpb[:, :, dc].astype(F32), NEG_INF)
    out = []
    for ds in range(NB_ROWS):
        w = t[:, ds:ds + NB_ROWS]
        out.append(jnp.transpose(w, (0, 2, 1, 3)).reshape(rpb.shape[0], GRID_W, NB_ROWS * GRID_W))
    return jnp.stack(out)


def _nat_kernel(rows, rps, q_ref, k_ref, v_ref, z_ref, bias_ref, o_ref):
    win = NB_ROWS * GRID_W
    starts, scores = [], []
    for u in range(rps):
        r = pl.program_id(1) * rps + u
        r0 = jnp.clip(r - NB_ROWS // 2, 0, rows - NB_ROWS)
        start = pl.multiple_of(r0 * GRID_W, GRID_W)
        kw = k_ref[pl.ds(start, win), :]
        s = lax.dot_general(q_ref[u * GRID_W:(u + 1) * GRID_W, :], kw, (((1,), (1,)), ((), ())),
                            preferred_element_type=F32)
        scores.append(s + bias_ref[r0 - r + (NB_ROWS - 1)])
        starts.append(start)
    probs = []
    for s in scores:
        p = jnp.exp(s - jnp.max(s, axis=-1, keepdims=True))
        probs.append((p.astype(BF16), jnp.sum(p, axis=-1, keepdims=True)))
    for u, (p, l) in enumerate(probs):
        qrows = slice(u * GRID_W, (u + 1) * GRID_W)
        o = jnp.dot(p, v_ref[pl.ds(starts[u], win), :], preferred_element_type=F32) / l
        o_ref[qrows, :] = (o * z_ref[qrows, :].astype(F32)).astype(BF16)


def _nat_attn(h, bias, rps=8):
    s = h.shape[0]
    rows = s // GRID_W
    blk = rps * GRID_W
    return pl.pallas_call(
        functools.partial(_nat_kernel, rows, rps),
        grid=(4, rows // rps),
        in_specs=[pl.BlockSpec((blk, LANES), lambda hh, r: (r, CB_BQ + hh)),
                  pl.BlockSpec((s, LANES), lambda hh, r: (0, CB_BK + hh)),
                  pl.BlockSpec((s, LANES), lambda hh, r: (0, CB_BV + hh)),
                  pl.BlockSpec((blk, LANES), lambda hh, r: (r, CB_BZ + hh)),
                  pl.BlockSpec((NB_ROWS, None, GRID_W, NB_ROWS * GRID_W), lambda hh, r: (0, hh, 0, 0))],
        out_specs=pl.BlockSpec((blk, LANES), lambda hh, r: (r, hh)),
        out_shape=jax.ShapeDtypeStruct((s, BRANCH_W), BF16),
        compiler_params=_cparams(("arbitrary", "arbitrary")),
        name="nat_attn",
    )(h, h, h, h, bias)


def _dil_masks(tq):
    row = np.arange(tq)[:, None]

    def band(col, rate):
        d = col - row
        return np.where((d >= 0) & (d <= 2 * DIL_SIDE * rate) & (d % rate == 0), 0.0, NEG_INF).astype(np.float32)

    width0 = tq + 2 * DIL_SIDE
    tables = [np.stack([band(np.arange(width0)[None, :] + dlt, 1) for dlt in (DIL_SIDE, 0, -DIL_SIDE)])]
    for rate in DIL_RATES[1:]:
        reach = DIL_SIDE * rate
        cols = np.arange(-reach, tq + 3 * reach)[None, :]
        full = band(cols, rate)
        tables.append(full.reshape(tq, -1, LANES).transpose(1, 0, 2))
    return [jnp.asarray(t) for t in tables]


def _dil_kernel(tq, nsub, q0_ref, q1_ref, q2_ref, k0_ref, k1_ref, k2_ref, v0_ref, v1_ref, v2_ref,
                z_ref, b0_ref, b1_ref, b2_ref, o_ref):
    seq = k0_ref.shape[0]
    q_refs, k_refs, v_refs = (q0_ref, q1_ref, q2_ref), (k0_ref, k1_ref, k2_ref), (v0_ref, v1_ref, v2_ref)
    b_refs = (b0_ref, b1_ref, b2_ref)
    scores, wins = [], []
    for u in range(nsub):
        t0 = (pl.program_id(1) * nsub + u) * tq
        rows = slice(u * tq, (u + 1) * tq)
        su, wu = [], []
        for rate, q_ref, k_ref, b_ref in zip(DIL_RATES, q_refs, k_refs, b_refs):
            reach = DIL_SIDE * rate
            width = tq + 2 * reach
            start = pl.multiple_of(jnp.clip(t0 - reach, 0, seq - width), DIL_SIDE)
            delta = start - t0 + reach
            s = lax.dot_general(q_ref[rows, :], k_ref[pl.ds(start, width), :], (((1,), (1,)), ((), ())),
                                preferred_element_type=F32)
            if rate == 1:
                s = s + b_ref[jnp.where(delta > 0, 0, jnp.where(delta < 0, 2, 1))]
            else:
                tile0 = (delta + reach) // LANES
                s = jnp.concatenate([s[:, j * LANES:(j + 1) * LANES] + b_ref[tile0 + j]
                                     for j in range(width // LANES)], axis=1)
            su.append(s)
            wu.append((start, width))
        scores.append(su)
        wins.append(wu)
    probs, dens = [], []
    for su in scores:
        m = su[0].max(axis=-1, keepdims=True)
        for s in su[1:]:
            m = jnp.maximum(m, s.max(axis=-1, keepdims=True))
        pu = [jnp.exp2(s - m) for s in su]
        l = pu[0].sum(axis=-1, keepdims=True)
        for p in pu[1:]:
            l = l + p.sum(axis=-1, keepdims=True)
        probs.append([p.astype(BF16) for p in pu])
        dens.append(l)
    for u, (pu, wu, l) in enumerate(zip(probs, wins, dens)):
        rows = slice(u * tq, (u + 1) * tq)
        o = None
        for p, (start, width), v_ref in zip(pu, wu, v_refs):
            pv = jnp.dot(p, v_ref[pl.ds(start, width), :], preferred_element_type=F32)
            o = pv if o is None else o + pv
        o_ref[rows, :] = (o / l * z_ref[rows, :].astype(F32)).astype(BF16)


def _dil_attn(h, tq=128, nsub=2):
    s = h.shape[0]
    masks = _dil_masks(tq)
    m_specs = [pl.BlockSpec(m.shape, lambda hh, i: (0, 0, 0)) for m in masks]
    blk = tq * nsub
    qs = [pl.BlockSpec((blk, LANES), functools.partial(lambda g, hh, i: (i, CB_DQ + 4 * g + hh), g))
          for g in range(3)]
    ks = [pl.BlockSpec((s, LANES), functools.partial(lambda g, hh, i: (0, CB_DK + 4 * g + hh), g))
          for g in range(3)]
    vs = [pl.BlockSpec((s, LANES), functools.partial(lambda g, hh, i: (0, CB_DV + 4 * g + hh), g))
          for g in range(3)]
    return pl.pallas_call(
        functools.partial(_dil_kernel, tq, nsub),
        grid=(4, s // blk),
        in_specs=qs + ks + vs + [pl.BlockSpec((blk, LANES), lambda hh, i: (i, CB_DZ + hh))] + m_specs,
        out_specs=pl.BlockSpec((blk, LANES), lambda hh, i: (i, hh)),
        out_shape=jax.ShapeDtypeStruct((s, BRANCH_W), BF16),
        compiler_params=_cparams(("arbitrary", "arbitrary")),
        name="dil_attn",
    )(*([h] * 10), *masks)


def _merge_kernel(ya_ref, yb_ref, yc_ref, yd_ref, ga_ref, gb_ref, gc_ref, gd_ref, wb_ref, o_ref):
    acc = None
    for n, (y_ref, g_ref) in enumerate(zip((ya_ref, yb_ref, yc_ref, yd_ref),
                                           (ga_ref, gb_ref, gc_ref, gd_ref))):
        proj = jnp.dot(y_ref[...], wb_ref[n], preferred_element_type=F32)
        term = proj * g_ref[...].astype(F32)
        acc = term if acc is None else acc + term
    o_ref[...] = acc.astype(BF16)


def _merge(ys, h, wb_bf, tm=1024):
    s = h.shape[0]
    tm = min(tm, s)
    n_col = D_MODEL // TN
    y_specs = [pl.BlockSpec((tm, BRANCH_W), lambda j, i: (i, 0)) for _ in range(N_BRANCH)]
    g_specs = [pl.BlockSpec((tm, TN), functools.partial(lambda n, j, i: (i, TILE_GATE0 + n_col * n + j), n))
               for n in range(N_BRANCH)]
    return pl.pallas_call(
        _merge_kernel,
        grid=(n_col, s // tm),
        in_specs=y_specs + g_specs + [pl.BlockSpec((N_BRANCH, BRANCH_W, TN), lambda j, i: (0, 0, j))],
        out_specs=pl.BlockSpec((tm, TN), lambda j, i: (i, j)),
        out_shape=jax.ShapeDtypeStruct((s, D_MODEL), BF16),
        compiler_params=_cparams(("arbitrary", "arbitrary")),
        name="branch_merge",
    )(*ys, h, h, h, h, wb_bf)


def _out_ln_kernel(alpha, m_ref, w_ref, x_ref, g_ref, b_ref, y_ref, ybf_ref):
    out = jnp.dot(m_ref[...], w_ref[...], preferred_element_type=F32)
    y = _ln_rows(alpha * x_ref[...] + out, g_ref[...], b_ref[...])
    y_ref[...] = y
    ybf_ref[...] = y.astype(BF16)


def _out_ln(merged, wo_bf, x, g, b, alpha, tm=512):
    s, d = x.shape
    return pl.pallas_call(
        functools.partial(_out_ln_kernel, alpha),
        grid=(s // tm,),
        in_specs=[pl.BlockSpec((tm, d), lambda i: (i, 0)),
                  pl.BlockSpec((d, d), lambda i: (0, 0)),
                  pl.BlockSpec((tm, d), lambda i: (i, 0)),
                  pl.BlockSpec((1, d), lambda i: (0, 0)),
                  pl.BlockSpec((1, d), lambda i: (0, 0))],
        out_specs=[pl.BlockSpec((tm, d), lambda i: (i, 0)),
                   pl.BlockSpec((tm, d), lambda i: (i, 0))],
        out_shape=[jax.ShapeDtypeStruct((s, d), F32), jax.ShapeDtypeStruct((s, d), BF16)],
        compiler_params=_cparams(("arbitrary",)),
        name="out_ln",
    )(merged, wo_bf, x, g.reshape(1, d), b.reshape(1, d))


def kernel(x, emb_ln_g, emb_ln_b, w_in, b_gate, diff_lambda, diff_subln_g, nat_rpb,
           gqa_q_norm_g, gqa_k_norm_g, w_branch, w_out, ln_g, ln_b):
    batch, seq, d = x.shape
    assert batch == 1 and d == D_MODEL and w_in.shape[-1] == D_IN
    depth = w_in.shape[0]
    alpha = (2 * depth) ** 0.25
    tabs = _rope_tables(seq)
    xf, xbf = _embed_ln(x[0], emb_ln_g, emb_ln_b)
    for l in range(depth):
        lam_init = 0.8 - 0.6 * math.exp(-0.3 * l)
        gains = jnp.stack([gqa_q_norm_g[l], gqa_k_norm_g[l]]).reshape(2, 1, LANES)
        h = _inproj(xbf, w_in, b_gate, l, tabs, gains)
        ya = _pair_attn(h, True, lam_init, diff_lambda[l], diff_subln_g[l])
        yb = _nat_attn(h, _nat_bias(nat_rpb[l]))
        yc = _pair_attn(h, False)
        yd = _dil_attn(h)
        merged = _merge((ya, yb, yc, yd), h, w_branch[l].astype(BF16))
        xf, xbf = _out_ln(merged, w_out[l].astype(BF16), xf, ln_g[l], ln_b[l], alpha)
    return xf[None]
```

```python
import functools
import math

import numpy as np
import jax
import jax.numpy as jnp
from jax import lax
from jax.experimental import pallas as pl
from jax.experimental.pallas import tpu as pltpu

F32 = jnp.float32
BF16 = jnp.bfloat16

D_MODEL = 2048
GRID_W = 64
HEAD_DIM = 128
N_BRANCH = 4
BRANCH_W = D_MODEL // 4
DA_QK = HEAD_DIM // 2
NB_ROWS = 8
NB_COLS = 16
AXIAL_THETA = 10000.0
DIL_RATES = (1, 4, 16)
DIL_SIDE = 64
ROPE_THETA = 500000.0
LN_EPS = 1e-5
RMS_EPS = 1e-6
NEG_INF = -1e30
LOG2E = math.log2(math.e)

LANES = 128
VT_PAD = 16
TN = 512
D_IN = 18944

ROT_TILES = (0, 1, 8, 9, 11, 12, 13, 14, 15, 16)
PLAIN_TILES = (2, 3, 4, 5, 6, 7, 10, 17, 18, 19, 20)
GATE_TILE0, N_GATE_TILES = 21, 16
ROT_VARIANT = (0, 1, 2, 3, 4, 4, 4, 5, 5, 5)
CB_AQ, CB_AK, CB_CQ, CB_CK, CB_CV, CB_DQ, CB_DK = 0, 4, 8, 12, 14, 16, 28
CB_AV, CB_AZ, CB_BQ, CB_BK, CB_BV, CB_BZ, CB_CZ, CB_DV, CB_DZ = 0, 4, 8, 12, 16, 20, 24, 28, 40
PLAIN_SILU = (1, 5, 6, 10)

VMEM_LIMIT = 56 * 1024 * 1024


def _cparams(sem):
    return pltpu.CompilerParams(dimension_semantics=sem, vmem_limit_bytes=VMEM_LIMIT)


def _select(j, values):
    out = values[-1]
    for t in range(len(values) - 2, -1, -1):
        out = jnp.where(j == t, values[t], out)
    return out


def _any_of(j, tiles):
    c = j == tiles[0]
    for t in tiles[1:]:
        c = c | (j == t)
    return c


def _ln_rows(x, g, b):
    mu = jnp.mean(x, axis=-1, keepdims=True)
    xc = x - mu
    var = jnp.mean(xc * xc, axis=-1, keepdims=True)
    return xc * lax.rsqrt(var + LN_EPS) * g + b


def _embed_ln_kernel(x_ref, g_ref, b_ref, y_ref, ybf_ref):
    y = _ln_rows(x_ref[...], g_ref[...], b_ref[...])
    y_ref[...] = y
    ybf_ref[...] = y.astype(BF16)


def _embed_ln(x, g, b, tm=512):
    s, d = x.shape
    return pl.pallas_call(
        _embed_ln_kernel,
        grid=(s // tm,),
        in_specs=[pl.BlockSpec((tm, d), lambda i: (i, 0)),
                  pl.BlockSpec((1, d), lambda i: (0, 0)),
                  pl.BlockSpec((1, d), lambda i: (0, 0))],
        out_specs=[pl.BlockSpec((tm, d), lambda i: (i, 0)),
                   pl.BlockSpec((tm, d), lambda i: (i, 0))],
        out_shape=[jax.ShapeDtypeStruct((s, d), F32), jax.ShapeDtypeStruct((s, d), BF16)],
        compiler_params=_cparams(("arbitrary",)),
        name="embed_ln",
    )(x, g.reshape(1, d), b.reshape(1, d))


def _rope_tables(seq):
    t = jnp.arange(seq)

    def cs(pos, dim, theta):
        half = dim // 2
        inv = jnp.power(jnp.float32(theta), -jnp.arange(half, dtype=jnp.float32) * 2.0 / dim)
        ang = pos.astype(jnp.float32)[:, None] * inv[None, :]
        return jnp.cos(ang), jnp.sin(ang)

    def group(cos, sin, width):
        half = cos.shape[1]
        pad = width - 2 * half
        one = jnp.ones((seq, pad), F32)
        zero = jnp.zeros((seq, pad), F32)
        zh = jnp.zeros((seq, half), F32)
        return (jnp.concatenate([cos, cos, one], 1),
                jnp.concatenate([zh, sin, zero], 1),
                jnp.concatenate([-sin, zh, zero], 1))

    ca, sa = cs(t, DA_QK // 4, ROPE_THETA)
    va = jnp.stack([jnp.concatenate([p, p], 1) for p in group(ca, sa, DA_QK)])
    cr, sr = cs(t // GRID_W, HEAD_DIM // 2, AXIAL_THETA)
    cc, sc = cs(t % GRID_W, HEAD_DIM // 2, AXIAL_THETA)
    vc = jnp.stack([jnp.concatenate([p, q], 1)
                    for p, q in zip(group(cr, sr, HEAD_DIM // 2), group(cc, sc, HEAD_DIM // 2))])
    cd, sd = cs(t, HEAD_DIM // 4, ROPE_THETA)
    vd = jnp.stack(group(cd, sd, HEAD_DIM))
    qa = DA_QK ** -0.5 * LOG2E
    qh = HEAD_DIM ** -0.5 * LOG2E
    return jnp.stack([va * qa, va, vc * qh, vc, vd * qh, vd])


ROT_HALF_A, ROT_HALF_C, ROT_HALF_D = DA_QK // 8, HEAD_DIM // 4, HEAD_DIM // 8


def _proj_kernel(tsub, make_branches, n_extra, x_ref, w_ref, *rest):
    extra = rest[:n_extra]
    o_ref, wbf_ref, acc_ref = rest[n_extra:]
    j = pl.program_id(0)
    n_sub = x_ref.shape[0] // tsub

    @pl.when(pl.program_id(1) == 0)
    def _():
        wbf_ref[...] = w_ref[...].astype(BF16)

    def matmul(u):
        acc_ref[u % 2] = jnp.dot(x_ref[u * tsub:(u + 1) * tsub, :], wbf_ref[...],
                                 preferred_element_type=F32)

    def pipeline(epilogue):
        matmul(0)
        for u in range(n_sub):
            if u + 1 < n_sub:
                matmul(u + 1)
            rows = slice(u * tsub, (u + 1) * tsub)
            for g in range(TN // LANES):
                lanes = slice(g * LANES, (g + 1) * LANES)
                o_ref[rows, lanes] = epilogue(acc_ref[u % 2, :, lanes], rows, g).astype(BF16)

    for cond, epilogue in make_branches(*extra):
        if cond is None:
            pipeline(epilogue)
        else:
            pl.when(cond(j))(functools.partial(pipeline, epilogue))


def _proj(name, xbf, w_in, layer, src_tiles, make_branches, extra, extra_specs, tm=1024, tsub=256):
    s, d = xbf.shape
    tm = math.gcd(tm, s)
    n_tiles = len(src_tiles)
    contiguous = src_tiles == tuple(range(src_tiles[0], src_tiles[0] + n_tiles))
    src = (lambda j: src_tiles[0] + j) if contiguous else (lambda j: _select(j, src_tiles))
    return pl.pallas_call(
        functools.partial(_proj_kernel, tsub, make_branches, len(extra)),
        grid=(n_tiles, s // tm),
        in_specs=[pl.BlockSpec((tm, d), lambda j, i: (i, 0)),
                  pl.BlockSpec((None, d, TN), lambda j, i: (layer, 0, src(j)))] + extra_specs(tm),
        out_specs=pl.BlockSpec((tm, TN), lambda j, i: (i, j)),
        out_shape=jax.ShapeDtypeStruct((s, n_tiles * TN), BF16),
        scratch_shapes=[pltpu.VMEM((d, TN), BF16), pltpu.VMEM((2, tsub, TN), F32)],
        compiler_params=_cparams(("arbitrary", "arbitrary")),
        name=name,
    )(xbf, w_in, *extra)


def _rot_branches(tab_ref, gn_ref):
    def rot(a, rows, half):
        return (a * tab_ref[0, rows, :] + pltpu.roll(a, half, 1) * tab_ref[1, rows, :]
                + pltpu.roll(a, LANES - half, 1) * tab_ref[2, rows, :])

    def rms(a, gain):
        ms = jnp.mean(a * a, axis=-1, keepdims=True)
        return a * lax.rsqrt(ms + RMS_EPS) * gain

    return [(lambda j: j <= 1, lambda a, rows, g: rot(a, rows, ROT_HALF_A)),
            (lambda j: j == 2, lambda a, rows, g: rot(rms(a, gn_ref[0]), rows, ROT_HALF_C)),
            (lambda j: j == 3, lambda a, rows, g: rot(rms(a, gn_ref[1]), rows, ROT_HALF_C) if g < 2 else a),
            (lambda j: j >= 4, lambda a, rows, g: rot(a, rows, ROT_HALF_D))]


def _plain_branches():
    return [(lambda j: _any_of(j, PLAIN_SILU), lambda a, rows, g: a * jax.nn.sigmoid(a)),
            (lambda j: jnp.logical_not(_any_of(j, PLAIN_SILU)), lambda a, rows, g: a)]


def _gate_branches(bg_ref):
    return [(None, lambda a, rows, g: jax.nn.sigmoid(a + bg_ref[:, g * LANES:(g + 1) * LANES]))]


def _inproj(xbf, w_in, b_gate, layer, tabs, gains):
    hr = _proj("inproj_rot", xbf, w_in, layer, ROT_TILES, _rot_branches, (tabs, gains),
               lambda tm: [pl.BlockSpec((None, 3, tm, LANES), lambda j, i: (_select(j, ROT_VARIANT), 0, i, 0)),
                           pl.BlockSpec((2, 1, LANES), lambda j, i: (0, 0, 0))])
    hp = _proj("inproj_plain", xbf, w_in, layer, PLAIN_TILES, _plain_branches, (), lambda tm: [])
    hg = _proj("inproj_gate", xbf, w_in, layer, tuple(range(GATE_TILE0, GATE_TILE0 + N_GATE_TILES)),
               _gate_branches, (b_gate.reshape(b_gate.shape[0], 1, -1),),
               lambda tm: [pl.BlockSpec((None, 1, TN), lambda j, i: (layer, 0, j))])
    return hr, hp, hg


def _pair_attn_kernel(diff, tq, tk, lam_init, *refs):
    if diff:
        q_ref, k_ref, v_ref, z_ref, lp_ref, sg_ref, o_ref, q2_ref, vt_ref, acc_ref, s_ref = refs
    else:
        q_ref, k_ref, v_ref, z_ref, o_ref, q2_ref, vt_ref, acc_ref, s_ref = refs
    seq = k_ref.shape[0]
    nq = 2 * tq

    @pl.when(pl.program_id(1) == 0)
    def _():
        for c in range(seq // tk):
            vt_ref[c, :LANES, :] = v_ref[c * tk:(c + 1) * tk, :].astype(F32).T.astype(BF16)
            vt_ref[c, LANES:, :] = jnp.ones((VT_PAD, tk), BF16)

    if diff:
        q = q_ref[...]
        lane = lax.broadcasted_iota(jnp.int32, q.shape, 1)
        zero = jnp.zeros_like(q)
        q2_ref[:tq, :] = jnp.where(lane < DA_QK, q, zero)
        q2_ref[tq:, :] = jnp.where(lane >= DA_QK, q, zero)
    else:
        q2_ref[:tq, :] = q_ref[:, :LANES]
        q2_ref[tq:, :] = q_ref[:, LANES:]
    acc_ref[...] = jnp.zeros(acc_ref.shape, F32)

    n_chunks = seq // tk

    def scores(c, slot):
        start = pl.multiple_of(c * tk, tk)
        s_ref[slot] = lax.dot_general(k_ref[pl.ds(start, tk), :], q2_ref[...], (((1,), (1,)), ((), ())),
                                      preferred_element_type=F32)

    def fold(c, slot, m_prev):
        s = s_ref[slot]
        m_new = jnp.maximum(m_prev, jnp.max(s, axis=0, keepdims=True))
        alpha = jnp.exp2(m_prev - m_new)
        p = jnp.exp2(s - m_new)
        acc_ref[...] = alpha * acc_ref[...] + jnp.dot(vt_ref[c], p.astype(BF16),
                                                      preferred_element_type=F32)
        return m_new

    unroll = 4 if n_chunks % 4 == 0 else 2

    def group(base, m, last):
        for u in range(unroll):
            if not (last and u == unroll - 1):
                scores(base + u + 1, (u + 1) % 2)
            m = fold(base + u, u % 2, m)
        return m

    scores(0, 0)
    m = jnp.full((1, nq), NEG_INF, F32)
    n_groups = n_chunks // unroll
    if n_groups > 1:
        m = lax.fori_loop(0, n_groups - 1, lambda j, mm: group(j * unroll, mm, False), m)
    group((n_groups - 1) * unroll, m, True)

    acc = acc_ref[...]
    o = (acc[:LANES] / acc[LANES:LANES + 1]).T
    if diff:
        lp = lp_ref[...]
        lam = (jnp.exp(jnp.sum(lp[0:1] * lp[1:2], keepdims=True))
               - jnp.exp(jnp.sum(lp[2:3] * lp[3:4], keepdims=True)) + lam_init)
        dlt = o[:tq] - lam * o[tq:]
        ms = jnp.mean(dlt * dlt, axis=-1, keepdims=True)
        y = dlt * lax.rsqrt(ms + RMS_EPS) * sg_ref[...] * (1.0 - lam_init)
        o_ref[...] = (y * z_ref[...].astype(F32)).astype(BF16)
    else:
        o_ref[:, :LANES] = (o[:tq] * z_ref[:, :LANES].astype(F32)).astype(BF16)
        o_ref[:, LANES:] = (o[tq:] * z_ref[:, LANES:].astype(F32)).astype(BF16)


def _pair_attn(hr, hp, diff, lam_init=0.0, lam_params=None, subln_g=None, tq=256, tk=512):
    s = hr.shape[0]
    if diff:
        n_kv, qw = 4, LANES
        q_map = lambda hh, i: (i, CB_AQ + hh)
        k_map = lambda hh, i: (0, CB_AK + hh)
        v_map = lambda hh, i: (0, CB_AV + hh)
        z_map = lambda hh, i: (i, CB_AZ + hh)
    else:
        n_kv, qw = 2, 2 * LANES
        q_map = lambda hh, i: (i, CB_CQ // 2 + hh)
        k_map = lambda hh, i: (0, CB_CK + hh)
        v_map = lambda hh, i: (0, CB_CV + hh)
        z_map = lambda hh, i: (i, CB_CZ // 2 + hh)
    in_specs = [pl.BlockSpec((tq, qw), q_map),
                pl.BlockSpec((s, LANES), k_map),
                pl.BlockSpec((s, LANES), v_map),
                pl.BlockSpec((tq, qw), z_map)]
    args = [hr, hr, hp if diff else hr, hp]
    if diff:
        in_specs += [pl.BlockSpec((4, DA_QK), lambda hh, i: (0, 0)),
                     pl.BlockSpec((1, LANES), lambda hh, i: (0, 0))]
        args += [lam_params, subln_g.reshape(1, LANES)]
    return pl.pallas_call(
        functools.partial(_pair_attn_kernel, diff, tq, tk, lam_init),
        grid=(n_kv, s // tq),
        in_specs=in_specs,
        out_specs=pl.BlockSpec((tq, qw), lambda hh, i: (i, hh)),
        out_shape=jax.ShapeDtypeStruct((s, BRANCH_W), BF16),
        scratch_shapes=[pltpu.VMEM((2 * tq, LANES), BF16),
                        pltpu.VMEM((s // tk, LANES + VT_PAD, tk), BF16),
                        pltpu.VMEM((LANES + VT_PAD, 2 * tq), F32),
                        pltpu.VMEM((2, tk, 2 * tq), F32)],
        compiler_params=_cparams(("arbitrary", "arbitrary")),
        name="diff_attn" if diff else "axial_gqa",
    )(*args)


def _nat_bias(rpb):
    n_layers, n_heads = rpb.shape[:2]
    cols = jnp.arange(GRID_W)
    c0 = jnp.clip(cols - NB_COLS // 2, 0, GRID_W - NB_COLS)
    kc = jnp.arange(GRID_W)
    valid = (kc[None, :] >= c0[:, None]) & (kc[None, :] < c0[:, None] + NB_COLS)
    dc = jnp.clip(kc[None, :] - cols[:, None] + (NB_COLS - 1), 0, 2 * NB_COLS - 2)
    t = jnp.where(valid, rpb[:, :, :, dc].astype(F32), NEG_INF)
    out = []
    for ds in range(NB_ROWS):
        w = t[:, :, ds:ds + NB_ROWS]
        out.append(jnp.transpose(w, (0, 1, 3, 2, 4)).reshape(n_layers, n_heads, GRID_W, NB_ROWS * GRID_W))
    return jnp.stack(out, axis=1)


def _nat_kernel(rows, rps, q_ref, k_ref, v_ref, z_ref, bias_ref, o_ref):
    win = NB_ROWS * GRID_W
    starts, scores = [], []
    for u in range(rps):
        r = pl.program_id(1) * rps + u
        r0 = jnp.clip(r - NB_ROWS // 2, 0, rows - NB_ROWS)
        start = pl.multiple_of(r0 * GRID_W, GRID_W)
        kw = k_ref[pl.ds(start, win), :]
        s = lax.dot_general(q_ref[u * GRID_W:(u + 1) * GRID_W, :], kw, (((1,), (1,)), ((), ())),
                            preferred_element_type=F32)
        scores.append(s * (HEAD_DIM ** -0.5) + bias_ref[r0 - r + (NB_ROWS - 1)])
        starts.append(start)
    probs = []
    for s in scores:
        p = jnp.exp(s - jnp.max(s, axis=-1, keepdims=True))
        probs.append((p.astype(BF16), jnp.sum(p, axis=-1, keepdims=True)))
    for u, (p, l) in enumerate(probs):
        qrows = slice(u * GRID_W, (u + 1) * GRID_W)
        o = jnp.dot(p, v_ref[pl.ds(starts[u], win), :], preferred_element_type=F32) / l
        o_ref[qrows, :] = (o * z_ref[qrows, :].astype(F32)).astype(BF16)


def _nat_attn(hp, bias, layer, rps=8):
    s = hp.shape[0]
    rows = s // GRID_W
    blk = rps * GRID_W
    return pl.pallas_call(
        functools.partial(_nat_kernel, rows, rps),
        grid=(4, rows // rps),
        in_specs=[pl.BlockSpec((blk, LANES), lambda hh, r: (r, CB_BQ + hh)),
                  pl.BlockSpec((s, LANES), lambda hh, r: (0, CB_BK + hh)),
                  pl.BlockSpec((s, LANES), lambda hh, r: (0, CB_BV + hh)),
                  pl.BlockSpec((blk, LANES), lambda hh, r: (r, CB_BZ + hh)),
                  pl.BlockSpec((None, NB_ROWS, None, GRID_W, NB_ROWS * GRID_W),
                               lambda hh, r: (layer, 0, hh, 0, 0))],
        out_specs=pl.BlockSpec((blk, LANES), lambda hh, r: (r, hh)),
        out_shape=jax.ShapeDtypeStruct((s, BRANCH_W), BF16),
        compiler_params=_cparams(("arbitrary", "arbitrary")),
        name="nat_attn",
    )(hp, hp, hp, hp, bias)


def _dil_masks(tq):
    row = np.arange(tq)[:, None]

    def band(col, rate):
        d = col - row
        return np.where((d >= 0) & (d <= 2 * DIL_SIDE * rate) & (d % rate == 0), 0.0, NEG_INF).astype(np.float32)

    width0 = tq + 2 * DIL_SIDE
    tables = [np.stack([band(np.arange(width0)[None, :] + dlt, 1) for dlt in (DIL_SIDE, 0, -DIL_SIDE)])]
    for rate in DIL_RATES[1:]:
        reach = DIL_SIDE * rate
        cols = np.arange(-reach, tq + 3 * reach)[None, :]
        full = band(cols, rate)
        tables.append(full.reshape(tq, -1, LANES).transpose(1, 0, 2))
    return [jnp.asarray(t) for t in tables]


def _dil_kernel(tq, nsub, q0_ref, q1_ref, q2_ref, k0_ref, k1_ref, k2_ref, v0_ref, v1_ref, v2_ref,
                z_ref, b0_ref, b1_ref, b2_ref, o_ref):
    seq = k0_ref.shape[0]
    q_refs, k_refs, v_refs = (q0_ref, q1_ref, q2_ref), (k0_ref, k1_ref, k2_ref), (v0_ref, v1_ref, v2_ref)
    b_refs = (b0_ref, b1_ref, b2_ref)
    scores, wins = [], []
    for u in range(nsub):
        t0 = (pl.program_id(1) * nsub + u) * tq
        rows = slice(u * tq, (u + 1) * tq)
        su, wu = [], []
        for rate, q_ref, k_ref, b_ref in zip(DIL_RATES, q_refs, k_refs, b_refs):
            reach = DIL_SIDE * rate
            width = tq + 2 * reach
            start = pl.multiple_of(jnp.clip(t0 - reach, 0, seq - width), DIL_SIDE)
            delta = start - t0 + reach
            s = lax.dot_general(q_ref[rows, :], k_ref[pl.ds(start, width), :], (((1,), (1,)), ((), ())),
                                preferred_element_type=F32)
            if rate == 1:
                s = s + b_ref[jnp.where(delta > 0, 0, jnp.where(delta < 0, 2, 1))]
            else:
                tile0 = (delta + reach) // LANES
                s = jnp.concatenate([s[:, j * LANES:(j + 1) * LANES] + b_ref[tile0 + j]
                                     for j in range(width // LANES)], axis=1)
            su.append(s)
            wu.append((start, width))
        scores.append(su)
        wins.append(wu)
    probs, dens = [], []
    for su in scores:
        m = su[0].max(axis=-1, keepdims=True)
        for s in su[1:]:
            m = jnp.maximum(m, s.max(axis=-1, keepdims=True))
        pu = [jnp.exp2(s - m) for s in su]
        l = pu[0].sum(axis=-1, keepdims=True)
        for p in pu[1:]:
            l = l + p.sum(axis=-1, keepdims=True)
        probs.append([p.astype(BF16) for p in pu])
        dens.append(l)
    for u, (pu, wu, l) in enumerate(zip(probs, wins, dens)):
        rows = slice(u * tq, (u + 1) * tq)
        o = None
        for p, (start, width), v_ref in zip(pu, wu, v_refs):
            pv = jnp.dot(p, v_ref[pl.ds(start, width), :], preferred_element_type=F32)
            o = pv if o is None else o + pv
        o_ref[rows, :] = (o / l * z_ref[rows, :].astype(F32)).astype(BF16)


def _dil_attn(hr, hp, tq=128, nsub=2):
    s = hr.shape[0]
    masks = _dil_masks(tq)
    m_specs = [pl.BlockSpec(m.shape, lambda hh, i: (0, 0, 0)) for m in masks]
    blk = tq * nsub
    qs = [pl.BlockSpec((blk, LANES), functools.partial(lambda g, hh, i: (i, CB_DQ + 4 * g + hh), g))
          for g in range(3)]
    ks = [pl.BlockSpec((s, LANES), functools.partial(lambda g, hh, i: (0, CB_DK + 4 * g + hh), g))
          for g in range(3)]
    vs = [pl.BlockSpec((s, LANES), functools.partial(lambda g, hh, i: (0, CB_DV + 4 * g + hh), g))
          for g in range(3)]
    return pl.pallas_call(
        functools.partial(_dil_kernel, tq, nsub),
        grid=(4, s // blk),
        in_specs=qs + ks + vs + [pl.BlockSpec((blk, LANES), lambda hh, i: (i, CB_DZ + hh))] + m_specs,
        out_specs=pl.BlockSpec((blk, LANES), lambda hh, i: (i, hh)),
        out_shape=jax.ShapeDtypeStruct((s, BRANCH_W), BF16),
        compiler_params=_cparams(("arbitrary", "arbitrary")),
        name="dil_attn",
    )(*([hr] * 6), *([hp] * 4), *masks)


def _merge_kernel(ya_ref, yb_ref, yc_ref, yd_ref, ga_ref, gb_ref, gc_ref, gd_ref, wb_ref, o_ref):
    acc = None
    for n, (y_ref, g_ref) in enumerate(zip((ya_ref, yb_ref, yc_ref, yd_ref),
                                           (ga_ref, gb_ref, gc_ref, gd_ref))):
        proj = jnp.dot(y_ref[...], wb_ref[n], preferred_element_type=F32)
        term = proj * g_ref[...].astype(F32)
        acc = term if acc is None else acc + term
    o_ref[...] = acc.astype(BF16)


def _merge(ys, hg, wb_bf, tm=1024):
    s = hg.shape[0]
    tm = math.gcd(tm, s)
    n_col = D_MODEL // TN
    y_specs = [pl.BlockSpec((tm, BRANCH_W), lambda j, i: (i, 0)) for _ in range(N_BRANCH)]
    g_specs = [pl.BlockSpec((tm, TN), functools.partial(lambda n, j, i: (i, n_col * n + j), n))
               for n in range(N_BRANCH)]
    return pl.pallas_call(
        _merge_kernel,
        grid=(n_col, s // tm),
        in_specs=y_specs + g_specs + [pl.BlockSpec((N_BRANCH, BRANCH_W, TN), lambda j, i: (0, 0, j))],
        out_specs=pl.BlockSpec((tm, TN), lambda j, i: (i, j)),
        out_shape=jax.ShapeDtypeStruct((s, D_MODEL), BF16),
        compiler_params=_cparams(("arbitrary", "arbitrary")),
        name="branch_merge",
    )(*ys, hg, hg, hg, hg, wb_bf)


def _out_ln_kernel(alpha, m_ref, w_ref, x_ref, g_ref, b_ref, y_ref, ybf_ref):
    out = jnp.dot(m_ref[...], w_ref[...], preferred_element_type=F32)
    y = _ln_rows(alpha * x_ref[...] + out, g_ref[...], b_ref[...])
    y_ref[...] = y
    ybf_ref[...] = y.astype(BF16)


def _out_ln(merged, wo_bf, x, g, b, alpha, tm=512):
    s, d = x.shape
    return pl.pallas_call(
        functools.partial(_out_ln_kernel, alpha),
        grid=(s // tm,),
        in_specs=[pl.BlockSpec((tm, d), lambda i: (i, 0)),
                  pl.BlockSpec((d, d), lambda i: (0, 0)),
                  pl.BlockSpec((tm, d), lambda i: (i, 0)),
                  pl.BlockSpec((1, d), lambda i: (0, 0)),
                  pl.BlockSpec((1, d), lambda i: (0, 0))],
        out_specs=[pl.BlockSpec((tm, d), lambda i: (i, 0)),
                   pl.BlockSpec((tm, d), lambda i: (i, 0))],
        out_shape=[jax.ShapeDtypeStruct((s, d), F32), jax.ShapeDtypeStruct((s, d), BF16)],
        compiler_params=_cparams(("arbitrary",)),
        name="out_ln",
    )(merged, wo_bf, x, g.reshape(1, d), b.reshape(1, d))


def kernel(x, emb_ln_g, emb_ln_b, w_in, b_gate, diff_lambda, diff_subln_g, nat_rpb,
           gqa_q_norm_g, gqa_k_norm_g, w_branch, w_out, ln_g, ln_b):
    batch, seq, d = x.shape
    assert batch == 1 and d == D_MODEL and w_in.shape[-1] == D_IN
    depth = w_in.shape[0]
    alpha = (2 * depth) ** 0.25
    tabs = _rope_tables(seq)
    nat_bias = _nat_bias(nat_rpb)
    wb_bf = w_branch.astype(BF16)
    wo_bf = w_out.astype(BF16)
    xf, xbf = _embed_ln(x[0], emb_ln_g, emb_ln_b)
    for l in range(depth):
        lam_init = 0.8 - 0.6 * math.exp(-0.3 * l)
        gains = jnp.stack([gqa_q_norm_g[l], gqa_k_norm_g[l]]).reshape(2, 1, LANES)
        hr, hp, hg = _inproj(xbf, w_in, b_gate, l, tabs, gains)
        ya = _pair_attn(hr, hp, True, lam_init, diff_lambda[l], diff_subln_g[l])
        yb = _nat_attn(hp, nat_bias, l)
        yc = _pair_attn(hr, hp, False)
        yd = _dil_attn(hr, hp)
        merged = _merge((ya, yb, yc, yd), hg, wb_bf[l])
        xf, xbf = _out_ln(merged, wo_bf[l], xf, ln_g[l], ln_b[l], alpha)
    return xf[None]
```

```python
import functools
import math

import numpy as np
import jax
import jax.numpy as jnp
from jax import lax
from jax.experimental import pallas as pl
from jax.experimental.pallas import tpu as pltpu

F32 = jnp.float32
BF16 = jnp.bfloat16

D_MODEL = 2048
GRID_W = 64
HEAD_DIM = 128
N_BRANCH = 4
BRANCH_W = D_MODEL // 4
DA_QK = HEAD_DIM // 2
NB_ROWS = 8
NB_COLS = 16
AXIAL_THETA = 10000.0
DIL_RATES = (1, 4, 16)
DIL_SIDE = 64
ROPE_THETA = 500000.0
LN_EPS = 1e-5
RMS_EPS = 1e-6
NEG_INF = -1e30
LOG2E = math.log2(math.e)

LANES = 128
VT_PAD = 16
TN = 512
D_IN = 18944

ROT_TILES = (0, 1, 8, 9, 11, 12, 13, 14, 15, 16)
PLAIN_TILES = (2, 3, 4, 5, 6, 7, 10, 17, 18, 19, 20)
GATE_TILE0, N_GATE_TILES = 21, 16
ROT_VARIANT = (0, 1, 2, 3, 4, 4, 4, 5, 5, 5)
CB_AQ, CB_AK, CB_CQ, CB_CK, CB_CV, CB_DQ, CB_DK = 0, 4, 8, 12, 14, 16, 28
CB_AV, CB_AZ, CB_BQ, CB_BK, CB_BV, CB_BZ, CB_CZ, CB_DV, CB_DZ = 0, 4, 8, 12, 16, 20, 24, 28, 40
PLAIN_SILU = (1, 5, 6, 10)

VMEM_LIMIT = 56 * 1024 * 1024


def _cparams(sem):
    return pltpu.CompilerParams(dimension_semantics=sem, vmem_limit_bytes=VMEM_LIMIT)


def _select(j, values):
    out = values[-1]
    for t in range(len(values) - 2, -1, -1):
        out = jnp.where(j == t, values[t], out)
    return out


def _any_of(j, tiles):
    c = j == tiles[0]
    for t in tiles[1:]:
        c = c | (j == t)
    return c


def _ln_rows(x, g, b):
    mu = jnp.mean(x, axis=-1, keepdims=True)
    xc = x - mu
    var = jnp.mean(xc * xc, axis=-1, keepdims=True)
    return xc * lax.rsqrt(var + LN_EPS) * g + b


def _embed_ln_kernel(x_ref, g_ref, b_ref, y_ref, ybf_ref):
    y = _ln_rows(x_ref[...], g_ref[...], b_ref[...])
    y_ref[...] = y
    ybf_ref[...] = y.astype(BF16)


def _embed_ln(x, g, b, tm=512):
    s, d = x.shape
    return pl.pallas_call(
        _embed_ln_kernel,
        grid=(s // tm,),
        in_specs=[pl.BlockSpec((tm, d), lambda i: (i, 0)),
                  pl.BlockSpec((1, d), lambda i: (0, 0)),
                  pl.BlockSpec((1, d), lambda i: (0, 0))],
        out_specs=[pl.BlockSpec((tm, d), lambda i: (i, 0)),
                   pl.BlockSpec((tm, d), lambda i: (i, 0))],
        out_shape=[jax.ShapeDtypeStruct((s, d), F32), jax.ShapeDtypeStruct((s, d), BF16)],
        compiler_params=_cparams(("arbitrary",)),
        name="embed_ln",
    )(x, g.reshape(1, d), b.reshape(1, d))


def _rope_tables(seq):
    t = jnp.arange(seq)

    def cs(pos, dim, theta):
        half = dim // 2
        inv = jnp.power(jnp.float32(theta), -jnp.arange(half, dtype=jnp.float32) * 2.0 / dim)
        ang = pos.astype(jnp.float32)[:, None] * inv[None, :]
        return jnp.cos(ang), jnp.sin(ang)

    def group(cos, sin, width):
        half = cos.shape[1]
        pad = width - 2 * half
        one = jnp.ones((seq, pad), F32)
        zero = jnp.zeros((seq, pad), F32)
        zh = jnp.zeros((seq, half), F32)
        return (jnp.concatenate([cos, cos, one], 1),
                jnp.concatenate([zh, sin, zero], 1),
                jnp.concatenate([-sin, zh, zero], 1))

    ca, sa = cs(t, DA_QK // 4, ROPE_THETA)
    va = jnp.stack([jnp.concatenate([p, p], 1) for p in group(ca, sa, DA_QK)])
    cr, sr = cs(t // GRID_W, HEAD_DIM // 2, AXIAL_THETA)
    cc, sc = cs(t % GRID_W, HEAD_DIM // 2, AXIAL_THETA)
    vc = jnp.stack([jnp.concatenate([p, q], 1)
                    for p, q in zip(group(cr, sr, HEAD_DIM // 2), group(cc, sc, HEAD_DIM // 2))])
    cd, sd = cs(t, HEAD_DIM // 4, ROPE_THETA)
    vd = jnp.stack(group(cd, sd, HEAD_DIM))
    qa = DA_QK ** -0.5 * LOG2E
    qh = HEAD_DIM ** -0.5 * LOG2E
    return jnp.stack([va * qa, va, vc * qh, vc, vd * qh, vd])


ROT_HALF_A, ROT_HALF_C, ROT_HALF_D = DA_QK // 8, HEAD_DIM // 4, HEAD_DIM // 8


def _proj_kernel(tsub, n_sub, make_branches, n_extra, x_ref, w_ref, *rest):
    extra = rest[:n_extra]
    o_ref, wbf_ref, acc_ref = rest[n_extra:]
    j = pl.program_id(0)
    n_pass = x_ref.shape[0] // (n_sub * tsub)

    @pl.when(pl.program_id(1) == 0)
    def _():
        wbf_ref[...] = w_ref[...].astype(BF16)

    def pipeline(epilogue):
        def one_pass(h, carry):
            def rows_of(u):
                return pl.ds(pl.multiple_of((h * n_sub + u) * tsub, tsub), tsub)

            def matmul(u):
                acc_ref[u % 2] = jnp.dot(x_ref[rows_of(u), :], wbf_ref[...],
                                         preferred_element_type=F32)

            matmul(0)
            for u in range(n_sub):
                if u + 1 < n_sub:
                    matmul(u + 1)
                for g in range(TN // LANES):
                    lanes = slice(g * LANES, (g + 1) * LANES)
                    o_ref[rows_of(u), lanes] = epilogue(acc_ref[u % 2, :, lanes], rows_of(u), g).astype(BF16)
            return carry

        lax.fori_loop(0, n_pass, one_pass, 0)

    for cond, epilogue in make_branches(*extra):
        if cond is None:
            pipeline(epilogue)
        else:
            pl.when(cond(j))(functools.partial(pipeline, epilogue))


def _proj(name, xbf, w_in, layer, src_tiles, make_branches, extra, extra_specs, tm=2048, tsub=256, n_sub=4):
    s, d = xbf.shape
    tm = math.gcd(tm, s)
    assert tm % (tsub * n_sub) == 0
    n_tiles = len(src_tiles)
    contiguous = src_tiles == tuple(range(src_tiles[0], src_tiles[0] + n_tiles))
    src = (lambda j: src_tiles[0] + j) if contiguous else (lambda j: _select(j, src_tiles))
    return pl.pallas_call(
        functools.partial(_proj_kernel, tsub, n_sub, make_branches, len(extra)),
        grid=(n_tiles, s // tm),
        in_specs=[pl.BlockSpec((tm, d), lambda j, i: (i, 0)),
                  pl.BlockSpec((None, d, TN), lambda j, i: (layer, 0, src(j)))] + extra_specs(tm),
        out_specs=pl.BlockSpec((tm, TN), lambda j, i: (i, j)),
        out_shape=jax.ShapeDtypeStruct((s, n_tiles * TN), BF16),
        scratch_shapes=[pltpu.VMEM((d, TN), BF16), pltpu.VMEM((2, tsub, TN), F32)],
        compiler_params=_cparams(("arbitrary", "arbitrary")),
        name=name,
    )(xbf, w_in, *extra)


def _rot_branches(tab_ref, gn_ref):
    def rot(a, rows, half):
        return (a * tab_ref[0, rows, :] + pltpu.roll(a, half, 1) * tab_ref[1, rows, :]
                + pltpu.roll(a, LANES - half, 1) * tab_ref[2, rows, :])

    def rms(a, gain):
        ms = jnp.mean(a * a, axis=-1, keepdims=True)
        return a * lax.rsqrt(ms + RMS_EPS) * gain

    return [(lambda j: j <= 1, lambda a, rows, g: rot(a, rows, ROT_HALF_A)),
            (lambda j: j == 2, lambda a, rows, g: rot(rms(a, gn_ref[0]), rows, ROT_HALF_C)),
            (lambda j: j == 3, lambda a, rows, g: rot(rms(a, gn_ref[1]), rows, ROT_HALF_C) if g < 2 else a),
            (lambda j: j >= 4, lambda a, rows, g: rot(a, rows, ROT_HALF_D))]


def _plain_branches():
    return [(lambda j: _any_of(j, PLAIN_SILU), lambda a, rows, g: a * jax.nn.sigmoid(a)),
            (lambda j: jnp.logical_not(_any_of(j, PLAIN_SILU)), lambda a, rows, g: a)]


def _gate_branches(bg_ref):
    return [(None, lambda a, rows, g: jax.nn.sigmoid(a + bg_ref[:, g * LANES:(g + 1) * LANES]))]


def _inproj(xbf, w_in, b_gate, layer, tabs, gains):
    hr = _proj("inproj_rot", xbf, w_in, layer, ROT_TILES, _rot_branches, (tabs, gains),
               lambda tm: [pl.BlockSpec((None, 3, tm, LANES), lambda j, i: (_select(j, ROT_VARIANT), 0, i, 0)),
                           pl.BlockSpec((2, 1, LANES), lambda j, i: (0, 0, 0))])
    hp = _proj("inproj_plain", xbf, w_in, layer, PLAIN_TILES, _plain_branches, (), lambda tm: [])
    hg = _proj("inproj_gate", xbf, w_in, layer, tuple(range(GATE_TILE0, GATE_TILE0 + N_GATE_TILES)),
               _gate_branches, (b_gate.reshape(b_gate.shape[0], 1, -1),),
               lambda tm: [pl.BlockSpec((None, 1, TN), lambda j, i: (layer, 0, j))])
    return hr, hp, hg


def _pair_attn_kernel(diff, tq, tk, lam_init, *refs):
    if diff:
        q_ref, k_ref, v_ref, z_ref, lp_ref, sg_ref, o_ref, q2_ref, vt_ref, acc_ref, s_ref = refs
    else:
        q_ref, k_ref, v_ref, z_ref, o_ref, q2_ref, vt_ref, acc_ref, s_ref = refs
    seq = k_ref.shape[0]
    nq = 2 * tq

    @pl.when(pl.program_id(1) == 0)
    def _():
        for c in range(seq // tk):
            vt_ref[c, :LANES, :] = v_ref[c * tk:(c + 1) * tk, :].astype(F32).T.astype(BF16)
            vt_ref[c, LANES:, :] = jnp.ones((VT_PAD, tk), BF16)

    if diff:
        q = q_ref[...]
        lane = lax.broadcasted_iota(jnp.int32, q.shape, 1)
        zero = jnp.zeros_like(q)
        q2_ref[:tq, :] = jnp.where(lane < DA_QK, q, zero)
        q2_ref[tq:, :] = jnp.where(lane >= DA_QK, q, zero)
    else:
        q2_ref[:tq, :] = q_ref[:, :LANES]
        q2_ref[tq:, :] = q_ref[:, LANES:]
    acc_ref[...] = jnp.zeros(acc_ref.shape, F32)

    n_chunks = seq // tk

    def scores(c, slot):
        start = pl.multiple_of(c * tk, tk)
        s_ref[slot] = lax.dot_general(k_ref[pl.ds(start, tk), :], q2_ref[...], (((1,), (1,)), ((), ())),
                                      preferred_element_type=F32)

    def fold(c, slot, m_prev):
        s = s_ref[slot]
        m_new = jnp.maximum(m_prev, jnp.max(s, axis=0, keepdims=True))
        alpha = jnp.exp2(m_prev - m_new)
        p = jnp.exp2(s - m_new)
        acc_ref[...] = alpha * acc_ref[...] + jnp.dot(vt_ref[c], p.astype(BF16),
                                                      preferred_element_type=F32)
        return m_new

    unroll = 4 if n_chunks % 4 == 0 else 2

    def group(base, m, last):
        for u in range(unroll):
            if not (last and u == unroll - 1):
                scores(base + u + 1, (u + 1) % 2)
            m = fold(base + u, u % 2, m)
        return m

    scores(0, 0)
    m = jnp.full((1, nq), NEG_INF, F32)
    n_groups = n_chunks // unroll
    if n_groups > 1:
        m = lax.fori_loop(0, n_groups - 1, lambda j, mm: group(j * unroll, mm, False), m)
    group((n_groups - 1) * unroll, m, True)

    acc = acc_ref[...]
    o = (acc[:LANES] / acc[LANES:LANES + 1]).T
    if diff:
        lp = lp_ref[...]
        lam = (jnp.exp(jnp.sum(lp[0:1] * lp[1:2], keepdims=True))
               - jnp.exp(jnp.sum(lp[2:3] * lp[3:4], keepdims=True)) + lam_init)
        dlt = o[:tq] - lam * o[tq:]
        ms = jnp.mean(dlt * dlt, axis=-1, keepdims=True)
        y = dlt * lax.rsqrt(ms + RMS_EPS) * sg_ref[...] * (1.0 - lam_init)
        o_ref[...] = (y * z_ref[...].astype(F32)).astype(BF16)
    else:
        o_ref[:, :LANES] = (o[:tq] * z_ref[:, :LANES].astype(F32)).astype(BF16)
        o_ref[:, LANES:] = (o[tq:] * z_ref[:, LANES:].astype(F32)).astype(BF16)


def _pair_attn(hr, hp, diff, lam_init=0.0, lam_params=None, subln_g=None, tq=512, tk=256):
    s = hr.shape[0]
    if diff:
        n_kv, qw = 4, LANES
        q_map = lambda hh, i: (i, CB_AQ + hh)
        k_map = lambda hh, i: (0, CB_AK + hh)
        v_map = lambda hh, i: (0, CB_AV + hh)
        z_map = lambda hh, i: (i, CB_AZ + hh)
    else:
        n_kv, qw = 2, 2 * LANES
        q_map = lambda hh, i: (i, CB_CQ // 2 + hh)
        k_map = lambda hh, i: (0, CB_CK + hh)
        v_map = lambda hh, i: (0, CB_CV + hh)
        z_map = lambda hh, i: (i, CB_CZ // 2 + hh)
    in_specs = [pl.BlockSpec((tq, qw), q_map),
                pl.BlockSpec((s, LANES), k_map),
                pl.BlockSpec((s, LANES), v_map),
                pl.BlockSpec((tq, qw), z_map)]
    args = [hr, hr, hp if diff else hr, hp]
    if diff:
        in_specs += [pl.BlockSpec((4, DA_QK), lambda hh, i: (0, 0)),
                     pl.BlockSpec((1, LANES), lambda hh, i: (0, 0))]
        args += [lam_params, subln_g.reshape(1, LANES)]
    return pl.pallas_call(
        functools.partial(_pair_attn_kernel, diff, tq, tk, lam_init),
        grid=(n_kv, s // tq),
        in_specs=in_specs,
        out_specs=pl.BlockSpec((tq, qw), lambda hh, i: (i, hh)),
        out_shape=jax.ShapeDtypeStruct((s, BRANCH_W), BF16),
        scratch_shapes=[pltpu.VMEM((2 * tq, LANES), BF16),
                        pltpu.VMEM((s // tk, LANES + VT_PAD, tk), BF16),
                        pltpu.VMEM((LANES + VT_PAD, 2 * tq), F32),
                        pltpu.VMEM((2, tk, 2 * tq), F32)],
        compiler_params=_cparams(("arbitrary", "arbitrary")),
        name="diff_attn" if diff else "axial_gqa",
    )(*args)


def _nat_bias(rpb):
    n_layers, n_heads = rpb.shape[:2]
    cols = jnp.arange(GRID_W)
    c0 = jnp.clip(cols - NB_COLS // 2, 0, GRID_W - NB_COLS)
    kc = jnp.arange(GRID_W)
    valid = (kc[None, :] >= c0[:, None]) & (kc[None, :] < c0[:, None] + NB_COLS)
    dc = jnp.clip(kc[None, :] - cols[:, None] + (NB_COLS - 1), 0, 2 * NB_COLS - 2)
    t = jnp.where(valid, rpb[:, :, :, dc].astype(F32), NEG_INF)
    out = []
    for ds in range(NB_ROWS):
        w = t[:, :, ds:ds + NB_ROWS]
        out.append(jnp.transpose(w, (0, 1, 3, 2, 4)).reshape(n_layers, n_heads, GRID_W, NB_ROWS * GRID_W))
    return jnp.stack(out, axis=1)


def _nat_kernel(rows, rps, q_ref, k_ref, v_ref, z_ref, bias_ref, o_ref):
    win = NB_ROWS * GRID_W
    starts, scores = [], []
    for u in range(rps):
        r = pl.program_id(1) * rps + u
        r0 = jnp.clip(r - NB_ROWS // 2, 0, rows - NB_ROWS)
        start = pl.multiple_of(r0 * GRID_W, GRID_W)
        kw = k_ref[pl.ds(start, win), :]
        s = lax.dot_general(q_ref[u * GRID_W:(u + 1) * GRID_W, :], kw, (((1,), (1,)), ((), ())),
                            preferred_element_type=F32)
        scores.append(s * (HEAD_DIM ** -0.5) + bias_ref[r0 - r + (NB_ROWS - 1)])
        starts.append(start)
    probs = []
    for s in scores:
        p = jnp.exp(s - jnp.max(s, axis=-1, keepdims=True))
        probs.append((p.astype(BF16), jnp.sum(p, axis=-1, keepdims=True)))
    for u, (p, l) in enumerate(probs):
        qrows = slice(u * GRID_W, (u + 1) * GRID_W)
        o = jnp.dot(p, v_ref[pl.ds(starts[u], win), :], preferred_element_type=F32) / l
        o_ref[qrows, :] = (o * z_ref[qrows, :].astype(F32)).astype(BF16)


def _nat_attn(hp, bias, layer, rps=8):
    s = hp.shape[0]
    rows = s // GRID_W
    blk = rps * GRID_W
    return pl.pallas_call(
        functools.partial(_nat_kernel, rows, rps),
        grid=(4, rows // rps),
        in_specs=[pl.BlockSpec((blk, LANES), lambda hh, r: (r, CB_BQ + hh)),
                  pl.BlockSpec((s, LANES), lambda hh, r: (0, CB_BK + hh)),
                  pl.BlockSpec((s, LANES), lambda hh, r: (0, CB_BV + hh)),
                  pl.BlockSpec((blk, LANES), lambda hh, r: (r, CB_BZ + hh)),
                  pl.BlockSpec((None, NB_ROWS, None, GRID_W, NB_ROWS * GRID_W),
                               lambda hh, r: (layer, 0, hh, 0, 0))],
        out_specs=pl.BlockSpec((blk, LANES), lambda hh, r: (r, hh)),
        out_shape=jax.ShapeDtypeStruct((s, BRANCH_W), BF16),
        compiler_params=_cparams(("arbitrary", "arbitrary")),
        name="nat_attn",
    )(hp, hp, hp, hp, bias)


def _dil_masks(tq):
    row = np.arange(tq)[:, None]

    def band(col, rate):
        d = col - row
        return np.where((d >= 0) & (d <= 2 * DIL_SIDE * rate) & (d % rate == 0), 0.0, NEG_INF).astype(np.float32)

    width0 = tq + 2 * DIL_SIDE
    tables = [np.stack([band(np.arange(width0)[None, :] + dlt, 1) for dlt in (DIL_SIDE, 0, -DIL_SIDE)])]
    for rate in DIL_RATES[1:]:
        reach = DIL_SIDE * rate
        cols = np.arange(-reach, tq + 3 * reach)[None, :]
        full = band(cols, rate)
        tables.append(full.reshape(tq, -1, LANES).transpose(1, 0, 2))
    return [jnp.asarray(t) for t in tables]


def _dil_kernel(tq, nsub, q0_ref, q1_ref, q2_ref, k0_ref, k1_ref, k2_ref, v0_ref, v1_ref, v2_ref,
                z_ref, b0_ref, b1_ref, b2_ref, o_ref):
    seq = k0_ref.shape[0]
    q_refs, k_refs, v_refs = (q0_ref, q1_ref, q2_ref), (k0_ref, k1_ref, k2_ref), (v0_ref, v1_ref, v2_ref)
    b_refs = (b0_ref, b1_ref, b2_ref)
    scores, wins = [], []
    for u in range(nsub):
        t0 = (pl.program_id(1) * nsub + u) * tq
        rows = slice(u * tq, (u + 1) * tq)
        su, wu = [], []
        for rate, q_ref, k_ref, b_ref in zip(DIL_RATES, q_refs, k_refs, b_refs):
            reach = DIL_SIDE * rate
            width = tq + 2 * reach
            start = pl.multiple_of(jnp.clip(t0 - reach, 0, seq - width), DIL_SIDE)
            delta = start - t0 + reach
            s = lax.dot_general(q_ref[rows, :], k_ref[pl.ds(start, width), :], (((1,), (1,)), ((), ())),
                                preferred_element_type=F32)
            if rate == 1:
                s = s + b_ref[jnp.where(delta > 0, 0, jnp.where(delta < 0, 2, 1))]
            else:
                tile0 = (delta + reach) // LANES
                s = jnp.concatenate([s[:, j * LANES:(j + 1) * LANES] + b_ref[tile0 + j]
                                     for j in range(width // LANES)], axis=1)
            su.append(s)
            wu.append((start, width))
        scores.append(su)
        wins.append(wu)
    probs, dens = [], []
    for su in scores:
        m = su[0].max(axis=-1, keepdims=True)
        for s in su[1:]:
            m = jnp.maximum(m, s.max(axis=-1, keepdims=True))
        pu = [jnp.exp2(s - m) for s in su]
        l = pu[0].sum(axis=-1, keepdims=True)
        for p in pu[1:]:
            l = l + p.sum(axis=-1, keepdims=True)
        probs.append([p.astype(BF16) for p in pu])
        dens.append(l)
    for u, (pu, wu, l) in enumerate(zip(probs, wins, dens)):
        rows = slice(u * tq, (u + 1) * tq)
        o = None
        for p, (start, width), v_ref in zip(pu, wu, v_refs):
            pv = jnp.dot(p, v_ref[pl.ds(start, width), :], preferred_element_type=F32)
            o = pv if o is None else o + pv
        o_ref[rows, :] = (o / l * z_ref[rows, :].astype(F32)).astype(BF16)


def _dil_attn(hr, hp, tq=128, nsub=2):
    s = hr.shape[0]
    masks = _dil_masks(tq)
    m_specs = [pl.BlockSpec(m.shape, lambda hh, i: (0, 0, 0)) for m in masks]
    blk = tq * nsub
    qs = [pl.BlockSpec((blk, LANES), functools.partial(lambda g, hh, i: (i, CB_DQ + 4 * g + hh), g))
          for g in range(3)]
    ks = [pl.BlockSpec((s, LANES), functools.partial(lambda g, hh, i: (0, CB_DK + 4 * g + hh), g))
          for g in range(3)]
    vs = [pl.BlockSpec((s, LANES), functools.partial(lambda g, hh, i: (0, CB_DV + 4 * g + hh), g))
          for g in range(3)]
    return pl.pallas_call(
        functools.partial(_dil_kernel, tq, nsub),
        grid=(4, s // blk),
        in_specs=qs + ks + vs + [pl.BlockSpec((blk, LANES), lambda hh, i: (i, CB_DZ + hh))] + m_specs,
        out_specs=pl.BlockSpec((blk, LANES), lambda hh, i: (i, hh)),
        out_shape=jax.ShapeDtypeStruct((s, BRANCH_W), BF16),
        compiler_params=_cparams(("arbitrary", "arbitrary")),
        name="dil_attn",
    )(*([hr] * 6), *([hp] * 4), *masks)


def _merge_kernel(ya_ref, yb_ref, yc_ref, yd_ref, ga_ref, gb_ref, gc_ref, gd_ref, wb_ref, o_ref):
    acc = None
    for n, (y_ref, g_ref) in enumerate(zip((ya_ref, yb_ref, yc_ref, yd_ref),
                                           (ga_ref, gb_ref, gc_ref, gd_ref))):
        proj = jnp.dot(y_ref[...], wb_ref[n], preferred_element_type=F32)
        term = proj * g_ref[...].astype(F32)
        acc = term if acc is None else acc + term
    o_ref[...] = acc.astype(BF16)


def _merge(ys, hg, wb_bf, layer, tm=1024):
    s = hg.shape[0]
    tm = math.gcd(tm, s)
    n_col = D_MODEL // TN
    y_specs = [pl.BlockSpec((tm, BRANCH_W), lambda j, i: (i, 0)) for _ in range(N_BRANCH)]
    g_specs = [pl.BlockSpec((tm, TN), functools.partial(lambda n, j, i: (i, n_col * n + j), n))
               for n in range(N_BRANCH)]
    return pl.pallas_call(
        _merge_kernel,
        grid=(n_col, s // tm),
        in_specs=y_specs + g_specs + [pl.BlockSpec((None, N_BRANCH, BRANCH_W, TN),
                                                   lambda j, i: (layer, 0, 0, j))],
        out_specs=pl.BlockSpec((tm, TN), lambda j, i: (i, j)),
        out_shape=jax.ShapeDtypeStruct((s, D_MODEL), BF16),
        compiler_params=_cparams(("arbitrary", "arbitrary")),
        name="branch_merge",
    )(*ys, hg, hg, hg, hg, wb_bf)


def _out_ln_kernel(alpha, m_ref, w_ref, x_ref, g_ref, b_ref, y_ref, ybf_ref):
    out = jnp.dot(m_ref[...], w_ref[...], preferred_element_type=F32)
    y = _ln_rows(alpha * x_ref[...] + out, g_ref[...], b_ref[...])
    y_ref[...] = y
    ybf_ref[...] = y.astype(BF16)


def _out_ln(merged, wo_bf, layer, x, g, b, alpha, tm=512):
    s, d = x.shape
    return pl.pallas_call(
        functools.partial(_out_ln_kernel, alpha),
        grid=(s // tm,),
        in_specs=[pl.BlockSpec((tm, d), lambda i: (i, 0)),
                  pl.BlockSpec((None, d, d), lambda i: (layer, 0, 0)),
                  pl.BlockSpec((tm, d), lambda i: (i, 0)),
                  pl.BlockSpec((1, d), lambda i: (0, 0)),
                  pl.BlockSpec((1, d), lambda i: (0, 0))],
        out_specs=[pl.BlockSpec((tm, d), lambda i: (i, 0)),
                   pl.BlockSpec((tm, d), lambda i: (i, 0))],
        out_shape=[jax.ShapeDtypeStruct((s, d), F32), jax.ShapeDtypeStruct((s, d), BF16)],
        compiler_params=_cparams(("arbitrary",)),
        name="out_ln",
    )(merged, wo_bf, x, g.reshape(1, d), b.reshape(1, d))


def kernel(x, emb_ln_g, emb_ln_b, w_in, b_gate, diff_lambda, diff_subln_g, nat_rpb,
           gqa_q_norm_g, gqa_k_norm_g, w_branch, w_out, ln_g, ln_b):
    batch, seq, d = x.shape
    assert batch == 1 and d == D_MODEL and w_in.shape[-1] == D_IN
    depth = w_in.shape[0]
    alpha = (2 * depth) ** 0.25
    tabs = _rope_tables(seq)
    nat_bias = _nat_bias(nat_rpb)
    wb_bf = w_branch.astype(BF16)
    wo_bf = w_out.astype(BF16)
    xf, xbf = _embed_ln(x[0], emb_ln_g, emb_ln_b)
    for l in range(depth):
        lam_init = 0.8 - 0.6 * math.exp(-0.3 * l)
        gains = jnp.stack([gqa_q_norm_g[l], gqa_k_norm_g[l]]).reshape(2, 1, LANES)
        hr, hp, hg = _inproj(xbf, w_in, b_gate, l, tabs, gains)
        ya = _pair_attn(hr, hp, True, lam_init, diff_lambda[l], diff_subln_g[l])
        yb = _nat_attn(hp, nat_bias, l)
        yc = _pair_attn(hr, hp, False)
        yd = _dil_attn(hr, hp)
        merged = _merge((ya, yb, yc, yd), hg, wb_bf, l)
        xf, xbf = _out_ln(merged, wo_bf, l, xf, ln_g[l], ln_b[l], alpha)
    return xf[None]
```

```python
import functools
import math

import numpy as np
import jax
import jax.numpy as jnp
from jax import lax
from jax.experimental import pallas as pl
from jax.experimental.pallas import tpu as pltpu

F32 = jnp.float32
BF16 = jnp.bfloat16

D_MODEL = 2048
GRID_W = 64
HEAD_DIM = 128
N_BRANCH = 4
BRANCH_W = D_MODEL // 4
DA_QK = HEAD_DIM // 2
NB_ROWS = 8
NB_COLS = 16
AXIAL_THETA = 10000.0
DIL_RATES = (1, 4, 16)
DIL_SIDE = 64
ROPE_THETA = 500000.0
LN_EPS = 1e-5
RMS_EPS = 1e-6
NEG_INF = -1e30
LOG2E = math.log2(math.e)

LANES = 128
VT_PAD = 16
TN = 512
D_IN = 18944

ROT_TILES = (0, 1, 8, 9, 11, 12, 13, 14, 15, 16)
PLAIN_TILES = (2, 3, 4, 5, 6, 7, 10, 17, 18, 19, 20)
GATE_TILE0, N_GATE_TILES = 21, 16
ROT_VARIANT = (0, 1, 2, 3, 4, 4, 4, 5, 5, 5)
CB_AQ, CB_AK, CB_CQ, CB_CK, CB_CV, CB_DQ, CB_DK = 0, 4, 8, 12, 14, 16, 28
CB_AV, CB_AZ, CB_BQ, CB_BK, CB_BV, CB_BZ, CB_CZ, CB_DV, CB_DZ = 0, 4, 8, 12, 16, 20, 24, 28, 40
PLAIN_SILU = (1, 5, 6, 10)

VMEM_LIMIT = 56 * 1024 * 1024


def _cparams(sem):
    return pltpu.CompilerParams(dimension_semantics=sem, vmem_limit_bytes=VMEM_LIMIT)


def _select(j, values):
    out = values[-1]
    for t in range(len(values) - 2, -1, -1):
        out = jnp.where(j == t, values[t], out)
    return out


def _any_of(j, tiles):
    c = j == tiles[0]
    for t in tiles[1:]:
        c = c | (j == t)
    return c


def _ln_rows(x, g, b):
    mu = jnp.mean(x, axis=-1, keepdims=True)
    xc = x - mu
    var = jnp.mean(xc * xc, axis=-1, keepdims=True)
    return xc * lax.rsqrt(var + LN_EPS) * g + b


def _embed_ln_kernel(x_ref, g_ref, b_ref, y_ref, ybf_ref):
    y = _ln_rows(x_ref[...], g_ref[...], b_ref[...])
    y_ref[...] = y
    ybf_ref[...] = y.astype(BF16)


def _embed_ln(x, g, b, tm=512):
    s, d = x.shape
    return pl.pallas_call(
        _embed_ln_kernel,
        grid=(s // tm,),
        in_specs=[pl.BlockSpec((tm, d), lambda i: (i, 0)),
                  pl.BlockSpec((1, d), lambda i: (0, 0)),
                  pl.BlockSpec((1, d), lambda i: (0, 0))],
        out_specs=[pl.BlockSpec((tm, d), lambda i: (i, 0)),
                   pl.BlockSpec((tm, d), lambda i: (i, 0))],
        out_shape=[jax.ShapeDtypeStruct((s, d), F32), jax.ShapeDtypeStruct((s, d), BF16)],
        compiler_params=_cparams(("arbitrary",)),
        name="embed_ln",
    )(x, g.reshape(1, d), b.reshape(1, d))


def _rope_tables(seq):
    t = np.arange(seq)

    def cs(pos, dim, theta):
        half = dim // 2
        inv = np.power(float(theta), -np.arange(half, dtype=np.float64) * 2.0 / dim)
        ang = pos.astype(np.float64)[:, None] * inv[None, :]
        return jnp.asarray(np.cos(ang), F32), jnp.asarray(np.sin(ang), F32)

    def group(cos, sin, width):
        half = cos.shape[1]
        pad = width - 2 * half
        one = jnp.ones((seq, pad), F32)
        zero = jnp.zeros((seq, pad), F32)
        zh = jnp.zeros((seq, half), F32)
        return (jnp.concatenate([cos, cos, one], 1),
                jnp.concatenate([zh, sin, zero], 1),
                jnp.concatenate([-sin, zh, zero], 1))

    ca, sa = cs(t, DA_QK // 4, ROPE_THETA)
    va = jnp.stack([jnp.concatenate([p, p], 1) for p in group(ca, sa, DA_QK)])
    cr, sr = cs(t // GRID_W, HEAD_DIM // 2, AXIAL_THETA)
    cc, sc = cs(t % GRID_W, HEAD_DIM // 2, AXIAL_THETA)
    vc = jnp.stack([jnp.concatenate([p, q], 1)
                    for p, q in zip(group(cr, sr, HEAD_DIM // 2), group(cc, sc, HEAD_DIM // 2))])
    cd, sd = cs(t, HEAD_DIM // 4, ROPE_THETA)
    vd = jnp.stack(group(cd, sd, HEAD_DIM))
    qa = DA_QK ** -0.5 * LOG2E
    qh = HEAD_DIM ** -0.5 * LOG2E
    return jnp.stack([va * qa, va, vc * qh, vc, vd * qh, vd])


ROT_HALF_A, ROT_HALF_C, ROT_HALF_D = DA_QK // 8, HEAD_DIM // 4, HEAD_DIM // 8


def _proj_kernel(tsub, n_sub, make_branches, n_extra, x_ref, w_ref, *rest):
    extra = rest[:n_extra]
    o_ref, wbf_ref, acc_ref = rest[n_extra:]
    j = pl.program_id(0)
    n_pass = x_ref.shape[0] // (n_sub * tsub)

    @pl.when(pl.program_id(1) == 0)
    def _():
        wbf_ref[...] = w_ref[...].astype(BF16)

    def pipeline(epilogue):
        def one_pass(h, carry):
            def rows_of(u):
                return pl.ds(pl.multiple_of((h * n_sub + u) * tsub, tsub), tsub)

            def matmul(u):
                acc_ref[u % 2] = jnp.dot(x_ref[rows_of(u), :], wbf_ref[...],
                                         preferred_element_type=F32)

            matmul(0)
            for u in range(n_sub):
                if u + 1 < n_sub:
                    matmul(u + 1)
                for g in range(TN // LANES):
                    lanes = slice(g * LANES, (g + 1) * LANES)
                    o_ref[rows_of(u), lanes] = epilogue(acc_ref[u % 2, :, lanes], rows_of(u), g).astype(BF16)
            return carry

        lax.fori_loop(0, n_pass, one_pass, 0)

    for cond, epilogue in make_branches(*extra):
        if cond is None:
            pipeline(epilogue)
        else:
            pl.when(cond(j))(functools.partial(pipeline, epilogue))


def _proj(name, xbf, w_in, layer, src_tiles, make_branches, extra, extra_specs, tm=2048, tsub=256, n_sub=4):
    s, d = xbf.shape
    tm = math.gcd(tm, s)
    n_sub = min(n_sub, tm // tsub)
    assert tm % (tsub * n_sub) == 0
    n_tiles = len(src_tiles)
    contiguous = src_tiles == tuple(range(src_tiles[0], src_tiles[0] + n_tiles))
    src = (lambda j: src_tiles[0] + j) if contiguous else (lambda j: _select(j, src_tiles))
    return pl.pallas_call(
        functools.partial(_proj_kernel, tsub, n_sub, make_branches, len(extra)),
        grid=(n_tiles, s // tm),
        in_specs=[pl.BlockSpec((tm, d), lambda j, i: (i, 0)),
                  pl.BlockSpec((None, d, TN), lambda j, i: (layer, 0, src(j)))] + extra_specs(tm),
        out_specs=pl.BlockSpec((tm, TN), lambda j, i: (i, j)),
        out_shape=jax.ShapeDtypeStruct((s, n_tiles * TN), BF16),
        scratch_shapes=[pltpu.VMEM((d, TN), BF16), pltpu.VMEM((2, tsub, TN), F32)],
        compiler_params=_cparams(("arbitrary", "arbitrary")),
        name=name,
    )(xbf, w_in, *extra)


def _rot_branches(tab_ref, gn_ref):
    def rot(a, rows, half):
        return (a * tab_ref[0, rows, :] + pltpu.roll(a, half, 1) * tab_ref[1, rows, :]
                + pltpu.roll(a, LANES - half, 1) * tab_ref[2, rows, :])

    def rms(a, gain):
        ms = jnp.mean(a * a, axis=-1, keepdims=True)
        return a * lax.rsqrt(ms + RMS_EPS) * gain

    return [(lambda j: j <= 1, lambda a, rows, g: rot(a, rows, ROT_HALF_A)),
            (lambda j: j == 2, lambda a, rows, g: rot(rms(a, gn_ref[0]), rows, ROT_HALF_C)),
            (lambda j: j == 3, lambda a, rows, g: rot(rms(a, gn_ref[1]), rows, ROT_HALF_C) if g < 2 else a),
            (lambda j: j >= 4, lambda a, rows, g: rot(a, rows, ROT_HALF_D))]


def _plain_branches():
    return [(lambda j: _any_of(j, PLAIN_SILU), lambda a, rows, g: a * jax.nn.sigmoid(a)),
            (lambda j: jnp.logical_not(_any_of(j, PLAIN_SILU)), lambda a, rows, g: a)]


def _gate_branches(bg_ref):
    return [(None, lambda a, rows, g: jax.nn.sigmoid(a + bg_ref[:, g * LANES:(g + 1) * LANES]))]


def _inproj(xbf, w_in, b_gate, layer, tabs, gains):
    hr = _proj("inproj_rot", xbf, w_in, layer, ROT_TILES, _rot_branches, (tabs, gains),
               lambda tm: [pl.BlockSpec((None, 3, tm, LANES), lambda j, i: (_select(j, ROT_VARIANT), 0, i, 0)),
                           pl.BlockSpec((2, 1, LANES), lambda j, i: (0, 0, 0))])
    hp = _proj("inproj_plain", xbf, w_in, layer, PLAIN_TILES, _plain_branches, (), lambda tm: [], n_sub=8)
    hg = _proj("inproj_gate", xbf, w_in, layer, tuple(range(GATE_TILE0, GATE_TILE0 + N_GATE_TILES)),
               _gate_branches, (b_gate.reshape(b_gate.shape[0], 1, -1),),
               lambda tm: [pl.BlockSpec((None, 1, TN), lambda j, i: (layer, 0, j))], n_sub=8)
    return hr, hp, hg


def _pair_attn_kernel(diff, tq, tk, lam_init, *refs):
    if diff:
        q_ref, k_ref, v_ref, z_ref, lp_ref, sg_ref, o_ref, q2_ref, vt_ref, acc_ref, s_ref = refs
    else:
        q_ref, k_ref, v_ref, z_ref, o_ref, q2_ref, vt_ref, acc_ref, s_ref = refs
    seq = k_ref.shape[0]
    nq = 2 * tq

    @pl.when(pl.program_id(1) == 0)
    def _():
        for c in range(seq // tk):
            vt_ref[c, :LANES, :] = v_ref[c * tk:(c + 1) * tk, :].astype(F32).T.astype(BF16)
            vt_ref[c, LANES:, :] = jnp.ones((VT_PAD, tk), BF16)

    if diff:
        qt = q_ref[...].astype(F32).T
        sub = lax.broadcasted_iota(jnp.int32, qt.shape, 0)
        q2_ref[:, :tq] = jnp.where(sub < DA_QK, qt, 0.0).astype(BF16)
        q2_ref[:, tq:] = jnp.where(sub >= DA_QK, qt, 0.0).astype(BF16)
    else:
        q2_ref[:, :tq] = q_ref[:, :LANES].astype(F32).T.astype(BF16)
        q2_ref[:, tq:] = q_ref[:, LANES:].astype(F32).T.astype(BF16)
    acc_ref[...] = jnp.zeros(acc_ref.shape, F32)

    n_chunks = seq // tk

    def scores(c, slot):
        start = pl.multiple_of(c * tk, tk)
        s_ref[slot] = jnp.dot(k_ref[pl.ds(start, tk), :], q2_ref[...],
                              preferred_element_type=F32)

    def fold(c, slot, m_prev):
        s = s_ref[slot]
        m_new = jnp.maximum(m_prev, jnp.max(s, axis=0, keepdims=True))
        alpha = jnp.exp2(m_prev - m_new)
        p = jnp.exp2(s - m_new)
        acc_ref[...] = alpha * acc_ref[...] + jnp.dot(vt_ref[c], p.astype(BF16),
                                                      preferred_element_type=F32)
        return m_new

    unroll = 4 if n_chunks % 4 == 0 else 2

    def group(base, m, last):
        for u in range(unroll):
            if not (last and u == unroll - 1):
                scores(base + u + 1, (u + 1) % 2)
            m = fold(base + u, u % 2, m)
        return m

    scores(0, 0)
    m = jnp.full((1, nq), NEG_INF, F32)
    n_groups = n_chunks // unroll
    if n_groups > 1:
        m = lax.fori_loop(0, n_groups - 1, lambda j, mm: group(j * unroll, mm, False), m)
    group((n_groups - 1) * unroll, m, True)

    acc = acc_ref[...]
    o = (acc[:LANES] / acc[LANES:LANES + 1]).T
    if diff:
        lp = lp_ref[...]
        lam = (jnp.exp(jnp.sum(lp[0:1] * lp[1:2], keepdims=True))
               - jnp.exp(jnp.sum(lp[2:3] * lp[3:4], keepdims=True)) + lam_init)
        dlt = o[:tq] - lam * o[tq:]
        ms = jnp.mean(dlt * dlt, axis=-1, keepdims=True)
        y = dlt * lax.rsqrt(ms + RMS_EPS) * sg_ref[...] * (1.0 - lam_init)
        o_ref[...] = (y * z_ref[...].astype(F32)).astype(BF16)
    else:
        o_ref[:, :LANES] = (o[:tq] * z_ref[:, :LANES].astype(F32)).astype(BF16)
        o_ref[:, LANES:] = (o[tq:] * z_ref[:, LANES:].astype(F32)).astype(BF16)


def _pair_attn(hr, hp, diff, lam_init=0.0, lam_params=None, subln_g=None, tq=512, tk=256):
    s = hr.shape[0]
    if diff:
        n_kv, qw = 4, LANES
        q_map = lambda hh, i: (i, CB_AQ + hh)
        k_map = lambda hh, i: (0, CB_AK + hh)
        v_map = lambda hh, i: (0, CB_AV + hh)
        z_map = lambda hh, i: (i, CB_AZ + hh)
    else:
        n_kv, qw = 2, 2 * LANES
        q_map = lambda hh, i: (i, CB_CQ // 2 + hh)
        k_map = lambda hh, i: (0, CB_CK + hh)
        v_map = lambda hh, i: (0, CB_CV + hh)
        z_map = lambda hh, i: (i, CB_CZ // 2 + hh)
    in_specs = [pl.BlockSpec((tq, qw), q_map),
                pl.BlockSpec((s, LANES), k_map),
                pl.BlockSpec((s, LANES), v_map),
                pl.BlockSpec((tq, qw), z_map)]
    args = [hr, hr, hp if diff else hr, hp]
    if diff:
        in_specs += [pl.BlockSpec((4, DA_QK), lambda hh, i: (0, 0)),
                     pl.BlockSpec((1, LANES), lambda hh, i: (0, 0))]
        args += [lam_params, subln_g.reshape(1, LANES)]
    return pl.pallas_call(
        functools.partial(_pair_attn_kernel, diff, tq, tk, lam_init),
        grid=(n_kv, s // tq),
        in_specs=in_specs,
        out_specs=pl.BlockSpec((tq, qw), lambda hh, i: (i, hh)),
        out_shape=jax.ShapeDtypeStruct((s, BRANCH_W), BF16),
        scratch_shapes=[pltpu.VMEM((LANES, 2 * tq), BF16),
                        pltpu.VMEM((s // tk, LANES + VT_PAD, tk), BF16),
                        pltpu.VMEM((LANES + VT_PAD, 2 * tq), F32),
                        pltpu.VMEM((2, tk, 2 * tq), F32)],
        compiler_params=_cparams(("arbitrary", "arbitrary")),
        name="diff_attn" if diff else "axial_gqa",
    )(*args)


def _nat_bias(rpb):
    n_layers, n_heads = rpb.shape[:2]
    cols = jnp.arange(GRID_W)
    c0 = jnp.clip(cols - NB_COLS // 2, 0, GRID_W - NB_COLS)
    kc = jnp.arange(GRID_W)
    valid = (kc[None, :] >= c0[:, None]) & (kc[None, :] < c0[:, None] + NB_COLS)
    lpad = GRID_W - NB_COLS
    padded = jnp.pad(rpb.astype(F32), ((0, 0), (0, 0), (0, 0), (lpad, lpad)))
    toep = jnp.stack([padded[..., GRID_W - 1 - c:2 * GRID_W - 1 - c] for c in range(GRID_W)], axis=-2)
    t = jnp.where(valid, toep, NEG_INF)
    out = []
    for ds in range(NB_ROWS):
        w = t[:, :, ds:ds + NB_ROWS]
        out.append(jnp.transpose(w, (0, 1, 3, 2, 4)).reshape(n_layers, n_heads, GRID_W, NB_ROWS * GRID_W))
    return jnp.stack(out, axis=1)


def _nat_kernel(rows, rps, q_ref, k_ref, v_ref, z_ref, bias_ref, o_ref):
    win = NB_ROWS * GRID_W
    starts, scores = [], []
    for u in range(rps):
        r = pl.program_id(1) * rps + u
        r0 = jnp.clip(r - NB_ROWS // 2, 0, rows - NB_ROWS)
        start = pl.multiple_of(r0 * GRID_W, GRID_W)
        kw = k_ref[pl.ds(start, win), :]
        s = lax.dot_general(q_ref[u * GRID_W:(u + 1) * GRID_W, :], kw, (((1,), (1,)), ((), ())),
                            preferred_element_type=F32)
        scores.append(s * (HEAD_DIM ** -0.5) + bias_ref[r0 - r + (NB_ROWS - 1)])
        starts.append(start)
    probs = []
    for s in scores:
        p = jnp.exp(s - jnp.max(s, axis=-1, keepdims=True))
        probs.append((p.astype(BF16), jnp.sum(p, axis=-1, keepdims=True)))
    for u, (p, l) in enumerate(probs):
        qrows = slice(u * GRID_W, (u + 1) * GRID_W)
        o = jnp.dot(p, v_ref[pl.ds(starts[u], win), :], preferred_element_type=F32) / l
        o_ref[qrows, :] = (o * z_ref[qrows, :].astype(F32)).astype(BF16)


def _nat_attn(hp, bias, layer, rps=8):
    s = hp.shape[0]
    rows = s // GRID_W
    blk = rps * GRID_W
    return pl.pallas_call(
        functools.partial(_nat_kernel, rows, rps),
        grid=(4, rows // rps),
        in_specs=[pl.BlockSpec((blk, LANES), lambda hh, r: (r, CB_BQ + hh)),
                  pl.BlockSpec((s, LANES), lambda hh, r: (0, CB_BK + hh)),
                  pl.BlockSpec((s, LANES), lambda hh, r: (0, CB_BV + hh)),
                  pl.BlockSpec((blk, LANES), lambda hh, r: (r, CB_BZ + hh)),
                  pl.BlockSpec((None, NB_ROWS, None, GRID_W, NB_ROWS * GRID_W),
                               lambda hh, r: (layer, 0, hh, 0, 0))],
        out_specs=pl.BlockSpec((blk, LANES), lambda hh, r: (r, hh)),
        out_shape=jax.ShapeDtypeStruct((s, BRANCH_W), BF16),
        compiler_params=_cparams(("arbitrary", "arbitrary")),
        name="nat_attn",
    )(hp, hp, hp, hp, bias)


def _dil_masks(tq):
    row = np.arange(tq)[:, None]

    def band(col, rate):
        d = col - row
        return np.where((d >= 0) & (d <= 2 * DIL_SIDE * rate) & (d % rate == 0), 0.0, NEG_INF).astype(np.float32)

    width0 = tq + 2 * DIL_SIDE
    tables = [np.stack([band(np.arange(width0)[None, :] + dlt, 1) for dlt in (DIL_SIDE, 0, -DIL_SIDE)])]
    for rate in DIL_RATES[1:]:
        reach = DIL_SIDE * rate
        cols = np.arange(-reach, tq + 3 * reach)[None, :]
        full = band(cols, rate)
        tables.append(full.reshape(tq, -1, LANES).transpose(1, 0, 2))
    return [jnp.asarray(t) for t in tables]


def _dil_kernel(tq, nsub, q0_ref, q1_ref, q2_ref, k0_ref, k1_ref, k2_ref, v0_ref, v1_ref, v2_ref,
                z_ref, b0_ref, b1_ref, b2_ref, o_ref):
    seq = k0_ref.shape[0]
    q_refs, k_refs, v_refs = (q0_ref, q1_ref, q2_ref), (k0_ref, k1_ref, k2_ref), (v0_ref, v1_ref, v2_ref)
    b_refs = (b0_ref, b1_ref, b2_ref)
    scores, wins = [], []
    for u in range(nsub):
        t0 = (pl.program_id(1) * nsub + u) * tq
        rows = slice(u * tq, (u + 1) * tq)
        su, wu = [], []
        for rate, q_ref, k_ref, b_ref in zip(DIL_RATES, q_refs, k_refs, b_refs):
            reach = DIL_SIDE * rate
            width = tq + 2 * reach
            start = pl.multiple_of(jnp.clip(t0 - reach, 0, seq - width), DIL_SIDE)
            delta = start - t0 + reach
            s = lax.dot_general(q_ref[rows, :], k_ref[pl.ds(start, width), :], (((1,), (1,)), ((), ())),
                                preferred_element_type=F32)
            if rate == 1:
                s = s + b_ref[jnp.where(delta > 0, 0, jnp.where(delta < 0, 2, 1))]
            else:
                tile0 = (delta + reach) // LANES
                s = jnp.concatenate([s[:, j * LANES:(j + 1) * LANES] + b_ref[tile0 + j]
                                     for j in range(width // LANES)], axis=1)
            su.append(s)
            wu.append((start, width))
        scores.append(su)
        wins.append(wu)
    probs, dens = [], []
    for su in scores:
        m = su[0].max(axis=-1, keepdims=True)
        for s in su[1:]:
            m = jnp.maximum(m, s.max(axis=-1, keepdims=True))
        pu = [jnp.exp2(s - m) for s in su]
        l = pu[0].sum(axis=-1, keepdims=True)
        for p in pu[1:]:
            l = l + p.sum(axis=-1, keepdims=True)
        probs.append([p.astype(BF16) for p in pu])
        dens.append(l)
    for u, (pu, wu, l) in enumerate(zip(probs, wins, dens)):
        rows = slice(u * tq, (u + 1) * tq)
        o = None
        for p, (start, width), v_ref in zip(pu, wu, v_refs):
            pv = jnp.dot(p, v_ref[pl.ds(start, width), :], preferred_element_type=F32)
            o = pv if o is None else o + pv
        o_ref[rows, :] = (o / l * z_ref[rows, :].astype(F32)).astype(BF16)


def _dil_attn(hr, hp, tq=128, nsub=2):
    s = hr.shape[0]
    masks = _dil_masks(tq)
    m_specs = [pl.BlockSpec(m.shape, lambda hh, i: (0, 0, 0)) for m in masks]
    blk = tq * nsub
    qs = [pl.BlockSpec((blk, LANES), functools.partial(lambda g, hh, i: (i, CB_DQ + 4 * g + hh), g))
          for g in range(3)]
    ks = [pl.BlockSpec((s, LANES), functools.partial(lambda g, hh, i: (0, CB_DK + 4 * g + hh), g))
          for g in range(3)]
    vs = [pl.BlockSpec((s, LANES), functools.partial(lambda g, hh, i: (0, CB_DV + 4 * g + hh), g))
          for g in range(3)]
    return pl.pallas_call(
        functools.partial(_dil_kernel, tq, nsub),
        grid=(4, s // blk),
        in_specs=qs + ks + vs + [pl.BlockSpec((blk, LANES), lambda hh, i: (i, CB_DZ + hh))] + m_specs,
        out_specs=pl.BlockSpec((blk, LANES), lambda hh, i: (i, hh)),
        out_shape=jax.ShapeDtypeStruct((s, BRANCH_W), BF16),
        compiler_params=_cparams(("arbitrary", "arbitrary")),
        name="dil_attn",
    )(*([hr] * 6), *([hp] * 4), *masks)


def _merge_kernel(ya_ref, yb_ref, yc_ref, yd_ref, ga_ref, gb_ref, gc_ref, gd_ref, wb_ref, o_ref, wbf_ref):
    @pl.when(pl.program_id(1) == 0)
    def _():
        wbf_ref[...] = wb_ref[...].astype(BF16)

    acc = None
    for n, (y_ref, g_ref) in enumerate(zip((ya_ref, yb_ref, yc_ref, yd_ref),
                                           (ga_ref, gb_ref, gc_ref, gd_ref))):
        proj = jnp.dot(y_ref[...], wbf_ref[n], preferred_element_type=F32)
        term = proj * g_ref[...].astype(F32)
        acc = term if acc is None else acc + term
    o_ref[...] = acc.astype(BF16)


def _merge(ys, hg, w_branch, layer, tm=1024):
    s = hg.shape[0]
    tm = math.gcd(tm, s)
    n_col = D_MODEL // TN
    y_specs = [pl.BlockSpec((tm, BRANCH_W), lambda j, i: (i, 0)) for _ in range(N_BRANCH)]
    g_specs = [pl.BlockSpec((tm, TN), functools.partial(lambda n, j, i: (i, n_col * n + j), n))
               for n in range(N_BRANCH)]
    return pl.pallas_call(
        _merge_kernel,
        grid=(n_col, s // tm),
        in_specs=y_specs + g_specs + [pl.BlockSpec((None, N_BRANCH, BRANCH_W, TN),
                                                   lambda j, i: (layer, 0, 0, j))],
        out_specs=pl.BlockSpec((tm, TN), lambda j, i: (i, j)),
        out_shape=jax.ShapeDtypeStruct((s, D_MODEL), BF16),
        scratch_shapes=[pltpu.VMEM((N_BRANCH, BRANCH_W, TN), BF16)],
        compiler_params=_cparams(("arbitrary", "arbitrary")),
        name="branch_merge",
    )(*ys, hg, hg, hg, hg, w_branch)


def _out_ln_kernel(alpha, m_ref, w_ref, x_ref, g_ref, b_ref, y_ref, ybf_ref):
    out = jnp.dot(m_ref[...], w_ref[...], preferred_element_type=F32)
    y = _ln_rows(alpha * x_ref[...] + out, g_ref[...], b_ref[...])
    y_ref[...] = y
    ybf_ref[...] = y.astype(BF16)


def _out_ln(merged, wo_bf, layer, x, g, b, alpha, tm=512):
    s, d = x.shape
    return pl.pallas_call(
        functools.partial(_out_ln_kernel, alpha),
        grid=(s // tm,),
        in_specs=[pl.BlockSpec((tm, d), lambda i: (i, 0)),
                  pl.BlockSpec((None, d, d), lambda i: (layer, 0, 0)),
                  pl.BlockSpec((tm, d), lambda i: (i, 0)),
                  pl.BlockSpec((1, d), lambda i: (0, 0)),
                  pl.BlockSpec((1, d), lambda i: (0, 0))],
        out_specs=[pl.BlockSpec((tm, d), lambda i: (i, 0)),
                   pl.BlockSpec((tm, d), lambda i: (i, 0))],
        out_shape=[jax.ShapeDtypeStruct((s, d), F32), jax.ShapeDtypeStruct((s, d), BF16)],
        compiler_params=_cparams(("arbitrary",)),
        name="out_ln",
    )(merged, wo_bf, x, g.reshape(1, d), b.reshape(1, d))


def kernel(x, emb_ln_g, emb_ln_b, w_in, b_gate, diff_lambda, diff_subln_g, nat_rpb,
           gqa_q_norm_g, gqa_k_norm_g, w_branch, w_out, ln_g, ln_b):
    batch, seq, d = x.shape
    assert batch == 1 and d == D_MODEL and w_in.shape[-1] == D_IN
    depth = w_in.shape[0]
    alpha = (2 * depth) ** 0.25
    tabs = _rope_tables(seq)
    nat_bias = _nat_bias(nat_rpb)
    wo_bf = w_out.astype(BF16)
    xf, xbf = _embed_ln(x[0], emb_ln_g, emb_ln_b)
    for l in range(depth):
        lam_init = 0.8 - 0.6 * math.exp(-0.3 * l)
        gains = jnp.stack([gqa_q_norm_g[l], gqa_k_norm_g[l]]).reshape(2, 1, LANES)
        hr, hp, hg = _inproj(xbf, w_in, b_gate, l, tabs, gains)
        ya = _pair_attn(hr, hp, True, lam_init, diff_lambda[l], diff_subln_g[l])
        yb = _nat_attn(hp, nat_bias, l)
        yc = _pair_attn(hr, hp, False)
        yd = _dil_attn(hr, hp)
        merged = _merge((ya, yb, yc, yd), hg, w_branch, l)
        xf, xbf = _out_ln(merged, wo_bf, l, xf, ln_g[l], ln_b[l], alpha)
    return xf[None]
```

```python
import functools
import math

import numpy as np
import jax
import jax.numpy as jnp
from jax import lax
from jax.experimental import pallas as pl
from jax.experimental.pallas import tpu as pltpu

F32 = jnp.float32
BF16 = jnp.bfloat16

D_MODEL = 2048
GRID_W = 64
HEAD_DIM = 128
N_BRANCH = 4
BRANCH_W = D_MODEL // 4
DA_QK = HEAD_DIM // 2
NB_ROWS = 8
NB_COLS = 16
AXIAL_THETA = 10000.0
DIL_RATES = (1, 4, 16)
DIL_SIDE = 64
ROPE_THETA = 500000.0
LN_EPS = 1e-5
RMS_EPS = 1e-6
NEG_INF = -1e30
LOG2E = math.log2(math.e)

LANES = 128
VT_PAD = 16
TN = 512
D_IN = 18944

ROT_TILES = (0, 1, 8, 9, 11, 12, 13, 14, 15, 16)
PLAIN_TILES = (2, 3, 4, 5, 6, 7, 10, 17, 18, 19, 20)
GATE_TILE0, N_GATE_TILES = 21, 16
ROT_VARIANT = (0, 1, 2, 3, 4, 4, 4, 5, 5, 5)
CB_AQ, CB_AK, CB_CQ, CB_CK, CB_CV, CB_DQ, CB_DK = 0, 4, 8, 12, 14, 16, 28
CB_AV, CB_AZ, CB_BQ, CB_BK, CB_BV, CB_BZ, CB_CZ, CB_DV, CB_DZ = 0, 4, 8, 12, 16, 20, 24, 28, 40
PLAIN_SILU = (1, 5, 6, 10)

VMEM_LIMIT = 56 * 1024 * 1024


def _cparams(sem):
    return pltpu.CompilerParams(dimension_semantics=sem, vmem_limit_bytes=VMEM_LIMIT)


def _select(j, values):
    out = values[-1]
    for t in range(len(values) - 2, -1, -1):
        out = jnp.where(j == t, values[t], out)
    return out


def _any_of(j, tiles):
    c = j == tiles[0]
    for t in tiles[1:]:
        c = c | (j == t)
    return c


def _ln_rows(x, g, b):
    mu = jnp.mean(x, axis=-1, keepdims=True)
    xc = x - mu
    var = jnp.mean(xc * xc, axis=-1, keepdims=True)
    return xc * lax.rsqrt(var + LN_EPS) * g + b


def _embed_ln_kernel(x_ref, g_ref, b_ref, y_ref, ybf_ref):
    y = _ln_rows(x_ref[...], g_ref[...], b_ref[...])
    y_ref[...] = y
    ybf_ref[...] = y.astype(BF16)


def _embed_ln(x, g, b, tm=512):
    s, d = x.shape
    return pl.pallas_call(
        _embed_ln_kernel,
        grid=(s // tm,),
        in_specs=[pl.BlockSpec((tm, d), lambda i: (i, 0)),
                  pl.BlockSpec((1, d), lambda i: (0, 0)),
                  pl.BlockSpec((1, d), lambda i: (0, 0))],
        out_specs=[pl.BlockSpec((tm, d), lambda i: (i, 0)),
                   pl.BlockSpec((tm, d), lambda i: (i, 0))],
        out_shape=[jax.ShapeDtypeStruct((s, d), F32), jax.ShapeDtypeStruct((s, d), BF16)],
        compiler_params=_cparams(("arbitrary",)),
        name="embed_ln",
    )(x, g.reshape(1, d), b.reshape(1, d))


def _rope_tables(seq):
    t = np.arange(seq)

    def cs(pos, dim, theta):
        half = dim // 2
        inv = np.power(float(theta), -np.arange(half, dtype=np.float64) * 2.0 / dim)
        ang = pos.astype(np.float64)[:, None] * inv[None, :]
        return jnp.asarray(np.cos(ang), F32), jnp.asarray(np.sin(ang), F32)

    def group(cos, sin, width):
        half = cos.shape[1]
        pad = width - 2 * half
        one = jnp.ones((seq, pad), F32)
        zero = jnp.zeros((seq, pad), F32)
        zh = jnp.zeros((seq, half), F32)
        return (jnp.concatenate([cos, cos, one], 1),
                jnp.concatenate([zh, sin, zero], 1),
                jnp.concatenate([-sin, zh, zero], 1))

    ca, sa = cs(t, DA_QK // 4, ROPE_THETA)
    va = jnp.stack([jnp.concatenate([p, p], 1) for p in group(ca, sa, DA_QK)])
    cr, sr = cs(t // GRID_W, HEAD_DIM // 2, AXIAL_THETA)
    cc, sc = cs(t % GRID_W, HEAD_DIM // 2, AXIAL_THETA)
    vc = jnp.stack([jnp.concatenate([p, q], 1)
                    for p, q in zip(group(cr, sr, HEAD_DIM // 2), group(cc, sc, HEAD_DIM // 2))])
    cd, sd = cs(t, HEAD_DIM // 4, ROPE_THETA)
    vd = jnp.stack(group(cd, sd, HEAD_DIM))
    qa = DA_QK ** -0.5 * LOG2E
    qh = HEAD_DIM ** -0.5 * LOG2E
    return jnp.stack([va * qa, va, vc * qh, vc, vd * qh, vd])


ROT_HALF_A, ROT_HALF_C, ROT_HALF_D = DA_QK // 8, HEAD_DIM // 4, HEAD_DIM // 8


def _proj_kernel(tsub, n_sub, make_branches, n_extra, x_ref, w_ref, *rest):
    extra = rest[:n_extra]
    o_ref, wbf_ref, acc_ref = rest[n_extra:]
    j = pl.program_id(0)
    n_pass = x_ref.shape[0] // (n_sub * tsub)

    @pl.when(pl.program_id(1) == 0)
    def _():
        wbf_ref[...] = w_ref[...].astype(BF16)

    def pipeline(epilogue):
        def one_pass(h, carry):
            def rows_of(u):
                return pl.ds(pl.multiple_of((h * n_sub + u) * tsub, tsub), tsub)

            def matmul(u):
                acc_ref[u % 2] = jnp.dot(x_ref[rows_of(u), :], wbf_ref[...],
                                         preferred_element_type=F32)

            matmul(0)
            for u in range(n_sub):
                if u + 1 < n_sub:
                    matmul(u + 1)
                for g in range(TN // LANES):
                    lanes = slice(g * LANES, (g + 1) * LANES)
                    y = epilogue(acc_ref[u % 2, :, lanes], rows_of(u), g).astype(BF16)
                    if len(o_ref.shape) == 3:
                        o_ref[g, rows_of(u), :] = y
                    else:
                        o_ref[rows_of(u), lanes] = y
            return carry

        lax.fori_loop(0, n_pass, one_pass, 0)

    for cond, epilogue in make_branches(*extra):
        if cond is None:
            pipeline(epilogue)
        else:
            pl.when(cond(j))(functools.partial(pipeline, epilogue))


def _proj(name, xbf, w_in, layer, src_tiles, make_branches, extra, extra_specs, head_major,
          tm=2048, tsub=256, n_sub=4):
    s, d = xbf.shape
    tm = math.gcd(tm, s)
    n_sub = min(n_sub, tm // tsub)
    assert tm % (tsub * n_sub) == 0
    n_tiles = len(src_tiles)
    contiguous = src_tiles == tuple(range(src_tiles[0], src_tiles[0] + n_tiles))
    src = (lambda j: src_tiles[0] + j) if contiguous else (lambda j: _select(j, src_tiles))
    if head_major:
        out_spec = pl.BlockSpec((TN // LANES, tm, LANES), lambda j, i: (j, i, 0))
        out_shape = jax.ShapeDtypeStruct((n_tiles * (TN // LANES), s, LANES), BF16)
    else:
        out_spec = pl.BlockSpec((tm, TN), lambda j, i: (i, j))
        out_shape = jax.ShapeDtypeStruct((s, n_tiles * TN), BF16)
    return pl.pallas_call(
        functools.partial(_proj_kernel, tsub, n_sub, make_branches, len(extra)),
        grid=(n_tiles, s // tm),
        in_specs=[pl.BlockSpec((tm, d), lambda j, i: (i, 0)),
                  pl.BlockSpec((None, d, TN), lambda j, i: (layer, 0, src(j)))] + extra_specs(tm),
        out_specs=out_spec,
        out_shape=out_shape,
        scratch_shapes=[pltpu.VMEM((d, TN), BF16), pltpu.VMEM((2, tsub, TN), F32)],
        compiler_params=_cparams(("arbitrary", "arbitrary")),
        name=name,
    )(xbf, w_in, *extra)


def _rot_branches(tab_ref, gn_ref):
    def rot(a, rows, half):
        return (a * tab_ref[0, rows, :] + pltpu.roll(a, half, 1) * tab_ref[1, rows, :]
                + pltpu.roll(a, LANES - half, 1) * tab_ref[2, rows, :])

    def rms(a, gain):
        ms = jnp.mean(a * a, axis=-1, keepdims=True)
        return a * lax.rsqrt(ms + RMS_EPS) * gain

    return [(lambda j: j <= 1, lambda a, rows, g: rot(a, rows, ROT_HALF_A)),
            (lambda j: j == 2, lambda a, rows, g: rot(rms(a, gn_ref[0]), rows, ROT_HALF_C)),
            (lambda j: j == 3, lambda a, rows, g: rot(rms(a, gn_ref[1]), rows, ROT_HALF_C) if g < 2 else a),
            (lambda j: j >= 4, lambda a, rows, g: rot(a, rows, ROT_HALF_D))]


def _plain_branches():
    return [(lambda j: _any_of(j, PLAIN_SILU), lambda a, rows, g: a * jax.nn.sigmoid(a)),
            (lambda j: jnp.logical_not(_any_of(j, PLAIN_SILU)), lambda a, rows, g: a)]


def _gate_branches(bg_ref):
    return [(None, lambda a, rows, g: jax.nn.sigmoid(a + bg_ref[:, g * LANES:(g + 1) * LANES]))]


def _inproj(xbf, w_in, b_gate, layer, tabs, gains):
    hr = _proj("inproj_rot", xbf, w_in, layer, ROT_TILES, _rot_branches, (tabs, gains),
               lambda tm: [pl.BlockSpec((None, 3, tm, LANES), lambda j, i: (_select(j, ROT_VARIANT), 0, i, 0)),
                           pl.BlockSpec((2, 1, LANES), lambda j, i: (0, 0, 0))], True)
    hp = _proj("inproj_plain", xbf, w_in, layer, PLAIN_TILES, _plain_branches, (), lambda tm: [], True,
               n_sub=8)
    hg = _proj("inproj_gate", xbf, w_in, layer, tuple(range(GATE_TILE0, GATE_TILE0 + N_GATE_TILES)),
               _gate_branches, (b_gate.reshape(b_gate.shape[0], 1, -1),),
               lambda tm: [pl.BlockSpec((None, 1, TN), lambda j, i: (layer, 0, j))], False, n_sub=8)
    return hr, hp, hg


def _pair_attn_kernel(diff, tq, tk, lam_init, *refs):
    if diff:
        q_ref, k_ref, v_ref, z_ref, lp_ref, sg_ref, o_ref, q2_ref, vt_ref, acc_ref, s_ref = refs
    else:
        q_ref, k_ref, v_ref, z_ref, o_ref, q2_ref, vt_ref, acc_ref, s_ref = refs
    seq = k_ref.shape[0]
    nq = 2 * tq

    @pl.when(pl.program_id(1) == 0)
    def _():
        for c in range(seq // tk):
            vt_ref[c, :LANES, :] = v_ref[c * tk:(c + 1) * tk, :].astype(F32).T.astype(BF16)
            vt_ref[c, LANES:, :] = jnp.ones((VT_PAD, tk), BF16)

    if diff:
        qt = q_ref[...].astype(F32).T
        sub = lax.broadcasted_iota(jnp.int32, qt.shape, 0)
        q2_ref[:, :tq] = jnp.where(sub < DA_QK, qt, 0.0).astype(BF16)
        q2_ref[:, tq:] = jnp.where(sub >= DA_QK, qt, 0.0).astype(BF16)
    else:
        q2_ref[:, :tq] = q_ref[0].astype(F32).T.astype(BF16)
        q2_ref[:, tq:] = q_ref[1].astype(F32).T.astype(BF16)
    acc_ref[...] = jnp.zeros(acc_ref.shape, F32)

    n_chunks = seq // tk

    def scores(c, slot):
        start = pl.multiple_of(c * tk, tk)
        s_ref[slot] = jnp.dot(k_ref[pl.ds(start, tk), :], q2_ref[...],
                              preferred_element_type=F32)

    def fold(c, slot, m_prev):
        s = s_ref[slot]
        m_new = jnp.maximum(m_prev, jnp.max(s, axis=0, keepdims=True))
        alpha = jnp.exp2(m_prev - m_new)
        p = jnp.exp2(s - m_new)
        acc_ref[...] = alpha * acc_ref[...] + jnp.dot(vt_ref[c], p.astype(BF16),
                                                      preferred_element_type=F32)
        return m_new

    unroll = 4 if n_chunks % 4 == 0 else 2

    def group(base, m, last):
        for u in range(unroll):
            if not (last and u == unroll - 1):
                scores(base + u + 1, (u + 1) % 2)
            m = fold(base + u, u % 2, m)
        return m

    scores(0, 0)
    m = jnp.full((1, nq), NEG_INF, F32)
    n_groups = n_chunks // unroll
    if n_groups > 1:
        m = lax.fori_loop(0, n_groups - 1, lambda j, mm: group(j * unroll, mm, False), m)
    group((n_groups - 1) * unroll, m, True)

    acc = acc_ref[...]
    o = (acc[:LANES] / acc[LANES:LANES + 1]).T
    if diff:
        lp = lp_ref[...]
        lam = (jnp.exp(jnp.sum(lp[0:1] * lp[1:2], keepdims=True))
               - jnp.exp(jnp.sum(lp[2:3] * lp[3:4], keepdims=True)) + lam_init)
        dlt = o[:tq] - lam * o[tq:]
        ms = jnp.mean(dlt * dlt, axis=-1, keepdims=True)
        y = dlt * lax.rsqrt(ms + RMS_EPS) * sg_ref[...] * (1.0 - lam_init)
        o_ref[...] = (y * z_ref[...].astype(F32)).astype(BF16)
    else:
        o_ref[:, :LANES] = (o[:tq] * z_ref[0].astype(F32)).astype(BF16)
        o_ref[:, LANES:] = (o[tq:] * z_ref[1].astype(F32)).astype(BF16)


def _pair_attn(hr, hp, diff, lam_init=0.0, lam_params=None, subln_g=None, tq=512, tk=256):
    s = hr.shape[1]
    if diff:
        n_kv, qw, qblk = 4, LANES, (None, tq, LANES)
        q_map = lambda hh, i: (CB_AQ + hh, i, 0)
        k_map = lambda hh, i: (CB_AK + hh, 0, 0)
        v_map = lambda hh, i: (CB_AV + hh, 0, 0)
        z_map = lambda hh, i: (CB_AZ + hh, i, 0)
    else:
        n_kv, qw, qblk = 2, 2 * LANES, (2, tq, LANES)
        q_map = lambda hh, i: (CB_CQ // 2 + hh, i, 0)
        k_map = lambda hh, i: (CB_CK + hh, 0, 0)
        v_map = lambda hh, i: (CB_CV + hh, 0, 0)
        z_map = lambda hh, i: (CB_CZ // 2 + hh, i, 0)
    in_specs = [pl.BlockSpec(qblk, q_map),
                pl.BlockSpec((None, s, LANES), k_map),
                pl.BlockSpec((None, s, LANES), v_map),
                pl.BlockSpec(qblk, z_map)]
    args = [hr, hr, hp if diff else hr, hp]
    if diff:
        in_specs += [pl.BlockSpec((4, DA_QK), lambda hh, i: (0, 0)),
                     pl.BlockSpec((1, LANES), lambda hh, i: (0, 0))]
        args += [lam_params, subln_g.reshape(1, LANES)]
    return pl.pallas_call(
        functools.partial(_pair_attn_kernel, diff, tq, tk, lam_init),
        grid=(n_kv, s // tq),
        in_specs=in_specs,
        out_specs=pl.BlockSpec((tq, qw), lambda hh, i: (i, hh)),
        out_shape=jax.ShapeDtypeStruct((s, BRANCH_W), BF16),
        scratch_shapes=[pltpu.VMEM((LANES, 2 * tq), BF16),
                        pltpu.VMEM((s // tk, LANES + VT_PAD, tk), BF16),
                        pltpu.VMEM((LANES + VT_PAD, 2 * tq), F32),
                        pltpu.VMEM((2, tk, 2 * tq), F32)],
        compiler_params=_cparams(("arbitrary", "arbitrary")),
        name="diff_attn" if diff else "axial_gqa",
    )(*args)


def _nat_bias(rpb):
    n_layers, n_heads = rpb.shape[:2]
    cols = jnp.arange(GRID_W)
    c0 = jnp.clip(cols - NB_COLS // 2, 0, GRID_W - NB_COLS)
    kc = jnp.arange(GRID_W)
    valid = (kc[None, :] >= c0[:, None]) & (kc[None, :] < c0[:, None] + NB_COLS)
    lpad = GRID_W - NB_COLS
    padded = jnp.pad(rpb.astype(F32), ((0, 0), (0, 0), (0, 0), (lpad, lpad)))
    toep = jnp.stack([padded[..., GRID_W - 1 - c:2 * GRID_W - 1 - c] for c in range(GRID_W)], axis=-2)
    t = jnp.where(valid, toep, NEG_INF)
    out = []
    for ds in range(NB_ROWS):
        w = t[:, :, ds:ds + NB_ROWS]
        out.append(jnp.transpose(w, (0, 1, 3, 2, 4)).reshape(n_layers, n_heads, GRID_W, NB_ROWS * GRID_W))
    return jnp.stack(out, axis=1)


def _nat_kernel(rows, rps, q_ref, k_ref, v_ref, z_ref, bias_ref, o_ref):
    win = NB_ROWS * GRID_W
    starts, scores = [], []
    for u in range(rps):
        r = pl.program_id(1) * rps + u
        r0 = jnp.clip(r - NB_ROWS // 2, 0, rows - NB_ROWS)
        start = pl.multiple_of(r0 * GRID_W, GRID_W)
        kw = k_ref[pl.ds(start, win), :]
        s = lax.dot_general(q_ref[u * GRID_W:(u + 1) * GRID_W, :], kw, (((1,), (1,)), ((), ())),
                            preferred_element_type=F32)
        scores.append(s * (HEAD_DIM ** -0.5) + bias_ref[r0 - r + (NB_ROWS - 1)])
        starts.append(start)
    probs = []
    for s in scores:
        p = jnp.exp(s - jnp.max(s, axis=-1, keepdims=True))
        probs.append((p.astype(BF16), jnp.sum(p, axis=-1, keepdims=True)))
    for u, (p, l) in enumerate(probs):
        qrows = slice(u * GRID_W, (u + 1) * GRID_W)
        o = jnp.dot(p, v_ref[pl.ds(starts[u], win), :], preferred_element_type=F32) / l
        o_ref[qrows, :] = (o * z_ref[qrows, :].astype(F32)).astype(BF16)


def _nat_attn(hp, bias, layer, rps=8):
    s = hp.shape[1]
    rows = s // GRID_W
    blk = rps * GRID_W
    return pl.pallas_call(
        functools.partial(_nat_kernel, rows, rps),
        grid=(4, rows // rps),
        in_specs=[pl.BlockSpec((None, blk, LANES), lambda hh, r: (CB_BQ + hh, r, 0)),
                  pl.BlockSpec((None, s, LANES), lambda hh, r: (CB_BK + hh, 0, 0)),
                  pl.BlockSpec((None, s, LANES), lambda hh, r: (CB_BV + hh, 0, 0)),
                  pl.BlockSpec((None, blk, LANES), lambda hh, r: (CB_BZ + hh, r, 0)),
                  pl.BlockSpec((None, NB_ROWS, None, GRID_W, NB_ROWS * GRID_W),
                               lambda hh, r: (layer, 0, hh, 0, 0))],
        out_specs=pl.BlockSpec((blk, LANES), lambda hh, r: (r, hh)),
        out_shape=jax.ShapeDtypeStruct((s, BRANCH_W), BF16),
        compiler_params=_cparams(("arbitrary", "arbitrary")),
        name="nat_attn",
    )(hp, hp, hp, hp, bias)


def _dil_masks(tq):
    row = np.arange(tq)[:, None]

    def band(col, rate):
        d = col - row
        return np.where((d >= 0) & (d <= 2 * DIL_SIDE * rate) & (d % rate == 0), 0.0, NEG_INF).astype(np.float32)

    width0 = tq + 2 * DIL_SIDE
    tables = [np.stack([band(np.arange(width0)[None, :] + dlt, 1) for dlt in (DIL_SIDE, 0, -DIL_SIDE)])]
    for rate in DIL_RATES[1:]:
        reach = DIL_SIDE * rate
        cols = np.arange(-reach, tq + 3 * reach)[None, :]
        full = band(cols, rate)
        tables.append(full.reshape(tq, -1, LANES).transpose(1, 0, 2))
    return [jnp.asarray(t) for t in tables]


def _dil_kernel(tq, nsub, q0_ref, q1_ref, q2_ref, k0_ref, k1_ref, k2_ref, v0_ref, v1_ref, v2_ref,
                z_ref, b0_ref, b1_ref, b2_ref, o_ref):
    seq = k0_ref.shape[0]
    q_refs, k_refs, v_refs = (q0_ref, q1_ref, q2_ref), (k0_ref, k1_ref, k2_ref), (v0_ref, v1_ref, v2_ref)
    b_refs = (b0_ref, b1_ref, b2_ref)
    scores, wins = [], []
    for u in range(nsub):
        t0 = (pl.program_id(1) * nsub + u) * tq
        rows = slice(u * tq, (u + 1) * tq)
        su, wu = [], []
        for rate, q_ref, k_ref, b_ref in zip(DIL_RATES, q_refs, k_refs, b_refs):
            reach = DIL_SIDE * rate
            width = tq + 2 * reach
            start = pl.multiple_of(jnp.clip(t0 - reach, 0, seq - width), DIL_SIDE)
            delta = start - t0 + reach
            s = lax.dot_general(q_ref[rows, :], k_ref[pl.ds(start, width), :], (((1,), (1,)), ((), ())),
                                preferred_element_type=F32)
            if rate == 1:
                s = s + b_ref[jnp.where(delta > 0, 0, jnp.where(delta < 0, 2, 1))]
            else:
                tile0 = (delta + reach) // LANES
                s = jnp.concatenate([s[:, j * LANES:(j + 1) * LANES] + b_ref[tile0 + j]
                                     for j in range(width // LANES)], axis=1)
            su.append(s)
            wu.append((start, width))
        scores.append(su)
        wins.append(wu)
    probs, dens = [], []
    for su in scores:
        m = su[0].max(axis=-1, keepdims=True)
        for s in su[1:]:
            m = jnp.maximum(m, s.max(axis=-1, keepdims=True))
        pu = [jnp.exp2(s - m) for s in su]
        l = pu[0].sum(axis=-1, keepdims=True)
        for p in pu[1:]:
            l = l + p.sum(axis=-1, keepdims=True)
        probs.append([p.astype(BF16) for p in pu])
        dens.append(l)
    for u, (pu, wu, l) in enumerate(zip(probs, wins, dens)):
        rows = slice(u * tq, (u + 1) * tq)
        o = None
        for p, (start, width), v_ref in zip(pu, wu, v_refs):
            pv = jnp.dot(p, v_ref[pl.ds(start, width), :], preferred_element_type=F32)
            o = pv if o is None else o + pv
        o_ref[rows, :] = (o / l * z_ref[rows, :].astype(F32)).astype(BF16)


def _dil_attn(hr, hp, tq=128, nsub=2):
    s = hr.shape[1]
    masks = _dil_masks(tq)
    m_specs = [pl.BlockSpec(m.shape, lambda hh, i: (0, 0, 0)) for m in masks]
    blk = tq * nsub
    qs = [pl.BlockSpec((None, blk, LANES), functools.partial(lambda g, hh, i: (CB_DQ + 4 * g + hh, i, 0), g))
          for g in range(3)]
    ks = [pl.BlockSpec((None, s, LANES), functools.partial(lambda g, hh, i: (CB_DK + 4 * g + hh, 0, 0), g))
          for g in range(3)]
    vs = [pl.BlockSpec((None, s, LANES), functools.partial(lambda g, hh, i: (CB_DV + 4 * g + hh, 0, 0), g))
          for g in range(3)]
    return pl.pallas_call(
        functools.partial(_dil_kernel, tq, nsub),
        grid=(4, s // blk),
        in_specs=qs + ks + vs + [pl.BlockSpec((None, blk, LANES), lambda hh, i: (CB_DZ + hh, i, 0))] + m_specs,
        out_specs=pl.BlockSpec((blk, LANES), lambda hh, i: (i, hh)),
        out_shape=jax.ShapeDtypeStruct((s, BRANCH_W), BF16),
        compiler_params=_cparams(("arbitrary", "arbitrary")),
        name="dil_attn",
    )(*([hr] * 6), *([hp] * 4), *masks)


def _merge_kernel(ya_ref, yb_ref, yc_ref, yd_ref, ga_ref, gb_ref, gc_ref, gd_ref, wb_ref, o_ref, wbf_ref):
    @pl.when(pl.program_id(1) == 0)
    def _():
        wbf_ref[...] = wb_ref[...].astype(BF16)

    acc = None
    for n, (y_ref, g_ref) in enumerate(zip((ya_ref, yb_ref, yc_ref, yd_ref),
                                           (ga_ref, gb_ref, gc_ref, gd_ref))):
        proj = jnp.dot(y_ref[...], wbf_ref[n], preferred_element_type=F32)
        term = proj * g_ref[...].astype(F32)
        acc = term if acc is None else acc + term
    o_ref[...] = acc.astype(BF16)


def _merge(ys, hg, w_branch, layer, tm=1024):
    s = hg.shape[0]
    tm = math.gcd(tm, s)
    n_col = D_MODEL // TN
    y_specs = [pl.BlockSpec((tm, BRANCH_W), lambda j, i: (i, 0)) for _ in range(N_BRANCH)]
    g_specs = [pl.BlockSpec((tm, TN), functools.partial(lambda n, j, i: (i, n_col * n + j), n))
               for n in range(N_BRANCH)]
    return pl.pallas_call(
        _merge_kernel,
        grid=(n_col, s // tm),
        in_specs=y_specs + g_specs + [pl.BlockSpec((None, N_BRANCH, BRANCH_W, TN),
                                                   lambda j, i: (layer, 0, 0, j))],
        out_specs=pl.BlockSpec((tm, TN), lambda j, i: (i, j)),
        out_shape=jax.ShapeDtypeStruct((s, D_MODEL), BF16),
        scratch_shapes=[pltpu.VMEM((N_BRANCH, BRANCH_W, TN), BF16)],
        compiler_params=_cparams(("arbitrary", "arbitrary")),
        name="branch_merge",
    )(*ys, hg, hg, hg, hg, w_branch)


def _out_ln_kernel(alpha, m_ref, w_ref, x_ref, g_ref, b_ref, y_ref, ybf_ref):
    out = jnp.dot(m_ref[...], w_ref[...], preferred_element_type=F32)
    y = _ln_rows(alpha * x_ref[...] + out, g_ref[...], b_ref[...])
    y_ref[...] = y
    ybf_ref[...] = y.astype(BF16)


def _out_ln(merged, wo_bf, layer, x, g, b, alpha, tm=512):
    s, d = x.shape
    return pl.pallas_call(
        functools.partial(_out_ln_kernel, alpha),
        grid=(s // tm,),
        in_specs=[pl.BlockSpec((tm, d), lambda i: (i, 0)),
                  pl.BlockSpec((None, d, d), lambda i: (layer, 0, 0)),
                  pl.BlockSpec((tm, d), lambda i: (i, 0)),
                  pl.BlockSpec((1, d), lambda i: (0, 0)),
                  pl.BlockSpec((1, d), lambda i: (0, 0))],
        out_specs=[pl.BlockSpec((tm, d), lambda i: (i, 0)),
                   pl.BlockSpec((tm, d), lambda i: (i, 0))],
        out_shape=[jax.ShapeDtypeStruct((s, d), F32), jax.ShapeDtypeStruct((s, d), BF16)],
        compiler_params=_cparams(("arbitrary",)),
        name="out_ln",
    )(merged, wo_bf, x, g.reshape(1, d), b.reshape(1, d))


def kernel(x, emb_ln_g, emb_ln_b, w_in, b_gate, diff_lambda, diff_subln_g, nat_rpb,
           gqa_q_norm_g, gqa_k_norm_g, w_branch, w_out, ln_g, ln_b):
    batch, seq, d = x.shape
    assert batch == 1 and d == D_MODEL and w_in.shape[-1] == D_IN
    depth = w_in.shape[0]
    alpha = (2 * depth) ** 0.25
    tabs = _rope_tables(seq)
    nat_bias = _nat_bias(nat_rpb)
    wo_bf = w_out.astype(BF16)
    xf, xbf = _embed_ln(x[0], emb_ln_g, emb_ln_b)
    for l in range(depth):
        lam_init = 0.8 - 0.6 * math.exp(-0.3 * l)
        gains = jnp.stack([gqa_q_norm_g[l], gqa_k_norm_g[l]]).reshape(2, 1, LANES)
        hr, hp, hg = _inproj(xbf, w_in, b_gate, l, tabs, gains)
        ya = _pair_attn(hr, hp, True, lam_init, diff_lambda[l], diff_subln_g[l])
        yb = _nat_attn(hp, nat_bias, l)
        yc = _pair_attn(hr, hp, False)
        yd = _dil_attn(hr, hp)
        merged = _merge((ya, yb, yc, yd), hg, w_branch, l)
        xf, xbf = _out_ln(merged, wo_bf, l, xf, ln_g[l], ln_b[l], alpha)
    return xf[None]
```

```python
import functools
import math

import numpy as np
import jax
import jax.numpy as jnp
from jax import lax
from jax.experimental import pallas as pl
from jax.experimental.pallas import tpu as pltpu

F32 = jnp.float32
BF16 = jnp.bfloat16

D_MODEL = 2048
GRID_W = 64
HEAD_DIM = 128
N_BRANCH = 4
BRANCH_W = D_MODEL // 4
DA_QK = HEAD_DIM // 2
NB_ROWS = 8
NB_COLS = 16
AXIAL_THETA = 10000.0
DIL_RATES = (1, 4, 16)
DIL_SIDE = 64
ROPE_THETA = 500000.0
LN_EPS = 1e-5
RMS_EPS = 1e-6
NEG_INF = -1e30
LOG2E = math.log2(math.e)

LANES = 128
VT_PAD = 16
TN = 512
D_IN = 18944

ROT_TILES = (0, 1, 8, 9, 11, 12, 13, 14, 15, 16)
PLAIN_TILES = (2, 3, 4, 5, 6, 7, 10, 17, 18, 19, 20)
GATE_TILE0, N_GATE_TILES = 21, 16
ROT_VARIANT = (0, 1, 2, 3, 4, 4, 4, 5, 5, 5)
CB_AQ, CB_AK, CB_CQ, CB_CK, CB_CV, CB_DQ, CB_DK = 0, 4, 8, 12, 14, 16, 28
CB_AV, CB_AZ, CB_BQ, CB_BK, CB_BV, CB_BZ, CB_CZ, CB_DV, CB_DZ = 0, 4, 8, 12, 16, 20, 24, 28, 40
PLAIN_SILU = (1, 5, 6, 10)

VMEM_LIMIT = 56 * 1024 * 1024


def _cparams(sem):
    return pltpu.CompilerParams(dimension_semantics=sem, vmem_limit_bytes=VMEM_LIMIT)


def _select(j, values):
    out = values[-1]
    for t in range(len(values) - 2, -1, -1):
        out = jnp.where(j == t, values[t], out)
    return out


def _any_of(j, tiles):
    c = j == tiles[0]
    for t in tiles[1:]:
        c = c | (j == t)
    return c


def _ln_rows(x, g, b):
    mu = jnp.mean(x, axis=-1, keepdims=True)
    xc = x - mu
    var = jnp.mean(xc * xc, axis=-1, keepdims=True)
    return xc * lax.rsqrt(var + LN_EPS) * g + b


def _embed_ln_kernel(x_ref, g_ref, b_ref, y_ref, ybf_ref):
    y = _ln_rows(x_ref[...], g_ref[...], b_ref[...])
    y_ref[...] = y
    ybf_ref[...] = y.astype(BF16)


def _embed_ln(x, g, b, tm=512):
    s, d = x.shape
    return pl.pallas_call(
        _embed_ln_kernel,
        grid=(s // tm,),
        in_specs=[pl.BlockSpec((tm, d), lambda i: (i, 0)),
                  pl.BlockSpec((1, d), lambda i: (0, 0)),
                  pl.BlockSpec((1, d), lambda i: (0, 0))],
        out_specs=[pl.BlockSpec((tm, d), lambda i: (i, 0)),
                   pl.BlockSpec((tm, d), lambda i: (i, 0))],
        out_shape=[jax.ShapeDtypeStruct((s, d), F32), jax.ShapeDtypeStruct((s, d), BF16)],
        compiler_params=_cparams(("arbitrary",)),
        name="embed_ln",
    )(x, g.reshape(1, d), b.reshape(1, d))


def _rope_tables(seq):
    t = np.arange(seq)

    def cs(pos, dim, theta):
        half = dim // 2
        inv = np.power(float(theta), -np.arange(half, dtype=np.float64) * 2.0 / dim)
        ang = pos.astype(np.float64)[:, None] * inv[None, :]
        return jnp.asarray(np.cos(ang), F32), jnp.asarray(np.sin(ang), F32)

    def group(cos, sin, width):
        half = cos.shape[1]
        pad = width - 2 * half
        one = jnp.ones((seq, pad), F32)
        zero = jnp.zeros((seq, pad), F32)
        zh = jnp.zeros((seq, half), F32)
        return (jnp.concatenate([cos, cos, one], 1),
                jnp.concatenate([zh, sin, zero], 1),
                jnp.concatenate([-sin, zh, zero], 1))

    ca, sa = cs(t, DA_QK // 4, ROPE_THETA)
    va = jnp.stack([jnp.concatenate([p, p], 1) for p in group(ca, sa, DA_QK)])
    cr, sr = cs(t // GRID_W, HEAD_DIM // 2, AXIAL_THETA)
    cc, sc = cs(t % GRID_W, HEAD_DIM // 2, AXIAL_THETA)
    vc = jnp.stack([jnp.concatenate([p, q], 1)
                    for p, q in zip(group(cr, sr, HEAD_DIM // 2), group(cc, sc, HEAD_DIM // 2))])
    cd, sd = cs(t, HEAD_DIM // 4, ROPE_THETA)
    vd = jnp.stack(group(cd, sd, HEAD_DIM))
    qa = DA_QK ** -0.5 * LOG2E
    qh = HEAD_DIM ** -0.5 * LOG2E
    return jnp.stack([va * qa, va, vc * qh, vc, vd * qh, vd])


ROT_HALF_A, ROT_HALF_C, ROT_HALF_D = DA_QK // 8, HEAD_DIM // 4, HEAD_DIM // 8


def _proj_kernel(tsub, n_sub, make_branches, n_extra, x_ref, w_ref, *rest):
    extra = rest[:n_extra]
    o_ref, wbf_ref, acc_ref = rest[n_extra:]
    j = pl.program_id(0)
    n_pass = x_ref.shape[0] // (n_sub * tsub)

    @pl.when(pl.program_id(1) == 0)
    def _():
        wbf_ref[...] = w_ref[...].astype(BF16)

    def pipeline(epilogue):
        def one_pass(h, carry):
            def rows_of(u):
                return pl.ds(pl.multiple_of((h * n_sub + u) * tsub, tsub), tsub)

            def matmul(u):
                acc_ref[u % 2] = jnp.dot(x_ref[rows_of(u), :], wbf_ref[...],
                                         preferred_element_type=F32)

            matmul(0)
            for u in range(n_sub):
                if u + 1 < n_sub:
                    matmul(u + 1)
                for g in range(TN // LANES):
                    lanes = slice(g * LANES, (g + 1) * LANES)
                    y = epilogue(acc_ref[u % 2, :, lanes], rows_of(u), g).astype(BF16)
                    if len(o_ref.shape) == 3:
                        o_ref[g, rows_of(u), :] = y
                    else:
                        o_ref[rows_of(u), lanes] = y
            return carry

        lax.fori_loop(0, n_pass, one_pass, 0)

    for cond, epilogue in make_branches(*extra):
        if cond is None:
            pipeline(epilogue)
        else:
            pl.when(cond(j))(functools.partial(pipeline, epilogue))


def _proj(name, xbf, w_in, layer, src_tiles, make_branches, extra, extra_specs, head_major,
          tm=2048, tsub=256, n_sub=4):
    s, d = xbf.shape
    tm = math.gcd(tm, s)
    n_sub = min(n_sub, tm // tsub)
    assert tm % (tsub * n_sub) == 0
    n_tiles = len(src_tiles)
    contiguous = src_tiles == tuple(range(src_tiles[0], src_tiles[0] + n_tiles))
    src = (lambda j: src_tiles[0] + j) if contiguous else (lambda j: _select(j, src_tiles))
    if head_major:
        out_spec = pl.BlockSpec((TN // LANES, tm, LANES), lambda j, i: (j, i, 0))
        out_shape = jax.ShapeDtypeStruct((n_tiles * (TN // LANES), s, LANES), BF16)
    else:
        out_spec = pl.BlockSpec((tm, TN), lambda j, i: (i, j))
        out_shape = jax.ShapeDtypeStruct((s, n_tiles * TN), BF16)
    return pl.pallas_call(
        functools.partial(_proj_kernel, tsub, n_sub, make_branches, len(extra)),
        grid=(n_tiles, s // tm),
        in_specs=[pl.BlockSpec((tm, d), lambda j, i: (i, 0)),
                  pl.BlockSpec((None, d, TN), lambda j, i: (layer, 0, src(j)))] + extra_specs(tm),
        out_specs=out_spec,
        out_shape=out_shape,
        scratch_shapes=[pltpu.VMEM((d, TN), BF16), pltpu.VMEM((2, tsub, TN), F32)],
        compiler_params=_cparams(("arbitrary", "arbitrary")),
        name=name,
    )(xbf, w_in, *extra)


def _rot_branches(tab_ref, gn_ref):
    def rot(a, rows, half):
        return (a * tab_ref[0, rows, :] + pltpu.roll(a, half, 1) * tab_ref[1, rows, :]
                + pltpu.roll(a, LANES - half, 1) * tab_ref[2, rows, :])

    def rms(a, gain):
        ms = jnp.mean(a * a, axis=-1, keepdims=True)
        return a * lax.rsqrt(ms + RMS_EPS) * gain

    return [(lambda j: j <= 1, lambda a, rows, g: rot(a, rows, ROT_HALF_A)),
            (lambda j: j == 2, lambda a, rows, g: rot(rms(a, gn_ref[0]), rows, ROT_HALF_C)),
            (lambda j: j == 3, lambda a, rows, g: rot(rms(a, gn_ref[1]), rows, ROT_HALF_C) if g < 2 else a),
            (lambda j: j >= 4, lambda a, rows, g: rot(a, rows, ROT_HALF_D))]


def _plain_branches():
    return [(lambda j: _any_of(j, PLAIN_SILU), lambda a, rows, g: a * jax.nn.sigmoid(a)),
            (lambda j: jnp.logical_not(_any_of(j, PLAIN_SILU)), lambda a, rows, g: a)]


def _gate_branches(bg_ref):
    return [(None, lambda a, rows, g: jax.nn.sigmoid(a + bg_ref[:, g * LANES:(g + 1) * LANES]))]


def _inproj(xbf, w_in, b_gate, layer, tabs, gains):
    hr = _proj("inproj_rot", xbf, w_in, layer, ROT_TILES, _rot_branches, (tabs, gains),
               lambda tm: [pl.BlockSpec((None, 3, tm, LANES), lambda j, i: (_select(j, ROT_VARIANT), 0, i, 0)),
                           pl.BlockSpec((2, 1, LANES), lambda j, i: (0, 0, 0))], True)
    hp = _proj("inproj_plain", xbf, w_in, layer, PLAIN_TILES, _plain_branches, (), lambda tm: [], True,
               n_sub=8)
    hg = _proj("inproj_gate", xbf, w_in, layer, tuple(range(GATE_TILE0, GATE_TILE0 + N_GATE_TILES)),
               _gate_branches, (b_gate.reshape(b_gate.shape[0], 1, -1),),
               lambda tm: [pl.BlockSpec((None, 1, TN), lambda j, i: (layer, 0, j))], False, n_sub=8)
    return hr, hp, hg


def _pair_attn_kernel(diff, tq, tk, lam_init, *refs):
    if diff:
        q_ref, k_ref, v_ref, z_ref, lp_ref, sg_ref, o_ref, q2_ref, vt_ref, acc_ref, s_ref = refs
    else:
        q_ref, k_ref, v_ref, z_ref, o_ref, q2_ref, vt_ref, acc_ref, s_ref = refs
    seq = k_ref.shape[0]
    nq = 2 * tq

    @pl.when(pl.program_id(1) == 0)
    def _():
        for c in range(seq // tk):
            vt_ref[c, :LANES, :] = v_ref[c * tk:(c + 1) * tk, :].astype(F32).T.astype(BF16)
            vt_ref[c, LANES:, :] = jnp.ones((VT_PAD, tk), BF16)

    if diff:
        q = q_ref[...]
        lane = lax.broadcasted_iota(jnp.int32, q.shape, 1)
        zero = jnp.zeros_like(q)
        q2_ref[:tq, :] = jnp.where(lane < DA_QK, q, zero)
        q2_ref[tq:, :] = jnp.where(lane >= DA_QK, q, zero)
    else:
        q2_ref[:tq, :] = q_ref[0]
        q2_ref[tq:, :] = q_ref[1]
    acc_ref[...] = jnp.zeros(acc_ref.shape, F32)

    n_chunks = seq // tk

    def scores(c, slot):
        start = pl.multiple_of(c * tk, tk)
        s = lax.dot_general(k_ref[pl.ds(start, tk), :], q2_ref[...], (((1,), (1,)), ((), ())),
                            preferred_element_type=F32)
        s_ref[slot] = s
        return jnp.max(s, axis=0, keepdims=True)

    def fold(c, slot, m_prev, m_chunk):
        m_new = jnp.maximum(m_prev, m_chunk)
        alpha = jnp.exp2(m_prev - m_new)
        p = jnp.exp2(s_ref[slot] - m_new)
        acc_ref[...] = alpha * acc_ref[...] + jnp.dot(vt_ref[c], p.astype(BF16),
                                                      preferred_element_type=F32)
        return m_new

    unroll = 4 if n_chunks % 4 == 0 else 2

    def group(base, carry, last):
        m, m_chunk = carry
        for u in range(unroll):
            m_next = m_chunk if (last and u == unroll - 1) else scores(base + u + 1, (u + 1) % 2)
            m = fold(base + u, u % 2, m, m_chunk)
            m_chunk = m_next
        return m, m_chunk

    carry = (jnp.full((1, nq), NEG_INF, F32), scores(0, 0))
    n_groups = n_chunks // unroll
    if n_groups > 1:
        carry = lax.fori_loop(0, n_groups - 1, lambda j, cc: group(j * unroll, cc, False), carry)
    group((n_groups - 1) * unroll, carry, True)

    acc = acc_ref[...]
    o = (acc[:LANES] / acc[LANES:LANES + 1]).T
    if diff:
        lp = lp_ref[...]
        lam = (jnp.exp(jnp.sum(lp[0:1] * lp[1:2], keepdims=True))
               - jnp.exp(jnp.sum(lp[2:3] * lp[3:4], keepdims=True)) + lam_init)
        dlt = o[:tq] - lam * o[tq:]
        ms = jnp.mean(dlt * dlt, axis=-1, keepdims=True)
        y = dlt * lax.rsqrt(ms + RMS_EPS) * sg_ref[...] * (1.0 - lam_init)
        o_ref[...] = (y * z_ref[...].astype(F32)).astype(BF16)
    else:
        o_ref[:, :LANES] = (o[:tq] * z_ref[0].astype(F32)).astype(BF16)
        o_ref[:, LANES:] = (o[tq:] * z_ref[1].astype(F32)).astype(BF16)


def _pair_attn(hr, hp, diff, lam_init=0.0, lam_params=None, subln_g=None, tq=512, tk=512):
    s = hr.shape[1]
    if diff:
        n_kv, qw, qblk = 4, LANES, (None, tq, LANES)
        q_map = lambda hh, i: (CB_AQ + hh, i, 0)
        k_map = lambda hh, i: (CB_AK + hh, 0, 0)
        v_map = lambda hh, i: (CB_AV + hh, 0, 0)
        z_map = lambda hh, i: (CB_AZ + hh, i, 0)
    else:
        n_kv, qw, qblk = 2, 2 * LANES, (2, tq, LANES)
        q_map = lambda hh, i: (CB_CQ // 2 + hh, i, 0)
        k_map = lambda hh, i: (CB_CK + hh, 0, 0)
        v_map = lambda hh, i: (CB_CV + hh, 0, 0)
        z_map = lambda hh, i: (CB_CZ // 2 + hh, i, 0)
    in_specs = [pl.BlockSpec(qblk, q_map),
                pl.BlockSpec((None, s, LANES), k_map),
                pl.BlockSpec((None, s, LANES), v_map),
                pl.BlockSpec(qblk, z_map)]
    args = [hr, hr, hp if diff else hr, hp]
    if diff:
        in_specs += [pl.BlockSpec((4, DA_QK), lambda hh, i: (0, 0)),
                     pl.BlockSpec((1, LANES), lambda hh, i: (0, 0))]
        args += [lam_params, subln_g.reshape(1, LANES)]
    return pl.pallas_call(
        functools.partial(_pair_attn_kernel, diff, tq, tk, lam_init),
        grid=(n_kv, s // tq),
        in_specs=in_specs,
        out_specs=pl.BlockSpec((tq, qw), lambda hh, i: (i, hh)),
        out_shape=jax.ShapeDtypeStruct((s, BRANCH_W), BF16),
        scratch_shapes=[pltpu.VMEM((2 * tq, LANES), BF16),
                        pltpu.VMEM((s // tk, LANES + VT_PAD, tk), BF16),
                        pltpu.VMEM((LANES + VT_PAD, 2 * tq), F32),
                        pltpu.VMEM((2, tk, 2 * tq), F32)],
        compiler_params=_cparams(("arbitrary", "arbitrary")),
        name="diff_attn" if diff else "axial_gqa",
    )(*args)


def _nat_bias(rpb):
    n_layers, n_heads = rpb.shape[:2]
    cols = jnp.arange(GRID_W)
    c0 = jnp.clip(cols - NB_COLS // 2, 0, GRID_W - NB_COLS)
    kc = jnp.arange(GRID_W)
    valid = (kc[None, :] >= c0[:, None]) & (kc[None, :] < c0[:, None] + NB_COLS)
    lpad = GRID_W - NB_COLS
    padded = jnp.pad(rpb.astype(F32), ((0, 0), (0, 0), (0, 0), (lpad, lpad)))
    toep = jnp.stack([padded[..., GRID_W - 1 - c:2 * GRID_W - 1 - c] for c in range(GRID_W)], axis=-2)
    t = jnp.where(valid, toep, NEG_INF)
    out = []
    for ds in range(NB_ROWS):
        w = t[:, :, ds:ds + NB_ROWS]
        out.append(jnp.transpose(w, (0, 1, 3, 2, 4)).reshape(n_layers, n_heads, GRID_W, NB_ROWS * GRID_W))
    return jnp.stack(out, axis=1)


def _nat_kernel(rows, rps, q_ref, k_ref, v_ref, z_ref, bias_ref, o_ref):
    win = NB_ROWS * GRID_W
    starts, scores = [], []
    for u in range(rps):
        r = pl.program_id(1) * rps + u
        r0 = jnp.clip(r - NB_ROWS // 2, 0, rows - NB_ROWS)
        start = pl.multiple_of(r0 * GRID_W, GRID_W)
        kw = k_ref[pl.ds(start, win), :]
        s = lax.dot_general(q_ref[u * GRID_W:(u + 1) * GRID_W, :], kw, (((1,), (1,)), ((), ())),
                            preferred_element_type=F32)
        scores.append(s * (HEAD_DIM ** -0.5) + bias_ref[r0 - r + (NB_ROWS - 1)])
        starts.append(start)
    probs = []
    for s in scores:
        p = jnp.exp(s - jnp.max(s, axis=-1, keepdims=True))
        probs.append((p.astype(BF16), jnp.sum(p, axis=-1, keepdims=True)))
    for u, (p, l) in enumerate(probs):
        qrows = slice(u * GRID_W, (u + 1) * GRID_W)
        o = jnp.dot(p, v_ref[pl.ds(starts[u], win), :], preferred_element_type=F32) / l
        o_ref[qrows, :] = (o * z_ref[qrows, :].astype(F32)).astype(BF16)


def _nat_attn(hp, bias, layer, rps=8):
    s = hp.shape[1]
    rows = s // GRID_W
    blk = rps * GRID_W
    return pl.pallas_call(
        functools.partial(_nat_kernel, rows, rps),
        grid=(4, rows // rps),
        in_specs=[pl.BlockSpec((None, blk, LANES), lambda hh, r: (CB_BQ + hh, r, 0)),
                  pl.BlockSpec((None, s, LANES), lambda hh, r: (CB_BK + hh, 0, 0)),
                  pl.BlockSpec((None, s, LANES), lambda hh, r: (CB_BV + hh, 0, 0)),
                  pl.BlockSpec((None, blk, LANES), lambda hh, r: (CB_BZ + hh, r, 0)),
                  pl.BlockSpec((None, NB_ROWS, None, GRID_W, NB_ROWS * GRID_W),
                               lambda hh, r: (layer, 0, hh, 0, 0))],
        out_specs=pl.BlockSpec((blk, LANES), lambda hh, r: (r, hh)),
        out_shape=jax.ShapeDtypeStruct((s, BRANCH_W), BF16),
        compiler_params=_cparams(("arbitrary", "arbitrary")),
        name="nat_attn",
    )(hp, hp, hp, hp, bias)


def _dil_masks(tq):
    row = np.arange(tq)[:, None]

    def band(col, rate):
        d = col - row
        return np.where((d >= 0) & (d <= 2 * DIL_SIDE * rate) & (d % rate == 0), 0.0, NEG_INF).astype(np.float32)

    width0 = tq + 2 * DIL_SIDE
    tables = [np.stack([band(np.arange(width0)[None, :] + dlt, 1) for dlt in (DIL_SIDE, 0, -DIL_SIDE)])]
    for rate in DIL_RATES[1:]:
        reach = DIL_SIDE * rate
        cols = np.arange(-reach, tq + 3 * reach)[None, :]
        full = band(cols, rate)
        tables.append(full.reshape(tq, -1, LANES).transpose(1, 0, 2))
    return [jnp.asarray(t) for t in tables]


def _dil_kernel(tq, nsub, q0_ref, q1_ref, q2_ref, k0_ref, k1_ref, k2_ref, v0_ref, v1_ref, v2_ref,
                z_ref, b0_ref, b1_ref, b2_ref, o_ref):
    seq = k0_ref.shape[0]
    q_refs, k_refs, v_refs = (q0_ref, q1_ref, q2_ref), (k0_ref, k1_ref, k2_ref), (v0_ref, v1_ref, v2_ref)
    b_refs = (b0_ref, b1_ref, b2_ref)
    scores, wins = [], []
    for u in range(nsub):
        t0 = (pl.program_id(1) * nsub + u) * tq
        rows = slice(u * tq, (u + 1) * tq)
        su, wu = [], []
        for rate, q_ref, k_ref, b_ref in zip(DIL_RATES, q_refs, k_refs, b_refs):
            reach = DIL_SIDE * rate
            width = tq + 2 * reach
            start = pl.multiple_of(jnp.clip(t0 - reach, 0, seq - width), DIL_SIDE)
            delta = start - t0 + reach
            s = lax.dot_general(q_ref[rows, :], k_ref[pl.ds(start, width), :], (((1,), (1,)), ((), ())),
                                preferred_element_type=F32)
            if rate == 1:
                s = s + b_ref[jnp.where(delta > 0, 0, jnp.where(delta < 0, 2, 1))]
            else:
                tile0 = (delta + reach) // LANES
                s = jnp.concatenate([s[:, j * LANES:(j + 1) * LANES] + b_ref[tile0 + j]
                                     for j in range(width // LANES)], axis=1)
            su.append(s)
            wu.append((start, width))
        scores.append(su)
        wins.append(wu)
    probs, dens = [], []
    for su in scores:
        m = su[0].max(axis=-1, keepdims=True)
        for s in su[1:]:
            m = jnp.maximum(m, s.max(axis=-1, keepdims=True))
        pu = [jnp.exp2(s - m) for s in su]
        l = pu[0].sum(axis=-1, keepdims=True)
        for p in pu[1:]:
            l = l + p.sum(axis=-1, keepdims=True)
        probs.append([p.astype(BF16) for p in pu])
        dens.append(l)
    for u, (pu, wu, l) in enumerate(zip(probs, wins, dens)):
        rows = slice(u * tq, (u + 1) * tq)
        o = None
        for p, (start, width), v_ref in zip(pu, wu, v_refs):
            pv = jnp.dot(p, v_ref[pl.ds(start, width), :], preferred_element_type=F32)
            o = pv if o is None else o + pv
        o_ref[rows, :] = (o / l * z_ref[rows, :].astype(F32)).astype(BF16)


def _dil_attn(hr, hp, tq=128, nsub=2):
    s = hr.shape[1]
    masks = _dil_masks(tq)
    m_specs = [pl.BlockSpec(m.shape, lambda hh, i: (0, 0, 0)) for m in masks]
    blk = tq * nsub
    qs = [pl.BlockSpec((None, blk, LANES), functools.partial(lambda g, hh, i: (CB_DQ + 4 * g + hh, i, 0), g))
          for g in range(3)]
    ks = [pl.BlockSpec((None, s, LANES), functools.partial(lambda g, hh, i: (CB_DK + 4 * g + hh, 0, 0), g))
          for g in range(3)]
    vs = [pl.BlockSpec((None, s, LANES), functools.partial(lambda g, hh, i: (CB_DV + 4 * g + hh, 0, 0), g))
          for g in range(3)]
    return pl.pallas_call(
        functools.partial(_dil_kernel, tq, nsub),
        grid=(4, s // blk),
        in_specs=qs + ks + vs + [pl.BlockSpec((None, blk, LANES), lambda hh, i: (CB_DZ + hh, i, 0))] + m_specs,
        out_specs=pl.BlockSpec((blk, LANES), lambda hh, i: (i, hh)),
        out_shape=jax.ShapeDtypeStruct((s, BRANCH_W), BF16),
        compiler_params=_cparams(("arbitrary", "arbitrary")),
        name="dil_attn",
    )(*([hr] * 6), *([hp] * 4), *masks)


def _merge_kernel(ya_ref, yb_ref, yc_ref, yd_ref, ga_ref, gb_ref, gc_ref, gd_ref, wb_ref, o_ref, wbf_ref):
    @pl.when(pl.program_id(1) == 0)
    def _():
        wbf_ref[...] = wb_ref[...].astype(BF16)

    acc = None
    for n, (y_ref, g_ref) in enumerate(zip((ya_ref, yb_ref, yc_ref, yd_ref),
                                           (ga_ref, gb_ref, gc_ref, gd_ref))):
        proj = jnp.dot(y_ref[...], wbf_ref[n], preferred_element_type=F32)
        term = proj * g_ref[...].astype(F32)
        acc = term if acc is None else acc + term
    o_ref[...] = acc.astype(BF16)


def _merge(ys, hg, w_branch, layer, tm=1024):
    s = hg.shape[0]
    tm = math.gcd(tm, s)
    n_col = D_MODEL // TN
    y_specs = [pl.BlockSpec((tm, BRANCH_W), lambda j, i: (i, 0)) for _ in range(N_BRANCH)]
    g_specs = [pl.BlockSpec((tm, TN), functools.partial(lambda n, j, i: (i, n_col * n + j), n))
               for n in range(N_BRANCH)]
    return pl.pallas_call(
        _merge_kernel,
        grid=(n_col, s // tm),
        in_specs=y_specs + g_specs + [pl.BlockSpec((None, N_BRANCH, BRANCH_W, TN),
                                                   lambda j, i: (layer, 0, 0, j))],
        out_specs=pl.BlockSpec((tm, TN), lambda j, i: (i, j)),
        out_shape=jax.ShapeDtypeStruct((s, D_MODEL), BF16),
        scratch_shapes=[pltpu.VMEM((N_BRANCH, BRANCH_W, TN), BF16)],
        compiler_params=_cparams(("arbitrary", "arbitrary")),
        name="branch_merge",
    )(*ys, hg, hg, hg, hg, w_branch)


def _out_ln_kernel(alpha, m_ref, w_ref, x_ref, g_ref, b_ref, y_ref, ybf_ref):
    out = jnp.dot(m_ref[...], w_ref[...], preferred_element_type=F32)
    y = _ln_rows(alpha * x_ref[...] + out, g_ref[...], b_ref[...])
    y_ref[...] = y
    ybf_ref[...] = y.astype(BF16)


def _out_ln(merged, wo_bf, layer, x, g, b, alpha, tm=512):
    s, d = x.shape
    return pl.pallas_call(
        functools.partial(_out_ln_kernel, alpha),
        grid=(s // tm,),
        in_specs=[pl.BlockSpec((tm, d), lambda i: (i, 0)),
                  pl.BlockSpec((None, d, d), lambda i: (layer, 0, 0)),
                  pl.BlockSpec((tm, d), lambda i: (i, 0)),
                  pl.BlockSpec((1, d), lambda i: (0, 0)),
                  pl.BlockSpec((1, d), lambda i: (0, 0))],
        out_specs=[pl.BlockSpec((tm, d), lambda i: (i, 0)),
                   pl.BlockSpec((tm, d), lambda i: (i, 0))],
        out_shape=[jax.ShapeDtypeStruct((s, d), F32), jax.ShapeDtypeStruct((s, d), BF16)],
        compiler_params=_cparams(("arbitrary",)),
        name="out_ln",
    )(merged, wo_bf, x, g.reshape(1, d), b.reshape(1, d))


def kernel(x, emb_ln_g, emb_ln_b, w_in, b_gate, diff_lambda, diff_subln_g, nat_rpb,
           gqa_q_norm_g, gqa_k_norm_g, w_branch, w_out, ln_g, ln_b):
    batch, seq, d = x.shape
    assert batch == 1 and d == D_MODEL and w_in.shape[-1] == D_IN
    depth = w_in.shape[0]
    alpha = (2 * depth) ** 0.25
    tabs = _rope_tables(seq)
    nat_bias = _nat_bias(nat_rpb)
    wo_bf = w_out.astype(BF16)
    xf, xbf = _embed_ln(x[0], emb_ln_g, emb_ln_b)
    for l in range(depth):
        lam_init = 0.8 - 0.6 * math.exp(-0.3 * l)
        gains = jnp.stack([gqa_q_norm_g[l], gqa_k_norm_g[l]]).reshape(2, 1, LANES)
        hr, hp, hg = _inproj(xbf, w_in, b_gate, l, tabs, gains)
        ya = _pair_attn(hr, hp, True, lam_init, diff_lambda[l], diff_subln_g[l])
        yb = _nat_attn(hp, nat_bias, l)
        yc = _pair_attn(hr, hp, False)
        yd = _dil_attn(hr, hp)
        merged = _merge((ya, yb, yc, yd), hg, w_branch, l)
        xf, xbf = _out_ln(merged, wo_bf, l, xf, ln_g[l], ln_b[l], alpha)
    return xf[None]
```

```python
import functools
import math

import numpy as np
import jax
import jax.numpy as jnp
from jax import lax
from jax.experimental import pallas as pl
from jax.experimental.pallas import tpu as pltpu

F32 = jnp.float32
BF16 = jnp.bfloat16

D_MODEL = 2048
GRID_W = 64
HEAD_DIM = 128
N_BRANCH = 4
BRANCH_W = D_MODEL // 4
DA_QK = HEAD_DIM // 2
NB_ROWS = 8
NB_COLS = 16
AXIAL_THETA = 10000.0
DIL_RATES = (1, 4, 16)
DIL_SIDE = 64
ROPE_THETA = 500000.0
LN_EPS = 1e-5
RMS_EPS = 1e-6
NEG_INF = -1e30
LOG2E = math.log2(math.e)

LANES = 128
VT_PAD = 16
TN = 512
D_IN = 18944

ROT_TILES = (0, 1, 8, 9, 11, 12, 13, 14, 15, 16)
PLAIN_TILES = (2, 3, 4, 5, 6, 7, 10, 17, 18, 19, 20)
GATE_TILE0, N_GATE_TILES = 21, 16
ROT_VARIANT = (0, 1, 2, 3, 4, 4, 4, 5, 5, 5)
CB_AQ, CB_AK, CB_CQ, CB_CK, CB_CV, CB_DQ, CB_DK = 0, 4, 8, 12, 14, 16, 28
CB_AV, CB_AZ, CB_BQ, CB_BK, CB_BV, CB_BZ, CB_CZ, CB_DV, CB_DZ = 0, 4, 8, 12, 16, 20, 24, 28, 40
PLAIN_SILU = (1, 5, 6, 10)

VMEM_LIMIT = 56 * 1024 * 1024


def _cparams(sem):
    return pltpu.CompilerParams(dimension_semantics=sem, vmem_limit_bytes=VMEM_LIMIT)


def _select(j, values):
    out = values[-1]
    for t in range(len(values) - 2, -1, -1):
        out = jnp.where(j == t, values[t], out)
    return out


def _any_of(j, tiles):
    c = j == tiles[0]
    for t in tiles[1:]:
        c = c | (j == t)
    return c


def _ln_rows(x, g, b):
    mu = jnp.mean(x, axis=-1, keepdims=True)
    xc = x - mu
    var = jnp.mean(xc * xc, axis=-1, keepdims=True)
    return xc * lax.rsqrt(var + LN_EPS) * g + b


def _embed_ln_kernel(x_ref, g_ref, b_ref, y_ref, ybf_ref):
    y = _ln_rows(x_ref[...], g_ref[...], b_ref[...])
    y_ref[...] = y
    ybf_ref[...] = y.astype(BF16)


def _embed_ln(x, g, b, tm=512):
    s, d = x.shape
    return pl.pallas_call(
        _embed_ln_kernel,
        grid=(s // tm,),
        in_specs=[pl.BlockSpec((tm, d), lambda i: (i, 0)),
                  pl.BlockSpec((1, d), lambda i: (0, 0)),
                  pl.BlockSpec((1, d), lambda i: (0, 0))],
        out_specs=[pl.BlockSpec((tm, d), lambda i: (i, 0)),
                   pl.BlockSpec((tm, d), lambda i: (i, 0))],
        out_shape=[jax.ShapeDtypeStruct((s, d), F32), jax.ShapeDtypeStruct((s, d), BF16)],
        compiler_params=_cparams(("arbitrary",)),
        name="embed_ln",
    )(x, g.reshape(1, d), b.reshape(1, d))


def _rope_tables(seq):
    t = np.arange(seq)

    def cs(pos, dim, theta):
        half = dim // 2
        inv = np.power(float(theta), -np.arange(half, dtype=np.float64) * 2.0 / dim)
        ang = pos.astype(np.float64)[:, None] * inv[None, :]
        return jnp.asarray(np.cos(ang), F32), jnp.asarray(np.sin(ang), F32)

    def group(cos, sin, width):
        half = cos.shape[1]
        pad = width - 2 * half
        one = jnp.ones((seq, pad), F32)
        zero = jnp.zeros((seq, pad), F32)
        zh = jnp.zeros((seq, half), F32)
        return (jnp.concatenate([cos, cos, one], 1),
                jnp.concatenate([zh, sin, zero], 1),
                jnp.concatenate([-sin, zh, zero], 1))

    ca, sa = cs(t, DA_QK // 4, ROPE_THETA)
    va = jnp.stack([jnp.concatenate([p, p], 1) for p in group(ca, sa, DA_QK)])
    cr, sr = cs(t // GRID_W, HEAD_DIM // 2, AXIAL_THETA)
    cc, sc = cs(t % GRID_W, HEAD_DIM // 2, AXIAL_THETA)
    vc = jnp.stack([jnp.concatenate([p, q], 1)
                    for p, q in zip(group(cr, sr, HEAD_DIM // 2), group(cc, sc, HEAD_DIM // 2))])
    cd, sd = cs(t, HEAD_DIM // 4, ROPE_THETA)
    vd = jnp.stack(group(cd, sd, HEAD_DIM))
    qa = DA_QK ** -0.5 * LOG2E
    qh = HEAD_DIM ** -0.5 * LOG2E
    return jnp.stack([va * qa, va, vc * qh, vc, vd * qh, vd])


ROT_HALF_A, ROT_HALF_C, ROT_HALF_D = DA_QK // 8, HEAD_DIM // 4, HEAD_DIM // 8


def _proj_kernel(tsub, n_sub, make_branches, n_extra, x_ref, w_ref, *rest):
    extra = rest[:n_extra]
    o_ref, wbf_ref, acc_ref = rest[n_extra:]
    j = pl.program_id(0)
    n_pass = x_ref.shape[0] // (n_sub * tsub)

    @pl.when(pl.program_id(1) == 0)
    def _():
        wbf_ref[...] = w_ref[...].astype(BF16)

    def pipeline(epilogue):
        def one_pass(h, carry):
            def rows_of(u):
                return pl.ds(pl.multiple_of((h * n_sub + u) * tsub, tsub), tsub)

            def matmul(u):
                acc_ref[u % 2] = jnp.dot(x_ref[rows_of(u), :], wbf_ref[...],
                                         preferred_element_type=F32)

            matmul(0)
            for u in range(n_sub):
                if u + 1 < n_sub:
                    matmul(u + 1)
                for g in range(TN // LANES):
                    lanes = slice(g * LANES, (g + 1) * LANES)
                    y = epilogue(acc_ref[u % 2, :, lanes], rows_of(u), g).astype(BF16)
                    if len(o_ref.shape) == 3:
                        o_ref[g, rows_of(u), :] = y
                    else:
                        o_ref[rows_of(u), lanes] = y
            return carry

        lax.fori_loop(0, n_pass, one_pass, 0)

    for cond, epilogue in make_branches(*extra):
        if cond is None:
            pipeline(epilogue)
        else:
            pl.when(cond(j))(functools.partial(pipeline, epilogue))


def _proj(name, xbf, w_in, layer, src_tiles, make_branches, extra, extra_specs, head_major,
          tm=2048, tsub=256, n_sub=4):
    s, d = xbf.shape
    tm = math.gcd(tm, s)
    n_sub = min(n_sub, tm // tsub)
    assert tm % (tsub * n_sub) == 0
    n_tiles = len(src_tiles)
    contiguous = src_tiles == tuple(range(src_tiles[0], src_tiles[0] + n_tiles))
    src = (lambda j: src_tiles[0] + j) if contiguous else (lambda j: _select(j, src_tiles))
    if head_major:
        out_spec = pl.BlockSpec((TN // LANES, tm, LANES), lambda j, i: (j, i, 0))
        out_shape = jax.ShapeDtypeStruct((n_tiles * (TN // LANES), s, LANES), BF16)
    else:
        out_spec = pl.BlockSpec((tm, TN), lambda j, i: (i, j))
        out_shape = jax.ShapeDtypeStruct((s, n_tiles * TN), BF16)
    return pl.pallas_call(
        functools.partial(_proj_kernel, tsub, n_sub, make_branches, len(extra)),
        grid=(n_tiles, s // tm),
        in_specs=[pl.BlockSpec((tm, d), lambda j, i: (i, 0)),
                  pl.BlockSpec((None, d, TN), lambda j, i: (layer, 0, src(j)))] + extra_specs(tm),
        out_specs=out_spec,
        out_shape=out_shape,
        scratch_shapes=[pltpu.VMEM((d, TN), BF16), pltpu.VMEM((2, tsub, TN), F32)],
        compiler_params=_cparams(("arbitrary", "arbitrary")),
        name=name,
    )(xbf, w_in, *extra)


def _rot_branches(tab_ref, gn_ref):
    def rot(a, rows, half):
        return (a * tab_ref[0, rows, :] + pltpu.roll(a, half, 1) * tab_ref[1, rows, :]
                + pltpu.roll(a, LANES - half, 1) * tab_ref[2, rows, :])

    def rms(a, gain):
        ms = jnp.mean(a * a, axis=-1, keepdims=True)
        return a * lax.rsqrt(ms + RMS_EPS) * gain

    return [(lambda j: j <= 1, lambda a, rows, g: rot(a, rows, ROT_HALF_A)),
            (lambda j: j == 2, lambda a, rows, g: rot(rms(a, gn_ref[0]), rows, ROT_HALF_C)),
            (lambda j: j == 3, lambda a, rows, g: rot(rms(a, gn_ref[1]), rows, ROT_HALF_C) if g < 2 else a),
            (lambda j: j >= 4, lambda a, rows, g: rot(a, rows, ROT_HALF_D))]


def _plain_branches():
    return [(lambda j: _any_of(j, PLAIN_SILU), lambda a, rows, g: a * jax.nn.sigmoid(a)),
            (lambda j: jnp.logical_not(_any_of(j, PLAIN_SILU)), lambda a, rows, g: a)]


def _gate_branches(bg_ref):
    return [(None, lambda a, rows, g: jax.nn.sigmoid(a + bg_ref[:, g * LANES:(g + 1) * LANES]))]


def _inproj(xbf, w_in, b_gate, layer, tabs, gains):
    hr = _proj("inproj_rot", xbf, w_in, layer, ROT_TILES, _rot_branches, (tabs, gains),
               lambda tm: [pl.BlockSpec((None, 3, tm, LANES), lambda j, i: (_select(j, ROT_VARIANT), 0, i, 0)),
                           pl.BlockSpec((2, 1, LANES), lambda j, i: (0, 0, 0))], True)
    hp = _proj("inproj_plain", xbf, w_in, layer, PLAIN_TILES, _plain_branches, (), lambda tm: [], True,
               n_sub=8)
    hg = _proj("inproj_gate", xbf, w_in, layer, tuple(range(GATE_TILE0, GATE_TILE0 + N_GATE_TILES)),
               _gate_branches, (b_gate.reshape(b_gate.shape[0], 1, -1),),
               lambda tm: [pl.BlockSpec((None, 1, TN), lambda j, i: (layer, 0, j))], False, n_sub=8)
    return hr, hp, hg


def _pair_attn_kernel(diff, tq, tk, lam_init, *refs):
    if diff:
        q_ref, k_ref, v_ref, z_ref, lp_ref, sg_ref, o_ref, q2_ref, vt_ref, acc_ref, s_ref = refs
    else:
        q_ref, k_ref, v_ref, z_ref, o_ref, q2_ref, vt_ref, acc_ref, s_ref = refs
    seq = k_ref.shape[0]
    nq = 2 * tq

    @pl.when(pl.program_id(1) == 0)
    def _():
        for c in range(seq // tk):
            vt_ref[c, :LANES, :] = v_ref[c * tk:(c + 1) * tk, :].astype(F32).T.astype(BF16)
            vt_ref[c, LANES:, :] = jnp.ones((VT_PAD, tk), BF16)

    if diff:
        q = q_ref[...]
        lane = lax.broadcasted_iota(jnp.int32, q.shape, 1)
        zero = jnp.zeros_like(q)
        q2_ref[:tq, :] = jnp.where(lane < DA_QK, q, zero)
        q2_ref[tq:, :] = jnp.where(lane >= DA_QK, q, zero)
    else:
        q2_ref[:tq, :] = q_ref[0]
        q2_ref[tq:, :] = q_ref[1]
    acc_ref[...] = jnp.zeros(acc_ref.shape, F32)

    n_chunks = seq // tk

    def scores(c, slot):
        start = pl.multiple_of(c * tk, tk)
        s = lax.dot_general(k_ref[pl.ds(start, tk), :], q2_ref[...], (((1,), (1,)), ((), ())),
                            preferred_element_type=F32)
        s_ref[slot] = s
        return jnp.max(s, axis=0, keepdims=True)

    def fold(c, slot, m_prev, m_chunk):
        m_new = jnp.maximum(m_prev, m_chunk)
        alpha = jnp.exp2(m_prev - m_new)
        p = jnp.exp2(s_ref[slot] - m_new)
        acc_ref[...] = alpha * acc_ref[...] + jnp.dot(vt_ref[c], p.astype(BF16),
                                                      preferred_element_type=F32)
        return m_new

    unroll = 4 if n_chunks % 4 == 0 else 2

    def group(base, carry, last):
        m, m_chunk = carry
        for u in range(unroll):
            m_next = m_chunk if (last and u == unroll - 1) else scores(base + u + 1, (u + 1) % 2)
            m = fold(base + u, u % 2, m, m_chunk)
            m_chunk = m_next
        return m, m_chunk

    carry = (jnp.full((1, nq), NEG_INF, F32), scores(0, 0))
    n_groups = n_chunks // unroll
    if n_groups > 1:
        carry = lax.fori_loop(0, n_groups - 1, lambda j, cc: group(j * unroll, cc, False), carry)
    group((n_groups - 1) * unroll, carry, True)

    acc = acc_ref[...]
    o = (acc[:LANES] / acc[LANES:LANES + 1]).T
    if diff:
        lp = lp_ref[...]
        lam = (jnp.exp(jnp.sum(lp[0:1] * lp[1:2], keepdims=True))
               - jnp.exp(jnp.sum(lp[2:3] * lp[3:4], keepdims=True)) + lam_init)
        dlt = o[:tq] - lam * o[tq:]
        ms = jnp.mean(dlt * dlt, axis=-1, keepdims=True)
        y = dlt * lax.rsqrt(ms + RMS_EPS) * sg_ref[...] * (1.0 - lam_init)
        o_ref[...] = (y * z_ref[...].astype(F32)).astype(BF16)
    else:
        o_ref[:, :LANES] = (o[:tq] * z_ref[0].astype(F32)).astype(BF16)
        o_ref[:, LANES:] = (o[tq:] * z_ref[1].astype(F32)).astype(BF16)


def _pair_attn(hr, hp, diff, lam_init=0.0, lam_params=None, subln_g=None, tq=512, tk=512):
    s = hr.shape[1]
    if diff:
        n_kv, qw, qblk = 4, LANES, (None, tq, LANES)
        q_map = lambda hh, i: (CB_AQ + hh, i, 0)
        k_map = lambda hh, i: (CB_AK + hh, 0, 0)
        v_map = lambda hh, i: (CB_AV + hh, 0, 0)
        z_map = lambda hh, i: (CB_AZ + hh, i, 0)
    else:
        n_kv, qw, qblk = 2, 2 * LANES, (2, tq, LANES)
        q_map = lambda hh, i: (CB_CQ // 2 + hh, i, 0)
        k_map = lambda hh, i: (CB_CK + hh, 0, 0)
        v_map = lambda hh, i: (CB_CV + hh, 0, 0)
        z_map = lambda hh, i: (CB_CZ // 2 + hh, i, 0)
    in_specs = [pl.BlockSpec(qblk, q_map),
                pl.BlockSpec((None, s, LANES), k_map),
                pl.BlockSpec((None, s, LANES), v_map),
                pl.BlockSpec(qblk, z_map)]
    args = [hr, hr, hp if diff else hr, hp]
    if diff:
        in_specs += [pl.BlockSpec((4, DA_QK), lambda hh, i: (0, 0)),
                     pl.BlockSpec((1, LANES), lambda hh, i: (0, 0))]
        args += [lam_params, subln_g.reshape(1, LANES)]
    return pl.pallas_call(
        functools.partial(_pair_attn_kernel, diff, tq, tk, lam_init),
        grid=(n_kv, s // tq),
        in_specs=in_specs,
        out_specs=pl.BlockSpec((tq, qw), lambda hh, i: (i, hh)),
        out_shape=jax.ShapeDtypeStruct((s, BRANCH_W), BF16),
        scratch_shapes=[pltpu.VMEM((2 * tq, LANES), BF16),
                        pltpu.VMEM((s // tk, LANES + VT_PAD, tk), BF16),
                        pltpu.VMEM((LANES + VT_PAD, 2 * tq), F32),
                        pltpu.VMEM((2, tk, 2 * tq), F32)],
        compiler_params=_cparams(("arbitrary", "arbitrary")),
        name="diff_attn" if diff else "axial_gqa",
    )(*args)


def _nat_bias(rpb):
    n_layers, n_heads = rpb.shape[:2]
    cols = jnp.arange(GRID_W)
    c0 = jnp.clip(cols - NB_COLS // 2, 0, GRID_W - NB_COLS)
    kc = jnp.arange(GRID_W)
    valid = (kc[None, :] >= c0[:, None]) & (kc[None, :] < c0[:, None] + NB_COLS)
    lpad = GRID_W - NB_COLS
    padded = jnp.pad(rpb.astype(F32), ((0, 0), (0, 0), (0, 0), (lpad, lpad)))
    toep = jnp.stack([padded[..., GRID_W - 1 - c:2 * GRID_W - 1 - c] for c in range(GRID_W)], axis=-2)
    t = jnp.where(valid, toep, NEG_INF)
    out = []
    for ds in range(NB_ROWS):
        w = t[:, :, ds:ds + NB_ROWS]
        out.append(jnp.transpose(w, (0, 1, 3, 2, 4)).reshape(n_layers, n_heads, GRID_W, NB_ROWS * GRID_W))
    return jnp.stack(out, axis=1)


def _nat_kernel(rows, rps, q_ref, k_ref, v_ref, z_ref, bias_ref, o_ref):
    win = NB_ROWS * GRID_W
    starts, scores = [], []
    for u in range(rps):
        r = pl.program_id(1) * rps + u
        r0 = jnp.clip(r - NB_ROWS // 2, 0, rows - NB_ROWS)
        start = pl.multiple_of(r0 * GRID_W, GRID_W)
        kw = k_ref[pl.ds(start, win), :]
        s = lax.dot_general(q_ref[u * GRID_W:(u + 1) * GRID_W, :], kw, (((1,), (1,)), ((), ())),
                            preferred_element_type=F32)
        scores.append(s * (HEAD_DIM ** -0.5) + bias_ref[r0 - r + (NB_ROWS - 1)])
        starts.append(start)
    probs = []
    for s in scores:
        p = jnp.exp(s - jnp.max(s, axis=-1, keepdims=True))
        probs.append((p.astype(BF16), jnp.sum(p, axis=-1, keepdims=True)))
    for u, (p, l) in enumerate(probs):
        qrows = slice(u * GRID_W, (u + 1) * GRID_W)
        o = jnp.dot(p, v_ref[pl.ds(starts[u], win), :], preferred_element_type=F32) / l
        o_ref[qrows, :] = (o * z_ref[qrows, :].astype(F32)).astype(BF16)


def _nat_attn(hp, bias, layer, rps=8):
    s = hp.shape[1]
    rows = s // GRID_W
    blk = rps * GRID_W
    return pl.pallas_call(
        functools.partial(_nat_kernel, rows, rps),
        grid=(4, rows // rps),
        in_specs=[pl.BlockSpec((None, blk, LANES), lambda hh, r: (CB_BQ + hh, r, 0)),
                  pl.BlockSpec((None, s, LANES), lambda hh, r: (CB_BK + hh, 0, 0)),
                  pl.BlockSpec((None, s, LANES), lambda hh, r: (CB_BV + hh, 0, 0)),
                  pl.BlockSpec((None, blk, LANES), lambda hh, r: (CB_BZ + hh, r, 0)),
                  pl.BlockSpec((None, NB_ROWS, None, GRID_W, NB_ROWS * GRID_W),
                               lambda hh, r: (layer, 0, hh, 0, 0))],
        out_specs=pl.BlockSpec((blk, LANES), lambda hh, r: (r, hh)),
        out_shape=jax.ShapeDtypeStruct((s, BRANCH_W), BF16),
        compiler_params=_cparams(("arbitrary", "arbitrary")),
        name="nat_attn",
    )(hp, hp, hp, hp, bias)


def _dil_masks(tq, rates):
    row = np.arange(tq)[:, None]

    def band(col, rate):
        d = col - row
        return np.where((d >= 0) & (d <= 2 * DIL_SIDE * rate) & (d % rate == 0), 0.0, NEG_INF).astype(np.float32)

    width0 = tq + 2 * DIL_SIDE
    tables = [np.stack([band(np.arange(width0)[None, :] + dlt, 1) for dlt in (DIL_SIDE, 0, -DIL_SIDE)])]
    assert rates[0] == 1
    for rate in rates[1:]:
        reach = DIL_SIDE * rate
        cols = np.arange(-reach, tq + 3 * reach)[None, :]
        full = band(cols, rate)
        tables.append(full.reshape(tq, -1, LANES).transpose(1, 0, 2))
    return [jnp.asarray(t) for t in tables]


def _dil_class_kernel(tq, q_ref, k_ref, v_ref, b_ref, o_ref, lse_ref):
    n_heads, length, _ = q_ref.shape
    width = tq + 2 * DIL_SIDE
    scored = []
    for hh in range(n_heads):
        for i0 in range(0, length, tq):
            ws = min(max(i0 - DIL_SIDE, 0), length - width)
            delta = ws - i0 + DIL_SIDE
            s = lax.dot_general(q_ref[hh, i0:i0 + tq, :], k_ref[hh, ws:ws + width, :],
                                (((1,), (1,)), ((), ())), preferred_element_type=F32)
            scored.append((hh, i0, ws, s + b_ref[0 if delta > 0 else (2 if delta < 0 else 1)]))
    stats = []
    for hh, i0, ws, s in scored:
        m = jnp.max(s, axis=-1, keepdims=True)
        p = jnp.exp2(s - m)
        stats.append((hh, i0, ws, p.astype(BF16), m, jnp.sum(p, axis=-1, keepdims=True)))
    for hh, i0, ws, p, m, l in stats:
        o_ref[hh, i0:i0 + tq, :] = jnp.dot(p, v_ref[hh, ws:ws + width, :], preferred_element_type=F32) / l
        lse_ref[hh, i0:i0 + tq, :] = jnp.broadcast_to(m + jnp.log2(l), (tq, LANES))


def _dil_kernel(tq, nsub, q0_ref, q1_ref, k0_ref, k1_ref, v0_ref, v1_ref, z_ref, b0_ref, b1_ref,
                o2_ref, lse2_ref, o_ref):
    seq = k0_ref.shape[0]
    q_refs, k_refs, v_refs, b_refs = (q0_ref, q1_ref), (k0_ref, k1_ref), (v0_ref, v1_ref), (b0_ref, b1_ref)
    scores, wins = [], []
    for u in range(nsub):
        t0 = (pl.program_id(1) * nsub + u) * tq
        rows = slice(u * tq, (u + 1) * tq)
        su, wu = [], []
        for rate, q_ref, k_ref, b_ref in zip(DIL_RATES, q_refs, k_refs, b_refs):
            reach = DIL_SIDE * rate
            width = tq + 2 * reach
            start = pl.multiple_of(jnp.clip(t0 - reach, 0, seq - width), DIL_SIDE)
            delta = start - t0 + reach
            s = lax.dot_general(q_ref[rows, :], k_ref[pl.ds(start, width), :], (((1,), (1,)), ((), ())),
                                preferred_element_type=F32)
            if rate == 1:
                s = s + b_ref[jnp.where(delta > 0, 0, jnp.where(delta < 0, 2, 1))]
            else:
                tile0 = (delta + reach) // LANES
                s = jnp.concatenate([s[:, j * LANES:(j + 1) * LANES] + b_ref[tile0 + j]
                                     for j in range(width // LANES)], axis=1)
            su.append(s)
            wu.append((start, width))
        scores.append(su)
        wins.append(wu)
    probs, dens, tops = [], [], []
    for su in scores:
        m = su[0].max(axis=-1, keepdims=True)
        for s in su[1:]:
            m = jnp.maximum(m, s.max(axis=-1, keepdims=True))
        pu = [jnp.exp2(s - m) for s in su]
        l = pu[0].sum(axis=-1, keepdims=True)
        for p in pu[1:]:
            l = l + p.sum(axis=-1, keepdims=True)
        probs.append([p.astype(BF16) for p in pu])
        dens.append(l)
        tops.append(m)
    for u, (pu, wu, l, m) in enumerate(zip(probs, wins, dens, tops)):
        rows = slice(u * tq, (u + 1) * tq)
        o = None
        for p, (start, width), v_ref in zip(pu, wu, v_refs):
            pv = jnp.dot(p, v_ref[pl.ds(start, width), :], preferred_element_type=F32)
            o = pv if o is None else o + pv
        lse2 = lse2_ref[rows, :1]
        top = jnp.maximum(m, lse2)
        w01 = jnp.exp2(m - top)
        w2 = jnp.exp2(lse2 - top)
        mixed = (w01 * o + w2 * o2_ref[rows, :]) / (w01 * l + w2)
        o_ref[rows, :] = (mixed * z_ref[rows, :].astype(F32)).astype(BF16)


def _dil_class_attn(hr, hp, mask, tq=128, heads_per_step=2):
    rate = DIL_RATES[-1]
    s = hr.shape[1]
    length = s // rate
    g = len(DIL_RATES) - 1
    hr3 = hr.reshape(hr.shape[0], length, rate * LANES)
    hp3 = hp.reshape(hp.shape[0], length, rate * LANES)
    blk = (heads_per_step, length, LANES)
    cbq, cbk, cbv = [(cb + 4 * g) // heads_per_step for cb in (CB_DQ, CB_DK, CB_DV)]
    o, lse = pl.pallas_call(
        functools.partial(_dil_class_kernel, tq),
        grid=(4 // heads_per_step, rate),
        in_specs=[pl.BlockSpec(blk, lambda p, a: (cbq + p, 0, a)),
                  pl.BlockSpec(blk, lambda p, a: (cbk + p, 0, a)),
                  pl.BlockSpec(blk, lambda p, a: (cbv + p, 0, a)),
                  pl.BlockSpec(mask.shape, lambda p, a: (0, 0, 0))],
        out_specs=[pl.BlockSpec(blk, lambda p, a: (p, 0, a)),
                   pl.BlockSpec(blk, lambda p, a: (p, 0, a))],
        out_shape=[jax.ShapeDtypeStruct((4, length, rate * LANES), F32),
                   jax.ShapeDtypeStruct((4, length, rate * LANES), F32)],
        compiler_params=_cparams(("arbitrary", "arbitrary")),
        name="dil_class_attn",
    )(hr3, hr3, hp3, mask)
    return o.reshape(4, s, LANES), lse.reshape(4, s, LANES)


def _dil_attn(hr, hp, tq=128, nsub=8):
    s = hr.shape[1]
    n_dense = len(DIL_RATES) - 1
    masks = _dil_masks(tq, DIL_RATES[:n_dense])
    o2, lse2 = _dil_class_attn(hr, hp, masks[0], tq)
    m_specs = [pl.BlockSpec(m.shape, lambda hh, i: (0, 0, 0)) for m in masks]
    blk = tq * nsub
    qs = [pl.BlockSpec((None, blk, LANES), functools.partial(lambda g, hh, i: (CB_DQ + 4 * g + hh, i, 0), g))
          for g in range(n_dense)]
    ks = [pl.BlockSpec((None, s, LANES), functools.partial(lambda g, hh, i: (CB_DK + 4 * g + hh, 0, 0), g))
          for g in range(n_dense)]
    vs = [pl.BlockSpec((None, s, LANES), functools.partial(lambda g, hh, i: (CB_DV + 4 * g + hh, 0, 0), g))
          for g in range(n_dense)]
    row_blk = pl.BlockSpec((None, blk, LANES), lambda hh, i: (hh, i, 0))
    return pl.pallas_call(
        functools.partial(_dil_kernel, tq, nsub),
        grid=(4, s // blk),
        in_specs=(qs + ks + vs + [pl.BlockSpec((None, blk, LANES), lambda hh, i: (CB_DZ + hh, i, 0))]
                  + m_specs + [row_blk, row_blk]),
        out_specs=pl.BlockSpec((blk, LANES), lambda hh, i: (i, hh)),
        out_shape=jax.ShapeDtypeStruct((s, BRANCH_W), BF16),
        compiler_params=_cparams(("arbitrary", "arbitrary")),
        name="dil_attn",
    )(*([hr] * (2 * n_dense)), *([hp] * (n_dense + 1)), *masks, o2, lse2)


def _merge_kernel(ya_ref, yb_ref, yc_ref, yd_ref, ga_ref, gb_ref, gc_ref, gd_ref, wb_ref, o_ref, wbf_ref):
    @pl.when(pl.program_id(1) == 0)
    def _():
        wbf_ref[...] = wb_ref[...].astype(BF16)

    acc = None
    for n, (y_ref, g_ref) in enumerate(zip((ya_ref, yb_ref, yc_ref, yd_ref),
                                           (ga_ref, gb_ref, gc_ref, gd_ref))):
        proj = jnp.dot(y_ref[...], wbf_ref[n], preferred_element_type=F32)
        term = proj * g_ref[...].astype(F32)
        acc = term if acc is None else acc + term
    o_ref[...] = acc.astype(BF16)


def _merge(ys, hg, w_branch, layer, tm=1024):
    s = hg.shape[0]
    tm = math.gcd(tm, s)
    n_col = D_MODEL // TN
    y_specs = [pl.BlockSpec((tm, BRANCH_W), lambda j, i: (i, 0)) for _ in range(N_BRANCH)]
    g_specs = [pl.BlockSpec((tm, TN), functools.partial(lambda n, j, i: (i, n_col * n + j), n))
               for n in range(N_BRANCH)]
    return pl.pallas_call(
        _merge_kernel,
        grid=(n_col, s // tm),
        in_specs=y_specs + g_specs + [pl.BlockSpec((None, N_BRANCH, BRANCH_W, TN),
                                                   lambda j, i: (layer, 0, 0, j))],
        out_specs=pl.BlockSpec((tm, TN), lambda j, i: (i, j)),
        out_shape=jax.ShapeDtypeStruct((s, D_MODEL), BF16),
        scratch_shapes=[pltpu.VMEM((N_BRANCH, BRANCH_W, TN), BF16)],
        compiler_params=_cparams(("arbitrary", "arbitrary")),
        name="branch_merge",
    )(*ys, hg, hg, hg, hg, w_branch)


def _out_ln_kernel(alpha, m_ref, w_ref, x_ref, g_ref, b_ref, y_ref, ybf_ref):
    out = jnp.dot(m_ref[...], w_ref[...], preferred_element_type=F32)
    y = _ln_rows(alpha * x_ref[...] + out, g_ref[...], b_ref[...])
    y_ref[...] = y
    ybf_ref[...] = y.astype(BF16)


def _out_ln(merged, wo_bf, layer, x, g, b, alpha, tm=512):
    s, d = x.shape
    return pl.pallas_call(
        functools.partial(_out_ln_kernel, alpha),
        grid=(s // tm,),
        in_specs=[pl.BlockSpec((tm, d), lambda i: (i, 0)),
                  pl.BlockSpec((None, d, d), lambda i: (layer, 0, 0)),
                  pl.BlockSpec((tm, d), lambda i: (i, 0)),
                  pl.BlockSpec((1, d), lambda i: (0, 0)),
                  pl.BlockSpec((1, d), lambda i: (0, 0))],
        out_specs=[pl.BlockSpec((tm, d), lambda i: (i, 0)),
                   pl.BlockSpec((tm, d), lambda i: (i, 0))],
        out_shape=[jax.ShapeDtypeStruct((s, d), F32), jax.ShapeDtypeStruct((s, d), BF16)],
        compiler_params=_cparams(("arbitrary",)),
        name="out_ln",
    )(merged, wo_bf, x, g.reshape(1, d), b.reshape(1, d))


def kernel(x, emb_ln_g, emb_ln_b, w_in, b_gate, diff_lambda, diff_subln_g, nat_rpb,
           gqa_q_norm_g, gqa_k_norm_g, w_branch, w_out, ln_g, ln_b):
    batch, seq, d = x.shape
    assert batch == 1 and d == D_MODEL and w_in.shape[-1] == D_IN
    depth = w_in.shape[0]
    alpha = (2 * depth) ** 0.25
    tabs = _rope_tables(seq)
    nat_bias = _nat_bias(nat_rpb)
    wo_bf = w_out.astype(BF16)
    xf, xbf = _embed_ln(x[0], emb_ln_g, emb_ln_b)
    for l in range(depth):
        lam_init = 0.8 - 0.6 * math.exp(-0.3 * l)
        gains = jnp.stack([gqa_q_norm_g[l], gqa_k_norm_g[l]]).reshape(2, 1, LANES)
        hr, hp, hg = _inproj(xbf, w_in, b_gate, l, tabs, gains)
        ya = _pair_attn(hr, hp, True, lam_init, diff_lambda[l], diff_subln_g[l])
        yb = _nat_attn(hp, nat_bias, l)
        yc = _pair_attn(hr, hp, False)
        yd = _dil_attn(hr, hp)
        merged = _merge((ya, yb, yc, yd), hg, w_branch, l)
        xf, xbf = _out_ln(merged, wo_bf, l, xf, ln_g[l], ln_b[l], alpha)
    return xf[None]
```

```python
import functools
import math

import numpy as np
import jax
import jax.numpy as jnp
from jax import lax
from jax.experimental import pallas as pl
from jax.experimental.pallas import tpu as pltpu

F32 = jnp.float32
BF16 = jnp.bfloat16

D_MODEL = 2048
GRID_W = 64
HEAD_DIM = 128
N_BRANCH = 4
BRANCH_W = D_MODEL // 4
DA_QK = HEAD_DIM // 2
NB_ROWS = 8
NB_COLS = 16
AXIAL_THETA = 10000.0
DIL_RATES = (1, 4, 16)
DIL_SIDE = 64
ROPE_THETA = 500000.0
LN_EPS = 1e-5
RMS_EPS = 1e-6
NEG_INF = -1e30
LOG2E = math.log2(math.e)

LANES = 128
VT_PAD = 16
TN = 512
D_IN = 18944

ROT_TILES = (0, 1, 8, 9, 11, 12, 14, 15)
PLAIN_TILES = (2, 3, 4, 5, 6, 7, 10, 17, 18, 20)
WIDE_TILES = (13, 16, 19)
GATE_TILE0, N_GATE_TILES = 21, 16
ROT_VARIANT = (0, 1, 2, 3, 4, 4, 5, 5)
ROT_VARIANT_DQ, ROT_VARIANT_DK = 4, 5
CB_AQ, CB_AK, CB_CQ, CB_CK, CB_CV, CB_DQ, CB_DK = 0, 4, 8, 12, 14, 16, 24
CB_AV, CB_AZ, CB_BQ, CB_BK, CB_BV, CB_BZ, CB_CZ, CB_DV, CB_DZ = 0, 4, 8, 12, 16, 20, 24, 28, 36
CB_WQ, CB_WK, CB_WV = 0, 4, 8
PLAIN_SILU = (1, 5, 6, 9)

VMEM_LIMIT = 56 * 1024 * 1024


def _cparams(sem):
    return pltpu.CompilerParams(dimension_semantics=sem, vmem_limit_bytes=VMEM_LIMIT)


def _select(j, values):
    out = values[-1]
    for t in range(len(values) - 2, -1, -1):
        out = jnp.where(j == t, values[t], out)
    return out


def _any_of(j, tiles):
    c = j == tiles[0]
    for t in tiles[1:]:
        c = c | (j == t)
    return c


def _ln_rows(x, g, b):
    mu = jnp.mean(x, axis=-1, keepdims=True)
    xc = x - mu
    var = jnp.mean(xc * xc, axis=-1, keepdims=True)
    return xc * lax.rsqrt(var + LN_EPS) * g + b


def _embed_ln_kernel(x_ref, g_ref, b_ref, y_ref, ybf_ref):
    y = _ln_rows(x_ref[...], g_ref[...], b_ref[...])
    y_ref[...] = y
    ybf_ref[...] = y.astype(BF16)


def _embed_ln(x, g, b, tm=512):
    s, d = x.shape
    return pl.pallas_call(
        _embed_ln_kernel,
        grid=(s // tm,),
        in_specs=[pl.BlockSpec((tm, d), lambda i: (i, 0)),
                  pl.BlockSpec((1, d), lambda i: (0, 0)),
                  pl.BlockSpec((1, d), lambda i: (0, 0))],
        out_specs=[pl.BlockSpec((tm, d), lambda i: (i, 0)),
                   pl.BlockSpec((tm, d), lambda i: (i, 0))],
        out_shape=[jax.ShapeDtypeStruct((s, d), F32), jax.ShapeDtypeStruct((s, d), BF16)],
        compiler_params=_cparams(("arbitrary",)),
        name="embed_ln",
    )(x, g.reshape(1, d), b.reshape(1, d))


def _rope_tables(seq):
    t = np.arange(seq)

    def cs(pos, dim, theta):
        half = dim // 2
        inv = np.power(float(theta), -np.arange(half, dtype=np.float64) * 2.0 / dim)
        ang = pos.astype(np.float64)[:, None] * inv[None, :]
        return jnp.asarray(np.cos(ang), F32), jnp.asarray(np.sin(ang), F32)

    def group(cos, sin, width):
        half = cos.shape[1]
        pad = width - 2 * half
        one = jnp.ones((seq, pad), F32)
        zero = jnp.zeros((seq, pad), F32)
        zh = jnp.zeros((seq, half), F32)
        return (jnp.concatenate([cos, cos, one], 1),
                jnp.concatenate([zh, sin, zero], 1),
                jnp.concatenate([-sin, zh, zero], 1))

    ca, sa = cs(t, DA_QK // 4, ROPE_THETA)
    va = jnp.stack([jnp.concatenate([p, p], 1) for p in group(ca, sa, DA_QK)])
    cr, sr = cs(t // GRID_W, HEAD_DIM // 2, AXIAL_THETA)
    cc, sc = cs(t % GRID_W, HEAD_DIM // 2, AXIAL_THETA)
    vc = jnp.stack([jnp.concatenate([p, q], 1)
                    for p, q in zip(group(cr, sr, HEAD_DIM // 2), group(cc, sc, HEAD_DIM // 2))])
    cd, sd = cs(t, HEAD_DIM // 4, ROPE_THETA)
    vd = jnp.stack(group(cd, sd, HEAD_DIM))
    qa = DA_QK ** -0.5 * LOG2E
    qh = HEAD_DIM ** -0.5 * LOG2E
    return jnp.stack([va * qa, va, vc * qh, vc, vd * qh, vd])


ROT_HALF_A, ROT_HALF_C, ROT_HALF_D = DA_QK // 8, HEAD_DIM // 4, HEAD_DIM // 8


def _proj_kernel(tsub, n_sub, make_branches, n_extra, x_ref, w_ref, *rest):
    extra = rest[:n_extra]
    o_ref, wbf_ref, acc_ref = rest[n_extra:]
    j = pl.program_id(0)
    n_pass = x_ref.shape[0] // (n_sub * tsub)

    @pl.when(pl.program_id(1) == 0)
    def _():
        wbf_ref[...] = w_ref[...].astype(BF16)

    def pipeline(epilogue):
        def one_pass(h, carry):
            def rows_of(u):
                return pl.ds(pl.multiple_of((h * n_sub + u) * tsub, tsub), tsub)

            def matmul(u):
                acc_ref[u % 2] = jnp.dot(x_ref[rows_of(u), :], wbf_ref[...],
                                         preferred_element_type=F32)

            matmul(0)
            for u in range(n_sub):
                if u + 1 < n_sub:
                    matmul(u + 1)
                for g in range(TN // LANES):
                    lanes = slice(g * LANES, (g + 1) * LANES)
                    y = epilogue(acc_ref[u % 2, :, lanes], rows_of(u), g).astype(o_ref.dtype)
                    if len(o_ref.shape) == 3:
                        o_ref[g, rows_of(u), :] = y
                    else:
                        o_ref[rows_of(u), lanes] = y
            return carry

        lax.fori_loop(0, n_pass, one_pass, 0)

    for cond, epilogue in make_branches(*extra):
        if cond is None:
            pipeline(epilogue)
        else:
            pl.when(cond(j))(functools.partial(pipeline, epilogue))


def _proj(name, xbf, w_in, layer, src_tiles, make_branches, extra, extra_specs, head_major,
          tm=2048, tsub=256, n_sub=4, dtype=BF16):
    s, d = xbf.shape
    tm = math.gcd(tm, s)
    n_sub = min(n_sub, tm // tsub)
    assert tm % (tsub * n_sub) == 0
    n_tiles = len(src_tiles)
    contiguous = src_tiles == tuple(range(src_tiles[0], src_tiles[0] + n_tiles))
    src = (lambda j: src_tiles[0] + j) if contiguous else (lambda j: _select(j, src_tiles))
    if head_major:
        out_spec = pl.BlockSpec((TN // LANES, tm, LANES), lambda j, i: (j, i, 0))
        out_shape = jax.ShapeDtypeStruct((n_tiles * (TN // LANES), s, LANES), dtype)
    else:
        out_spec = pl.BlockSpec((tm, TN), lambda j, i: (i, j))
        out_shape = jax.ShapeDtypeStruct((s, n_tiles * TN), dtype)
    return pl.pallas_call(
        functools.partial(_proj_kernel, tsub, n_sub, make_branches, len(extra)),
        grid=(n_tiles, s // tm),
        in_specs=[pl.BlockSpec((tm, d), lambda j, i: (i, 0)),
                  pl.BlockSpec((None, d, TN), lambda j, i: (layer, 0, src(j)))] + extra_specs(tm),
        out_specs=out_spec,
        out_shape=out_shape,
        scratch_shapes=[pltpu.VMEM((d, TN), BF16), pltpu.VMEM((2, tsub, TN), F32)],
        compiler_params=_cparams(("arbitrary", "arbitrary")),
        name=name,
    )(xbf, w_in, *extra)


def _rot_branches(tab_ref, gn_ref):
    def rot(a, rows, half):
        return (a * tab_ref[0, rows, :] + pltpu.roll(a, half, 1) * tab_ref[1, rows, :]
                + pltpu.roll(a, LANES - half, 1) * tab_ref[2, rows, :])

    def rms(a, gain):
        ms = jnp.mean(a * a, axis=-1, keepdims=True)
        return a * lax.rsqrt(ms + RMS_EPS) * gain

    return [(lambda j: j <= 1, lambda a, rows, g: rot(a, rows, ROT_HALF_A)),
            (lambda j: j == 2, lambda a, rows, g: rot(rms(a, gn_ref[0]), rows, ROT_HALF_C)),
            (lambda j: j == 3, lambda a, rows, g: rot(rms(a, gn_ref[1]), rows, ROT_HALF_C) if g < 2 else a),
            (lambda j: j >= 4, lambda a, rows, g: rot(a, rows, ROT_HALF_D))]


def _plain_branches():
    return [(lambda j: _any_of(j, PLAIN_SILU), lambda a, rows, g: a * jax.nn.sigmoid(a)),
            (lambda j: jnp.logical_not(_any_of(j, PLAIN_SILU)), lambda a, rows, g: a)]


def _gate_branches(bg_ref):
    return [(None, lambda a, rows, g: jax.nn.sigmoid(a + bg_ref[:, g * LANES:(g + 1) * LANES]))]


def _wide_branches(tab_ref):
    def rot(a, rows):
        return (a * tab_ref[0, rows, :] + pltpu.roll(a, ROT_HALF_D, 1) * tab_ref[1, rows, :]
                + pltpu.roll(a, LANES - ROT_HALF_D, 1) * tab_ref[2, rows, :])

    return [(lambda j: j <= 1, lambda a, rows, g: rot(a, rows)),
            (lambda j: j == 2, lambda a, rows, g: a)]


def _inproj(xbf, w_in, b_gate, layer, tabs, gains):
    hr = _proj("inproj_rot", xbf, w_in, layer, ROT_TILES, _rot_branches, (tabs, gains),
               lambda tm: [pl.BlockSpec((None, 3, tm, LANES), lambda j, i: (_select(j, ROT_VARIANT), 0, i, 0)),
                           pl.BlockSpec((2, 1, LANES), lambda j, i: (0, 0, 0))], True)
    hp = _proj("inproj_plain", xbf, w_in, layer, PLAIN_TILES, _plain_branches, (), lambda tm: [], True,
               n_sub=8)
    hg = _proj("inproj_gate", xbf, w_in, layer, tuple(range(GATE_TILE0, GATE_TILE0 + N_GATE_TILES)),
               _gate_branches, (b_gate.reshape(b_gate.shape[0], 1, -1),),
               lambda tm: [pl.BlockSpec((None, 1, TN), lambda j, i: (layer, 0, j))], False, n_sub=8)
    hw = _proj("inproj_wide", xbf, w_in, layer, WIDE_TILES, _wide_branches, (tabs,),
               lambda tm: [pl.BlockSpec((None, 3, tm, LANES),
                                        lambda j, i: (jnp.where(j == 0, ROT_VARIANT_DQ, ROT_VARIANT_DK), 0, i, 0))],
               True, n_sub=8, dtype=F32)
    return hr, hp, hg, hw


def _pair_attn_kernel(diff, tq, tk, lam_init, *refs):
    if diff:
        q_ref, k_ref, v_ref, z_ref, lp_ref, sg_ref, o_ref, q2_ref, vt_ref, acc_ref, s_ref = refs
    else:
        q_ref, k_ref, v_ref, z_ref, o_ref, q2_ref, vt_ref, acc_ref, s_ref = refs
    seq = k_ref.shape[0]
    nq = 2 * tq

    @pl.when(pl.program_id(1) == 0)
    def _():
        for c in range(seq // tk):
            vt_ref[c, :LANES, :] = v_ref[c * tk:(c + 1) * tk, :].astype(F32).T.astype(BF16)
            vt_ref[c, LANES:, :] = jnp.ones((VT_PAD, tk), BF16)

    if diff:
        q = q_ref[...]
        lane = lax.broadcasted_iota(jnp.int32, q.shape, 1)
        zero = jnp.zeros_like(q)
        q2_ref[:tq, :] = jnp.where(lane < DA_QK, q, zero)
        q2_ref[tq:, :] = jnp.where(lane >= DA_QK, q, zero)
    else:
        q2_ref[:tq, :] = q_ref[0]
        q2_ref[tq:, :] = q_ref[1]
    acc_ref[...] = jnp.zeros(acc_ref.shape, F32)

    n_chunks = seq // tk

    def scores(c, slot):
        start = pl.multiple_of(c * tk, tk)
        s = lax.dot_general(k_ref[pl.ds(start, tk), :], q2_ref[...], (((1,), (1,)), ((), ())),
                            preferred_element_type=F32)
        s_ref[slot] = s
        return jnp.max(s, axis=0, keepdims=True)

    def fold(c, slot, m_prev, m_chunk):
        m_new = jnp.maximum(m_prev, m_chunk)
        alpha = jnp.exp2(m_prev - m_new)
        p = jnp.exp2(s_ref[slot] - m_new)
        acc_ref[...] = alpha * acc_ref[...] + jnp.dot(vt_ref[c], p.astype(BF16),
                                                      preferred_element_type=F32)
        return m_new

    unroll = 4 if n_chunks % 4 == 0 else 2

    def group(base, carry, last):
        m, m_chunk = carry
        for u in range(unroll):
            m_next = m_chunk if (last and u == unroll - 1) else scores(base + u + 1, (u + 1) % 2)
            m = fold(base + u, u % 2, m, m_chunk)
            m_chunk = m_next
        return m, m_chunk

    carry = (jnp.full((1, nq), NEG_INF, F32), scores(0, 0))
    n_groups = n_chunks // unroll
    if n_groups > 1:
        carry = lax.fori_loop(0, n_groups - 1, lambda j, cc: group(j * unroll, cc, False), carry)
    group((n_groups - 1) * unroll, carry, True)

    acc = acc_ref[...]
    o = (acc[:LANES] / acc[LANES:LANES + 1]).T
    if diff:
        lp = lp_ref[...]
        lam = (jnp.exp(jnp.sum(lp[0:1] * lp[1:2], keepdims=True))
               - jnp.exp(jnp.sum(lp[2:3] * lp[3:4], keepdims=True)) + lam_init)
        dlt = o[:tq] - lam * o[tq:]
        ms = jnp.mean(dlt * dlt, axis=-1, keepdims=True)
        y = dlt * lax.rsqrt(ms + RMS_EPS) * sg_ref[...] * (1.0 - lam_init)
        o_ref[...] = (y * z_ref[...].astype(F32)).astype(BF16)
    else:
        o_ref[:, :LANES] = (o[:tq] * z_ref[0].astype(F32)).astype(BF16)
        o_ref[:, LANES:] = (o[tq:] * z_ref[1].astype(F32)).astype(BF16)


def _pair_attn(hr, hp, diff, lam_init=0.0, lam_params=None, subln_g=None, tq=512, tk=512):
    s = hr.shape[1]
    if diff:
        n_kv, qw, qblk = 4, LANES, (None, tq, LANES)
        q_map = lambda hh, i: (CB_AQ + hh, i, 0)
        k_map = lambda hh, i: (CB_AK + hh, 0, 0)
        v_map = lambda hh, i: (CB_AV + hh, 0, 0)
        z_map = lambda hh, i: (CB_AZ + hh, i, 0)
    else:
        n_kv, qw, qblk = 2, 2 * LANES, (2, tq, LANES)
        q_map = lambda hh, i: (CB_CQ // 2 + hh, i, 0)
        k_map = lambda hh, i: (CB_CK + hh, 0, 0)
        v_map = lambda hh, i: (CB_CV + hh, 0, 0)
        z_map = lambda hh, i: (CB_CZ // 2 + hh, i, 0)
    in_specs = [pl.BlockSpec(qblk, q_map),
                pl.BlockSpec((None, s, LANES), k_map),
                pl.BlockSpec((None, s, LANES), v_map),
                pl.BlockSpec(qblk, z_map)]
    args = [hr, hr, hp if diff else hr, hp]
    if diff:
        in_specs += [pl.BlockSpec((4, DA_QK), lambda hh, i: (0, 0)),
                     pl.BlockSpec((1, LANES), lambda hh, i: (0, 0))]
        args += [lam_params, subln_g.reshape(1, LANES)]
    return pl.pallas_call(
        functools.partial(_pair_attn_kernel, diff, tq, tk, lam_init),
        grid=(n_kv, s // tq),
        in_specs=in_specs,
        out_specs=pl.BlockSpec((tq, qw), lambda hh, i: (i, hh)),
        out_shape=jax.ShapeDtypeStruct((s, BRANCH_W), BF16),
        scratch_shapes=[pltpu.VMEM((2 * tq, LANES), BF16),
                        pltpu.VMEM((s // tk, LANES + VT_PAD, tk), BF16),
                        pltpu.VMEM((LANES + VT_PAD, 2 * tq), F32),
                        pltpu.VMEM((2, tk, 2 * tq), F32)],
        compiler_params=_cparams(("arbitrary", "arbitrary")),
        name="diff_attn" if diff else "axial_gqa",
    )(*args)


def _nat_bias(rpb):
    n_layers, n_heads = rpb.shape[:2]
    cols = jnp.arange(GRID_W)
    c0 = jnp.clip(cols - NB_COLS // 2, 0, GRID_W - NB_COLS)
    kc = jnp.arange(GRID_W)
    valid = (kc[None, :] >= c0[:, None]) & (kc[None, :] < c0[:, None] + NB_COLS)
    lpad = GRID_W - NB_COLS
    padded = jnp.pad(rpb.astype(F32), ((0, 0), (0, 0), (0, 0), (lpad, lpad)))
    toep = jnp.stack([padded[..., GRID_W - 1 - c:2 * GRID_W - 1 - c] for c in range(GRID_W)], axis=-2)
    t = jnp.where(valid, toep, NEG_INF)
    out = []
    for ds in range(NB_ROWS):
        w = t[:, :, ds:ds + NB_ROWS]
        out.append(jnp.transpose(w, (0, 1, 3, 2, 4)).reshape(n_layers, n_heads, GRID_W, NB_ROWS * GRID_W))
    return jnp.stack(out, axis=1)


def _nat_kernel(rows, rps, q_ref, k_ref, v_ref, z_ref, bias_ref, o_ref):
    win = NB_ROWS * GRID_W
    starts, scores = [], []
    for u in range(rps):
        r = pl.program_id(1) * rps + u
        r0 = jnp.clip(r - NB_ROWS // 2, 0, rows - NB_ROWS)
        start = pl.multiple_of(r0 * GRID_W, GRID_W)
        kw = k_ref[pl.ds(start, win), :]
        s = lax.dot_general(q_ref[u * GRID_W:(u + 1) * GRID_W, :], kw, (((1,), (1,)), ((), ())),
                            preferred_element_type=F32)
        scores.append(s * (HEAD_DIM ** -0.5) + bias_ref[r0 - r + (NB_ROWS - 1)])
        starts.append(start)
    probs = []
    for s in scores:
        p = jnp.exp(s - jnp.max(s, axis=-1, keepdims=True))
        probs.append((p.astype(BF16), jnp.sum(p, axis=-1, keepdims=True)))
    for u, (p, l) in enumerate(probs):
        qrows = slice(u * GRID_W, (u + 1) * GRID_W)
        o = jnp.dot(p, v_ref[pl.ds(starts[u], win), :], preferred_element_type=F32) / l
        o_ref[qrows, :] = (o * z_ref[qrows, :].astype(F32)).astype(BF16)


def _nat_attn(hp, bias, layer, rps=8):
    s = hp.shape[1]
    rows = s // GRID_W
    blk = rps * GRID_W
    return pl.pallas_call(
        functools.partial(_nat_kernel, rows, rps),
        grid=(4, rows // rps),
        in_specs=[pl.BlockSpec((None, blk, LANES), lambda hh, r: (CB_BQ + hh, r, 0)),
                  pl.BlockSpec((None, s, LANES), lambda hh, r: (CB_BK + hh, 0, 0)),
                  pl.BlockSpec((None, s, LANES), lambda hh, r: (CB_BV + hh, 0, 0)),
                  pl.BlockSpec((None, blk, LANES), lambda hh, r: (CB_BZ + hh, r, 0)),
                  pl.BlockSpec((None, NB_ROWS, None, GRID_W, NB_ROWS * GRID_W),
                               lambda hh, r: (layer, 0, hh, 0, 0))],
        out_specs=pl.BlockSpec((blk, LANES), lambda hh, r: (r, hh)),
        out_shape=jax.ShapeDtypeStruct((s, BRANCH_W), BF16),
        compiler_params=_cparams(("arbitrary", "arbitrary")),
        name="nat_attn",
    )(hp, hp, hp, hp, bias)


def _dil_masks(tq, rates):
    row = np.arange(tq)[:, None]

    def band(col, rate):
        d = col - row
        return np.where((d >= 0) & (d <= 2 * DIL_SIDE * rate) & (d % rate == 0), 0.0, NEG_INF).astype(np.float32)

    width0 = tq + 2 * DIL_SIDE
    tables = [np.stack([band(np.arange(width0)[None, :] + dlt, 1) for dlt in (DIL_SIDE, 0, -DIL_SIDE)])]
    assert rates[0] == 1
    for rate in rates[1:]:
        reach = DIL_SIDE * rate
        cols = np.arange(-reach, tq + 3 * reach)[None, :]
        full = band(cols, rate)
        tables.append(full.reshape(tq, -1, LANES).transpose(1, 0, 2))
    return [jnp.asarray(t) for t in tables]


def _dil_class_kernel(tq, rate, q_ref, k_ref, v_ref, b_ref, o_ref, lse_ref):
    length = q_ref.shape[0] // rate
    width = tq + 2 * DIL_SIDE
    for a in range(rate):
        cls = pl.ds(a, length, stride=rate)
        qa = q_ref[cls, :].astype(BF16)
        ka = k_ref[cls, :].astype(BF16)
        va = v_ref[cls, :].astype(BF16)
        scored = []
        for i0 in range(0, length, tq):
            ws = min(max(i0 - DIL_SIDE, 0), length - width)
            delta = ws - i0 + DIL_SIDE
            s = lax.dot_general(qa[i0:i0 + tq], ka[ws:ws + width], (((1,), (1,)), ((), ())),
                                preferred_element_type=F32)
            scored.append((i0, ws, s + b_ref[0 if delta > 0 else (2 if delta < 0 else 1)]))
        stats = []
        for i0, ws, s in scored:
            m = jnp.max(s, axis=-1, keepdims=True)
            p = jnp.exp2(s - m)
            stats.append((i0, ws, p.astype(BF16), m, jnp.sum(p, axis=-1, keepdims=True)))
        for i0, ws, p, m, l in stats:
            out_rows = pl.ds(a + rate * i0, tq, stride=rate)
            o_ref[out_rows, :] = jnp.dot(p, va[ws:ws + width], preferred_element_type=F32) / l
            lse_ref[out_rows, :] = jnp.broadcast_to(m + jnp.log2(l), (tq, LANES))


def _dil_kernel(tq, nsub, q0_ref, q1_ref, k0_ref, k1_ref, v0_ref, v1_ref, z_ref, b0_ref, b1_ref,
                o2_ref, lse2_ref, o_ref):
    seq = k0_ref.shape[0]
    q_refs, k_refs, v_refs, b_refs = (q0_ref, q1_ref), (k0_ref, k1_ref), (v0_ref, v1_ref), (b0_ref, b1_ref)
    scores, wins = [], []
    for u in range(nsub):
        t0 = (pl.program_id(1) * nsub + u) * tq
        rows = slice(u * tq, (u + 1) * tq)
        su, wu = [], []
        for rate, q_ref, k_ref, b_ref in zip(DIL_RATES, q_refs, k_refs, b_refs):
            reach = DIL_SIDE * rate
            width = tq + 2 * reach
            start = pl.multiple_of(jnp.clip(t0 - reach, 0, seq - width), DIL_SIDE)
            delta = start - t0 + reach
            s = lax.dot_general(q_ref[rows, :], k_ref[pl.ds(start, width), :], (((1,), (1,)), ((), ())),
                                preferred_element_type=F32)
            if rate == 1:
                s = s + b_ref[jnp.where(delta > 0, 0, jnp.where(delta < 0, 2, 1))]
            else:
                tile0 = (delta + reach) // LANES
                s = jnp.concatenate([s[:, j * LANES:(j + 1) * LANES] + b_ref[tile0 + j]
                                     for j in range(width // LANES)], axis=1)
            su.append(s)
            wu.append((start, width))
        scores.append(su)
        wins.append(wu)
    probs, dens, tops = [], [], []
    for su in scores:
        m = su[0].max(axis=-1, keepdims=True)
        for s in su[1:]:
            m = jnp.maximum(m, s.max(axis=-1, keepdims=True))
        pu = [jnp.exp2(s - m) for s in su]
        l = pu[0].sum(axis=-1, keepdims=True)
        for p in pu[1:]:
            l = l + p.sum(axis=-1, keepdims=True)
        probs.append([p.astype(BF16) for p in pu])
        dens.append(l)
        tops.append(m)
    for u, (pu, wu, l, m) in enumerate(zip(probs, wins, dens, tops)):
        rows = slice(u * tq, (u + 1) * tq)
        o = None
        for p, (start, width), v_ref in zip(pu, wu, v_refs):
            pv = jnp.dot(p, v_ref[pl.ds(start, width), :], preferred_element_type=F32)
            o = pv if o is None else o + pv
        lse2 = lse2_ref[rows, :1]
        top = jnp.maximum(m, lse2)
        w01 = jnp.exp2(m - top)
        w2 = jnp.exp2(lse2 - top)
        mixed = (w01 * o + w2 * o2_ref[rows, :]) / (w01 * l + w2)
        o_ref[rows, :] = (mixed * z_ref[rows, :].astype(F32)).astype(BF16)


def _dil_class_attn(hw, mask, tq=128):
    s = hw.shape[1]
    head = lambda cb: pl.BlockSpec((None, s, LANES), lambda hh: (cb + hh, 0, 0))
    return pl.pallas_call(
        functools.partial(_dil_class_kernel, tq, DIL_RATES[-1]),
        grid=(4,),
        in_specs=[head(CB_WQ), head(CB_WK), head(CB_WV), pl.BlockSpec(mask.shape, lambda hh: (0, 0, 0))],
        out_specs=[head(0), head(0)],
        out_shape=[jax.ShapeDtypeStruct((4, s, LANES), F32), jax.ShapeDtypeStruct((4, s, LANES), F32)],
        compiler_params=_cparams(("arbitrary",)),
        name="dil_class_attn",
    )(hw, hw, hw, mask)


def _dil_attn(hr, hp, hw, tq=128, nsub=8):
    s = hr.shape[1]
    n_dense = len(DIL_RATES) - 1
    masks = _dil_masks(tq, DIL_RATES[:n_dense])
    o2, lse2 = _dil_class_attn(hw, masks[0], tq)
    m_specs = [pl.BlockSpec(m.shape, lambda hh, i: (0, 0, 0)) for m in masks]
    blk = tq * nsub
    qs = [pl.BlockSpec((None, blk, LANES), functools.partial(lambda g, hh, i: (CB_DQ + 4 * g + hh, i, 0), g))
          for g in range(n_dense)]
    ks = [pl.BlockSpec((None, s, LANES), functools.partial(lambda g, hh, i: (CB_DK + 4 * g + hh, 0, 0), g))
          for g in range(n_dense)]
    vs = [pl.BlockSpec((None, s, LANES), functools.partial(lambda g, hh, i: (CB_DV + 4 * g + hh, 0, 0), g))
          for g in range(n_dense)]
    row_blk = pl.BlockSpec((None, blk, LANES), lambda hh, i: (hh, i, 0))
    return pl.pallas_call(
        functools.partial(_dil_kernel, tq, nsub),
        grid=(4, s // blk),
        in_specs=(qs + ks + vs + [pl.BlockSpec((None, blk, LANES), lambda hh, i: (CB_DZ + hh, i, 0))]
                  + m_specs + [row_blk, row_blk]),
        out_specs=pl.BlockSpec((blk, LANES), lambda hh, i: (i, hh)),
        out_shape=jax.ShapeDtypeStruct((s, BRANCH_W), BF16),
        compiler_params=_cparams(("arbitrary", "arbitrary")),
        name="dil_attn",
    )(*([hr] * (2 * n_dense)), *([hp] * (n_dense + 1)), *masks, o2, lse2)


def _merge_kernel(ya_ref, yb_ref, yc_ref, yd_ref, ga_ref, gb_ref, gc_ref, gd_ref, wb_ref, o_ref, wbf_ref):
    @pl.when(pl.program_id(1) == 0)
    def _():
        wbf_ref[...] = wb_ref[...].astype(BF16)

    acc = None
    for n, (y_ref, g_ref) in enumerate(zip((ya_ref, yb_ref, yc_ref, yd_ref),
                                           (ga_ref, gb_ref, gc_ref, gd_ref))):
        proj = jnp.dot(y_ref[...], wbf_ref[n], preferred_element_type=F32)
        term = proj * g_ref[...].astype(F32)
        acc = term if acc is None else acc + term
    o_ref[...] = acc.astype(BF16)


def _merge(ys, hg, w_branch, layer, tm=1024):
    s = hg.shape[0]
    tm = math.gcd(tm, s)
    n_col = D_MODEL // TN
    y_specs = [pl.BlockSpec((tm, BRANCH_W), lambda j, i: (i, 0)) for _ in range(N_BRANCH)]
    g_specs = [pl.BlockSpec((tm, TN), functools.partial(lambda n, j, i: (i, n_col * n + j), n))
               for n in range(N_BRANCH)]
    return pl.pallas_call(
        _merge_kernel,
        grid=(n_col, s // tm),
        in_specs=y_specs + g_specs + [pl.BlockSpec((None, N_BRANCH, BRANCH_W, TN),
                                                   lambda j, i: (layer, 0, 0, j))],
        out_specs=pl.BlockSpec((tm, TN), lambda j, i: (i, j)),
        out_shape=jax.ShapeDtypeStruct((s, D_MODEL), BF16),
        scratch_shapes=[pltpu.VMEM((N_BRANCH, BRANCH_W, TN), BF16)],
        compiler_params=_cparams(("arbitrary", "arbitrary")),
        name="branch_merge",
    )(*ys, hg, hg, hg, hg, w_branch)


def _out_ln_kernel(alpha, m_ref, w_ref, x_ref, g_ref, b_ref, y_ref, ybf_ref):
    out = jnp.dot(m_ref[...], w_ref[...], preferred_element_type=F32)
    y = _ln_rows(alpha * x_ref[...] + out, g_ref[...], b_ref[...])
    y_ref[...] = y
    ybf_ref[...] = y.astype(BF16)


def _out_ln(merged, wo_bf, layer, x, g, b, alpha, tm=512):
    s, d = x.shape
    return pl.pallas_call(
        functools.partial(_out_ln_kernel, alpha),
        grid=(s // tm,),
        in_specs=[pl.BlockSpec((tm, d), lambda i: (i, 0)),
                  pl.BlockSpec((None, d, d), lambda i: (layer, 0, 0)),
                  pl.BlockSpec((tm, d), lambda i: (i, 0)),
                  pl.BlockSpec((1, d), lambda i: (0, 0)),
                  pl.BlockSpec((1, d), lambda i: (0, 0))],
        out_specs=[pl.BlockSpec((tm, d), lambda i: (i, 0)),
                   pl.BlockSpec((tm, d), lambda i: (i, 0))],
        out_shape=[jax.ShapeDtypeStruct((s, d), F32), jax.ShapeDtypeStruct((s, d), BF16)],
        compiler_params=_cparams(("arbitrary",)),
        name="out_ln",
    )(merged, wo_bf, x, g.reshape(1, d), b.reshape(1, d))


def kernel(x, emb_ln_g, emb_ln_b, w_in, b_gate, diff_lambda, diff_subln_g, nat_rpb,
           gqa_q_norm_g, gqa_k_norm_g, w_branch, w_out, ln_g, ln_b):
    batch, seq, d = x.shape
    assert batch == 1 and d == D_MODEL and w_in.shape[-1] == D_IN
    depth = w_in.shape[0]
    alpha = (2 * depth) ** 0.25
    tabs = _rope_tables(seq)
    nat_bias = _nat_bias(nat_rpb)
    wo_bf = w_out.astype(BF16)
    xf, xbf = _embed_ln(x[0], emb_ln_g, emb_ln_b)
    for l in range(depth):
        lam_init = 0.8 - 0.6 * math.exp(-0.3 * l)
        gains = jnp.stack([gqa_q_norm_g[l], gqa_k_norm_g[l]]).reshape(2, 1, LANES)
        hr, hp, hg, hw = _inproj(xbf, w_in, b_gate, l, tabs, gains)
        ya = _pair_attn(hr, hp, True, lam_init, diff_lambda[l], diff_subln_g[l])
        yb = _nat_attn(hp, nat_bias, l)
        yc = _pair_attn(hr, hp, False)
        yd = _dil_attn(hr, hp, hw)
        merged = _merge((ya, yb, yc, yd), hg, w_branch, l)
        xf, xbf = _out_ln(merged, wo_bf, l, xf, ln_g[l], ln_b[l], alpha)
    return xf[None]
```

```python
import functools
import math

import numpy as np
import jax
import jax.numpy as jnp
from jax import lax
from jax.experimental import pallas as pl
from jax.experimental.pallas import tpu as pltpu

F32 = jnp.float32
BF16 = jnp.bfloat16

D_MODEL = 2048
GRID_W = 64
HEAD_DIM = 128
N_BRANCH = 4
BRANCH_W = D_MODEL // 4
DA_QK = HEAD_DIM // 2
NB_ROWS = 8
NB_COLS = 16
AXIAL_THETA = 10000.0
DIL_RATES = (1, 4, 16)
DIL_SIDE = 64
ROPE_THETA = 500000.0
LN_EPS = 1e-5
RMS_EPS = 1e-6
NEG_INF = -1e30
LOG2E = math.log2(math.e)

LANES = 128
VT_PAD = 16
TN = 512
D_IN = 18944

ROT_TILES = (0, 1, 8, 9, 11, 12, 14, 15)
PLAIN_TILES = (2, 3, 4, 5, 6, 7, 10, 17, 18, 20)
WIDE_TILES = (13, 16, 19)
GATE_TILE0, N_GATE_TILES = 21, 16
ROT_VARIANT = (0, 1, 2, 3, 4, 4, 5, 5)
ROT_VARIANT_DQ, ROT_VARIANT_DK = 4, 5
CB_AQ, CB_AK, CB_CQ, CB_CK, CB_CV, CB_DQ, CB_DK = 0, 4, 8, 12, 14, 16, 24
CB_AV, CB_AZ, CB_BQ, CB_BK, CB_BV, CB_BZ, CB_CZ, CB_DV, CB_DZ = 0, 4, 8, 12, 16, 20, 24, 28, 36
CB_WQ, CB_WK, CB_WV = 0, 4, 8
PLAIN_SILU = (1, 5, 6, 9)

VMEM_LIMIT = 56 * 1024 * 1024


def _cparams(sem):
    return pltpu.CompilerParams(dimension_semantics=sem, vmem_limit_bytes=VMEM_LIMIT)


def _select(j, values):
    out = values[-1]
    for t in range(len(values) - 2, -1, -1):
        out = jnp.where(j == t, values[t], out)
    return out


def _any_of(j, tiles):
    c = j == tiles[0]
    for t in tiles[1:]:
        c = c | (j == t)
    return c


def _ln_rows(x, g, b):
    mu = jnp.mean(x, axis=-1, keepdims=True)
    xc = x - mu
    var = jnp.mean(xc * xc, axis=-1, keepdims=True)
    return xc * lax.rsqrt(var + LN_EPS) * g + b


def _embed_ln_kernel(x_ref, g_ref, b_ref, y_ref, ybf_ref):
    y = _ln_rows(x_ref[...], g_ref[...], b_ref[...])
    y_ref[...] = y
    ybf_ref[...] = y.astype(BF16)


def _embed_ln(x, g, b, tm=512):
    s, d = x.shape
    return pl.pallas_call(
        _embed_ln_kernel,
        grid=(s // tm,),
        in_specs=[pl.BlockSpec((tm, d), lambda i: (i, 0)),
                  pl.BlockSpec((1, d), lambda i: (0, 0)),
                  pl.BlockSpec((1, d), lambda i: (0, 0))],
        out_specs=[pl.BlockSpec((tm, d), lambda i: (i, 0)),
                   pl.BlockSpec((tm, d), lambda i: (i, 0))],
        out_shape=[jax.ShapeDtypeStruct((s, d), F32), jax.ShapeDtypeStruct((s, d), BF16)],
        compiler_params=_cparams(("arbitrary",)),
        name="embed_ln",
    )(x, g.reshape(1, d), b.reshape(1, d))


def _rope_tables(seq):
    t = np.arange(seq)

    def cs(pos, dim, theta):
        half = dim // 2
        inv = np.power(float(theta), -np.arange(half, dtype=np.float64) * 2.0 / dim)
        ang = pos.astype(np.float64)[:, None] * inv[None, :]
        return jnp.asarray(np.cos(ang), F32), jnp.asarray(np.sin(ang), F32)

    def group(cos, sin, width):
        half = cos.shape[1]
        pad = width - 2 * half
        one = jnp.ones((seq, pad), F32)
        zero = jnp.zeros((seq, pad), F32)
        zh = jnp.zeros((seq, half), F32)
        return (jnp.concatenate([cos, cos, one], 1),
                jnp.concatenate([zh, sin, zero], 1),
                jnp.concatenate([-sin, zh, zero], 1))

    ca, sa = cs(t, DA_QK // 4, ROPE_THETA)
    va = jnp.stack([jnp.concatenate([p, p], 1) for p in group(ca, sa, DA_QK)])
    cr, sr = cs(t // GRID_W, HEAD_DIM // 2, AXIAL_THETA)
    cc, sc = cs(t % GRID_W, HEAD_DIM // 2, AXIAL_THETA)
    vc = jnp.stack([jnp.concatenate([p, q], 1)
                    for p, q in zip(group(cr, sr, HEAD_DIM // 2), group(cc, sc, HEAD_DIM // 2))])
    cd, sd = cs(t, HEAD_DIM // 4, ROPE_THETA)
    vd = jnp.stack(group(cd, sd, HEAD_DIM))
    qa = DA_QK ** -0.5 * LOG2E
    qh = HEAD_DIM ** -0.5 * LOG2E
    return jnp.stack([va * qa, va, vc * qh, vc, vd * qh, vd])


ROT_HALF_A, ROT_HALF_C, ROT_HALF_D = DA_QK // 8, HEAD_DIM // 4, HEAD_DIM // 8


def _proj_kernel(tsub, n_sub, make_branches, n_extra, x_ref, w_ref, *rest):
    extra = rest[:n_extra]
    o_ref, wbf_ref, acc_ref = rest[n_extra:]
    j = pl.program_id(0)
    n_pass = x_ref.shape[0] // (n_sub * tsub)

    @pl.when(pl.program_id(1) == 0)
    def _():
        wbf_ref[...] = w_ref[...].astype(BF16)

    def pipeline(epilogue):
        def one_pass(h, carry):
            def rows_of(u):
                return pl.ds(pl.multiple_of((h * n_sub + u) * tsub, tsub), tsub)

            def matmul(u):
                acc_ref[u % 2] = jnp.dot(x_ref[rows_of(u), :], wbf_ref[...],
                                         preferred_element_type=F32)

            matmul(0)
            for u in range(n_sub):
                if u + 1 < n_sub:
                    matmul(u + 1)
                for g in range(TN // LANES):
                    lanes = slice(g * LANES, (g + 1) * LANES)
                    y = epilogue(acc_ref[u % 2, :, lanes], rows_of(u), g).astype(o_ref.dtype)
                    if len(o_ref.shape) == 3:
                        o_ref[g, rows_of(u), :] = y
                    else:
                        o_ref[rows_of(u), lanes] = y
            return carry

        lax.fori_loop(0, n_pass, one_pass, 0)

    for cond, epilogue in make_branches(*extra):
        if cond is None:
            pipeline(epilogue)
        else:
            pl.when(cond(j))(functools.partial(pipeline, epilogue))


def _proj(name, xbf, w_in, layer, src_tiles, make_branches, extra, extra_specs, head_major,
          tm=2048, tsub=256, n_sub=4, dtype=BF16):
    s, d = xbf.shape
    tm = math.gcd(tm, s)
    n_sub = min(n_sub, tm // tsub)
    assert tm % (tsub * n_sub) == 0
    n_tiles = len(src_tiles)
    contiguous = src_tiles == tuple(range(src_tiles[0], src_tiles[0] + n_tiles))
    src = (lambda j: src_tiles[0] + j) if contiguous else (lambda j: _select(j, src_tiles))
    if head_major:
        out_spec = pl.BlockSpec((TN // LANES, tm, LANES), lambda j, i: (j, i, 0))
        out_shape = jax.ShapeDtypeStruct((n_tiles * (TN // LANES), s, LANES), dtype)
    else:
        out_spec = pl.BlockSpec((tm, TN), lambda j, i: (i, j))
        out_shape = jax.ShapeDtypeStruct((s, n_tiles * TN), dtype)
    return pl.pallas_call(
        functools.partial(_proj_kernel, tsub, n_sub, make_branches, len(extra)),
        grid=(n_tiles, s // tm),
        in_specs=[pl.BlockSpec((tm, d), lambda j, i: (i, 0)),
                  pl.BlockSpec((None, d, TN), lambda j, i: (layer, 0, src(j)))] + extra_specs(tm),
        out_specs=out_spec,
        out_shape=out_shape,
        scratch_shapes=[pltpu.VMEM((d, TN), BF16), pltpu.VMEM((2, tsub, TN), F32)],
        compiler_params=_cparams(("arbitrary", "arbitrary")),
        name=name,
    )(xbf, w_in, *extra)


def _rot_branches(tab_ref, gn_ref):
    def rot(a, rows, half):
        return (a * tab_ref[0, rows, :] + pltpu.roll(a, half, 1) * tab_ref[1, rows, :]
                + pltpu.roll(a, LANES - half, 1) * tab_ref[2, rows, :])

    def rms(a, gain):
        ms = jnp.mean(a * a, axis=-1, keepdims=True)
        return a * lax.rsqrt(ms + RMS_EPS) * gain

    return [(lambda j: j <= 1, lambda a, rows, g: rot(a, rows, ROT_HALF_A)),
            (lambda j: j == 2, lambda a, rows, g: rot(rms(a, gn_ref[0]), rows, ROT_HALF_C)),
            (lambda j: j == 3, lambda a, rows, g: rot(rms(a, gn_ref[1]), rows, ROT_HALF_C) if g < 2 else a),
            (lambda j: j >= 4, lambda a, rows, g: rot(a, rows, ROT_HALF_D))]


def _plain_branches():
    return [(lambda j: _any_of(j, PLAIN_SILU), lambda a, rows, g: a * jax.nn.sigmoid(a)),
            (lambda j: jnp.logical_not(_any_of(j, PLAIN_SILU)), lambda a, rows, g: a)]


def _gate_branches(bg_ref):
    return [(None, lambda a, rows, g: jax.nn.sigmoid(a + bg_ref[:, g * LANES:(g + 1) * LANES]))]


def _wide_branches(tab_ref):
    def rot(a, rows):
        return (a * tab_ref[0, rows, :] + pltpu.roll(a, ROT_HALF_D, 1) * tab_ref[1, rows, :]
                + pltpu.roll(a, LANES - ROT_HALF_D, 1) * tab_ref[2, rows, :])

    return [(lambda j: j <= 1, lambda a, rows, g: rot(a, rows)),
            (lambda j: j == 2, lambda a, rows, g: a)]


def _inproj(xbf, w_in, b_gate, layer, tabs, gains):
    hr = _proj("inproj_rot", xbf, w_in, layer, ROT_TILES, _rot_branches, (tabs, gains),
               lambda tm: [pl.BlockSpec((None, 3, tm, LANES), lambda j, i: (_select(j, ROT_VARIANT), 0, i, 0)),
                           pl.BlockSpec((2, 1, LANES), lambda j, i: (0, 0, 0))], True)
    hp = _proj("inproj_plain", xbf, w_in, layer, PLAIN_TILES, _plain_branches, (), lambda tm: [], True,
               n_sub=8)
    hg = _proj("inproj_gate", xbf, w_in, layer, tuple(range(GATE_TILE0, GATE_TILE0 + N_GATE_TILES)),
               _gate_branches, (b_gate.reshape(b_gate.shape[0], 1, -1),),
               lambda tm: [pl.BlockSpec((None, 1, TN), lambda j, i: (layer, 0, j))], False, n_sub=8)
    hw = _proj("inproj_wide", xbf, w_in, layer, WIDE_TILES, _wide_branches, (tabs,),
               lambda tm: [pl.BlockSpec((None, 3, tm, LANES),
                                        lambda j, i: (jnp.where(j == 0, ROT_VARIANT_DQ, ROT_VARIANT_DK), 0, i, 0))],
               True, n_sub=8, dtype=F32)
    return hr, hp, hg, hw


def _pair_attn_kernel(diff, tq, tk, lam_init, *refs):
    if diff:
        q_ref, k_ref, v_ref, z_ref, lp_ref, sg_ref, o_ref, q2_ref, vt_ref, acc_ref, s_ref = refs
    else:
        q_ref, k_ref, v_ref, z_ref, o_ref, q2_ref, vt_ref, acc_ref, s_ref = refs
    seq = k_ref.shape[0]
    nq = 2 * tq

    @pl.when(pl.program_id(1) == 0)
    def _():
        for c in range(seq // tk):
            vt_ref[c, :LANES, :] = v_ref[c * tk:(c + 1) * tk, :].astype(F32).T.astype(BF16)
            vt_ref[c, LANES:, :] = jnp.ones((VT_PAD, tk), BF16)

    if diff:
        q = q_ref[...]
        lane = lax.broadcasted_iota(jnp.int32, q.shape, 1)
        zero = jnp.zeros_like(q)
        q2_ref[:tq, :] = jnp.where(lane < DA_QK, q, zero)
        q2_ref[tq:, :] = jnp.where(lane >= DA_QK, q, zero)
    else:
        q2_ref[:tq, :] = q_ref[0]
        q2_ref[tq:, :] = q_ref[1]
    acc_ref[...] = jnp.zeros(acc_ref.shape, F32)

    n_chunks = seq // tk

    def scores(c, slot):
        start = pl.multiple_of(c * tk, tk)
        s = lax.dot_general(k_ref[pl.ds(start, tk), :], q2_ref[...], (((1,), (1,)), ((), ())),
                            preferred_element_type=F32)
        s_ref[slot] = s
        return jnp.max(s, axis=0, keepdims=True)

    def fold(c, slot, m_prev, m_chunk):
        m_new = jnp.maximum(m_prev, m_chunk)
        alpha = jnp.exp2(m_prev - m_new)
        p = jnp.exp2(s_ref[slot] - m_new)
        acc_ref[...] = alpha * acc_ref[...] + jnp.dot(vt_ref[c], p.astype(BF16),
                                                      preferred_element_type=F32)
        return m_new

    unroll = 4 if n_chunks % 4 == 0 else 2

    def group(base, carry, last):
        m, m_chunk = carry
        for u in range(unroll):
            m_next = m_chunk if (last and u == unroll - 1) else scores(base + u + 1, (u + 1) % 2)
            m = fold(base + u, u % 2, m, m_chunk)
            m_chunk = m_next
        return m, m_chunk

    carry = (jnp.full((1, nq), NEG_INF, F32), scores(0, 0))
    n_groups = n_chunks // unroll
    if n_groups > 1:
        carry = lax.fori_loop(0, n_groups - 1, lambda j, cc: group(j * unroll, cc, False), carry)
    group((n_groups - 1) * unroll, carry, True)

    acc = acc_ref[...]
    o = (acc[:LANES] / acc[LANES:LANES + 1]).T
    if diff:
        lp = lp_ref[...]
        lam = (jnp.exp(jnp.sum(lp[0:1] * lp[1:2], keepdims=True))
               - jnp.exp(jnp.sum(lp[2:3] * lp[3:4], keepdims=True)) + lam_init)
        dlt = o[:tq] - lam * o[tq:]
        ms = jnp.mean(dlt * dlt, axis=-1, keepdims=True)
        y = dlt * lax.rsqrt(ms + RMS_EPS) * sg_ref[...] * (1.0 - lam_init)
        o_ref[...] = (y * z_ref[...].astype(F32)).astype(BF16)
    else:
        o_ref[:, :LANES] = (o[:tq] * z_ref[0].astype(F32)).astype(BF16)
        o_ref[:, LANES:] = (o[tq:] * z_ref[1].astype(F32)).astype(BF16)


def _pair_attn(hr, hp, diff, lam_init=0.0, lam_params=None, subln_g=None, tq=512, tk=512):
    s = hr.shape[1]
    if diff:
        n_kv, qw, qblk = 4, LANES, (None, tq, LANES)
        q_map = lambda hh, i: (CB_AQ + hh, i, 0)
        k_map = lambda hh, i: (CB_AK + hh, 0, 0)
        v_map = lambda hh, i: (CB_AV + hh, 0, 0)
        z_map = lambda hh, i: (CB_AZ + hh, i, 0)
    else:
        n_kv, qw, qblk = 2, 2 * LANES, (2, tq, LANES)
        q_map = lambda hh, i: (CB_CQ // 2 + hh, i, 0)
        k_map = lambda hh, i: (CB_CK + hh, 0, 0)
        v_map = lambda hh, i: (CB_CV + hh, 0, 0)
        z_map = lambda hh, i: (CB_CZ // 2 + hh, i, 0)
    in_specs = [pl.BlockSpec(qblk, q_map),
                pl.BlockSpec((None, s, LANES), k_map),
                pl.BlockSpec((None, s, LANES), v_map),
                pl.BlockSpec(qblk, z_map)]
    args = [hr, hr, hp if diff else hr, hp]
    if diff:
        in_specs += [pl.BlockSpec((4, DA_QK), lambda hh, i: (0, 0)),
                     pl.BlockSpec((1, LANES), lambda hh, i: (0, 0))]
        args += [lam_params, subln_g.reshape(1, LANES)]
    return pl.pallas_call(
        functools.partial(_pair_attn_kernel, diff, tq, tk, lam_init),
        grid=(n_kv, s // tq),
        in_specs=in_specs,
        out_specs=pl.BlockSpec((tq, qw), lambda hh, i: (i, hh)),
        out_shape=jax.ShapeDtypeStruct((s, BRANCH_W), BF16),
        scratch_shapes=[pltpu.VMEM((2 * tq, LANES), BF16),
                        pltpu.VMEM((s // tk, LANES + VT_PAD, tk), BF16),
                        pltpu.VMEM((LANES + VT_PAD, 2 * tq), F32),
                        pltpu.VMEM((2, tk, 2 * tq), F32)],
        compiler_params=_cparams(("arbitrary", "arbitrary")),
        name="diff_attn" if diff else "axial_gqa",
    )(*args)


def _nat_bias(rpb):
    n_layers, n_heads = rpb.shape[:2]
    cols = jnp.arange(GRID_W)
    c0 = jnp.clip(cols - NB_COLS // 2, 0, GRID_W - NB_COLS)
    kc = jnp.arange(GRID_W)
    valid = (kc[None, :] >= c0[:, None]) & (kc[None, :] < c0[:, None] + NB_COLS)
    lpad = GRID_W - NB_COLS
    padded = jnp.pad(rpb.astype(F32), ((0, 0), (0, 0), (0, 0), (lpad, lpad)))
    toep = jnp.stack([padded[..., GRID_W - 1 - c:2 * GRID_W - 1 - c] for c in range(GRID_W)], axis=-2)
    t = jnp.where(valid, toep, NEG_INF)
    out = []
    for ds in range(NB_ROWS):
        w = t[:, :, ds:ds + NB_ROWS]
        out.append(jnp.transpose(w, (0, 1, 3, 2, 4)).reshape(n_layers, n_heads, GRID_W, NB_ROWS * GRID_W))
    return jnp.stack(out, axis=1)


def _nat_kernel(rows, rps, q_ref, k_ref, v_ref, z_ref, bias_ref, o_ref):
    win = NB_ROWS * GRID_W
    starts, scores = [], []
    for u in range(rps):
        r = pl.program_id(1) * rps + u
        r0 = jnp.clip(r - NB_ROWS // 2, 0, rows - NB_ROWS)
        start = pl.multiple_of(r0 * GRID_W, GRID_W)
        kw = k_ref[pl.ds(start, win), :]
        s = lax.dot_general(q_ref[u * GRID_W:(u + 1) * GRID_W, :], kw, (((1,), (1,)), ((), ())),
                            preferred_element_type=F32)
        scores.append(s * (HEAD_DIM ** -0.5) + bias_ref[r0 - r + (NB_ROWS - 1)])
        starts.append(start)
    probs = []
    for s in scores:
        p = jnp.exp(s - jnp.max(s, axis=-1, keepdims=True))
        probs.append((p.astype(BF16), jnp.sum(p, axis=-1, keepdims=True)))
    for u, (p, l) in enumerate(probs):
        qrows = slice(u * GRID_W, (u + 1) * GRID_W)
        o = jnp.dot(p, v_ref[pl.ds(starts[u], win), :], preferred_element_type=F32) / l
        o_ref[qrows, :] = (o * z_ref[qrows, :].astype(F32)).astype(BF16)


def _nat_attn(hp, bias, layer, rps=16):
    s = hp.shape[1]
    rows = s // GRID_W
    blk = rps * GRID_W
    return pl.pallas_call(
        functools.partial(_nat_kernel, rows, rps),
        grid=(4, rows // rps),
        in_specs=[pl.BlockSpec((None, blk, LANES), lambda hh, r: (CB_BQ + hh, r, 0)),
                  pl.BlockSpec((None, s, LANES), lambda hh, r: (CB_BK + hh, 0, 0)),
                  pl.BlockSpec((None, s, LANES), lambda hh, r: (CB_BV + hh, 0, 0)),
                  pl.BlockSpec((None, blk, LANES), lambda hh, r: (CB_BZ + hh, r, 0)),
                  pl.BlockSpec((None, NB_ROWS, None, GRID_W, NB_ROWS * GRID_W),
                               lambda hh, r: (layer, 0, hh, 0, 0))],
        out_specs=pl.BlockSpec((blk, LANES), lambda hh, r: (r, hh)),
        out_shape=jax.ShapeDtypeStruct((s, BRANCH_W), BF16),
        compiler_params=_cparams(("arbitrary", "arbitrary")),
        name="nat_attn",
    )(hp, hp, hp, hp, bias)


def _dil_masks(tq, rates):
    row = np.arange(tq)[:, None]

    def band(col, rate):
        d = col - row
        return np.where((d >= 0) & (d <= 2 * DIL_SIDE * rate) & (d % rate == 0), 0.0, NEG_INF).astype(np.float32)

    width0 = tq + 2 * DIL_SIDE
    tables = [np.stack([band(np.arange(width0)[None, :] + dlt, 1) for dlt in (DIL_SIDE, 0, -DIL_SIDE)])]
    assert rates[0] == 1
    for rate in rates[1:]:
        reach = DIL_SIDE * rate
        cols = np.arange(-reach, tq + 3 * reach)[None, :]
        full = band(cols, rate)
        tables.append(full.reshape(tq, -1, LANES).transpose(1, 0, 2))
    return [jnp.asarray(t) for t in tables]


def _dil_class_kernel(tq, rate, q_ref, k_ref, v_ref, b_ref, o_ref, lse_ref):
    length = q_ref.shape[0] // rate
    width = tq + 2 * DIL_SIDE
    for a in range(rate):
        cls = pl.ds(a, length, stride=rate)
        qa = q_ref[cls, :].astype(BF16)
        ka = k_ref[cls, :].astype(BF16)
        va = v_ref[cls, :].astype(BF16)
        scored = []
        for i0 in range(0, length, tq):
            ws = min(max(i0 - DIL_SIDE, 0), length - width)
            delta = ws - i0 + DIL_SIDE
            s = lax.dot_general(qa[i0:i0 + tq], ka[ws:ws + width], (((1,), (1,)), ((), ())),
                                preferred_element_type=F32)
            scored.append((i0, ws, s + b_ref[0 if delta > 0 else (2 if delta < 0 else 1)]))
        stats = []
        for i0, ws, s in scored:
            m = jnp.max(s, axis=-1, keepdims=True)
            p = jnp.exp2(s - m)
            stats.append((i0, ws, p.astype(BF16), m, jnp.sum(p, axis=-1, keepdims=True)))
        for i0, ws, p, m, l in stats:
            out_rows = pl.ds(a + rate * i0, tq, stride=rate)
            o_ref[out_rows, :] = jnp.dot(p, va[ws:ws + width], preferred_element_type=F32) / l
            lse_ref[out_rows, :] = jnp.broadcast_to(m + jnp.log2(l), (tq, LANES))


def _dil_kernel(tq, nsub, q0_ref, q1_ref, k0_ref, k1_ref, v0_ref, v1_ref, z_ref, b0_ref, b1_ref,
                o2_ref, lse2_ref, o_ref):
    seq = k0_ref.shape[0]
    q_refs, k_refs, v_refs, b_refs = (q0_ref, q1_ref), (k0_ref, k1_ref), (v0_ref, v1_ref), (b0_ref, b1_ref)
    scores, wins = [], []
    for u in range(nsub):
        t0 = (pl.program_id(1) * nsub + u) * tq
        rows = slice(u * tq, (u + 1) * tq)
        su, wu = [], []
        for rate, q_ref, k_ref, b_ref in zip(DIL_RATES, q_refs, k_refs, b_refs):
            reach = DIL_SIDE * rate
            width = tq + 2 * reach
            start = pl.multiple_of(jnp.clip(t0 - reach, 0, seq - width), DIL_SIDE)
            delta = start - t0 + reach
            s = lax.dot_general(q_ref[rows, :], k_ref[pl.ds(start, width), :], (((1,), (1,)), ((), ())),
                                preferred_element_type=F32)
            if rate == 1:
                s = s + b_ref[jnp.where(delta > 0, 0, jnp.where(delta < 0, 2, 1))]
            else:
                tile0 = (delta + reach) // LANES
                s = jnp.concatenate([s[:, j * LANES:(j + 1) * LANES] + b_ref[tile0 + j]
                                     for j in range(width // LANES)], axis=1)
            su.append(s)
            wu.append((start, width))
        scores.append(su)
        wins.append(wu)
    probs, dens, tops = [], [], []
    for su in scores:
        m = su[0].max(axis=-1, keepdims=True)
        for s in su[1:]:
            m = jnp.maximum(m, s.max(axis=-1, keepdims=True))
        pu = [jnp.exp2(s - m) for s in su]
        l = pu[0].sum(axis=-1, keepdims=True)
        for p in pu[1:]:
            l = l + p.sum(axis=-1, keepdims=True)
        probs.append([p.astype(BF16) for p in pu])
        dens.append(l)
        tops.append(m)
    for u, (pu, wu, l, m) in enumerate(zip(probs, wins, dens, tops)):
        rows = slice(u * tq, (u + 1) * tq)
        o = None
        for p, (start, width), v_ref in zip(pu, wu, v_refs):
            pv = jnp.dot(p, v_ref[pl.ds(start, width), :], preferred_element_type=F32)
            o = pv if o is None else o + pv
        lse2 = lse2_ref[rows, :1]
        top = jnp.maximum(m, lse2)
        w01 = jnp.exp2(m - top)
        w2 = jnp.exp2(lse2 - top)
        mixed = (w01 * o + w2 * o2_ref[rows, :]) / (w01 * l + w2)
        o_ref[rows, :] = (mixed * z_ref[rows, :].astype(F32)).astype(BF16)


def _dil_class_attn(hw, mask, tq=128):
    s = hw.shape[1]
    head = lambda cb: pl.BlockSpec((None, s, LANES), lambda hh: (cb + hh, 0, 0))
    return pl.pallas_call(
        functools.partial(_dil_class_kernel, tq, DIL_RATES[-1]),
        grid=(4,),
        in_specs=[head(CB_WQ), head(CB_WK), head(CB_WV), pl.BlockSpec(mask.shape, lambda hh: (0, 0, 0))],
        out_specs=[head(0), head(0)],
        out_shape=[jax.ShapeDtypeStruct((4, s, LANES), F32), jax.ShapeDtypeStruct((4, s, LANES), F32)],
        compiler_params=_cparams(("arbitrary",)),
        name="dil_class_attn",
    )(hw, hw, hw, mask)


def _dil_attn(hr, hp, hw, tq=128, nsub=8):
    s = hr.shape[1]
    n_dense = len(DIL_RATES) - 1
    masks = _dil_masks(tq, DIL_RATES[:n_dense])
    o2, lse2 = _dil_class_attn(hw, masks[0], tq)
    m_specs = [pl.BlockSpec(m.shape, lambda hh, i: (0, 0, 0)) for m in masks]
    blk = tq * nsub
    qs = [pl.BlockSpec((None, blk, LANES), functools.partial(lambda g, hh, i: (CB_DQ + 4 * g + hh, i, 0), g))
          for g in range(n_dense)]
    ks = [pl.BlockSpec((None, s, LANES), functools.partial(lambda g, hh, i: (CB_DK + 4 * g + hh, 0, 0), g))
          for g in range(n_dense)]
    vs = [pl.BlockSpec((None, s, LANES), functools.partial(lambda g, hh, i: (CB_DV + 4 * g + hh, 0, 0), g))
          for g in range(n_dense)]
    row_blk = pl.BlockSpec((None, blk, LANES), lambda hh, i: (hh, i, 0))
    return pl.pallas_call(
        functools.partial(_dil_kernel, tq, nsub),
        grid=(4, s // blk),
        in_specs=(qs + ks + vs + [pl.BlockSpec((None, blk, LANES), lambda hh, i: (CB_DZ + hh, i, 0))]
                  + m_specs + [row_blk, row_blk]),
        out_specs=pl.BlockSpec((blk, LANES), lambda hh, i: (i, hh)),
        out_shape=jax.ShapeDtypeStruct((s, BRANCH_W), BF16),
        compiler_params=_cparams(("arbitrary", "arbitrary")),
        name="dil_attn",
    )(*([hr] * (2 * n_dense)), *([hp] * (n_dense + 1)), *masks, o2, lse2)


def _merge_kernel(ya_ref, yb_ref, yc_ref, yd_ref, ga_ref, gb_ref, gc_ref, gd_ref, wb_ref, o_ref, wbf_ref):
    @pl.when(pl.program_id(1) == 0)
    def _():
        wbf_ref[...] = wb_ref[...].astype(BF16)

    acc = None
    for n, (y_ref, g_ref) in enumerate(zip((ya_ref, yb_ref, yc_ref, yd_ref),
                                           (ga_ref, gb_ref, gc_ref, gd_ref))):
        proj = jnp.dot(y_ref[...], wbf_ref[n], preferred_element_type=F32)
        term = proj * g_ref[...].astype(F32)
        acc = term if acc is None else acc + term
    o_ref[...] = acc.astype(BF16)


def _merge(ys, hg, w_branch, layer, tm=2048):
    s = hg.shape[0]
    tm = math.gcd(tm, s)
    n_col = D_MODEL // TN
    y_specs = [pl.BlockSpec((tm, BRANCH_W), lambda j, i: (i, 0)) for _ in range(N_BRANCH)]
    g_specs = [pl.BlockSpec((tm, TN), functools.partial(lambda n, j, i: (i, n_col * n + j), n))
               for n in range(N_BRANCH)]
    return pl.pallas_call(
        _merge_kernel,
        grid=(n_col, s // tm),
        in_specs=y_specs + g_specs + [pl.BlockSpec((None, N_BRANCH, BRANCH_W, TN),
                                                   lambda j, i: (layer, 0, 0, j))],
        out_specs=pl.BlockSpec((tm, TN), lambda j, i: (i, j)),
        out_shape=jax.ShapeDtypeStruct((s, D_MODEL), BF16),
        scratch_shapes=[pltpu.VMEM((N_BRANCH, BRANCH_W, TN), BF16)],
        compiler_params=_cparams(("arbitrary", "arbitrary")),
        name="branch_merge",
    )(*ys, hg, hg, hg, hg, w_branch)


def _out_ln_kernel(alpha, tsub, m_ref, w_ref, x_ref, g_ref, b_ref, y_ref, ybf_ref, acc_ref):
    n_sub = m_ref.shape[0] // tsub

    def matmul(u):
        acc_ref[u % 2] = jnp.dot(m_ref[u * tsub:(u + 1) * tsub, :], w_ref[...], preferred_element_type=F32)

    matmul(0)
    for u in range(n_sub):
        if u + 1 < n_sub:
            matmul(u + 1)
        rows = slice(u * tsub, (u + 1) * tsub)
        y = _ln_rows(alpha * x_ref[rows, :] + acc_ref[u % 2], g_ref[...], b_ref[...])
        y_ref[rows, :] = y
        ybf_ref[rows, :] = y.astype(BF16)


def _out_ln(merged, wo_bf, layer, x, g, b, alpha, tm=512, tsub=128):
    s, d = x.shape
    return pl.pallas_call(
        functools.partial(_out_ln_kernel, alpha, tsub),
        grid=(s // tm,),
        in_specs=[pl.BlockSpec((tm, d), lambda i: (i, 0)),
                  pl.BlockSpec((None, d, d), lambda i: (layer, 0, 0)),
                  pl.BlockSpec((tm, d), lambda i: (i, 0)),
                  pl.BlockSpec((1, d), lambda i: (0, 0)),
                  pl.BlockSpec((1, d), lambda i: (0, 0))],
        out_specs=[pl.BlockSpec((tm, d), lambda i: (i, 0)),
                   pl.BlockSpec((tm, d), lambda i: (i, 0))],
        out_shape=[jax.ShapeDtypeStruct((s, d), F32), jax.ShapeDtypeStruct((s, d), BF16)],
        scratch_shapes=[pltpu.VMEM((2, tsub, d), F32)],
        compiler_params=_cparams(("arbitrary",)),
        name="out_ln",
    )(merged, wo_bf, x, g.reshape(1, d), b.reshape(1, d))


def kernel(x, emb_ln_g, emb_ln_b, w_in, b_gate, diff_lambda, diff_subln_g, nat_rpb,
           gqa_q_norm_g, gqa_k_norm_g, w_branch, w_out, ln_g, ln_b):
    batch, seq, d = x.shape
    assert batch == 1 and d == D_MODEL and w_in.shape[-1] == D_IN
    depth = w_in.shape[0]
    alpha = (2 * depth) ** 0.25
    tabs = _rope_tables(seq)
    nat_bias = _nat_bias(nat_rpb)
    wo_bf = w_out.astype(BF16)
    xf, xbf = _embed_ln(x[0], emb_ln_g, emb_ln_b)
    for l in range(depth):
        lam_init = 0.8 - 0.6 * math.exp(-0.3 * l)
        gains = jnp.stack([gqa_q_norm_g[l], gqa_k_norm_g[l]]).reshape(2, 1, LANES)
        hr, hp, hg, hw = _inproj(xbf, w_in, b_gate, l, tabs, gains)
        ya = _pair_attn(hr, hp, True, lam_init, diff_lambda[l], diff_subln_g[l])
        yb = _nat_attn(hp, nat_bias, l)
        yc = _pair_attn(hr, hp, False)
        yd = _dil_attn(hr, hp, hw)
        merged = _merge((ya, yb, yc, yd), hg, w_branch, l)
        xf, xbf = _out_ln(merged, wo_bf, l, xf, ln_g[l], ln_b[l], alpha)
    return xf[None]
```

```python
import functools
import math

import numpy as np
import jax
import jax.numpy as jnp
from jax import lax
from jax.experimental import pallas as pl
from jax.experimental.pallas import tpu as pltpu

F32 = jnp.float32
BF16 = jnp.bfloat16

D_MODEL = 2048
GRID_W = 64
HEAD_DIM = 128
N_BRANCH = 4
BRANCH_W = D_MODEL // 4
DA_QK = HEAD_DIM // 2
NB_ROWS = 8
NB_COLS = 16
AXIAL_THETA = 10000.0
DIL_RATES = (1, 4, 16)
DIL_SIDE = 64
ROPE_THETA = 500000.0
LN_EPS = 1e-5
RMS_EPS = 1e-6
NEG_INF = -1e30
LOG2E = math.log2(math.e)

LANES = 128
VT_PAD = 16
TN = 512
D_IN = 18944

ROT_TILES = (0, 1, 8, 9, 11, 12, 14, 15)
PLAIN_TILES = (2, 3, 4, 5, 6, 7, 10, 17, 18, 20)
WIDE_TILES = (13, 16, 19)
GATE_TILE0, N_GATE_TILES = 21, 16
ROT_VARIANT = (0, 1, 2, 3, 4, 4, 5, 5)
ROT_VARIANT_DQ, ROT_VARIANT_DK = 4, 5
CB_AQ, CB_AK, CB_CQ, CB_CK, CB_CV, CB_DQ, CB_DK = 0, 4, 8, 12, 14, 16, 24
CB_AV, CB_AZ, CB_BQ, CB_BK, CB_BV, CB_BZ, CB_CZ, CB_DV, CB_DZ = 0, 4, 8, 12, 16, 20, 24, 28, 36
CB_WQ, CB_WK, CB_WV = 0, 4, 8
PLAIN_SILU = (1, 5, 6, 9)

VMEM_LIMIT = 56 * 1024 * 1024


def _cparams(sem):
    return pltpu.CompilerParams(dimension_semantics=sem, vmem_limit_bytes=VMEM_LIMIT)


def _select(j, values):
    out = values[-1]
    for t in range(len(values) - 2, -1, -1):
        out = jnp.where(j == t, values[t], out)
    return out


def _any_of(j, tiles):
    c = j == tiles[0]
    for t in tiles[1:]:
        c = c | (j == t)
    return c


def _ln_rows(x, g, b):
    mu = jnp.mean(x, axis=-1, keepdims=True)
    xc = x - mu
    var = jnp.mean(xc * xc, axis=-1, keepdims=True)
    return xc * lax.rsqrt(var + LN_EPS) * g + b


def _embed_ln_kernel(x_ref, g_ref, b_ref, y_ref, ybf_ref):
    y = _ln_rows(x_ref[...], g_ref[...], b_ref[...])
    y_ref[...] = y
    ybf_ref[...] = y.astype(BF16)


def _embed_ln(x, g, b, tm=512):
    s, d = x.shape
    return pl.pallas_call(
        _embed_ln_kernel,
        grid=(s // tm,),
        in_specs=[pl.BlockSpec((tm, d), lambda i: (i, 0)),
                  pl.BlockSpec((1, d), lambda i: (0, 0)),
                  pl.BlockSpec((1, d), lambda i: (0, 0))],
        out_specs=[pl.BlockSpec((tm, d), lambda i: (i, 0)),
                   pl.BlockSpec((tm, d), lambda i: (i, 0))],
        out_shape=[jax.ShapeDtypeStruct((s, d), F32), jax.ShapeDtypeStruct((s, d), BF16)],
        compiler_params=_cparams(("arbitrary",)),
        name="embed_ln",
    )(x, g.reshape(1, d), b.reshape(1, d))


def _rope_tables(seq):
    t = np.arange(seq)

    def cs(pos, dim, theta):
        half = dim // 2
        inv = np.power(float(theta), -np.arange(half, dtype=np.float64) * 2.0 / dim)
        ang = pos.astype(np.float64)[:, None] * inv[None, :]
        return jnp.asarray(np.cos(ang), F32), jnp.asarray(np.sin(ang), F32)

    def group(cos, sin, width):
        half = cos.shape[1]
        pad = width - 2 * half
        one = jnp.ones((seq, pad), F32)
        zero = jnp.zeros((seq, pad), F32)
        zh = jnp.zeros((seq, half), F32)
        return (jnp.concatenate([cos, cos, one], 1),
                jnp.concatenate([zh, sin, zero], 1),
                jnp.concatenate([-sin, zh, zero], 1))

    ca, sa = cs(t, DA_QK // 4, ROPE_THETA)
    va = jnp.stack([jnp.concatenate([p, p], 1) for p in group(ca, sa, DA_QK)])
    cr, sr = cs(t // GRID_W, HEAD_DIM // 2, AXIAL_THETA)
    cc, sc = cs(t % GRID_W, HEAD_DIM // 2, AXIAL_THETA)
    vc = jnp.stack([jnp.concatenate([p, q], 1)
                    for p, q in zip(group(cr, sr, HEAD_DIM // 2), group(cc, sc, HEAD_DIM // 2))])
    cd, sd = cs(t, HEAD_DIM // 4, ROPE_THETA)
    vd = jnp.stack(group(cd, sd, HEAD_DIM))
    qa = DA_QK ** -0.5 * LOG2E
    qh = HEAD_DIM ** -0.5 * LOG2E
    return jnp.stack([va * qa, va, vc * qh, vc, vd * qh, vd])


ROT_HALF_A, ROT_HALF_C, ROT_HALF_D = DA_QK // 8, HEAD_DIM // 4, HEAD_DIM // 8


def _proj_kernel(tsub, n_sub, make_branches, n_extra, x_ref, w_ref, *rest):
    extra = rest[:n_extra]
    o_ref, wbf_ref, acc_ref = rest[n_extra:]
    j = pl.program_id(0)
    n_pass = x_ref.shape[0] // (n_sub * tsub)

    @pl.when(pl.program_id(1) == 0)
    def _():
        wbf_ref[...] = w_ref[...].astype(BF16)

    def pipeline(epilogue):
        def one_pass(h, carry):
            def rows_of(u):
                return pl.ds(pl.multiple_of((h * n_sub + u) * tsub, tsub), tsub)

            def matmul(u):
                acc_ref[u % 2] = jnp.dot(x_ref[rows_of(u), :], wbf_ref[...],
                                         preferred_element_type=F32)

            matmul(0)
            for u in range(n_sub):
                if u + 1 < n_sub:
                    matmul(u + 1)
                for g in range(TN // LANES):
                    lanes = slice(g * LANES, (g + 1) * LANES)
                    y = epilogue(acc_ref[u % 2, :, lanes], rows_of(u), g).astype(o_ref.dtype)
                    if len(o_ref.shape) == 3:
                        o_ref[g, rows_of(u), :] = y
                    else:
                        o_ref[rows_of(u), lanes] = y
            return carry

        lax.fori_loop(0, n_pass, one_pass, 0)

    for cond, epilogue in make_branches(*extra):
        if cond is None:
            pipeline(epilogue)
        else:
            pl.when(cond(j))(functools.partial(pipeline, epilogue))


def _proj(name, xbf, w_in, layer, src_tiles, make_branches, extra, extra_specs, head_major,
          tm=2048, tsub=256, n_sub=4, dtype=BF16):
    s, d = xbf.shape
    tm = math.gcd(tm, s)
    n_sub = min(n_sub, tm // tsub)
    assert tm % (tsub * n_sub) == 0
    n_tiles = len(src_tiles)
    contiguous = src_tiles == tuple(range(src_tiles[0], src_tiles[0] + n_tiles))
    src = (lambda j: src_tiles[0] + j) if contiguous else (lambda j: _select(j, src_tiles))
    if head_major:
        out_spec = pl.BlockSpec((TN // LANES, tm, LANES), lambda j, i: (j, i, 0))
        out_shape = jax.ShapeDtypeStruct((n_tiles * (TN // LANES), s, LANES), dtype)
    else:
        out_spec = pl.BlockSpec((tm, TN), lambda j, i: (i, j))
        out_shape = jax.ShapeDtypeStruct((s, n_tiles * TN), dtype)
    return pl.pallas_call(
        functools.partial(_proj_kernel, tsub, n_sub, make_branches, len(extra)),
        grid=(n_tiles, s // tm),
        in_specs=[pl.BlockSpec((tm, d), lambda j, i: (i, 0)),
                  pl.BlockSpec((None, d, TN), lambda j, i: (layer, 0, src(j)))] + extra_specs(tm),
        out_specs=out_spec,
        out_shape=out_shape,
        scratch_shapes=[pltpu.VMEM((d, TN), BF16), pltpu.VMEM((2, tsub, TN), F32)],
        compiler_params=_cparams(("arbitrary", "arbitrary")),
        name=name,
    )(xbf, w_in, *extra)


def _rot_branches(tab_ref, gn_ref):
    def rot(a, rows, half):
        return (a * tab_ref[0, rows, :] + pltpu.roll(a, half, 1) * tab_ref[1, rows, :]
                + pltpu.roll(a, LANES - half, 1) * tab_ref[2, rows, :])

    def rms(a, gain):
        ms = jnp.mean(a * a, axis=-1, keepdims=True)
        return a * lax.rsqrt(ms + RMS_EPS) * gain

    return [(lambda j: j <= 1, lambda a, rows, g: rot(a, rows, ROT_HALF_A)),
            (lambda j: j == 2, lambda a, rows, g: rot(rms(a, gn_ref[0]), rows, ROT_HALF_C)),
            (lambda j: j == 3, lambda a, rows, g: rot(rms(a, gn_ref[1]), rows, ROT_HALF_C) if g < 2 else a),
            (lambda j: j >= 4, lambda a, rows, g: rot(a, rows, ROT_HALF_D))]


def _plain_branches():
    return [(lambda j: _any_of(j, PLAIN_SILU), lambda a, rows, g: a * jax.nn.sigmoid(a)),
            (lambda j: jnp.logical_not(_any_of(j, PLAIN_SILU)), lambda a, rows, g: a)]


def _gate_branches(bg_ref):
    return [(None, lambda a, rows, g: jax.nn.sigmoid(a + bg_ref[:, g * LANES:(g + 1) * LANES]))]


def _wide_branches(tab_ref):
    def rot(a, rows):
        return (a * tab_ref[0, rows, :] + pltpu.roll(a, ROT_HALF_D, 1) * tab_ref[1, rows, :]
                + pltpu.roll(a, LANES - ROT_HALF_D, 1) * tab_ref[2, rows, :])

    return [(lambda j: j <= 1, lambda a, rows, g: rot(a, rows)),
            (lambda j: j == 2, lambda a, rows, g: a)]


def _inproj(xbf, w_in, b_gate, layer, tabs, gains):
    hr = _proj("inproj_rot", xbf, w_in, layer, ROT_TILES, _rot_branches, (tabs, gains),
               lambda tm: [pl.BlockSpec((None, 3, tm, LANES), lambda j, i: (_select(j, ROT_VARIANT), 0, i, 0)),
                           pl.BlockSpec((None, 2, 1, LANES), lambda j, i: (layer, 0, 0, 0))], True)
    hp = _proj("inproj_plain", xbf, w_in, layer, PLAIN_TILES, _plain_branches, (), lambda tm: [], True,
               n_sub=8)
    hg = _proj("inproj_gate", xbf, w_in, layer, tuple(range(GATE_TILE0, GATE_TILE0 + N_GATE_TILES)),
               _gate_branches, (b_gate,),
               lambda tm: [pl.BlockSpec((None, 1, TN), lambda j, i: (layer, 0, j))], False, n_sub=8)
    hw = _proj("inproj_wide", xbf, w_in, layer, WIDE_TILES, _wide_branches, (tabs,),
               lambda tm: [pl.BlockSpec((None, 3, tm, LANES),
                                        lambda j, i: (jnp.where(j == 0, ROT_VARIANT_DQ, ROT_VARIANT_DK), 0, i, 0))],
               True, n_sub=8, dtype=F32)
    return hr, hp, hg, hw


def _pair_attn_kernel(diff, tq, tk, lam_init, *refs):
    if diff:
        q_ref, k_ref, v_ref, z_ref, lp_ref, sg_ref, o_ref, q2_ref, vt_ref, acc_ref, s_ref = refs
    else:
        q_ref, k_ref, v_ref, z_ref, o_ref, q2_ref, vt_ref, acc_ref, s_ref = refs
    seq = k_ref.shape[0]
    nq = 2 * tq

    @pl.when(pl.program_id(1) == 0)
    def _():
        for c in range(seq // tk):
            vt_ref[c, :LANES, :] = v_ref[c * tk:(c + 1) * tk, :].astype(F32).T.astype(BF16)
            vt_ref[c, LANES:, :] = jnp.ones((VT_PAD, tk), BF16)

    if diff:
        q = q_ref[...]
        lane = lax.broadcasted_iota(jnp.int32, q.shape, 1)
        zero = jnp.zeros_like(q)
        q2_ref[:tq, :] = jnp.where(lane < DA_QK, q, zero)
        q2_ref[tq:, :] = jnp.where(lane >= DA_QK, q, zero)
    else:
        q2_ref[:tq, :] = q_ref[0]
        q2_ref[tq:, :] = q_ref[1]
    acc_ref[...] = jnp.zeros(acc_ref.shape, F32)

    n_chunks = seq // tk

    def scores(c, slot):
        start = pl.multiple_of(c * tk, tk)
        s = lax.dot_general(k_ref[pl.ds(start, tk), :], q2_ref[...], (((1,), (1,)), ((), ())),
                            preferred_element_type=F32)
        s_ref[slot] = s
        return jnp.max(s, axis=0, keepdims=True)

    def fold(c, slot, m_prev, m_chunk):
        m_new = jnp.maximum(m_prev, m_chunk)
        alpha = jnp.exp2(m_prev - m_new)
        p = jnp.exp2(s_ref[slot] - m_new)
        acc_ref[...] = alpha * acc_ref[...] + jnp.dot(vt_ref[c], p.astype(BF16),
                                                      preferred_element_type=F32)
        return m_new

    unroll = 4 if n_chunks % 4 == 0 else 2

    def group(base, carry, last):
        m, m_chunk = carry
        for u in range(unroll):
            m_next = m_chunk if (last and u == unroll - 1) else scores(base + u + 1, (u + 1) % 2)
            m = fold(base + u, u % 2, m, m_chunk)
            m_chunk = m_next
        return m, m_chunk

    carry = (jnp.full((1, nq), NEG_INF, F32), scores(0, 0))
    n_groups = n_chunks // unroll
    if n_groups > 1:
        carry = lax.fori_loop(0, n_groups - 1, lambda j, cc: group(j * unroll, cc, False), carry)
    group((n_groups - 1) * unroll, carry, True)

    acc = acc_ref[...]
    o = (acc[:LANES] / acc[LANES:LANES + 1]).T
    if diff:
        lp = lp_ref[...]
        lam = (jnp.exp(jnp.sum(lp[0:1] * lp[1:2], keepdims=True))
               - jnp.exp(jnp.sum(lp[2:3] * lp[3:4], keepdims=True)) + lam_init)
        dlt = o[:tq] - lam * o[tq:]
        ms = jnp.mean(dlt * dlt, axis=-1, keepdims=True)
        y = dlt * lax.rsqrt(ms + RMS_EPS) * sg_ref[...] * (1.0 - lam_init)
        o_ref[...] = (y * z_ref[...].astype(F32)).astype(BF16)
    else:
        o_ref[:, :LANES] = (o[:tq] * z_ref[0].astype(F32)).astype(BF16)
        o_ref[:, LANES:] = (o[tq:] * z_ref[1].astype(F32)).astype(BF16)


def _pair_attn(hr, hp, diff, layer=0, lam_init=0.0, lam_params=None, subln_g=None, tq=512, tk=512):
    s = hr.shape[1]
    if diff:
        n_kv, qw, qblk = 4, LANES, (None, tq, LANES)
        q_map = lambda hh, i: (CB_AQ + hh, i, 0)
        k_map = lambda hh, i: (CB_AK + hh, 0, 0)
        v_map = lambda hh, i: (CB_AV + hh, 0, 0)
        z_map = lambda hh, i: (CB_AZ + hh, i, 0)
    else:
        n_kv, qw, qblk = 2, 2 * LANES, (2, tq, LANES)
        q_map = lambda hh, i: (CB_CQ // 2 + hh, i, 0)
        k_map = lambda hh, i: (CB_CK + hh, 0, 0)
        v_map = lambda hh, i: (CB_CV + hh, 0, 0)
        z_map = lambda hh, i: (CB_CZ // 2 + hh, i, 0)
    in_specs = [pl.BlockSpec(qblk, q_map),
                pl.BlockSpec((None, s, LANES), k_map),
                pl.BlockSpec((None, s, LANES), v_map),
                pl.BlockSpec(qblk, z_map)]
    args = [hr, hr, hp if diff else hr, hp]
    if diff:
        in_specs += [pl.BlockSpec((None, 4, DA_QK), lambda hh, i: (layer, 0, 0)),
                     pl.BlockSpec((None, 1, LANES), lambda hh, i: (layer, 0, 0))]
        args += [lam_params, subln_g]
    return pl.pallas_call(
        functools.partial(_pair_attn_kernel, diff, tq, tk, lam_init),
        grid=(n_kv, s // tq),
        in_specs=in_specs,
        out_specs=pl.BlockSpec((tq, qw), lambda hh, i: (i, hh)),
        out_shape=jax.ShapeDtypeStruct((s, BRANCH_W), BF16),
        scratch_shapes=[pltpu.VMEM((2 * tq, LANES), BF16),
                        pltpu.VMEM((s // tk, LANES + VT_PAD, tk), BF16),
                        pltpu.VMEM((LANES + VT_PAD, 2 * tq), F32),
                        pltpu.VMEM((2, tk, 2 * tq), F32)],
        compiler_params=_cparams(("arbitrary", "arbitrary")),
        name="diff_attn" if diff else "axial_gqa",
    )(*args)


def _nat_bias(rpb):
    n_layers, n_heads = rpb.shape[:2]
    cols = np.arange(GRID_W)
    c0 = np.clip(cols - NB_COLS // 2, 0, GRID_W - NB_COLS)
    valid = (cols[None, :] >= c0[:, None]) & (cols[None, :] < c0[:, None] + NB_COLS)
    col_sel = (np.arange(2 * NB_COLS - 1)[:, None, None]
               == cols[None, None, :] - cols[None, :, None] + (NB_COLS - 1)).astype(np.float32)
    row_sel = (np.arange(2 * NB_ROWS - 1)[None, None, :]
               == np.arange(NB_ROWS)[:, None, None] + np.arange(NB_ROWS)[None, :, None]).astype(np.float32)
    exact = lax.Precision.HIGHEST
    toep = jnp.einsum('lhdr,rck->lhdck', rpb.astype(F32), col_sel, precision=exact)
    t = jnp.where(valid, toep, NEG_INF)
    bias = jnp.einsum('lhdck,sid->lshcik', t, row_sel, precision=exact)
    return bias.reshape(n_layers, NB_ROWS, n_heads, GRID_W, NB_ROWS * GRID_W)


def _nat_kernel(rows, rps, q_ref, k_ref, v_ref, z_ref, bias_ref, o_ref):
    win = NB_ROWS * GRID_W
    starts, scores = [], []
    for u in range(rps):
        r = pl.program_id(1) * rps + u
        r0 = jnp.clip(r - NB_ROWS // 2, 0, rows - NB_ROWS)
        start = pl.multiple_of(r0 * GRID_W, GRID_W)
        kw = k_ref[pl.ds(start, win), :]
        s = lax.dot_general(q_ref[u * GRID_W:(u + 1) * GRID_W, :], kw, (((1,), (1,)), ((), ())),
                            preferred_element_type=F32)
        scores.append(s * (HEAD_DIM ** -0.5) + bias_ref[r0 - r + (NB_ROWS - 1)])
        starts.append(start)
    probs = []
    for s in scores:
        p = jnp.exp(s - jnp.max(s, axis=-1, keepdims=True))
        probs.append((p.astype(BF16), jnp.sum(p, axis=-1, keepdims=True)))
    for u, (p, l) in enumerate(probs):
        qrows = slice(u * GRID_W, (u + 1) * GRID_W)
        o = jnp.dot(p, v_ref[pl.ds(starts[u], win), :], preferred_element_type=F32) / l
        o_ref[qrows, :] = (o * z_ref[qrows, :].astype(F32)).astype(BF16)


def _nat_attn(hp, bias, layer, rps=16):
    s = hp.shape[1]
    rows = s // GRID_W
    blk = rps * GRID_W
    return pl.pallas_call(
        functools.partial(_nat_kernel, rows, rps),
        grid=(4, rows // rps),
        in_specs=[pl.BlockSpec((None, blk, LANES), lambda hh, r: (CB_BQ + hh, r, 0)),
                  pl.BlockSpec((None, s, LANES), lambda hh, r: (CB_BK + hh, 0, 0)),
                  pl.BlockSpec((None, s, LANES), lambda hh, r: (CB_BV + hh, 0, 0)),
                  pl.BlockSpec((None, blk, LANES), lambda hh, r: (CB_BZ + hh, r, 0)),
                  pl.BlockSpec((None, NB_ROWS, None, GRID_W, NB_ROWS * GRID_W),
                               lambda hh, r: (layer, 0, hh, 0, 0))],
        out_specs=pl.BlockSpec((blk, LANES), lambda hh, r: (r, hh)),
        out_shape=jax.ShapeDtypeStruct((s, BRANCH_W), BF16),
        compiler_params=_cparams(("arbitrary", "arbitrary")),
        name="nat_attn",
    )(hp, hp, hp, hp, bias)


def _dil_masks(tq, rates):
    row = np.arange(tq)[:, None]

    def band(col, rate):
        d = col - row
        return np.where((d >= 0) & (d <= 2 * DIL_SIDE * rate) & (d % rate == 0), 0.0, NEG_INF).astype(np.float32)

    width0 = tq + 2 * DIL_SIDE
    tables = [np.stack([band(np.arange(width0)[None, :] + dlt, 1) for dlt in (DIL_SIDE, 0, -DIL_SIDE)])]
    assert rates[0] == 1
    for rate in rates[1:]:
        reach = DIL_SIDE * rate
        cols = np.arange(-reach, tq + 3 * reach)[None, :]
        full = band(cols, rate)
        tables.append(full.reshape(tq, -1, LANES).transpose(1, 0, 2))
    return [jnp.asarray(t) for t in tables]


def _dil_class_kernel(tq, rate, q_ref, k_ref, v_ref, b_ref, o_ref, lse_ref):
    length = q_ref.shape[0] // rate
    width = tq + 2 * DIL_SIDE
    for a in range(rate):
        cls = pl.ds(a, length, stride=rate)
        qa = q_ref[cls, :].astype(BF16)
        ka = k_ref[cls, :].astype(BF16)
        va = v_ref[cls, :].astype(BF16)
        scored = []
        for i0 in range(0, length, tq):
            ws = min(max(i0 - DIL_SIDE, 0), length - width)
            delta = ws - i0 + DIL_SIDE
            s = lax.dot_general(qa[i0:i0 + tq], ka[ws:ws + width], (((1,), (1,)), ((), ())),
                                preferred_element_type=F32)
            scored.append((i0, ws, s + b_ref[0 if delta > 0 else (2 if delta < 0 else 1)]))
        stats = []
        for i0, ws, s in scored:
            m = jnp.max(s, axis=-1, keepdims=True)
            p = jnp.exp2(s - m)
            stats.append((i0, ws, p.astype(BF16), m, jnp.sum(p, axis=-1, keepdims=True)))
        for i0, ws, p, m, l in stats:
            out_rows = pl.ds(a + rate * i0, tq, stride=rate)
            o_ref[out_rows, :] = jnp.dot(p, va[ws:ws + width], preferred_element_type=F32) / l
            lse_ref[out_rows, :] = jnp.broadcast_to(m + jnp.log2(l), (tq, LANES))


def _dil_kernel(tq, nsub, q0_ref, q1_ref, k0_ref, k1_ref, v0_ref, v1_ref, z_ref, b0_ref, b1_ref,
                o2_ref, lse2_ref, o_ref):
    seq = k0_ref.shape[0]
    q_refs, k_refs, v_refs, b_refs = (q0_ref, q1_ref), (k0_ref, k1_ref), (v0_ref, v1_ref), (b0_ref, b1_ref)
    scores, wins = [], []
    for u in range(nsub):
        t0 = (pl.program_id(1) * nsub + u) * tq
        rows = slice(u * tq, (u + 1) * tq)
        su, wu = [], []
        for rate, q_ref, k_ref, b_ref in zip(DIL_RATES, q_refs, k_refs, b_refs):
            reach = DIL_SIDE * rate
            width = tq + 2 * reach
            start = pl.multiple_of(jnp.clip(t0 - reach, 0, seq - width), DIL_SIDE)
            delta = start - t0 + reach
            s = lax.dot_general(q_ref[rows, :], k_ref[pl.ds(start, width), :], (((1,), (1,)), ((), ())),
                                preferred_element_type=F32)
            if rate == 1:
                s = s + b_ref[jnp.where(delta > 0, 0, jnp.where(delta < 0, 2, 1))]
            else:
                tile0 = (delta + reach) // LANES
                s = jnp.concatenate([s[:, j * LANES:(j + 1) * LANES] + b_ref[tile0 + j]
                                     for j in range(width // LANES)], axis=1)
            su.append(s)
            wu.append((start, width))
        scores.append(su)
        wins.append(wu)
    probs, dens, tops = [], [], []
    for su in scores:
        m = su[0].max(axis=-1, keepdims=True)
        for s in su[1:]:
            m = jnp.maximum(m, s.max(axis=-1, keepdims=True))
        pu = [jnp.exp2(s - m) for s in su]
        l = pu[0].sum(axis=-1, keepdims=True)
        for p in pu[1:]:
            l = l + p.sum(axis=-1, keepdims=True)
        probs.append([p.astype(BF16) for p in pu])
        dens.append(l)
        tops.append(m)
    for u, (pu, wu, l, m) in enumerate(zip(probs, wins, dens, tops)):
        rows = slice(u * tq, (u + 1) * tq)
        o = None
        for p, (start, width), v_ref in zip(pu, wu, v_refs):
            pv = jnp.dot(p, v_ref[pl.ds(start, width), :], preferred_element_type=F32)
            o = pv if o is None else o + pv
        lse2 = lse2_ref[rows, :1]
        top = jnp.maximum(m, lse2)
        w01 = jnp.exp2(m - top)
        w2 = jnp.exp2(lse2 - top)
        mixed = (w01 * o + w2 * o2_ref[rows, :]) / (w01 * l + w2)
        o_ref[rows, :] = (mixed * z_ref[rows, :].astype(F32)).astype(BF16)


def _dil_class_attn(hw, mask, tq=128):
    s = hw.shape[1]
    head = lambda cb: pl.BlockSpec((None, s, LANES), lambda hh: (cb + hh, 0, 0))
    return pl.pallas_call(
        functools.partial(_dil_class_kernel, tq, DIL_RATES[-1]),
        grid=(4,),
        in_specs=[head(CB_WQ), head(CB_WK), head(CB_WV), pl.BlockSpec(mask.shape, lambda hh: (0, 0, 0))],
        out_specs=[head(0), head(0)],
        out_shape=[jax.ShapeDtypeStruct((4, s, LANES), F32), jax.ShapeDtypeStruct((4, s, LANES), F32)],
        compiler_params=_cparams(("arbitrary",)),
        name="dil_class_attn",
    )(hw, hw, hw, mask)


def _dil_attn(hr, hp, hw, tq=128, nsub=8):
    s = hr.shape[1]
    n_dense = len(DIL_RATES) - 1
    masks = _dil_masks(tq, DIL_RATES[:n_dense])
    o2, lse2 = _dil_class_attn(hw, masks[0], tq)
    m_specs = [pl.BlockSpec(m.shape, lambda hh, i: (0, 0, 0)) for m in masks]
    blk = tq * nsub
    qs = [pl.BlockSpec((None, blk, LANES), functools.partial(lambda g, hh, i: (CB_DQ + 4 * g + hh, i, 0), g))
          for g in range(n_dense)]
    ks = [pl.BlockSpec((None, s, LANES), functools.partial(lambda g, hh, i: (CB_DK + 4 * g + hh, 0, 0), g))
          for g in range(n_dense)]
    vs = [pl.BlockSpec((None, s, LANES), functools.partial(lambda g, hh, i: (CB_DV + 4 * g + hh, 0, 0), g))
          for g in range(n_dense)]
    row_blk = pl.BlockSpec((None, blk, LANES), lambda hh, i: (hh, i, 0))
    return pl.pallas_call(
        functools.partial(_dil_kernel, tq, nsub),
        grid=(4, s // blk),
        in_specs=(qs + ks + vs + [pl.BlockSpec((None, blk, LANES), lambda hh, i: (CB_DZ + hh, i, 0))]
                  + m_specs + [row_blk, row_blk]),
        out_specs=pl.BlockSpec((blk, LANES), lambda hh, i: (i, hh)),
        out_shape=jax.ShapeDtypeStruct((s, BRANCH_W), BF16),
        compiler_params=_cparams(("arbitrary", "arbitrary")),
        name="dil_attn",
    )(*([hr] * (2 * n_dense)), *([hp] * (n_dense + 1)), *masks, o2, lse2)


def _merge_kernel(ya_ref, yb_ref, yc_ref, yd_ref, ga_ref, gb_ref, gc_ref, gd_ref, wb_ref, o_ref, wbf_ref):
    @pl.when(pl.program_id(1) == 0)
    def _():
        wbf_ref[...] = wb_ref[...].astype(BF16)

    acc = None
    for n, (y_ref, g_ref) in enumerate(zip((ya_ref, yb_ref, yc_ref, yd_ref),
                                           (ga_ref, gb_ref, gc_ref, gd_ref))):
        proj = jnp.dot(y_ref[...], wbf_ref[n], preferred_element_type=F32)
        term = proj * g_ref[...].astype(F32)
        acc = term if acc is None else acc + term
    o_ref[...] = acc.astype(BF16)


def _merge(ys, hg, w_branch, layer, tm=2048):
    s = hg.shape[0]
    tm = math.gcd(tm, s)
    n_col = D_MODEL // TN
    y_specs = [pl.BlockSpec((tm, BRANCH_W), lambda j, i: (i, 0)) for _ in range(N_BRANCH)]
    g_specs = [pl.BlockSpec((tm, TN), functools.partial(lambda n, j, i: (i, n_col * n + j), n))
               for n in range(N_BRANCH)]
    return pl.pallas_call(
        _merge_kernel,
        grid=(n_col, s // tm),
        in_specs=y_specs + g_specs + [pl.BlockSpec((None, N_BRANCH, BRANCH_W, TN),
                                                   lambda j, i: (layer, 0, 0, j))],
        out_specs=pl.BlockSpec((tm, TN), lambda j, i: (i, j)),
        out_shape=jax.ShapeDtypeStruct((s, D_MODEL), BF16),
        scratch_shapes=[pltpu.VMEM((N_BRANCH, BRANCH_W, TN), BF16)],
        compiler_params=_cparams(("arbitrary", "arbitrary")),
        name="branch_merge",
    )(*ys, hg, hg, hg, hg, w_branch)


def _out_ln_kernel(alpha, tsub, m_ref, w_ref, x_ref, g_ref, b_ref, y_ref, ybf_ref, acc_ref):
    n_sub = m_ref.shape[0] // tsub

    def matmul(u):
        acc_ref[u % 2] = jnp.dot(m_ref[u * tsub:(u + 1) * tsub, :], w_ref[...], preferred_element_type=F32)

    matmul(0)
    for u in range(n_sub):
        if u + 1 < n_sub:
            matmul(u + 1)
        rows = slice(u * tsub, (u + 1) * tsub)
        y = _ln_rows(alpha * x_ref[rows, :] + acc_ref[u % 2], g_ref[...], b_ref[...])
        y_ref[rows, :] = y
        ybf_ref[rows, :] = y.astype(BF16)


def _out_ln(merged, wo_bf, layer, x, g, b, alpha, tm=512, tsub=128):
    s, d = x.shape
    return pl.pallas_call(
        functools.partial(_out_ln_kernel, alpha, tsub),
        grid=(s // tm,),
        in_specs=[pl.BlockSpec((tm, d), lambda i: (i, 0)),
                  pl.BlockSpec((None, d, d), lambda i: (layer, 0, 0)),
                  pl.BlockSpec((tm, d), lambda i: (i, 0)),
                  pl.BlockSpec((None, 1, d), lambda i: (layer, 0, 0)),
                  pl.BlockSpec((None, 1, d), lambda i: (layer, 0, 0))],
        out_specs=[pl.BlockSpec((tm, d), lambda i: (i, 0)),
                   pl.BlockSpec((tm, d), lambda i: (i, 0))],
        out_shape=[jax.ShapeDtypeStruct((s, d), F32), jax.ShapeDtypeStruct((s, d), BF16)],
        scratch_shapes=[pltpu.VMEM((2, tsub, d), F32)],
        compiler_params=_cparams(("arbitrary",)),
        name="out_ln",
    )(merged, wo_bf, x, g, b)


def kernel(x, emb_ln_g, emb_ln_b, w_in, b_gate, diff_lambda, diff_subln_g, nat_rpb,
           gqa_q_norm_g, gqa_k_norm_g, w_branch, w_out, ln_g, ln_b):
    batch, seq, d = x.shape
    assert batch == 1 and d == D_MODEL and w_in.shape[-1] == D_IN
    depth = w_in.shape[0]
    alpha = (2 * depth) ** 0.25
    tabs = _rope_tables(seq)
    nat_bias = _nat_bias(nat_rpb)
    wo_bf = w_out.astype(BF16)
    gains = jnp.stack([gqa_q_norm_g, gqa_k_norm_g], axis=1).reshape(depth, 2, 1, LANES)
    b_gate3 = b_gate.reshape(depth, 1, -1)
    subln3 = diff_subln_g.reshape(depth, 1, LANES)
    ln_g3, ln_b3 = ln_g.reshape(depth, 1, d), ln_b.reshape(depth, 1, d)
    xf, xbf = _embed_ln(x[0], emb_ln_g, emb_ln_b)
    for l in range(depth):
        lam_init = 0.8 - 0.6 * math.exp(-0.3 * l)
        hr, hp, hg, hw = _inproj(xbf, w_in, b_gate3, l, tabs, gains)
        ya = _pair_attn(hr, hp, True, l, lam_init, diff_lambda, subln3)
        yb = _nat_attn(hp, nat_bias, l)
        yc = _pair_attn(hr, hp, False)
        yd = _dil_attn(hr, hp, hw)
        merged = _merge((ya, yb, yc, yd), hg, w_branch, l)
        xf, xbf = _out_ln(merged, wo_bf, l, xf, ln_g3, ln_b3, alpha)
    return xf[None]
```

```python
import functools
import math

import numpy as np
import jax
import jax.numpy as jnp
from jax import lax
from jax.experimental import pallas as pl
from jax.experimental.pallas import tpu as pltpu

F32 = jnp.float32
BF16 = jnp.bfloat16

D_MODEL = 2048
GRID_W = 64
HEAD_DIM = 128
N_BRANCH = 4
BRANCH_W = D_MODEL // 4
DA_QK = HEAD_DIM // 2
NB_ROWS = 8
NB_COLS = 16
AXIAL_THETA = 10000.0
DIL_RATES = (1, 4, 16)
DIL_SIDE = 64
ROPE_THETA = 500000.0
LN_EPS = 1e-5
RMS_EPS = 1e-6
NEG_INF = -1e30
LOG2E = math.log2(math.e)

LANES = 128
VT_PAD = 16
TN = 512
D_IN = 18944

ROT_TILES = (0, 1, 8, 9, 11, 12, 14, 15)
PLAIN_TILES = (2, 3, 4, 5, 6, 7, 10, 17, 18, 20)
WIDE_TILES = (13, 16, 19)
GATE_TILE0, N_GATE_TILES = 21, 16
ROT_VARIANT = (0, 1, 2, 3, 4, 4, 5, 5)
ROT_VARIANT_DQ, ROT_VARIANT_DK = 4, 5
CB_AQ, CB_AK, CB_CQ, CB_CK, CB_CV, CB_DQ, CB_DK = 0, 4, 8, 12, 14, 16, 24
CB_AV, CB_AZ, CB_BQ, CB_BK, CB_BV, CB_BZ, CB_CZ, CB_DV, CB_DZ = 0, 4, 8, 12, 16, 20, 24, 28, 36
CB_WQ, CB_WK, CB_WV = 0, 4, 8
PLAIN_SILU = (1, 5, 6, 9)

VMEM_LIMIT = 56 * 1024 * 1024


def _cparams(sem):
    return pltpu.CompilerParams(dimension_semantics=sem, vmem_limit_bytes=VMEM_LIMIT)


def _select(j, values):
    out = values[-1]
    for t in range(len(values) - 2, -1, -1):
        out = jnp.where(j == t, values[t], out)
    return out


def _any_of(j, tiles):
    c = j == tiles[0]
    for t in tiles[1:]:
        c = c | (j == t)
    return c


def _ln_rows(x, g, b):
    mu = jnp.mean(x, axis=-1, keepdims=True)
    xc = x - mu
    var = jnp.mean(xc * xc, axis=-1, keepdims=True)
    return xc * lax.rsqrt(var + LN_EPS) * g + b


def _embed_ln_kernel(x_ref, g_ref, b_ref, y_ref, ybf_ref):
    y = _ln_rows(x_ref[...], g_ref[...], b_ref[...])
    y_ref[...] = y
    ybf_ref[...] = y.astype(BF16)


def _embed_ln(x, g, b, tm=512):
    s, d = x.shape
    return pl.pallas_call(
        _embed_ln_kernel,
        grid=(s // tm,),
        in_specs=[pl.BlockSpec((tm, d), lambda i: (i, 0)),
                  pl.BlockSpec((1, d), lambda i: (0, 0)),
                  pl.BlockSpec((1, d), lambda i: (0, 0))],
        out_specs=[pl.BlockSpec((tm, d), lambda i: (i, 0)),
                   pl.BlockSpec((tm, d), lambda i: (i, 0))],
        out_shape=[jax.ShapeDtypeStruct((s, d), F32), jax.ShapeDtypeStruct((s, d), BF16)],
        compiler_params=_cparams(("arbitrary",)),
        name="embed_ln",
    )(x, g.reshape(1, d), b.reshape(1, d))


def _rope_tables(seq):
    t = np.arange(seq)

    def cs(pos, dim, theta):
        half = dim // 2
        inv = np.power(float(theta), -np.arange(half, dtype=np.float64) * 2.0 / dim)
        ang = pos.astype(np.float64)[:, None] * inv[None, :]
        return jnp.asarray(np.cos(ang), F32), jnp.asarray(np.sin(ang), F32)

    def group(cos, sin, width):
        half = cos.shape[1]
        pad = width - 2 * half
        one = jnp.ones((seq, pad), F32)
        zero = jnp.zeros((seq, pad), F32)
        zh = jnp.zeros((seq, half), F32)
        return (jnp.concatenate([cos, cos, one], 1),
                jnp.concatenate([zh, sin, zero], 1),
                jnp.concatenate([-sin, zh, zero], 1))

    ca, sa = cs(t, DA_QK // 4, ROPE_THETA)
    va = jnp.stack([jnp.concatenate([p, p], 1) for p in group(ca, sa, DA_QK)])
    cr, sr = cs(t // GRID_W, HEAD_DIM // 2, AXIAL_THETA)
    cc, sc = cs(t % GRID_W, HEAD_DIM // 2, AXIAL_THETA)
    vc = jnp.stack([jnp.concatenate([p, q], 1)
                    for p, q in zip(group(cr, sr, HEAD_DIM // 2), group(cc, sc, HEAD_DIM // 2))])
    cd, sd = cs(t, HEAD_DIM // 4, ROPE_THETA)
    vd = jnp.stack(group(cd, sd, HEAD_DIM))
    qa = DA_QK ** -0.5 * LOG2E
    qh = HEAD_DIM ** -0.5 * LOG2E
    return jnp.stack([va * qa, va, vc * qh, vc, vd * qh, vd])


ROT_HALF_A, ROT_HALF_C, ROT_HALF_D = DA_QK // 8, HEAD_DIM // 4, HEAD_DIM // 8


def _proj_kernel(tsub, n_sub, make_branches, n_extra, x_ref, w_ref, *rest):
    extra = rest[:n_extra]
    o_ref, wbf_ref, acc_ref = rest[n_extra:]
    j = pl.program_id(0)
    n_pass = x_ref.shape[0] // (n_sub * tsub)

    @pl.when(pl.program_id(1) == 0)
    def _():
        wbf_ref[...] = w_ref[...].astype(BF16)

    def pipeline(epilogue):
        def one_pass(h, carry):
            def rows_of(u):
                return pl.ds(pl.multiple_of((h * n_sub + u) * tsub, tsub), tsub)

            def matmul(u):
                acc_ref[u % 2] = jnp.dot(x_ref[rows_of(u), :], wbf_ref[...],
                                         preferred_element_type=F32)

            matmul(0)
            for u in range(n_sub):
                if u + 1 < n_sub:
                    matmul(u + 1)
                for g in range(TN // LANES):
                    lanes = slice(g * LANES, (g + 1) * LANES)
                    y = epilogue(acc_ref[u % 2, :, lanes], rows_of(u), g).astype(o_ref.dtype)
                    if len(o_ref.shape) == 3:
                        o_ref[g, rows_of(u), :] = y
                    else:
                        o_ref[rows_of(u), lanes] = y
            return carry

        lax.fori_loop(0, n_pass, one_pass, 0)

    for cond, epilogue in make_branches(*extra):
        if cond is None:
            pipeline(epilogue)
        else:
            pl.when(cond(j))(functools.partial(pipeline, epilogue))


def _proj(name, xbf, w_in, layer, src_tiles, make_branches, extra, extra_specs, head_major,
          tm=2048, tsub=256, n_sub=4, dtype=BF16):
    s, d = xbf.shape
    tm = math.gcd(tm, s)
    n_sub = min(n_sub, tm // tsub)
    assert tm % (tsub * n_sub) == 0
    n_tiles = len(src_tiles)
    contiguous = src_tiles == tuple(range(src_tiles[0], src_tiles[0] + n_tiles))
    src = (lambda j: src_tiles[0] + j) if contiguous else (lambda j: _select(j, src_tiles))
    if head_major:
        out_spec = pl.BlockSpec((TN // LANES, tm, LANES), lambda j, i: (j, i, 0))
        out_shape = jax.ShapeDtypeStruct((n_tiles * (TN // LANES), s, LANES), dtype)
    else:
        out_spec = pl.BlockSpec((tm, TN), lambda j, i: (i, j))
        out_shape = jax.ShapeDtypeStruct((s, n_tiles * TN), dtype)
    return pl.pallas_call(
        functools.partial(_proj_kernel, tsub, n_sub, make_branches, len(extra)),
        grid=(n_tiles, s // tm),
        in_specs=[pl.BlockSpec((tm, d), lambda j, i: (i, 0)),
                  pl.BlockSpec((None, d, TN), lambda j, i: (layer, 0, src(j)))] + extra_specs(tm),
        out_specs=out_spec,
        out_shape=out_shape,
        scratch_shapes=[pltpu.VMEM((d, TN), BF16), pltpu.VMEM((2, tsub, TN), F32)],
        compiler_params=_cparams(("arbitrary", "arbitrary")),
        name=name,
    )(xbf, w_in, *extra)


def _rot_branches(tab_ref, gn_ref):
    def rot(a, rows, half):
        return (a * tab_ref[0, rows, :] + pltpu.roll(a, half, 1) * tab_ref[1, rows, :]
                + pltpu.roll(a, LANES - half, 1) * tab_ref[2, rows, :])

    def rms(a, gain):
        ms = jnp.mean(a * a, axis=-1, keepdims=True)
        return a * lax.rsqrt(ms + RMS_EPS) * gain

    return [(lambda j: j <= 1, lambda a, rows, g: rot(a, rows, ROT_HALF_A)),
            (lambda j: j == 2, lambda a, rows, g: rot(rms(a, gn_ref[0]), rows, ROT_HALF_C)),
            (lambda j: j == 3, lambda a, rows, g: rot(rms(a, gn_ref[1]), rows, ROT_HALF_C) if g < 2 else a),
            (lambda j: j >= 4, lambda a, rows, g: rot(a, rows, ROT_HALF_D))]


def _plain_branches():
    return [(lambda j: _any_of(j, PLAIN_SILU), lambda a, rows, g: a * jax.nn.sigmoid(a)),
            (lambda j: jnp.logical_not(_any_of(j, PLAIN_SILU)), lambda a, rows, g: a)]


def _gate_branches(bg_ref):
    return [(None, lambda a, rows, g: jax.nn.sigmoid(a + bg_ref[:, g * LANES:(g + 1) * LANES]))]


def _wide_branches(tab_ref):
    def rot(a, rows):
        return (a * tab_ref[0, rows, :] + pltpu.roll(a, ROT_HALF_D, 1) * tab_ref[1, rows, :]
                + pltpu.roll(a, LANES - ROT_HALF_D, 1) * tab_ref[2, rows, :])

    return [(lambda j: j <= 1, lambda a, rows, g: rot(a, rows)),
            (lambda j: j == 2, lambda a, rows, g: a)]


def _inproj(xbf, w_in, b_gate, layer, tabs, gains):
    hr = _proj("inproj_rot", xbf, w_in, layer, ROT_TILES, _rot_branches, (tabs, gains),
               lambda tm: [pl.BlockSpec((None, 3, tm, LANES), lambda j, i: (_select(j, ROT_VARIANT), 0, i, 0)),
                           pl.BlockSpec((None, 2, 1, LANES), lambda j, i: (layer, 0, 0, 0))], True)
    hp = _proj("inproj_plain", xbf, w_in, layer, PLAIN_TILES, _plain_branches, (), lambda tm: [], True,
               n_sub=8)
    hg = _proj("inproj_gate", xbf, w_in, layer, tuple(range(GATE_TILE0, GATE_TILE0 + N_GATE_TILES)),
               _gate_branches, (b_gate,),
               lambda tm: [pl.BlockSpec((None, 1, TN), lambda j, i: (layer, 0, j))], False, n_sub=8)
    hw = _proj("inproj_wide", xbf, w_in, layer, WIDE_TILES, _wide_branches, (tabs,),
               lambda tm: [pl.BlockSpec((None, 3, tm, LANES),
                                        lambda j, i: (jnp.where(j == 0, ROT_VARIANT_DQ, ROT_VARIANT_DK), 0, i, 0))],
               True, n_sub=8, dtype=F32)
    return hr, hp, hg, hw


def _pair_attn_kernel(diff, tq, tk, lam_init, *refs):
    if diff:
        q_ref, k_ref, v_ref, z_ref, lp_ref, sg_ref, o_ref, q2_ref, vt_ref, acc_ref, s_ref = refs
    else:
        q_ref, k_ref, v_ref, z_ref, o_ref, q2_ref, vt_ref, acc_ref, s_ref = refs
    seq = k_ref.shape[0]
    nq = 2 * tq

    @pl.when(pl.program_id(1) == 0)
    def _():
        for c in range(seq // tk):
            vt_ref[c, :LANES, :] = v_ref[c * tk:(c + 1) * tk, :].astype(F32).T.astype(BF16)
            vt_ref[c, LANES:, :] = jnp.ones((VT_PAD, tk), BF16)

    if diff:
        q = q_ref[...]
        lane = lax.broadcasted_iota(jnp.int32, q.shape, 1)
        zero = jnp.zeros_like(q)
        q2_ref[:tq, :] = jnp.where(lane < DA_QK, q, zero)
        q2_ref[tq:, :] = jnp.where(lane >= DA_QK, q, zero)
    else:
        q2_ref[:tq, :] = q_ref[0]
        q2_ref[tq:, :] = q_ref[1]
    acc_ref[...] = jnp.zeros(acc_ref.shape, F32)

    n_chunks = seq // tk

    def scores(c, slot):
        start = pl.multiple_of(c * tk, tk)
        s = lax.dot_general(k_ref[pl.ds(start, tk), :], q2_ref[...], (((1,), (1,)), ((), ())),
                            preferred_element_type=F32)
        s_ref[slot] = s
        return jnp.max(s, axis=0, keepdims=True)

    def fold(c, slot, m_prev, m_chunk):
        m_new = jnp.maximum(m_prev, m_chunk)
        alpha = jnp.exp2(m_prev - m_new)
        p = jnp.exp2(s_ref[slot] - m_new)
        acc_ref[...] = alpha * acc_ref[...] + jnp.dot(vt_ref[c], p.astype(BF16),
                                                      preferred_element_type=F32)
        return m_new

    unroll = 4 if n_chunks % 4 == 0 else 2

    def group(base, carry, last):
        m, m_chunk = carry
        for u in range(unroll):
            m_next = m_chunk if (last and u == unroll - 1) else scores(base + u + 1, (u + 1) % 2)
            m = fold(base + u, u % 2, m, m_chunk)
            m_chunk = m_next
        return m, m_chunk

    carry = (jnp.full((1, nq), NEG_INF, F32), scores(0, 0))
    n_groups = n_chunks // unroll
    if n_groups > 1:
        carry = lax.fori_loop(0, n_groups - 1, lambda j, cc: group(j * unroll, cc, False), carry)
    group((n_groups - 1) * unroll, carry, True)

    acc = acc_ref[...]
    o = (acc[:LANES] / acc[LANES:LANES + 1]).T
    if diff:
        lp = lp_ref[...]
        lam = (jnp.exp(jnp.sum(lp[0:1] * lp[1:2], keepdims=True))
               - jnp.exp(jnp.sum(lp[2:3] * lp[3:4], keepdims=True)) + lam_init)
        dlt = o[:tq] - lam * o[tq:]
        ms = jnp.mean(dlt * dlt, axis=-1, keepdims=True)
        y = dlt * lax.rsqrt(ms + RMS_EPS) * sg_ref[...] * (1.0 - lam_init)
        o_ref[...] = (y * z_ref[...].astype(F32)).astype(BF16)
    else:
        o_ref[:, :LANES] = (o[:tq] * z_ref[0].astype(F32)).astype(BF16)
        o_ref[:, LANES:] = (o[tq:] * z_ref[1].astype(F32)).astype(BF16)


def _pair_attn(hr, hp, diff, layer=0, lam_init=0.0, lam_params=None, subln_g=None, tq=512, tk=512):
    s = hr.shape[1]
    if diff:
        n_kv, qw, qblk = 4, LANES, (None, tq, LANES)
        q_map = lambda hh, i: (CB_AQ + hh, i, 0)
        k_map = lambda hh, i: (CB_AK + hh, 0, 0)
        v_map = lambda hh, i: (CB_AV + hh, 0, 0)
        z_map = lambda hh, i: (CB_AZ + hh, i, 0)
    else:
        n_kv, qw, qblk = 2, 2 * LANES, (2, tq, LANES)
        q_map = lambda hh, i: (CB_CQ // 2 + hh, i, 0)
        k_map = lambda hh, i: (CB_CK + hh, 0, 0)
        v_map = lambda hh, i: (CB_CV + hh, 0, 0)
        z_map = lambda hh, i: (CB_CZ // 2 + hh, i, 0)
    in_specs = [pl.BlockSpec(qblk, q_map),
                pl.BlockSpec((None, s, LANES), k_map),
                pl.BlockSpec((None, s, LANES), v_map),
                pl.BlockSpec(qblk, z_map)]
    args = [hr, hr, hp if diff else hr, hp]
    if diff:
        in_specs += [pl.BlockSpec((None, 4, DA_QK), lambda hh, i: (layer, 0, 0)),
                     pl.BlockSpec((None, 1, LANES), lambda hh, i: (layer, 0, 0))]
        args += [lam_params, subln_g]
    return pl.pallas_call(
        functools.partial(_pair_attn_kernel, diff, tq, tk, lam_init),
        grid=(n_kv, s // tq),
        in_specs=in_specs,
        out_specs=pl.BlockSpec((tq, qw), lambda hh, i: (i, hh)),
        out_shape=jax.ShapeDtypeStruct((s, BRANCH_W), BF16),
        scratch_shapes=[pltpu.VMEM((2 * tq, LANES), BF16),
                        pltpu.VMEM((s // tk, LANES + VT_PAD, tk), BF16),
                        pltpu.VMEM((LANES + VT_PAD, 2 * tq), F32),
                        pltpu.VMEM((2, tk, 2 * tq), F32)],
        compiler_params=_cparams(("arbitrary", "arbitrary")),
        name="diff_attn" if diff else "axial_gqa",
    )(*args)


def _nat_bias(rpb):
    n_layers, n_heads = rpb.shape[:2]
    cols = np.arange(GRID_W)
    c0 = np.clip(cols - NB_COLS // 2, 0, GRID_W - NB_COLS)
    valid = (cols[None, :] >= c0[:, None]) & (cols[None, :] < c0[:, None] + NB_COLS)
    col_sel = (np.arange(2 * NB_COLS - 1)[:, None, None]
               == cols[None, None, :] - cols[None, :, None] + (NB_COLS - 1)).astype(np.float32)
    row_sel = (np.arange(2 * NB_ROWS - 1)[None, None, :]
               == np.arange(NB_ROWS)[:, None, None] + np.arange(NB_ROWS)[None, :, None]).astype(np.float32)
    exact = lax.Precision.HIGHEST
    toep = jnp.einsum('lhdr,rck->lhdck', rpb.astype(F32), col_sel, precision=exact)
    t = jnp.where(valid, toep, NEG_INF)
    bias = jnp.einsum('lhdck,sid->lshcik', t, row_sel, precision=exact)
    return bias.reshape(n_layers, NB_ROWS, n_heads, GRID_W, NB_ROWS * GRID_W)


def _nat_kernel(rows, rps, q_ref, k_ref, v_ref, z_ref, bias_ref, o_ref):
    win = NB_ROWS * GRID_W
    starts, scores = [], []
    for u in range(rps):
        r = pl.program_id(1) * rps + u
        r0 = jnp.clip(r - NB_ROWS // 2, 0, rows - NB_ROWS)
        start = pl.multiple_of(r0 * GRID_W, GRID_W)
        kw = k_ref[pl.ds(start, win), :]
        s = lax.dot_general(q_ref[u * GRID_W:(u + 1) * GRID_W, :], kw, (((1,), (1,)), ((), ())),
                            preferred_element_type=F32)
        scores.append(s * (HEAD_DIM ** -0.5) + bias_ref[r0 - r + (NB_ROWS - 1)])
        starts.append(start)
    probs = []
    for s in scores:
        p = jnp.exp(s - jnp.max(s, axis=-1, keepdims=True))
        probs.append((p.astype(BF16), jnp.sum(p, axis=-1, keepdims=True)))
    for u, (p, l) in enumerate(probs):
        qrows = slice(u * GRID_W, (u + 1) * GRID_W)
        o = jnp.dot(p, v_ref[pl.ds(starts[u], win), :], preferred_element_type=F32) / l
        o_ref[qrows, :] = (o * z_ref[qrows, :].astype(F32)).astype(BF16)


def _nat_attn(hp, bias, layer, rps=32):
    s = hp.shape[1]
    rows = s // GRID_W
    blk = rps * GRID_W
    return pl.pallas_call(
        functools.partial(_nat_kernel, rows, rps),
        grid=(4, rows // rps),
        in_specs=[pl.BlockSpec((None, blk, LANES), lambda hh, r: (CB_BQ + hh, r, 0)),
                  pl.BlockSpec((None, s, LANES), lambda hh, r: (CB_BK + hh, 0, 0)),
                  pl.BlockSpec((None, s, LANES), lambda hh, r: (CB_BV + hh, 0, 0)),
                  pl.BlockSpec((None, blk, LANES), lambda hh, r: (CB_BZ + hh, r, 0)),
                  pl.BlockSpec((None, NB_ROWS, None, GRID_W, NB_ROWS * GRID_W),
                               lambda hh, r: (layer, 0, hh, 0, 0))],
        out_specs=pl.BlockSpec((blk, LANES), lambda hh, r: (r, hh)),
        out_shape=jax.ShapeDtypeStruct((s, BRANCH_W), BF16),
        compiler_params=_cparams(("arbitrary", "arbitrary")),
        name="nat_attn",
    )(hp, hp, hp, hp, bias)


def _dil_masks(tq, rates):
    row = np.arange(tq)[:, None]

    def band(col, rate):
        d = col - row
        return np.where((d >= 0) & (d <= 2 * DIL_SIDE * rate) & (d % rate == 0), 0.0, NEG_INF).astype(np.float32)

    width0 = tq + 2 * DIL_SIDE
    tables = [np.stack([band(np.arange(width0)[None, :] + dlt, 1) for dlt in (DIL_SIDE, 0, -DIL_SIDE)])]
    assert rates[0] == 1
    for rate in rates[1:]:
        reach = DIL_SIDE * rate
        cols = np.arange(-reach, tq + 3 * reach)[None, :]
        full = band(cols, rate)
        tables.append(full.reshape(tq, -1, LANES).transpose(1, 0, 2))
    return [jnp.asarray(t) for t in tables]


def _dil_class_kernel(tq, rate, q_ref, k_ref, v_ref, b_ref, o_ref, lse_ref):
    length = q_ref.shape[0] // rate
    width = tq + 2 * DIL_SIDE
    for a in range(rate):
        cls = pl.ds(a, length, stride=rate)
        qa = q_ref[cls, :].astype(BF16)
        ka = k_ref[cls, :].astype(BF16)
        va = v_ref[cls, :].astype(BF16)
        scored = []
        for i0 in range(0, length, tq):
            ws = min(max(i0 - DIL_SIDE, 0), length - width)
            delta = ws - i0 + DIL_SIDE
            s = lax.dot_general(qa[i0:i0 + tq], ka[ws:ws + width], (((1,), (1,)), ((), ())),
                                preferred_element_type=F32)
            scored.append((i0, ws, s + b_ref[0 if delta > 0 else (2 if delta < 0 else 1)]))
        stats = []
        for i0, ws, s in scored:
            m = jnp.max(s, axis=-1, keepdims=True)
            p = jnp.exp2(s - m)
            stats.append((i0, ws, p.astype(BF16), m, jnp.sum(p, axis=-1, keepdims=True)))
        for i0, ws, p, m, l in stats:
            out_rows = pl.ds(a + rate * i0, tq, stride=rate)
            o_ref[out_rows, :] = jnp.dot(p, va[ws:ws + width], preferred_element_type=F32) / l
            lse_ref[out_rows, :] = jnp.broadcast_to(m + jnp.log2(l), (tq, LANES))


def _dil_kernel(tq, nsub, q0_ref, q1_ref, k0_ref, k1_ref, v0_ref, v1_ref, z_ref, b0_ref, b1_ref,
                o2_ref, lse2_ref, o_ref):
    seq = k0_ref.shape[0]
    q_refs, k_refs, v_refs, b_refs = (q0_ref, q1_ref), (k0_ref, k1_ref), (v0_ref, v1_ref), (b0_ref, b1_ref)
    scores, wins = [], []
    for u in range(nsub):
        t0 = (pl.program_id(1) * nsub + u) * tq
        rows = slice(u * tq, (u + 1) * tq)
        su, wu = [], []
        for rate, q_ref, k_ref, b_ref in zip(DIL_RATES, q_refs, k_refs, b_refs):
            reach = DIL_SIDE * rate
            width = tq + 2 * reach
            start = pl.multiple_of(jnp.clip(t0 - reach, 0, seq - width), DIL_SIDE)
            delta = start - t0 + reach
            s = lax.dot_general(q_ref[rows, :], k_ref[pl.ds(start, width), :], (((1,), (1,)), ((), ())),
                                preferred_element_type=F32)
            if rate == 1:
                s = s + b_ref[jnp.where(delta > 0, 0, jnp.where(delta < 0, 2, 1))]
            else:
                tile0 = (delta + reach) // LANES
                s = jnp.concatenate([s[:, j * LANES:(j + 1) * LANES] + b_ref[tile0 + j]
                                     for j in range(width // LANES)], axis=1)
            su.append(s)
            wu.append((start, width))
        scores.append(su)
        wins.append(wu)
    probs, dens, tops = [], [], []
    for su in scores:
        m = su[0].max(axis=-1, keepdims=True)
        for s in su[1:]:
            m = jnp.maximum(m, s.max(axis=-1, keepdims=True))
        pu = [jnp.exp2(s - m) for s in su]
        l = pu[0].sum(axis=-1, keepdims=True)
        for p in pu[1:]:
            l = l + p.sum(axis=-1, keepdims=True)
        probs.append([p.astype(BF16) for p in pu])
        dens.append(l)
        tops.append(m)
    for u, (pu, wu, l, m) in enumerate(zip(probs, wins, dens, tops)):
        rows = slice(u * tq, (u + 1) * tq)
        o = None
        for p, (start, width), v_ref in zip(pu, wu, v_refs):
            pv = jnp.dot(p, v_ref[pl.ds(start, width), :], preferred_element_type=F32)
            o = pv if o is None else o + pv
        lse2 = lse2_ref[rows, :1]
        top = jnp.maximum(m, lse2)
        w01 = jnp.exp2(m - top)
        w2 = jnp.exp2(lse2 - top)
        mixed = (w01 * o + w2 * o2_ref[rows, :]) / (w01 * l + w2)
        o_ref[rows, :] = (mixed * z_ref[rows, :].astype(F32)).astype(BF16)


def _dil_class_attn(hw, mask, tq=128):
    s = hw.shape[1]
    head = lambda cb: pl.BlockSpec((None, s, LANES), lambda hh: (cb + hh, 0, 0))
    return pl.pallas_call(
        functools.partial(_dil_class_kernel, tq, DIL_RATES[-1]),
        grid=(4,),
        in_specs=[head(CB_WQ), head(CB_WK), head(CB_WV), pl.BlockSpec(mask.shape, lambda hh: (0, 0, 0))],
        out_specs=[head(0), head(0)],
        out_shape=[jax.ShapeDtypeStruct((4, s, LANES), F32), jax.ShapeDtypeStruct((4, s, LANES), F32)],
        compiler_params=_cparams(("arbitrary",)),
        name="dil_class_attn",
    )(hw, hw, hw, mask)


def _dil_attn(hr, hp, hw, tq=128, nsub=16):
    s = hr.shape[1]
    n_dense = len(DIL_RATES) - 1
    masks = _dil_masks(tq, DIL_RATES[:n_dense])
    o2, lse2 = _dil_class_attn(hw, masks[0], tq)
    m_specs = [pl.BlockSpec(m.shape, lambda hh, i: (0, 0, 0)) for m in masks]
    blk = tq * nsub
    qs = [pl.BlockSpec((None, blk, LANES), functools.partial(lambda g, hh, i: (CB_DQ + 4 * g + hh, i, 0), g))
          for g in range(n_dense)]
    ks = [pl.BlockSpec((None, s, LANES), functools.partial(lambda g, hh, i: (CB_DK + 4 * g + hh, 0, 0), g))
          for g in range(n_dense)]
    vs = [pl.BlockSpec((None, s, LANES), functools.partial(lambda g, hh, i: (CB_DV + 4 * g + hh, 0, 0), g))
          for g in range(n_dense)]
    row_blk = pl.BlockSpec((None, blk, LANES), lambda hh, i: (hh, i, 0))
    return pl.pallas_call(
        functools.partial(_dil_kernel, tq, nsub),
        grid=(4, s // blk),
        in_specs=(qs + ks + vs + [pl.BlockSpec((None, blk, LANES), lambda hh, i: (CB_DZ + hh, i, 0))]
                  + m_specs + [row_blk, row_blk]),
        out_specs=pl.BlockSpec((blk, LANES), lambda hh, i: (i, hh)),
        out_shape=jax.ShapeDtypeStruct((s, BRANCH_W), BF16),
        compiler_params=_cparams(("arbitrary", "arbitrary")),
        name="dil_attn",
    )(*([hr] * (2 * n_dense)), *([hp] * (n_dense + 1)), *masks, o2, lse2)


def _merge_kernel(ya_ref, yb_ref, yc_ref, yd_ref, ga_ref, gb_ref, gc_ref, gd_ref, wb_ref, o_ref, wbf_ref):
    @pl.when(pl.program_id(1) == 0)
    def _():
        wbf_ref[...] = wb_ref[...].astype(BF16)

    acc = None
    for n, (y_ref, g_ref) in enumerate(zip((ya_ref, yb_ref, yc_ref, yd_ref),
                                           (ga_ref, gb_ref, gc_ref, gd_ref))):
        proj = jnp.dot(y_ref[...], wbf_ref[n], preferred_element_type=F32)
        term = proj * g_ref[...].astype(F32)
        acc = term if acc is None else acc + term
    o_ref[...] = acc.astype(BF16)


def _merge(ys, hg, w_branch, layer, tm=2048):
    s = hg.shape[0]
    tm = math.gcd(tm, s)
    n_col = D_MODEL // TN
    y_specs = [pl.BlockSpec((tm, BRANCH_W), lambda j, i: (i, 0)) for _ in range(N_BRANCH)]
    g_specs = [pl.BlockSpec((tm, TN), functools.partial(lambda n, j, i: (i, n_col * n + j), n))
               for n in range(N_BRANCH)]
    return pl.pallas_call(
        _merge_kernel,
        grid=(n_col, s // tm),
        in_specs=y_specs + g_specs + [pl.BlockSpec((None, N_BRANCH, BRANCH_W, TN),
                                                   lambda j, i: (layer, 0, 0, j))],
        out_specs=pl.BlockSpec((tm, TN), lambda j, i: (i, j)),
        out_shape=jax.ShapeDtypeStruct((s, D_MODEL), BF16),
        scratch_shapes=[pltpu.VMEM((N_BRANCH, BRANCH_W, TN), BF16)],
        compiler_params=_cparams(("arbitrary", "arbitrary")),
        name="branch_merge",
    )(*ys, hg, hg, hg, hg, w_branch)


def _out_ln_kernel(alpha, tsub, m_ref, w_ref, x_ref, g_ref, b_ref, y_ref, ybf_ref, acc_ref):
    n_sub = m_ref.shape[0] // tsub

    def matmul(u):
        acc_ref[u % 2] = jnp.dot(m_ref[u * tsub:(u + 1) * tsub, :], w_ref[...], preferred_element_type=F32)

    matmul(0)
    for u in range(n_sub):
        if u + 1 < n_sub:
            matmul(u + 1)
        rows = slice(u * tsub, (u + 1) * tsub)
        y = _ln_rows(alpha * x_ref[rows, :] + acc_ref[u % 2], g_ref[...], b_ref[...])
        y_ref[rows, :] = y
        ybf_ref[rows, :] = y.astype(BF16)


def _out_ln(merged, wo_bf, layer, x, g, b, alpha, tm=512, tsub=128):
    s, d = x.shape
    return pl.pallas_call(
        functools.partial(_out_ln_kernel, alpha, tsub),
        grid=(s // tm,),
        in_specs=[pl.BlockSpec((tm, d), lambda i: (i, 0)),
                  pl.BlockSpec((None, d, d), lambda i: (layer, 0, 0)),
                  pl.BlockSpec((tm, d), lambda i: (i, 0)),
                  pl.BlockSpec((None, 1, d), lambda i: (layer, 0, 0)),
                  pl.BlockSpec((None, 1, d), lambda i: (layer, 0, 0))],
        out_specs=[pl.BlockSpec((tm, d), lambda i: (i, 0)),
                   pl.BlockSpec((tm, d), lambda i: (i, 0))],
        out_shape=[jax.ShapeDtypeStruct((s, d), F32), jax.ShapeDtypeStruct((s, d), BF16)],
        scratch_shapes=[pltpu.VMEM((2, tsub, d), F32)],
        compiler_params=_cparams(("arbitrary",)),
        name="out_ln",
    )(merged, wo_bf, x, g, b)


def kernel(x, emb_ln_g, emb_ln_b, w_in, b_gate, diff_lambda, diff_subln_g, nat_rpb,
           gqa_q_norm_g, gqa_k_norm_g, w_branch, w_out, ln_g, ln_b):
    batch, seq, d = x.shape
    assert batch == 1 and d == D_MODEL and w_in.shape[-1] == D_IN
    depth = w_in.shape[0]
    alpha = (2 * depth) ** 0.25
    tabs = _rope_tables(seq)
    nat_bias = _nat_bias(nat_rpb)
    wo_bf = w_out.astype(BF16)
    gains = jnp.stack([gqa_q_norm_g, gqa_k_norm_g], axis=1).reshape(depth, 2, 1, LANES)
    b_gate3 = b_gate.reshape(depth, 1, -1)
    subln3 = diff_subln_g.reshape(depth, 1, LANES)
    ln_g3, ln_b3 = ln_g.reshape(depth, 1, d), ln_b.reshape(depth, 1, d)
    xf, xbf = _embed_ln(x[0], emb_ln_g, emb_ln_b)
    for l in range(depth):
        lam_init = 0.8 - 0.6 * math.exp(-0.3 * l)
        hr, hp, hg, hw = _inproj(xbf, w_in, b_gate3, l, tabs, gains)
        ya = _pair_attn(hr, hp, True, l, lam_init, diff_lambda, subln3)
        yb = _nat_attn(hp, nat_bias, l)
        yc = _pair_attn(hr, hp, False)
        yd = _dil_attn(hr, hp, hw)
        merged = _merge((ya, yb, yc, yd), hg, w_branch, l)
        xf, xbf = _out_ln(merged, wo_bf, l, xf, ln_g3, ln_b3, alpha)
    return xf[None]
```

```python
import functools
import math

import numpy as np
import jax
import jax.numpy as jnp
from jax import lax
from jax.experimental import pallas as pl
from jax.experimental.pallas import tpu as pltpu

F32 = jnp.float32
BF16 = jnp.bfloat16

D_MODEL = 2048
GRID_W = 64
HEAD_DIM = 128
N_BRANCH = 4
BRANCH_W = D_MODEL // 4
DA_QK = HEAD_DIM // 2
NB_ROWS = 8
NB_COLS = 16
AXIAL_THETA = 10000.0
DIL_RATES = (1, 4, 16)
DIL_SIDE = 64
ROPE_THETA = 500000.0
LN_EPS = 1e-5
RMS_EPS = 1e-6
NEG_INF = -1e30
LOG2E = math.log2(math.e)

LANES = 128
VT_PAD = 16
TN = 512
D_IN = 18944

ROT_TILES = (0, 1, 8, 9, 11, 12, 14, 15)
PLAIN_TILES = (2, 3, 4, 5, 6, 7, 10, 17, 18, 20)
WIDE_TILES = (13, 16, 19)
GATE_TILE0, N_GATE_TILES = 21, 16
ROT_VARIANT = (0, 1, 2, 3, 4, 4, 5, 5)
ROT_VARIANT_DQ, ROT_VARIANT_DK = 4, 5
CB_AQ, CB_AK, CB_CQ, CB_CK, CB_CV, CB_DQ, CB_DK = 0, 4, 8, 12, 14, 16, 24
CB_AV, CB_AZ, CB_BQ, CB_BK, CB_BV, CB_BZ, CB_CZ, CB_DV, CB_DZ = 0, 4, 8, 12, 16, 20, 24, 28, 36
CB_WQ, CB_WK, CB_WV = 0, 4, 8
PLAIN_SILU = (1, 5, 6, 9)

VMEM_LIMIT = 56 * 1024 * 1024


def _cparams(sem):
    return pltpu.CompilerParams(dimension_semantics=sem, vmem_limit_bytes=VMEM_LIMIT)


def _select(j, values):
    out = values[-1]
    for t in range(len(values) - 2, -1, -1):
        out = jnp.where(j == t, values[t], out)
    return out


def _any_of(j, tiles):
    c = j == tiles[0]
    for t in tiles[1:]:
        c = c | (j == t)
    return c


def _ln_rows(x, g, b):
    mu = jnp.mean(x, axis=-1, keepdims=True)
    xc = x - mu
    var = jnp.mean(xc * xc, axis=-1, keepdims=True)
    return xc * lax.rsqrt(var + LN_EPS) * g + b


def _embed_ln_kernel(x_ref, g_ref, b_ref, y_ref, ybf_ref):
    y = _ln_rows(x_ref[...], g_ref[...], b_ref[...])
    y_ref[...] = y
    ybf_ref[...] = y.astype(BF16)


def _embed_ln(x, g, b, tm=512):
    s, d = x.shape
    return pl.pallas_call(
        _embed_ln_kernel,
        grid=(s // tm,),
        in_specs=[pl.BlockSpec((tm, d), lambda i: (i, 0)),
                  pl.BlockSpec((1, d), lambda i: (0, 0)),
                  pl.BlockSpec((1, d), lambda i: (0, 0))],
        out_specs=[pl.BlockSpec((tm, d), lambda i: (i, 0)),
                   pl.BlockSpec((tm, d), lambda i: (i, 0))],
        out_shape=[jax.ShapeDtypeStruct((s, d), F32), jax.ShapeDtypeStruct((s, d), BF16)],
        compiler_params=_cparams(("arbitrary",)),
        name="embed_ln",
    )(x, g.reshape(1, d), b.reshape(1, d))


def _rope_tables(seq):
    t = np.arange(seq)

    def cs(pos, dim, theta):
        half = dim // 2
        inv = np.power(float(theta), -np.arange(half, dtype=np.float64) * 2.0 / dim)
        ang = pos.astype(np.float64)[:, None] * inv[None, :]
        return jnp.asarray(np.cos(ang), F32), jnp.asarray(np.sin(ang), F32)

    def group(cos, sin, width):
        half = cos.shape[1]
        pad = width - 2 * half
        one = jnp.ones((seq, pad), F32)
        zero = jnp.zeros((seq, pad), F32)
        zh = jnp.zeros((seq, half), F32)
        return (jnp.concatenate([cos, cos, one], 1),
                jnp.concatenate([zh, sin, zero], 1),
                jnp.concatenate([-sin, zh, zero], 1))

    ca, sa = cs(t, DA_QK // 4, ROPE_THETA)
    va = jnp.stack([jnp.concatenate([p, p], 1) for p in group(ca, sa, DA_QK)])
    cr, sr = cs(t // GRID_W, HEAD_DIM // 2, AXIAL_THETA)
    cc, sc = cs(t % GRID_W, HEAD_DIM // 2, AXIAL_THETA)
    vc = jnp.stack([jnp.concatenate([p, q], 1)
                    for p, q in zip(group(cr, sr, HEAD_DIM // 2), group(cc, sc, HEAD_DIM // 2))])
    cd, sd = cs(t, HEAD_DIM // 4, ROPE_THETA)
    vd = jnp.stack(group(cd, sd, HEAD_DIM))
    qa = DA_QK ** -0.5 * LOG2E
    qh = HEAD_DIM ** -0.5 * LOG2E
    return jnp.stack([va * qa, va, vc * qh, vc, vd * qh, vd])


ROT_HALF_A, ROT_HALF_C, ROT_HALF_D = DA_QK // 8, HEAD_DIM // 4, HEAD_DIM // 8


def _proj_kernel(tsub, n_sub, make_branches, n_extra, x_ref, w_ref, *rest):
    extra = rest[:n_extra]
    o_ref, wbf_ref, acc_ref = rest[n_extra:]
    j = pl.program_id(0)
    n_pass = x_ref.shape[0] // (n_sub * tsub)

    @pl.when(pl.program_id(1) == 0)
    def _():
        wbf_ref[...] = w_ref[...].astype(BF16)

    def pipeline(epilogue):
        def one_pass(h, carry):
            def rows_of(u):
                return pl.ds(pl.multiple_of((h * n_sub + u) * tsub, tsub), tsub)

            def matmul(u):
                acc_ref[u % 2] = jnp.dot(x_ref[rows_of(u), :], wbf_ref[...],
                                         preferred_element_type=F32)

            matmul(0)
            for u in range(n_sub):
                if u + 1 < n_sub:
                    matmul(u + 1)
                for g in range(TN // LANES):
                    lanes = slice(g * LANES, (g + 1) * LANES)
                    y = epilogue(acc_ref[u % 2, :, lanes], rows_of(u), g).astype(o_ref.dtype)
                    if len(o_ref.shape) == 3:
                        o_ref[g, rows_of(u), :] = y
                    else:
                        o_ref[rows_of(u), lanes] = y
            return carry

        lax.fori_loop(0, n_pass, one_pass, 0)

    for cond, epilogue in make_branches(*extra):
        if cond is None:
            pipeline(epilogue)
        else:
            pl.when(cond(j))(functools.partial(pipeline, epilogue))


def _proj(name, xbf, w_in, layer, src_tiles, make_branches, extra, extra_specs, head_major,
          tm=2048, tsub=256, n_sub=4, dtype=BF16):
    s, d = xbf.shape
    tm = math.gcd(tm, s)
    n_sub = min(n_sub, tm // tsub)
    assert tm % (tsub * n_sub) == 0
    n_tiles = len(src_tiles)
    contiguous = src_tiles == tuple(range(src_tiles[0], src_tiles[0] + n_tiles))
    src = (lambda j: src_tiles[0] + j) if contiguous else (lambda j: _select(j, src_tiles))
    if head_major:
        out_spec = pl.BlockSpec((TN // LANES, tm, LANES), lambda j, i: (j, i, 0))
        out_shape = jax.ShapeDtypeStruct((n_tiles * (TN // LANES), s, LANES), dtype)
    else:
        out_spec = pl.BlockSpec((tm, TN), lambda j, i: (i, j))
        out_shape = jax.ShapeDtypeStruct((s, n_tiles * TN), dtype)
    return pl.pallas_call(
        functools.partial(_proj_kernel, tsub, n_sub, make_branches, len(extra)),
        grid=(n_tiles, s // tm),
        in_specs=[pl.BlockSpec((tm, d), lambda j, i: (i, 0)),
                  pl.BlockSpec((None, d, TN), lambda j, i: (layer, 0, src(j)))] + extra_specs(tm),
        out_specs=out_spec,
        out_shape=out_shape,
        scratch_shapes=[pltpu.VMEM((d, TN), BF16), pltpu.VMEM((2, tsub, TN), F32)],
        compiler_params=_cparams(("arbitrary", "arbitrary")),
        name=name,
    )(xbf, w_in, *extra)


def _rot_branches(tab_ref, gn_ref):
    def rot(a, rows, half):
        return (a * tab_ref[0, rows, :] + pltpu.roll(a, half, 1) * tab_ref[1, rows, :]
                + pltpu.roll(a, LANES - half, 1) * tab_ref[2, rows, :])

    def rms(a, gain):
        ms = jnp.mean(a * a, axis=-1, keepdims=True)
        return a * lax.rsqrt(ms + RMS_EPS) * gain

    return [(lambda j: j <= 1, lambda a, rows, g: rot(a, rows, ROT_HALF_A)),
            (lambda j: j == 2, lambda a, rows, g: rot(rms(a, gn_ref[0]), rows, ROT_HALF_C)),
            (lambda j: j == 3, lambda a, rows, g: rot(rms(a, gn_ref[1]), rows, ROT_HALF_C) if g < 2 else a),
            (lambda j: j >= 4, lambda a, rows, g: rot(a, rows, ROT_HALF_D))]


def _plain_branches():
    return [(lambda j: _any_of(j, PLAIN_SILU), lambda a, rows, g: a * jax.nn.sigmoid(a)),
            (lambda j: jnp.logical_not(_any_of(j, PLAIN_SILU)), lambda a, rows, g: a)]


def _gate_branches(bg_ref):
    return [(None, lambda a, rows, g: jax.nn.sigmoid(a + bg_ref[:, g * LANES:(g + 1) * LANES]))]


def _wide_branches(tab_ref):
    def rot(a, rows):
        return (a * tab_ref[0, rows, :] + pltpu.roll(a, ROT_HALF_D, 1) * tab_ref[1, rows, :]
                + pltpu.roll(a, LANES - ROT_HALF_D, 1) * tab_ref[2, rows, :])

    return [(lambda j: j <= 1, lambda a, rows, g: rot(a, rows)),
            (lambda j: j == 2, lambda a, rows, g: a)]


def _inproj(xbf, w_in, b_gate, layer, tabs, gains):
    hr = _proj("inproj_rot", xbf, w_in, layer, ROT_TILES, _rot_branches, (tabs, gains),
               lambda tm: [pl.BlockSpec((None, 3, tm, LANES), lambda j, i: (_select(j, ROT_VARIANT), 0, i, 0)),
                           pl.BlockSpec((None, 2, 1, LANES), lambda j, i: (layer, 0, 0, 0))], True,
               n_sub=8, tsub=128)
    hp = _proj("inproj_plain", xbf, w_in, layer, PLAIN_TILES, _plain_branches, (), lambda tm: [], True,
               n_sub=16, tsub=128)
    hg = _proj("inproj_gate", xbf, w_in, layer, tuple(range(GATE_TILE0, GATE_TILE0 + N_GATE_TILES)),
               _gate_branches, (b_gate,),
               lambda tm: [pl.BlockSpec((None, 1, TN), lambda j, i: (layer, 0, j))], False, n_sub=16, tsub=128)
    hw = _proj("inproj_wide", xbf, w_in, layer, WIDE_TILES, _wide_branches, (tabs,),
               lambda tm: [pl.BlockSpec((None, 3, tm, LANES),
                                        lambda j, i: (jnp.where(j == 0, ROT_VARIANT_DQ, ROT_VARIANT_DK), 0, i, 0))],
               True, n_sub=16, tsub=128, dtype=F32)
    return hr, hp, hg, hw


def _pair_attn_kernel(diff, tq, tk, lam_init, *refs):
    if diff:
        q_ref, k_ref, v_ref, z_ref, lp_ref, sg_ref, o_ref, q2_ref, vt_ref, acc_ref, s_ref = refs
    else:
        q_ref, k_ref, v_ref, z_ref, o_ref, q2_ref, vt_ref, acc_ref, s_ref = refs
    seq = k_ref.shape[0]
    nq = 2 * tq

    @pl.when(pl.program_id(1) == 0)
    def _():
        for c in range(seq // tk):
            vt_ref[c, :LANES, :] = v_ref[c * tk:(c + 1) * tk, :].astype(F32).T.astype(BF16)
            vt_ref[c, LANES:, :] = jnp.ones((VT_PAD, tk), BF16)

    if diff:
        q = q_ref[...]
        lane = lax.broadcasted_iota(jnp.int32, q.shape, 1)
        zero = jnp.zeros_like(q)
        q2_ref[:tq, :] = jnp.where(lane < DA_QK, q, zero)
        q2_ref[tq:, :] = jnp.where(lane >= DA_QK, q, zero)
    else:
        q2_ref[:tq, :] = q_ref[0]
        q2_ref[tq:, :] = q_ref[1]
    acc_ref[...] = jnp.zeros(acc_ref.shape, F32)

    n_chunks = seq // tk

    def scores(c, slot):
        start = pl.multiple_of(c * tk, tk)
        s = lax.dot_general(k_ref[pl.ds(start, tk), :], q2_ref[...], (((1,), (1,)), ((), ())),
                            preferred_element_type=F32)
        s_ref[slot] = s
        return jnp.max(s, axis=0, keepdims=True)

    def fold(c, slot, m_prev, m_chunk):
        m_new = jnp.maximum(m_prev, m_chunk)
        alpha = jnp.exp2(m_prev - m_new)
        p = jnp.exp2(s_ref[slot] - m_new)
        acc_ref[...] = alpha * acc_ref[...] + jnp.dot(vt_ref[c], p.astype(BF16),
                                                      preferred_element_type=F32)
        return m_new

    unroll = 4 if n_chunks % 4 == 0 else 2

    def group(base, carry, last):
        m, m_chunk = carry
        for u in range(unroll):
            m_next = m_chunk if (last and u == unroll - 1) else scores(base + u + 1, (u + 1) % 2)
            m = fold(base + u, u % 2, m, m_chunk)
            m_chunk = m_next
        return m, m_chunk

    carry = (jnp.full((1, nq), NEG_INF, F32), scores(0, 0))
    n_groups = n_chunks // unroll
    if n_groups > 1:
        carry = lax.fori_loop(0, n_groups - 1, lambda j, cc: group(j * unroll, cc, False), carry)
    group((n_groups - 1) * unroll, carry, True)

    acc = acc_ref[...]
    o = (acc[:LANES] / acc[LANES:LANES + 1]).T
    if diff:
        lp = lp_ref[...]
        lam = (jnp.exp(jnp.sum(lp[0:1] * lp[1:2], keepdims=True))
               - jnp.exp(jnp.sum(lp[2:3] * lp[3:4], keepdims=True)) + lam_init)
        dlt = o[:tq] - lam * o[tq:]
        ms = jnp.mean(dlt * dlt, axis=-1, keepdims=True)
        y = dlt * lax.rsqrt(ms + RMS_EPS) * sg_ref[...] * (1.0 - lam_init)
        o_ref[...] = (y * z_ref[...].astype(F32)).astype(BF16)
    else:
        o_ref[:, :LANES] = (o[:tq] * z_ref[0].astype(F32)).astype(BF16)
        o_ref[:, LANES:] = (o[tq:] * z_ref[1].astype(F32)).astype(BF16)


def _pair_attn(hr, hp, diff, layer=0, lam_init=0.0, lam_params=None, subln_g=None, tq=512, tk=512):
    s = hr.shape[1]
    if diff:
        n_kv, qw, qblk = 4, LANES, (None, tq, LANES)
        q_map = lambda hh, i: (CB_AQ + hh, i, 0)
        k_map = lambda hh, i: (CB_AK + hh, 0, 0)
        v_map = lambda hh, i: (CB_AV + hh, 0, 0)
        z_map = lambda hh, i: (CB_AZ + hh, i, 0)
    else:
        n_kv, qw, qblk = 2, 2 * LANES, (2, tq, LANES)
        q_map = lambda hh, i: (CB_CQ // 2 + hh, i, 0)
        k_map = lambda hh, i: (CB_CK + hh, 0, 0)
        v_map = lambda hh, i: (CB_CV + hh, 0, 0)
        z_map = lambda hh, i: (CB_CZ // 2 + hh, i, 0)
    in_specs = [pl.BlockSpec(qblk, q_map),
                pl.BlockSpec((None, s, LANES), k_map),
                pl.BlockSpec((None, s, LANES), v_map),
                pl.BlockSpec(qblk, z_map)]
    args = [hr, hr, hp if diff else hr, hp]
    if diff:
        in_specs += [pl.BlockSpec((None, 4, DA_QK), lambda hh, i: (layer, 0, 0)),
                     pl.BlockSpec((None, 1, LANES), lambda hh, i: (layer, 0, 0))]
        args += [lam_params, subln_g]
    return pl.pallas_call(
        functools.partial(_pair_attn_kernel, diff, tq, tk, lam_init),
        grid=(n_kv, s // tq),
        in_specs=in_specs,
        out_specs=pl.BlockSpec((tq, qw), lambda hh, i: (i, hh)),
        out_shape=jax.ShapeDtypeStruct((s, BRANCH_W), BF16),
        scratch_shapes=[pltpu.VMEM((2 * tq, LANES), BF16),
                        pltpu.VMEM((s // tk, LANES + VT_PAD, tk), BF16),
                        pltpu.VMEM((LANES + VT_PAD, 2 * tq), F32),
                        pltpu.VMEM((2, tk, 2 * tq), F32)],
        compiler_params=_cparams(("arbitrary", "arbitrary")),
        name="diff_attn" if diff else "axial_gqa",
    )(*args)


def _nat_bias(rpb):
    n_layers, n_heads = rpb.shape[:2]
    cols = np.arange(GRID_W)
    c0 = np.clip(cols - NB_COLS // 2, 0, GRID_W - NB_COLS)
    valid = (cols[None, :] >= c0[:, None]) & (cols[None, :] < c0[:, None] + NB_COLS)
    col_sel = (np.arange(2 * NB_COLS - 1)[:, None, None]
               == cols[None, None, :] - cols[None, :, None] + (NB_COLS - 1)).astype(np.float32)
    row_sel = (np.arange(2 * NB_ROWS - 1)[None, None, :]
               == np.arange(NB_ROWS)[:, None, None] + np.arange(NB_ROWS)[None, :, None]).astype(np.float32)
    exact = lax.Precision.HIGHEST
    toep = jnp.einsum('lhdr,rck->lhdck', rpb.astype(F32), col_sel, precision=exact)
    t = jnp.where(valid, toep, NEG_INF)
    bias = jnp.einsum('lhdck,sid->lshcik', t, row_sel, precision=exact)
    return bias.reshape(n_layers, NB_ROWS, n_heads, GRID_W, NB_ROWS * GRID_W)


def _nat_kernel(rows, rps, q_ref, k_ref, v_ref, z_ref, bias_ref, o_ref):
    win = NB_ROWS * GRID_W
    starts, scores = [], []
    for u in range(rps):
        r = pl.program_id(1) * rps + u
        r0 = jnp.clip(r - NB_ROWS // 2, 0, rows - NB_ROWS)
        start = pl.multiple_of(r0 * GRID_W, GRID_W)
        kw = k_ref[pl.ds(start, win), :]
        s = lax.dot_general(q_ref[u * GRID_W:(u + 1) * GRID_W, :], kw, (((1,), (1,)), ((), ())),
                            preferred_element_type=F32)
        scores.append(s * (HEAD_DIM ** -0.5) + bias_ref[r0 - r + (NB_ROWS - 1)])
        starts.append(start)
    probs = []
    for s in scores:
        p = jnp.exp(s - jnp.max(s, axis=-1, keepdims=True))
        probs.append((p.astype(BF16), jnp.sum(p, axis=-1, keepdims=True)))
    for u, (p, l) in enumerate(probs):
        qrows = slice(u * GRID_W, (u + 1) * GRID_W)
        o = jnp.dot(p, v_ref[pl.ds(starts[u], win), :], preferred_element_type=F32) / l
        o_ref[qrows, :] = (o * z_ref[qrows, :].astype(F32)).astype(BF16)


def _nat_attn(hp, bias, layer, rps=32):
    s = hp.shape[1]
    rows = s // GRID_W
    blk = rps * GRID_W
    return pl.pallas_call(
        functools.partial(_nat_kernel, rows, rps),
        grid=(4, rows // rps),
        in_specs=[pl.BlockSpec((None, blk, LANES), lambda hh, r: (CB_BQ + hh, r, 0)),
                  pl.BlockSpec((None, s, LANES), lambda hh, r: (CB_BK + hh, 0, 0)),
                  pl.BlockSpec((None, s, LANES), lambda hh, r: (CB_BV + hh, 0, 0)),
                  pl.BlockSpec((None, blk, LANES), lambda hh, r: (CB_BZ + hh, r, 0)),
                  pl.BlockSpec((None, NB_ROWS, None, GRID_W, NB_ROWS * GRID_W),
                               lambda hh, r: (layer, 0, hh, 0, 0))],
        out_specs=pl.BlockSpec((blk, LANES), lambda hh, r: (r, hh)),
        out_shape=jax.ShapeDtypeStruct((s, BRANCH_W), BF16),
        compiler_params=_cparams(("arbitrary", "arbitrary")),
        name="nat_attn",
    )(hp, hp, hp, hp, bias)


def _dil_masks(tq, rates):
    row = np.arange(tq)[:, None]

    def band(col, rate):
        d = col - row
        return np.where((d >= 0) & (d <= 2 * DIL_SIDE * rate) & (d % rate == 0), 0.0, NEG_INF).astype(np.float32)

    width0 = tq + 2 * DIL_SIDE
    tables = [np.stack([band(np.arange(width0)[None, :] + dlt, 1) for dlt in (DIL_SIDE, 0, -DIL_SIDE)])]
    assert rates[0] == 1
    for rate in rates[1:]:
        reach = DIL_SIDE * rate
        cols = np.arange(-reach, tq + 3 * reach)[None, :]
        full = band(cols, rate)
        tables.append(full.reshape(tq, -1, LANES).transpose(1, 0, 2))
    return [jnp.asarray(t) for t in tables]


def _dil_class_kernel(tq, rate, q_ref, k_ref, v_ref, b_ref, o_ref, lse_ref):
    length = q_ref.shape[0] // rate
    width = tq + 2 * DIL_SIDE
    for a in range(rate):
        cls = pl.ds(a, length, stride=rate)
        qa = q_ref[cls, :].astype(BF16)
        ka = k_ref[cls, :].astype(BF16)
        va = v_ref[cls, :].astype(BF16)
        scored = []
        for i0 in range(0, length, tq):
            ws = min(max(i0 - DIL_SIDE, 0), length - width)
            delta = ws - i0 + DIL_SIDE
            s = lax.dot_general(qa[i0:i0 + tq], ka[ws:ws + width], (((1,), (1,)), ((), ())),
                                preferred_element_type=F32)
            scored.append((i0, ws, s + b_ref[0 if delta > 0 else (2 if delta < 0 else 1)]))
        stats = []
        for i0, ws, s in scored:
            m = jnp.max(s, axis=-1, keepdims=True)
            p = jnp.exp2(s - m)
            stats.append((i0, ws, p.astype(BF16), m, jnp.sum(p, axis=-1, keepdims=True)))
        for i0, ws, p, m, l in stats:
            out_rows = pl.ds(a + rate * i0, tq, stride=rate)
            o_ref[out_rows, :] = jnp.dot(p, va[ws:ws + width], preferred_element_type=F32) / l
            lse_ref[out_rows, :] = jnp.broadcast_to(m + jnp.log2(l), (tq, LANES))


def _dil_kernel(tq, nsub, q0_ref, q1_ref, k0_ref, k1_ref, v0_ref, v1_ref, z_ref, b0_ref, b1_ref,
                o2_ref, lse2_ref, o_ref):
    seq = k0_ref.shape[0]
    q_refs, k_refs, v_refs, b_refs = (q0_ref, q1_ref), (k0_ref, k1_ref), (v0_ref, v1_ref), (b0_ref, b1_ref)
    scores, wins = [], []
    for u in range(nsub):
        t0 = (pl.program_id(1) * nsub + u) * tq
        rows = slice(u * tq, (u + 1) * tq)
        su, wu = [], []
        for rate, q_ref, k_ref, b_ref in zip(DIL_RATES, q_refs, k_refs, b_refs):
            reach = DIL_SIDE * rate
            width = tq + 2 * reach
            start = pl.multiple_of(jnp.clip(t0 - reach, 0, seq - width), DIL_SIDE)
            delta = start - t0 + reach
            s = lax.dot_general(q_ref[rows, :], k_ref[pl.ds(start, width), :], (((1,), (1,)), ((), ())),
                                preferred_element_type=F32)
            if rate == 1:
                s = s + b_ref[jnp.where(delta > 0, 0, jnp.where(delta < 0, 2, 1))]
            else:
                tile0 = (delta + reach) // LANES
                s = jnp.concatenate([s[:, j * LANES:(j + 1) * LANES] + b_ref[tile0 + j]
                                     for j in range(width // LANES)], axis=1)
            su.append(s)
            wu.append((start, width))
        scores.append(su)
        wins.append(wu)
    probs, dens, tops = [], [], []
    for su in scores:
        m = su[0].max(axis=-1, keepdims=True)
        for s in su[1:]:
            m = jnp.maximum(m, s.max(axis=-1, keepdims=True))
        pu = [jnp.exp2(s - m) for s in su]
        l = pu[0].sum(axis=-1, keepdims=True)
        for p in pu[1:]:
            l = l + p.sum(axis=-1, keepdims=True)
        probs.append([p.astype(BF16) for p in pu])
        dens.append(l)
        tops.append(m)
    for u, (pu, wu, l, m) in enumerate(zip(probs, wins, dens, tops)):
        rows = slice(u * tq, (u + 1) * tq)
        o = None
        for p, (start, width), v_ref in zip(pu, wu, v_refs):
            pv = jnp.dot(p, v_ref[pl.ds(start, width), :], preferred_element_type=F32)
            o = pv if o is None else o + pv
        lse2 = lse2_ref[rows, :1]
        top = jnp.maximum(m, lse2)
        w01 = jnp.exp2(m - top)
        w2 = jnp.exp2(lse2 - top)
        mixed = (w01 * o + w2 * o2_ref[rows, :]) / (w01 * l + w2)
        o_ref[rows, :] = (mixed * z_ref[rows, :].astype(F32)).astype(BF16)


def _dil_class_attn(hw, mask, tq=128):
    s = hw.shape[1]
    head = lambda cb: pl.BlockSpec((None, s, LANES), lambda hh: (cb + hh, 0, 0))
    return pl.pallas_call(
        functools.partial(_dil_class_kernel, tq, DIL_RATES[-1]),
        grid=(4,),
        in_specs=[head(CB_WQ), head(CB_WK), head(CB_WV), pl.BlockSpec(mask.shape, lambda hh: (0, 0, 0))],
        out_specs=[head(0), head(0)],
        out_shape=[jax.ShapeDtypeStruct((4, s, LANES), F32), jax.ShapeDtypeStruct((4, s, LANES), F32)],
        compiler_params=_cparams(("arbitrary",)),
        name="dil_class_attn",
    )(hw, hw, hw, mask)


def _dil_attn(hr, hp, hw, tq=128, nsub=16):
    s = hr.shape[1]
    n_dense = len(DIL_RATES) - 1
    masks = _dil_masks(tq, DIL_RATES[:n_dense])
    o2, lse2 = _dil_class_attn(hw, masks[0], tq)
    m_specs = [pl.BlockSpec(m.shape, lambda hh, i: (0, 0, 0)) for m in masks]
    blk = tq * nsub
    qs = [pl.BlockSpec((None, blk, LANES), functools.partial(lambda g, hh, i: (CB_DQ + 4 * g + hh, i, 0), g))
          for g in range(n_dense)]
    ks = [pl.BlockSpec((None, s, LANES), functools.partial(lambda g, hh, i: (CB_DK + 4 * g + hh, 0, 0), g))
          for g in range(n_dense)]
    vs = [pl.BlockSpec((None, s, LANES), functools.partial(lambda g, hh, i: (CB_DV + 4 * g + hh, 0, 0), g))
          for g in range(n_dense)]
    row_blk = pl.BlockSpec((None, blk, LANES), lambda hh, i: (hh, i, 0))
    return pl.pallas_call(
        functools.partial(_dil_kernel, tq, nsub),
        grid=(4, s // blk),
        in_specs=(qs + ks + vs + [pl.BlockSpec((None, blk, LANES), lambda hh, i: (CB_DZ + hh, i, 0))]
                  + m_specs + [row_blk, row_blk]),
        out_specs=pl.BlockSpec((blk, LANES), lambda hh, i: (i, hh)),
        out_shape=jax.ShapeDtypeStruct((s, BRANCH_W), BF16),
        compiler_params=_cparams(("arbitrary", "arbitrary")),
        name="dil_attn",
    )(*([hr] * (2 * n_dense)), *([hp] * (n_dense + 1)), *masks, o2, lse2)


def _merge_kernel(ya_ref, yb_ref, yc_ref, yd_ref, ga_ref, gb_ref, gc_ref, gd_ref, wb_ref, o_ref, wbf_ref):
    @pl.when(pl.program_id(1) == 0)
    def _():
        wbf_ref[...] = wb_ref[...].astype(BF16)

    acc = None
    for n, (y_ref, g_ref) in enumerate(zip((ya_ref, yb_ref, yc_ref, yd_ref),
                                           (ga_ref, gb_ref, gc_ref, gd_ref))):
        proj = jnp.dot(y_ref[...], wbf_ref[n], preferred_element_type=F32)
        term = proj * g_ref[...].astype(F32)
        acc = term if acc is None else acc + term
    o_ref[...] = acc.astype(BF16)


def _merge(ys, hg, w_branch, layer, tm=2048):
    s = hg.shape[0]
    tm = math.gcd(tm, s)
    n_col = D_MODEL // TN
    y_specs = [pl.BlockSpec((tm, BRANCH_W), lambda j, i: (i, 0)) for _ in range(N_BRANCH)]
    g_specs = [pl.BlockSpec((tm, TN), functools.partial(lambda n, j, i: (i, n_col * n + j), n))
               for n in range(N_BRANCH)]
    return pl.pallas_call(
        _merge_kernel,
        grid=(n_col, s // tm),
        in_specs=y_specs + g_specs + [pl.BlockSpec((None, N_BRANCH, BRANCH_W, TN),
                                                   lambda j, i: (layer, 0, 0, j))],
        out_specs=pl.BlockSpec((tm, TN), lambda j, i: (i, j)),
        out_shape=jax.ShapeDtypeStruct((s, D_MODEL), BF16),
        scratch_shapes=[pltpu.VMEM((N_BRANCH, BRANCH_W, TN), BF16)],
        compiler_params=_cparams(("arbitrary", "arbitrary")),
        name="branch_merge",
    )(*ys, hg, hg, hg, hg, w_branch)


def _out_ln_kernel(alpha, tsub, m_ref, w_ref, x_ref, g_ref, b_ref, y_ref, ybf_ref, acc_ref):
    n_sub = m_ref.shape[0] // tsub

    def matmul(u):
        acc_ref[u % 2] = jnp.dot(m_ref[u * tsub:(u + 1) * tsub, :], w_ref[...], preferred_element_type=F32)

    matmul(0)
    for u in range(n_sub):
        if u + 1 < n_sub:
            matmul(u + 1)
        rows = slice(u * tsub, (u + 1) * tsub)
        y = _ln_rows(alpha * x_ref[rows, :] + acc_ref[u % 2], g_ref[...], b_ref[...])
        y_ref[rows, :] = y
        ybf_ref[rows, :] = y.astype(BF16)


def _out_ln(merged, wo_bf, layer, x, g, b, alpha, tm=512, tsub=128):
    s, d = x.shape
    return pl.pallas_call(
        functools.partial(_out_ln_kernel, alpha, tsub),
        grid=(s // tm,),
        in_specs=[pl.BlockSpec((tm, d), lambda i: (i, 0)),
                  pl.BlockSpec((None, d, d), lambda i: (layer, 0, 0)),
                  pl.BlockSpec((tm, d), lambda i: (i, 0)),
                  pl.BlockSpec((None, 1, d), lambda i: (layer, 0, 0)),
                  pl.BlockSpec((None, 1, d), lambda i: (layer, 0, 0))],
        out_specs=[pl.BlockSpec((tm, d), lambda i: (i, 0)),
                   pl.BlockSpec((tm, d), lambda i: (i, 0))],
        out_shape=[jax.ShapeDtypeStruct((s, d), F32), jax.ShapeDtypeStruct((s, d), BF16)],
        scratch_shapes=[pltpu.VMEM((2, tsub, d), F32)],
        compiler_params=_cparams(("arbitrary",)),
        name="out_ln",
    )(merged, wo_bf, x, g, b)


def kernel(x, emb_ln_g, emb_ln_b, w_in, b_gate, diff_lambda, diff_subln_g, nat_rpb,
           gqa_q_norm_g, gqa_k_norm_g, w_branch, w_out, ln_g, ln_b):
    batch, seq, d = x.shape
    assert batch == 1 and d == D_MODEL and w_in.shape[-1] == D_IN
    depth = w_in.shape[0]
    alpha = (2 * depth) ** 0.25
    tabs = _rope_tables(seq)
    nat_bias = _nat_bias(nat_rpb)
    wo_bf = w_out.astype(BF16)
    gains = jnp.stack([gqa_q_norm_g, gqa_k_norm_g], axis=1).reshape(depth, 2, 1, LANES)
    b_gate3 = b_gate.reshape(depth, 1, -1)
    subln3 = diff_subln_g.reshape(depth, 1, LANES)
    ln_g3, ln_b3 = ln_g.reshape(depth, 1, d), ln_b.reshape(depth, 1, d)
    xf, xbf = _embed_ln(x[0], emb_ln_g, emb_ln_b)
    for l in range(depth):
        lam_init = 0.8 - 0.6 * math.exp(-0.3 * l)
        hr, hp, hg, hw = _inproj(xbf, w_in, b_gate3, l, tabs, gains)
        ya = _pair_attn(hr, hp, True, l, lam_init, diff_lambda, subln3)
        yb = _nat_attn(hp, nat_bias, l)
        yc = _pair_attn(hr, hp, False)
        yd = _dil_attn(hr, hp, hw)
        merged = _merge((ya, yb, yc, yd), hg, w_branch, l)
        xf, xbf = _out_ln(merged, wo_bf, l, xf, ln_g3, ln_b3, alpha)
    return xf[None]
```

```python
import functools
import math

import numpy as np
import jax
import jax.numpy as jnp
from jax import lax
from jax.experimental import pallas as pl
from jax.experimental.pallas import tpu as pltpu

F32 = jnp.float32
BF16 = jnp.bfloat16

D_MODEL = 2048
GRID_W = 64
HEAD_DIM = 128
N_BRANCH = 4
BRANCH_W = D_MODEL // 4
DA_QK = HEAD_DIM // 2
NB_ROWS = 8
NB_COLS = 16
AXIAL_THETA = 10000.0
DIL_RATES = (1, 4, 16)
DIL_SIDE = 64
ROPE_THETA = 500000.0
LN_EPS = 1e-5
RMS_EPS = 1e-6
NEG_INF = -1e30
LOG2E = math.log2(math.e)

LANES = 128
VT_PAD = 16
TN = 512
D_IN = 18944

ROT_TILES = (0, 1, 8, 9, 11, 12, 14, 15)
PLAIN_TILES = (2, 3, 4, 5, 6, 7, 10, 17, 18, 20)
WIDE_TILES = (13, 16, 19)
GATE_TILE0, N_GATE_TILES = 21, 16
ROT_VARIANT = (0, 1, 2, 3, 4, 4, 5, 5)
ROT_VARIANT_DQ, ROT_VARIANT_DK = 4, 5
CB_AQ, CB_AK, CB_CQ, CB_CK, CB_CV, CB_DQ, CB_DK = 0, 4, 8, 12, 14, 16, 24
CB_AV, CB_AZ, CB_BQ, CB_BK, CB_BV, CB_BZ, CB_CZ, CB_DV, CB_DZ = 0, 4, 8, 12, 16, 20, 24, 28, 36
CB_WQ, CB_WK, CB_WV = 0, 4, 8
PLAIN_SILU = (1, 5, 6, 9)

VMEM_LIMIT = 56 * 1024 * 1024


def _cparams(sem):
    return pltpu.CompilerParams(dimension_semantics=sem, vmem_limit_bytes=VMEM_LIMIT)


def _select(j, values):
    out = values[-1]
    for t in range(len(values) - 2, -1, -1):
        out = jnp.where(j == t, values[t], out)
    return out


def _any_of(j, tiles):
    c = j == tiles[0]
    for t in tiles[1:]:
        c = c | (j == t)
    return c


def _ln_rows(x, g, b):
    mu = jnp.mean(x, axis=-1, keepdims=True)
    xc = x - mu
    var = jnp.mean(xc * xc, axis=-1, keepdims=True)
    return xc * lax.rsqrt(var + LN_EPS) * g + b


def _embed_ln_kernel(x_ref, g_ref, b_ref, y_ref, ybf_ref):
    y = _ln_rows(x_ref[...], g_ref[...], b_ref[...])
    y_ref[...] = y
    ybf_ref[...] = y.astype(BF16)


def _embed_ln(x, g, b, tm=512):
    s, d = x.shape
    return pl.pallas_call(
        _embed_ln_kernel,
        grid=(s // tm,),
        in_specs=[pl.BlockSpec((tm, d), lambda i: (i, 0)),
                  pl.BlockSpec((1, d), lambda i: (0, 0)),
                  pl.BlockSpec((1, d), lambda i: (0, 0))],
        out_specs=[pl.BlockSpec((tm, d), lambda i: (i, 0)),
                   pl.BlockSpec((tm, d), lambda i: (i, 0))],
        out_shape=[jax.ShapeDtypeStruct((s, d), F32), jax.ShapeDtypeStruct((s, d), BF16)],
        compiler_params=_cparams(("arbitrary",)),
        name="embed_ln",
    )(x, g.reshape(1, d), b.reshape(1, d))


def _rope_tables(seq):
    t = np.arange(seq)

    def cs(pos, dim, theta):
        half = dim // 2
        inv = np.power(float(theta), -np.arange(half, dtype=np.float64) * 2.0 / dim)
        ang = pos.astype(np.float64)[:, None] * inv[None, :]
        return jnp.asarray(np.cos(ang), F32), jnp.asarray(np.sin(ang), F32)

    def group(cos, sin, width):
        half = cos.shape[1]
        pad = width - 2 * half
        one = jnp.ones((seq, pad), F32)
        zero = jnp.zeros((seq, pad), F32)
        zh = jnp.zeros((seq, half), F32)
        return (jnp.concatenate([cos, cos, one], 1),
                jnp.concatenate([zh, sin, zero], 1),
                jnp.concatenate([-sin, zh, zero], 1))

    ca, sa = cs(t, DA_QK // 4, ROPE_THETA)
    va = jnp.stack([jnp.concatenate([p, p], 1) for p in group(ca, sa, DA_QK)])
    cr, sr = cs(t // GRID_W, HEAD_DIM // 2, AXIAL_THETA)
    cc, sc = cs(t % GRID_W, HEAD_DIM // 2, AXIAL_THETA)
    vc = jnp.stack([jnp.concatenate([p, q], 1)
                    for p, q in zip(group(cr, sr, HEAD_DIM // 2), group(cc, sc, HEAD_DIM // 2))])
    cd, sd = cs(t, HEAD_DIM // 4, ROPE_THETA)
    vd = jnp.stack(group(cd, sd, HEAD_DIM))
    qa = DA_QK ** -0.5 * LOG2E
    qh = HEAD_DIM ** -0.5 * LOG2E
    return jnp.stack([va * qa, va, vc * qh, vc, vd * qh, vd])


ROT_HALF_A, ROT_HALF_C, ROT_HALF_D = DA_QK // 8, HEAD_DIM // 4, HEAD_DIM // 8


def _proj_kernel(tsub, n_sub, make_branches, n_extra, x_ref, w_ref, *rest):
    extra = rest[:n_extra]
    o_ref, wbf_ref, acc_ref = rest[n_extra:]
    j = pl.program_id(0)
    n_pass = x_ref.shape[0] // (n_sub * tsub)

    @pl.when(pl.program_id(1) == 0)
    def _():
        wbf_ref[...] = w_ref[...].astype(BF16)

    def pipeline(epilogue):
        def one_pass(h, carry):
            def rows_of(u):
                return pl.ds(pl.multiple_of((h * n_sub + u) * tsub, tsub), tsub)

            def matmul(u):
                acc_ref[u % 2] = jnp.dot(x_ref[rows_of(u), :], wbf_ref[...],
                                         preferred_element_type=F32)

            matmul(0)
            for u in range(n_sub):
                if u + 1 < n_sub:
                    matmul(u + 1)
                for g in range(TN // LANES):
                    lanes = slice(g * LANES, (g + 1) * LANES)
                    y = epilogue(acc_ref[u % 2, :, lanes], rows_of(u), g).astype(o_ref.dtype)
                    if len(o_ref.shape) == 3:
                        o_ref[g, rows_of(u), :] = y
                    else:
                        o_ref[rows_of(u), lanes] = y
            return carry

        lax.fori_loop(0, n_pass, one_pass, 0)

    for cond, epilogue in make_branches(*extra):
        if cond is None:
            pipeline(epilogue)
        else:
            pl.when(cond(j))(functools.partial(pipeline, epilogue))


def _proj(name, xbf, w_in, layer, src_tiles, make_branches, extra, extra_specs, head_major,
          tm=2048, tsub=256, n_sub=4, dtype=BF16):
    s, d = xbf.shape
    tm = math.gcd(tm, s)
    n_sub = min(n_sub, tm // tsub)
    assert tm % (tsub * n_sub) == 0
    n_tiles = len(src_tiles)
    contiguous = src_tiles == tuple(range(src_tiles[0], src_tiles[0] + n_tiles))
    src = (lambda j: src_tiles[0] + j) if contiguous else (lambda j: _select(j, src_tiles))
    if head_major:
        out_spec = pl.BlockSpec((TN // LANES, tm, LANES), lambda j, i: (j, i, 0))
        out_shape = jax.ShapeDtypeStruct((n_tiles * (TN // LANES), s, LANES), dtype)
    else:
        out_spec = pl.BlockSpec((tm, TN), lambda j, i: (i, j))
        out_shape = jax.ShapeDtypeStruct((s, n_tiles * TN), dtype)
    return pl.pallas_call(
        functools.partial(_proj_kernel, tsub, n_sub, make_branches, len(extra)),
        grid=(n_tiles, s // tm),
        in_specs=[pl.BlockSpec((tm, d), lambda j, i: (i, 0)),
                  pl.BlockSpec((None, d, TN), lambda j, i: (layer, 0, src(j)))] + extra_specs(tm),
        out_specs=out_spec,
        out_shape=out_shape,
        scratch_shapes=[pltpu.VMEM((d, TN), BF16), pltpu.VMEM((2, tsub, TN), F32)],
        compiler_params=_cparams(("arbitrary", "arbitrary")),
        name=name,
    )(xbf, w_in, *extra)


def _rot_branches(tab_ref, gn_ref):
    def rot(a, rows, half):
        return (a * tab_ref[0, rows, :] + pltpu.roll(a, half, 1) * tab_ref[1, rows, :]
                + pltpu.roll(a, LANES - half, 1) * tab_ref[2, rows, :])

    def rms(a, gain):
        ms = jnp.mean(a * a, axis=-1, keepdims=True)
        return a * lax.rsqrt(ms + RMS_EPS) * gain

    return [(lambda j: j <= 1, lambda a, rows, g: rot(a, rows, ROT_HALF_A)),
            (lambda j: j == 2, lambda a, rows, g: rot(rms(a, gn_ref[0]), rows, ROT_HALF_C)),
            (lambda j: j == 3, lambda a, rows, g: rot(rms(a, gn_ref[1]), rows, ROT_HALF_C) if g < 2 else a),
            (lambda j: j >= 4, lambda a, rows, g: rot(a, rows, ROT_HALF_D))]


def _plain_branches():
    return [(lambda j: _any_of(j, PLAIN_SILU), lambda a, rows, g: a * jax.nn.sigmoid(a)),
            (lambda j: jnp.logical_not(_any_of(j, PLAIN_SILU)), lambda a, rows, g: a)]


def _gate_branches(bg_ref):
    return [(None, lambda a, rows, g: jax.nn.sigmoid(a + bg_ref[:, g * LANES:(g + 1) * LANES]))]


def _wide_branches(tab_ref):
    def rot(a, rows):
        return (a * tab_ref[0, rows, :] + pltpu.roll(a, ROT_HALF_D, 1) * tab_ref[1, rows, :]
                + pltpu.roll(a, LANES - ROT_HALF_D, 1) * tab_ref[2, rows, :])

    return [(lambda j: j <= 1, lambda a, rows, g: rot(a, rows)),
            (lambda j: j == 2, lambda a, rows, g: a)]


def _inproj(xbf, w_in, b_gate, layer, tabs, gains):
    hr = _proj("inproj_rot", xbf, w_in, layer, ROT_TILES, _rot_branches, (tabs, gains),
               lambda tm: [pl.BlockSpec((None, 3, tm, LANES), lambda j, i: (_select(j, ROT_VARIANT), 0, i, 0)),
                           pl.BlockSpec((None, 2, 1, LANES), lambda j, i: (layer, 0, 0, 0))], True)
    hp = _proj("inproj_plain", xbf, w_in, layer, PLAIN_TILES, _plain_branches, (), lambda tm: [], True,
               n_sub=8)
    hg = _proj("inproj_gate", xbf, w_in, layer, tuple(range(GATE_TILE0, GATE_TILE0 + N_GATE_TILES)),
               _gate_branches, (b_gate,),
               lambda tm: [pl.BlockSpec((None, 1, TN), lambda j, i: (layer, 0, j))], False, n_sub=8)
    hw = _proj("inproj_wide", xbf, w_in, layer, WIDE_TILES, _wide_branches, (tabs,),
               lambda tm: [pl.BlockSpec((None, 3, tm, LANES),
                                        lambda j, i: (jnp.where(j == 0, ROT_VARIANT_DQ, ROT_VARIANT_DK), 0, i, 0))],
               True, n_sub=8, dtype=F32)
    return hr, hp, hg, hw


def _pair_attn_kernel(diff, tq, tk, lam_init, *refs):
    if diff:
        q_ref, k_ref, v_ref, z_ref, lp_ref, sg_ref, o_ref, q2_ref, vt_ref, acc_ref, s_ref = refs
    else:
        q_ref, k_ref, v_ref, z_ref, o_ref, q2_ref, vt_ref, acc_ref, s_ref = refs
    seq = k_ref.shape[0]
    nq = 2 * tq

    @pl.when(pl.program_id(1) == 0)
    def _():
        for c in range(seq // tk):
            vt_ref[c, :LANES, :] = v_ref[c * tk:(c + 1) * tk, :].astype(F32).T.astype(BF16)
            vt_ref[c, LANES:, :] = jnp.ones((VT_PAD, tk), BF16)

    if diff:
        q = q_ref[...]
        lane = lax.broadcasted_iota(jnp.int32, q.shape, 1)
        zero = jnp.zeros_like(q)
        q2_ref[:tq, :] = jnp.where(lane < DA_QK, q, zero)
        q2_ref[tq:, :] = jnp.where(lane >= DA_QK, q, zero)
    else:
        q2_ref[:tq, :] = q_ref[0]
        q2_ref[tq:, :] = q_ref[1]
    acc_ref[...] = jnp.zeros(acc_ref.shape, F32)

    n_chunks = seq // tk

    def scores(c, slot):
        start = pl.multiple_of(c * tk, tk)
        s = lax.dot_general(k_ref[pl.ds(start, tk), :], q2_ref[...], (((1,), (1,)), ((), ())),
                            preferred_element_type=F32)
        s_ref[slot] = s
        return jnp.max(s, axis=0, keepdims=True)

    def fold(c, slot, m_prev, m_chunk):
        m_new = jnp.maximum(m_prev, m_chunk)
        alpha = jnp.exp2(m_prev - m_new)
        p = jnp.exp2(s_ref[slot] - m_new)
        acc_ref[...] = alpha * acc_ref[...] + jnp.dot(vt_ref[c], p.astype(BF16),
                                                      preferred_element_type=F32)
        return m_new

    unroll = 4 if n_chunks % 4 == 0 else 2

    def group(base, carry, last):
        m, m_chunk = carry
        for u in range(unroll):
            m_next = m_chunk if (last and u == unroll - 1) else scores(base + u + 1, (u + 1) % 2)
            m = fold(base + u, u % 2, m, m_chunk)
            m_chunk = m_next
        return m, m_chunk

    carry = (jnp.full((1, nq), NEG_INF, F32), scores(0, 0))
    n_groups = n_chunks // unroll
    if n_groups > 1:
        carry = lax.fori_loop(0, n_groups - 1, lambda j, cc: group(j * unroll, cc, False), carry)
    group((n_groups - 1) * unroll, carry, True)

    acc = acc_ref[...]
    o = (acc[:LANES] / acc[LANES:LANES + 1]).T
    if diff:
        lp = lp_ref[...]
        lam = (jnp.exp(jnp.sum(lp[0:1] * lp[1:2], keepdims=True))
               - jnp.exp(jnp.sum(lp[2:3] * lp[3:4], keepdims=True)) + lam_init)
        dlt = o[:tq] - lam * o[tq:]
        ms = jnp.mean(dlt * dlt, axis=-1, keepdims=True)
        y = dlt * lax.rsqrt(ms + RMS_EPS) * sg_ref[...] * (1.0 - lam_init)
        o_ref[...] = (y * z_ref[...].astype(F32)).astype(BF16)
    else:
        o_ref[:, :LANES] = (o[:tq] * z_ref[0].astype(F32)).astype(BF16)
        o_ref[:, LANES:] = (o[tq:] * z_ref[1].astype(F32)).astype(BF16)


def _pair_attn(hr, hp, diff, layer=0, lam_init=0.0, lam_params=None, subln_g=None, tq=512, tk=512):
    s = hr.shape[1]
    if diff:
        n_kv, qw, qblk = 4, LANES, (None, tq, LANES)
        q_map = lambda hh, i: (CB_AQ + hh, i, 0)
        k_map = lambda hh, i: (CB_AK + hh, 0, 0)
        v_map = lambda hh, i: (CB_AV + hh, 0, 0)
        z_map = lambda hh, i: (CB_AZ + hh, i, 0)
    else:
        n_kv, qw, qblk = 2, 2 * LANES, (2, tq, LANES)
        q_map = lambda hh, i: (CB_CQ // 2 + hh, i, 0)
        k_map = lambda hh, i: (CB_CK + hh, 0, 0)
        v_map = lambda hh, i: (CB_CV + hh, 0, 0)
        z_map = lambda hh, i: (CB_CZ // 2 + hh, i, 0)
    in_specs = [pl.BlockSpec(qblk, q_map),
                pl.BlockSpec((None, s, LANES), k_map),
                pl.BlockSpec((None, s, LANES), v_map),
                pl.BlockSpec(qblk, z_map)]
    args = [hr, hr, hp if diff else hr, hp]
    if diff:
        in_specs += [pl.BlockSpec((None, 4, DA_QK), lambda hh, i: (layer, 0, 0)),
                     pl.BlockSpec((None, 1, LANES), lambda hh, i: (layer, 0, 0))]
        args += [lam_params, subln_g]
    return pl.pallas_call(
        functools.partial(_pair_attn_kernel, diff, tq, tk, lam_init),
        grid=(n_kv, s // tq),
        in_specs=in_specs,
        out_specs=pl.BlockSpec((tq, qw), lambda hh, i: (i, hh)),
        out_shape=jax.ShapeDtypeStruct((s, BRANCH_W), BF16),
        scratch_shapes=[pltpu.VMEM((2 * tq, LANES), BF16),
                        pltpu.VMEM((s // tk, LANES + VT_PAD, tk), BF16),
                        pltpu.VMEM((LANES + VT_PAD, 2 * tq), F32),
                        pltpu.VMEM((2, tk, 2 * tq), F32)],
        compiler_params=_cparams(("arbitrary", "arbitrary")),
        name="diff_attn" if diff else "axial_gqa",
    )(*args)


def _nat_bias(rpb):
    n_layers, n_heads = rpb.shape[:2]
    cols = np.arange(GRID_W)
    c0 = np.clip(cols - NB_COLS // 2, 0, GRID_W - NB_COLS)
    valid = (cols[None, :] >= c0[:, None]) & (cols[None, :] < c0[:, None] + NB_COLS)
    col_sel = (np.arange(2 * NB_COLS - 1)[:, None, None]
               == cols[None, None, :] - cols[None, :, None] + (NB_COLS - 1)).astype(np.float32)
    row_sel = (np.arange(2 * NB_ROWS - 1)[None, None, :]
               == np.arange(NB_ROWS)[:, None, None] + np.arange(NB_ROWS)[None, :, None]).astype(np.float32)
    exact = lax.Precision.HIGHEST
    toep = jnp.einsum('lhdr,rck->lhdck', rpb.astype(F32), col_sel, precision=exact)
    t = jnp.where(valid, toep, NEG_INF)
    bias = jnp.einsum('lhdck,sid->lshcik', t, row_sel, precision=exact)
    return bias.reshape(n_layers, NB_ROWS, n_heads, GRID_W, NB_ROWS * GRID_W)


def _nat_kernel(rows, rps, q_ref, k_ref, v_ref, z_ref, bias_ref, o_ref):
    win = NB_ROWS * GRID_W
    starts, scores = [], []
    for u in range(rps):
        r = pl.program_id(1) * rps + u
        r0 = jnp.clip(r - NB_ROWS // 2, 0, rows - NB_ROWS)
        start = pl.multiple_of(r0 * GRID_W, GRID_W)
        kw = k_ref[pl.ds(start, win), :]
        s = lax.dot_general(q_ref[u * GRID_W:(u + 1) * GRID_W, :], kw, (((1,), (1,)), ((), ())),
                            preferred_element_type=F32)
        scores.append(s * (HEAD_DIM ** -0.5) + bias_ref[r0 - r + (NB_ROWS - 1)])
        starts.append(start)
    probs = []
    for s in scores:
        p = jnp.exp(s - jnp.max(s, axis=-1, keepdims=True))
        probs.append((p.astype(BF16), jnp.sum(p, axis=-1, keepdims=True)))
    for u, (p, l) in enumerate(probs):
        qrows = slice(u * GRID_W, (u + 1) * GRID_W)
        o = jnp.dot(p, v_ref[pl.ds(starts[u], win), :], preferred_element_type=F32) / l
        o_ref[qrows, :] = (o * z_ref[qrows, :].astype(F32)).astype(BF16)


def _nat_attn(hp, bias, layer, rps=32):
    s = hp.shape[1]
    rows = s // GRID_W
    blk = rps * GRID_W
    return pl.pallas_call(
        functools.partial(_nat_kernel, rows, rps),
        grid=(4, rows // rps),
        in_specs=[pl.BlockSpec((None, blk, LANES), lambda hh, r: (CB_BQ + hh, r, 0)),
                  pl.BlockSpec((None, s, LANES), lambda hh, r: (CB_BK + hh, 0, 0)),
                  pl.BlockSpec((None, s, LANES), lambda hh, r: (CB_BV + hh, 0, 0)),
                  pl.BlockSpec((None, blk, LANES), lambda hh, r: (CB_BZ + hh, r, 0)),
                  pl.BlockSpec((None, NB_ROWS, None, GRID_W, NB_ROWS * GRID_W),
                               lambda hh, r: (layer, 0, hh, 0, 0))],
        out_specs=pl.BlockSpec((blk, LANES), lambda hh, r: (r, hh)),
        out_shape=jax.ShapeDtypeStruct((s, BRANCH_W), BF16),
        compiler_params=_cparams(("arbitrary", "arbitrary")),
        name="nat_attn",
    )(hp, hp, hp, hp, bias)


def _dil_masks(tq, rates):
    row = np.arange(tq)[:, None]

    def band(col, rate):
        d = col - row
        return np.where((d >= 0) & (d <= 2 * DIL_SIDE * rate) & (d % rate == 0), 0.0, NEG_INF).astype(np.float32)

    width0 = tq + 2 * DIL_SIDE
    tables = [np.stack([band(np.arange(width0)[None, :] + dlt, 1) for dlt in (DIL_SIDE, 0, -DIL_SIDE)])]
    assert rates[0] == 1
    for rate in rates[1:]:
        reach = DIL_SIDE * rate
        cols = np.arange(-reach, tq + 3 * reach)[None, :]
        full = band(cols, rate)
        tables.append(full.reshape(tq, -1, LANES).transpose(1, 0, 2))
    return [jnp.asarray(t) for t in tables]


def _dil_class_kernel(tq, rate, q_ref, k_ref, v_ref, b_ref, o_ref, lse_ref):
    length = q_ref.shape[0] // rate
    width = tq + 2 * DIL_SIDE
    for a in range(rate):
        cls = pl.ds(a, length, stride=rate)
        qa = q_ref[cls, :].astype(BF16)
        ka = k_ref[cls, :].astype(BF16)
        va = v_ref[cls, :].astype(BF16)
        scored = []
        for i0 in range(0, length, tq):
            ws = min(max(i0 - DIL_SIDE, 0), length - width)
            delta = ws - i0 + DIL_SIDE
            s = lax.dot_general(qa[i0:i0 + tq], ka[ws:ws + width], (((1,), (1,)), ((), ())),
                                preferred_element_type=F32)
            scored.append((i0, ws, s + b_ref[0 if delta > 0 else (2 if delta < 0 else 1)]))
        stats = []
        for i0, ws, s in scored:
            m = jnp.max(s, axis=-1, keepdims=True)
            p = jnp.exp2(s - m)
            stats.append((i0, ws, p.astype(BF16), m, jnp.sum(p, axis=-1, keepdims=True)))
        for i0, ws, p, m, l in stats:
            out_rows = pl.ds(a + rate * i0, tq, stride=rate)
            o_ref[out_rows, :] = jnp.dot(p, va[ws:ws + width], preferred_element_type=F32) / l
            lse_ref[out_rows, :] = jnp.broadcast_to(m + jnp.log2(l), (tq, LANES))


def _dil_kernel(tq, nsub, q0_ref, q1_ref, k0_ref, k1_ref, v0_ref, v1_ref, z_ref, b0_ref, b1_ref,
                o2_ref, lse2_ref, o_ref):
    seq = k0_ref.shape[0]
    q_refs, k_refs, v_refs, b_refs = (q0_ref, q1_ref), (k0_ref, k1_ref), (v0_ref, v1_ref), (b0_ref, b1_ref)
    scores, wins = [], []
    for u in range(nsub):
        t0 = (pl.program_id(1) * nsub + u) * tq
        rows = slice(u * tq, (u + 1) * tq)
        su, wu = [], []
        for rate, q_ref, k_ref, b_ref in zip(DIL_RATES, q_refs, k_refs, b_refs):
            reach = DIL_SIDE * rate
            width = tq + 2 * reach
            start = pl.multiple_of(jnp.clip(t0 - reach, 0, seq - width), DIL_SIDE)
            delta = start - t0 + reach
            s = lax.dot_general(q_ref[rows, :], k_ref[pl.ds(start, width), :], (((1,), (1,)), ((), ())),
                                preferred_element_type=F32)
            if rate == 1:
                s = s + b_ref[jnp.where(delta > 0, 0, jnp.where(delta < 0, 2, 1))]
            else:
                tile0 = (delta + reach) // LANES
                s = jnp.concatenate([s[:, j * LANES:(j + 1) * LANES] + b_ref[tile0 + j]
                                     for j in range(width // LANES)], axis=1)
            su.append(s)
            wu.append((start, width))
        scores.append(su)
        wins.append(wu)
    probs, dens, tops = [], [], []
    for su in scores:
        m = su[0].max(axis=-1, keepdims=True)
        for s in su[1:]:
            m = jnp.maximum(m, s.max(axis=-1, keepdims=True))
        pu = [jnp.exp2(s - m) for s in su]
        l = pu[0].sum(axis=-1, keepdims=True)
        for p in pu[1:]:
            l = l + p.sum(axis=-1, keepdims=True)
        probs.append([p.astype(BF16) for p in pu])
        dens.append(l)
        tops.append(m)
    for u, (pu, wu, l, m) in enumerate(zip(probs, wins, dens, tops)):
        rows = slice(u * tq, (u + 1) * tq)
        o = None
        for p, (start, width), v_ref in zip(pu, wu, v_refs):
            pv = jnp.dot(p, v_ref[pl.ds(start, width), :], preferred_element_type=F32)
            o = pv if o is None else o + pv
        lse2 = lse2_ref[rows, :1]
        top = jnp.maximum(m, lse2)
        w01 = jnp.exp2(m - top)
        w2 = jnp.exp2(lse2 - top)
        mixed = (w01 * o + w2 * o2_ref[rows, :]) / (w01 * l + w2)
        o_ref[rows, :] = (mixed * z_ref[rows, :].astype(F32)).astype(BF16)


def _dil_class_attn(hw, mask, tq=128):
    s = hw.shape[1]
    head = lambda cb: pl.BlockSpec((None, s, LANES), lambda hh: (cb + hh, 0, 0))
    return pl.pallas_call(
        functools.partial(_dil_class_kernel, tq, DIL_RATES[-1]),
        grid=(4,),
        in_specs=[head(CB_WQ), head(CB_WK), head(CB_WV), pl.BlockSpec(mask.shape, lambda hh: (0, 0, 0))],
        out_specs=[head(0), head(0)],
        out_shape=[jax.ShapeDtypeStruct((4, s, LANES), F32), jax.ShapeDtypeStruct((4, s, LANES), F32)],
        compiler_params=_cparams(("arbitrary",)),
        name="dil_class_attn",
    )(hw, hw, hw, mask)


def _dil_attn(hr, hp, hw, tq=128, nsub=16):
    s = hr.shape[1]
    n_dense = len(DIL_RATES) - 1
    masks = _dil_masks(tq, DIL_RATES[:n_dense])
    o2, lse2 = _dil_class_attn(hw, masks[0], tq)
    m_specs = [pl.BlockSpec(m.shape, lambda hh, i: (0, 0, 0)) for m in masks]
    blk = tq * nsub
    qs = [pl.BlockSpec((None, blk, LANES), functools.partial(lambda g, hh, i: (CB_DQ + 4 * g + hh, i, 0), g))
          for g in range(n_dense)]
    ks = [pl.BlockSpec((None, s, LANES), functools.partial(lambda g, hh, i: (CB_DK + 4 * g + hh, 0, 0), g))
          for g in range(n_dense)]
    vs = [pl.BlockSpec((None, s, LANES), functools.partial(lambda g, hh, i: (CB_DV + 4 * g + hh, 0, 0), g))
          for g in range(n_dense)]
    row_blk = pl.BlockSpec((None, blk, LANES), lambda hh, i: (hh, i, 0))
    return pl.pallas_call(
        functools.partial(_dil_kernel, tq, nsub),
        grid=(4, s // blk),
        in_specs=(qs + ks + vs + [pl.BlockSpec((None, blk, LANES), lambda hh, i: (CB_DZ + hh, i, 0))]
                  + m_specs + [row_blk, row_blk]),
        out_specs=pl.BlockSpec((blk, LANES), lambda hh, i: (i, hh)),
        out_shape=jax.ShapeDtypeStruct((s, BRANCH_W), BF16),
        compiler_params=_cparams(("arbitrary", "arbitrary")),
        name="dil_attn",
    )(*([hr] * (2 * n_dense)), *([hp] * (n_dense + 1)), *masks, o2, lse2)


def _merge_kernel(ya_ref, yb_ref, yc_ref, yd_ref, ga_ref, gb_ref, gc_ref, gd_ref, wb_ref, o_ref, wbf_ref):
    @pl.when(pl.program_id(1) == 0)
    def _():
        wbf_ref[...] = wb_ref[...].astype(BF16)

    tsub = min(256, o_ref.shape[0])
    for r0 in range(0, o_ref.shape[0], tsub):
        rows = slice(r0, r0 + tsub)
        acc = None
        for n, (y_ref, g_ref) in enumerate(zip((ya_ref, yb_ref, yc_ref, yd_ref),
                                               (ga_ref, gb_ref, gc_ref, gd_ref))):
            proj = jnp.dot(y_ref[rows, :], wbf_ref[n], preferred_element_type=F32)
            term = proj * g_ref[rows, :].astype(F32)
            acc = term if acc is None else acc + term
        o_ref[rows, :] = acc.astype(BF16)


def _merge(ys, hg, w_branch, layer, tm=2048):
    s = hg.shape[0]
    tm = math.gcd(tm, s)
    n_col = D_MODEL // TN
    y_specs = [pl.BlockSpec((tm, BRANCH_W), lambda j, i: (i, 0)) for _ in range(N_BRANCH)]
    g_specs = [pl.BlockSpec((tm, TN), functools.partial(lambda n, j, i: (i, n_col * n + j), n))
               for n in range(N_BRANCH)]
    return pl.pallas_call(
        _merge_kernel,
        grid=(n_col, s // tm),
        in_specs=y_specs + g_specs + [pl.BlockSpec((None, N_BRANCH, BRANCH_W, TN),
                                                   lambda j, i: (layer, 0, 0, j))],
        out_specs=pl.BlockSpec((tm, TN), lambda j, i: (i, j)),
        out_shape=jax.ShapeDtypeStruct((s, D_MODEL), BF16),
        scratch_shapes=[pltpu.VMEM((N_BRANCH, BRANCH_W, TN), BF16)],
        compiler_params=_cparams(("arbitrary", "arbitrary")),
        name="branch_merge",
    )(*ys, hg, hg, hg, hg, w_branch)


def _out_ln_kernel(alpha, tsub, m_ref, w_ref, x_ref, g_ref, b_ref, y_ref, ybf_ref, acc_ref):
    n_sub = m_ref.shape[0] // tsub

    def matmul(u):
        acc_ref[u % 2] = jnp.dot(m_ref[u * tsub:(u + 1) * tsub, :], w_ref[...], preferred_element_type=F32)

    matmul(0)
    for u in range(n_sub):
        if u + 1 < n_sub:
            matmul(u + 1)
        rows = slice(u * tsub, (u + 1) * tsub)
        y = _ln_rows(alpha * x_ref[rows, :] + acc_ref[u % 2], g_ref[...], b_ref[...])
        y_ref[rows, :] = y
        ybf_ref[rows, :] = y.astype(BF16)


def _out_ln(merged, wo_bf, layer, x, g, b, alpha, tm=512, tsub=128):
    s, d = x.shape
    return pl.pallas_call(
        functools.partial(_out_ln_kernel, alpha, tsub),
        grid=(s // tm,),
        in_specs=[pl.BlockSpec((tm, d), lambda i: (i, 0)),
                  pl.BlockSpec((None, d, d), lambda i: (layer, 0, 0)),
                  pl.BlockSpec((tm, d), lambda i: (i, 0)),
                  pl.BlockSpec((None, 1, d), lambda i: (layer, 0, 0)),
                  pl.BlockSpec((None, 1, d), lambda i: (layer, 0, 0))],
        out_specs=[pl.BlockSpec((tm, d), lambda i: (i, 0)),
                   pl.BlockSpec((tm, d), lambda i: (i, 0))],
        out_shape=[jax.ShapeDtypeStruct((s, d), F32), jax.ShapeDtypeStruct((s, d), BF16)],
        scratch_shapes=[pltpu.VMEM((2, tsub, d), F32)],
        compiler_params=_cparams(("arbitrary",)),
        name="out_ln",
    )(merged, wo_bf, x, g, b)


def kernel(x, emb_ln_g, emb_ln_b, w_in, b_gate, diff_lambda, diff_subln_g, nat_rpb,
           gqa_q_norm_g, gqa_k_norm_g, w_branch, w_out, ln_g, ln_b):
    batch, seq, d = x.shape
    assert batch == 1 and d == D_MODEL and w_in.shape[-1] == D_IN
    depth = w_in.shape[0]
    alpha = (2 * depth) ** 0.25
    tabs = _rope_tables(seq)
    nat_bias = _nat_bias(nat_rpb)
    wo_bf = w_out.astype(BF16)
    gains = jnp.stack([gqa_q_norm_g, gqa_k_norm_g], axis=1).reshape(depth, 2, 1, LANES)
    b_gate3 = b_gate.reshape(depth, 1, -1)
    subln3 = diff_subln_g.reshape(depth, 1, LANES)
    ln_g3, ln_b3 = ln_g.reshape(depth, 1, d), ln_b.reshape(depth, 1, d)
    xf, xbf = _embed_ln(x[0], emb_ln_g, emb_ln_b)
    for l in range(depth):
        lam_init = 0.8 - 0.6 * math.exp(-0.3 * l)
        hr, hp, hg, hw = _inproj(xbf, w_in, b_gate3, l, tabs, gains)
        ya = _pair_attn(hr, hp, True, l, lam_init, diff_lambda, subln3)
        yb = _nat_attn(hp, nat_bias, l)
        yc = _pair_attn(hr, hp, False)
        yd = _dil_attn(hr, hp, hw)
        merged = _merge((ya, yb, yc, yd), hg, w_branch, l)
        xf, xbf = _out_ln(merged, wo_bf, l, xf, ln_g3, ln_b3, alpha)
    return xf[None]
```

```python
import functools
import math

import numpy as np
import jax
import jax.numpy as jnp
from jax import lax
from jax.experimental import pallas as pl
from jax.experimental.pallas import tpu as pltpu

F32 = jnp.float32
BF16 = jnp.bfloat16

D_MODEL = 2048
GRID_W = 64
HEAD_DIM = 128
N_BRANCH = 4
BRANCH_W = D_MODEL // 4
DA_QK = HEAD_DIM // 2
NB_ROWS = 8
NB_COLS = 16
AXIAL_THETA = 10000.0
DIL_RATES = (1, 4, 16)
DIL_SIDE = 64
ROPE_THETA = 500000.0
LN_EPS = 1e-5
RMS_EPS = 1e-6
NEG_INF = -1e30
LOG2E = math.log2(math.e)

LANES = 128
VT_PAD = 16
TN = 512
D_IN = 18944

ROT_TILES = (0, 1, 8, 9, 11, 12, 14, 15)
PLAIN_TILES = (2, 3, 4, 5, 6, 7, 10, 17, 18, 20)
WIDE_TILES = (13, 16, 19)
GATE_TILE0, N_GATE_TILES = 21, 16
ROT_VARIANT = (0, 1, 2, 3, 4, 4, 5, 5)
ROT_VARIANT_DQ, ROT_VARIANT_DK = 4, 5
CB_AQ, CB_AK, CB_CQ, CB_CK, CB_CV, CB_DQ, CB_DK = 0, 4, 8, 12, 14, 16, 24
CB_AV, CB_AZ, CB_BQ, CB_BK, CB_BV, CB_BZ, CB_CZ, CB_DV, CB_DZ = 0, 4, 8, 12, 16, 20, 24, 28, 36
CB_WQ, CB_WK, CB_WV = 0, 4, 8
PLAIN_SILU = (1, 5, 6, 9)

VMEM_LIMIT = 56 * 1024 * 1024


def _cparams(sem):
    return pltpu.CompilerParams(dimension_semantics=sem, vmem_limit_bytes=VMEM_LIMIT)


def _select(j, values):
    out = values[-1]
    for t in range(len(values) - 2, -1, -1):
        out = jnp.where(j == t, values[t], out)
    return out


def _any_of(j, tiles):
    c = j == tiles[0]
    for t in tiles[1:]:
        c = c | (j == t)
    return c


def _ln_rows(x, g, b):
    mu = jnp.mean(x, axis=-1, keepdims=True)
    xc = x - mu
    var = jnp.mean(xc * xc, axis=-1, keepdims=True)
    return xc * lax.rsqrt(var + LN_EPS) * g + b


def _embed_ln_kernel(x_ref, g_ref, b_ref, y_ref, ybf_ref):
    y = _ln_rows(x_ref[...], g_ref[...], b_ref[...])
    y_ref[...] = y
    ybf_ref[...] = y.astype(BF16)


def _embed_ln(x, g, b, tm=512):
    s, d = x.shape
    return pl.pallas_call(
        _embed_ln_kernel,
        grid=(s // tm,),
        in_specs=[pl.BlockSpec((tm, d), lambda i: (i, 0)),
                  pl.BlockSpec((1, d), lambda i: (0, 0)),
                  pl.BlockSpec((1, d), lambda i: (0, 0))],
        out_specs=[pl.BlockSpec((tm, d), lambda i: (i, 0)),
                   pl.BlockSpec((tm, d), lambda i: (i, 0))],
        out_shape=[jax.ShapeDtypeStruct((s, d), F32), jax.ShapeDtypeStruct((s, d), BF16)],
        compiler_params=_cparams(("arbitrary",)),
        name="embed_ln",
    )(x, g.reshape(1, d), b.reshape(1, d))


def _rope_tables(seq):
    t = np.arange(seq)

    def cs(pos, dim, theta):
        half = dim // 2
        inv = np.power(float(theta), -np.arange(half, dtype=np.float64) * 2.0 / dim)
        ang = pos.astype(np.float64)[:, None] * inv[None, :]
        return jnp.asarray(np.cos(ang), F32), jnp.asarray(np.sin(ang), F32)

    def group(cos, sin, width):
        half = cos.shape[1]
        pad = width - 2 * half
        one = jnp.ones((seq, pad), F32)
        zero = jnp.zeros((seq, pad), F32)
        zh = jnp.zeros((seq, half), F32)
        return (jnp.concatenate([cos, cos, one], 1),
                jnp.concatenate([zh, sin, zero], 1),
                jnp.concatenate([-sin, zh, zero], 1))

    ca, sa = cs(t, DA_QK // 4, ROPE_THETA)
    va = jnp.stack([jnp.concatenate([p, p], 1) for p in group(ca, sa, DA_QK)])
    cr, sr = cs(t // GRID_W, HEAD_DIM // 2, AXIAL_THETA)
    cc, sc = cs(t % GRID_W, HEAD_DIM // 2, AXIAL_THETA)
    vc = jnp.stack([jnp.concatenate([p, q], 1)
                    for p, q in zip(group(cr, sr, HEAD_DIM // 2), group(cc, sc, HEAD_DIM // 2))])
    cd, sd = cs(t, HEAD_DIM // 4, ROPE_THETA)
    vd = jnp.stack(group(cd, sd, HEAD_DIM))
    qa = DA_QK ** -0.5 * LOG2E
    qh = HEAD_DIM ** -0.5 * LOG2E
    return jnp.stack([va * qa, va, vc * qh, vc, vd * qh, vd])


ROT_HALF_A, ROT_HALF_C, ROT_HALF_D = DA_QK // 8, HEAD_DIM // 4, HEAD_DIM // 8


def _proj_kernel(tsub, n_sub, make_branches, n_extra, x_ref, w_ref, *rest):
    extra = rest[:n_extra]
    o_ref, wbf_ref, acc_ref = rest[n_extra:]
    j = pl.program_id(0)
    n_pass = x_ref.shape[0] // (n_sub * tsub)

    @pl.when(pl.program_id(1) == 0)
    def _():
        wbf_ref[...] = w_ref[...].astype(BF16)

    def pipeline(epilogue):
        def one_pass(h, carry):
            def rows_of(u):
                return pl.ds(pl.multiple_of((h * n_sub + u) * tsub, tsub), tsub)

            def matmul(u):
                acc_ref[u % 2] = jnp.dot(x_ref[rows_of(u), :], wbf_ref[...],
                                         preferred_element_type=F32)

            matmul(0)
            for u in range(n_sub):
                if u + 1 < n_sub:
                    matmul(u + 1)
                for g in range(TN // LANES):
                    lanes = slice(g * LANES, (g + 1) * LANES)
                    y = epilogue(acc_ref[u % 2, :, lanes], rows_of(u), g).astype(o_ref.dtype)
                    if len(o_ref.shape) == 3:
                        o_ref[g, rows_of(u), :] = y
                    else:
                        o_ref[rows_of(u), lanes] = y
            return carry

        lax.fori_loop(0, n_pass, one_pass, 0)

    for cond, epilogue in make_branches(*extra):
        if cond is None:
            pipeline(epilogue)
        else:
            pl.when(cond(j))(functools.partial(pipeline, epilogue))


def _proj(name, xbf, w_in, layer, src_tiles, make_branches, extra, extra_specs, head_major,
          tm=2048, tsub=256, n_sub=4, dtype=BF16):
    s, d = xbf.shape
    tm = math.gcd(tm, s)
    n_sub = min(n_sub, tm // tsub)
    assert tm % (tsub * n_sub) == 0
    n_tiles = len(src_tiles)
    contiguous = src_tiles == tuple(range(src_tiles[0], src_tiles[0] + n_tiles))
    src = (lambda j: src_tiles[0] + j) if contiguous else (lambda j: _select(j, src_tiles))
    if head_major:
        out_spec = pl.BlockSpec((TN // LANES, tm, LANES), lambda j, i: (j, i, 0))
        out_shape = jax.ShapeDtypeStruct((n_tiles * (TN // LANES), s, LANES), dtype)
    else:
        out_spec = pl.BlockSpec((tm, TN), lambda j, i: (i, j))
        out_shape = jax.ShapeDtypeStruct((s, n_tiles * TN), dtype)
    return pl.pallas_call(
        functools.partial(_proj_kernel, tsub, n_sub, make_branches, len(extra)),
        grid=(n_tiles, s // tm),
        in_specs=[pl.BlockSpec((tm, d), lambda j, i: (i, 0)),
                  pl.BlockSpec((None, d, TN), lambda j, i: (layer, 0, src(j)))] + extra_specs(tm),
        out_specs=out_spec,
        out_shape=out_shape,
        scratch_shapes=[pltpu.VMEM((d, TN), BF16), pltpu.VMEM((2, tsub, TN), F32)],
        compiler_params=_cparams(("arbitrary", "arbitrary")),
        name=name,
    )(xbf, w_in, *extra)


def _rot_branches(tab_ref, gn_ref):
    def rot(a, rows, half):
        return (a * tab_ref[0, rows, :] + pltpu.roll(a, half, 1) * tab_ref[1, rows, :]
                + pltpu.roll(a, LANES - half, 1) * tab_ref[2, rows, :])

    def rms(a, gain):
        ms = jnp.mean(a * a, axis=-1, keepdims=True)
        return a * lax.rsqrt(ms + RMS_EPS) * gain

    return [(lambda j: j <= 1, lambda a, rows, g: rot(a, rows, ROT_HALF_A)),
            (lambda j: j == 2, lambda a, rows, g: rot(rms(a, gn_ref[0]), rows, ROT_HALF_C)),
            (lambda j: j == 3, lambda a, rows, g: rot(rms(a, gn_ref[1]), rows, ROT_HALF_C) if g < 2 else a),
            (lambda j: j >= 4, lambda a, rows, g: rot(a, rows, ROT_HALF_D))]


def _plain_branches():
    return [(lambda j: _any_of(j, PLAIN_SILU), lambda a, rows, g: a * jax.nn.sigmoid(a)),
            (lambda j: jnp.logical_not(_any_of(j, PLAIN_SILU)), lambda a, rows, g: a)]


def _gate_branches(bg_ref):
    return [(None, lambda a, rows, g: jax.nn.sigmoid(a + bg_ref[:, g * LANES:(g + 1) * LANES]))]


def _wide_branches(tab_ref):
    def rot(a, rows):
        return (a * tab_ref[0, rows, :] + pltpu.roll(a, ROT_HALF_D, 1) * tab_ref[1, rows, :]
                + pltpu.roll(a, LANES - ROT_HALF_D, 1) * tab_ref[2, rows, :])

    return [(lambda j: j <= 1, lambda a, rows, g: rot(a, rows)),
            (lambda j: j == 2, lambda a, rows, g: a)]


def _inproj(xbf, w_in, b_gate, layer, tabs, gains):
    hr = _proj("inproj_rot", xbf, w_in, layer, ROT_TILES, _rot_branches, (tabs, gains),
               lambda tm: [pl.BlockSpec((None, 3, tm, LANES), lambda j, i: (_select(j, ROT_VARIANT), 0, i, 0)),
                           pl.BlockSpec((None, 2, 1, LANES), lambda j, i: (layer, 0, 0, 0))], True)
    hp = _proj("inproj_plain", xbf, w_in, layer, PLAIN_TILES, _plain_branches, (), lambda tm: [], True,
               n_sub=8)
    hg = _proj("inproj_gate", xbf, w_in, layer, tuple(range(GATE_TILE0, GATE_TILE0 + N_GATE_TILES)),
               _gate_branches, (b_gate,),
               lambda tm: [pl.BlockSpec((None, 1, TN), lambda j, i: (layer, 0, j))], False, n_sub=8)
    hw = _proj("inproj_wide", xbf, w_in, layer, WIDE_TILES, _wide_branches, (tabs,),
               lambda tm: [pl.BlockSpec((None, 3, tm, LANES),
                                        lambda j, i: (jnp.where(j == 0, ROT_VARIANT_DQ, ROT_VARIANT_DK), 0, i, 0))],
               True, n_sub=8, dtype=F32)
    return hr, hp, hg, hw


def _pair_attn_kernel(diff, tq, tk, lam_init, *refs):
    if diff:
        q_ref, k_ref, v_ref, z_ref, lp_ref, sg_ref, o_ref, q2_ref, vt_ref, acc_ref, s_ref = refs
    else:
        q_ref, k_ref, v_ref, z_ref, o_ref, q2_ref, vt_ref, acc_ref, s_ref = refs
    seq = k_ref.shape[0]
    nq = 2 * tq

    @pl.when(pl.program_id(1) == 0)
    def _():
        for c in range(seq // tk):
            vt_ref[c, :LANES, :] = v_ref[c * tk:(c + 1) * tk, :].astype(F32).T.astype(BF16)
            vt_ref[c, LANES:, :] = jnp.ones((VT_PAD, tk), BF16)

    if diff:
        q = q_ref[...]
        lane = lax.broadcasted_iota(jnp.int32, q.shape, 1)
        zero = jnp.zeros_like(q)
        q2_ref[:tq, :] = jnp.where(lane < DA_QK, q, zero)
        q2_ref[tq:, :] = jnp.where(lane >= DA_QK, q, zero)
    else:
        q2_ref[:tq, :] = q_ref[0]
        q2_ref[tq:, :] = q_ref[1]
    acc_ref[...] = jnp.zeros(acc_ref.shape, F32)

    n_chunks = seq // tk

    def scores(c, slot):
        start = pl.multiple_of(c * tk, tk)
        s = lax.dot_general(k_ref[pl.ds(start, tk), :], q2_ref[...], (((1,), (1,)), ((), ())),
                            preferred_element_type=F32)
        s_ref[slot] = s
        return jnp.max(s, axis=0, keepdims=True)

    def fold(c, slot, m_prev, m_chunk):
        m_new = jnp.maximum(m_prev, m_chunk)
        alpha = jnp.exp2(m_prev - m_new)
        p = jnp.exp2(s_ref[slot] - m_new)
        acc_ref[...] = alpha * acc_ref[...] + jnp.dot(vt_ref[c], p.astype(BF16),
                                                      preferred_element_type=F32)
        return m_new

    unroll = 4 if n_chunks % 4 == 0 else 2

    def group(base, carry, last):
        m, m_chunk = carry
        for u in range(unroll):
            m_next = m_chunk if (last and u == unroll - 1) else scores(base + u + 1, (u + 1) % 2)
            m = fold(base + u, u % 2, m, m_chunk)
            m_chunk = m_next
        return m, m_chunk

    carry = (jnp.full((1, nq), NEG_INF, F32), scores(0, 0))
    n_groups = n_chunks // unroll
    if n_groups > 1:
        carry = lax.fori_loop(0, n_groups - 1, lambda j, cc: group(j * unroll, cc, False), carry)
    group((n_groups - 1) * unroll, carry, True)

    acc = acc_ref[...]
    o = (acc[:LANES] / acc[LANES:LANES + 1]).T
    if diff:
        lp = lp_ref[...]
        lam = (jnp.exp(jnp.sum(lp[0:1] * lp[1:2], keepdims=True))
               - jnp.exp(jnp.sum(lp[2:3] * lp[3:4], keepdims=True)) + lam_init)
        dlt = o[:tq] - lam * o[tq:]
        ms = jnp.mean(dlt * dlt, axis=-1, keepdims=True)
        y = dlt * lax.rsqrt(ms + RMS_EPS) * sg_ref[...] * (1.0 - lam_init)
        o_ref[...] = (y * z_ref[...].astype(F32)).astype(BF16)
    else:
        o_ref[:, :LANES] = (o[:tq] * z_ref[0].astype(F32)).astype(BF16)
        o_ref[:, LANES:] = (o[tq:] * z_ref[1].astype(F32)).astype(BF16)


def _pair_attn(hr, hp, diff, layer=0, lam_init=0.0, lam_params=None, subln_g=None, tq=512, tk=512):
    s = hr.shape[1]
    if diff:
        n_kv, qw, qblk = 4, LANES, (None, tq, LANES)
        q_map = lambda hh, i: (CB_AQ + hh, i, 0)
        k_map = lambda hh, i: (CB_AK + hh, 0, 0)
        v_map = lambda hh, i: (CB_AV + hh, 0, 0)
        z_map = lambda hh, i: (CB_AZ + hh, i, 0)
    else:
        n_kv, qw, qblk = 2, 2 * LANES, (2, tq, LANES)
        q_map = lambda hh, i: (CB_CQ // 2 + hh, i, 0)
        k_map = lambda hh, i: (CB_CK + hh, 0, 0)
        v_map = lambda hh, i: (CB_CV + hh, 0, 0)
        z_map = lambda hh, i: (CB_CZ // 2 + hh, i, 0)
    in_specs = [pl.BlockSpec(qblk, q_map),
                pl.BlockSpec((None, s, LANES), k_map),
                pl.BlockSpec((None, s, LANES), v_map),
                pl.BlockSpec(qblk, z_map)]
    args = [hr, hr, hp if diff else hr, hp]
    if diff:
        in_specs += [pl.BlockSpec((None, 4, DA_QK), lambda hh, i: (layer, 0, 0)),
                     pl.BlockSpec((None, 1, LANES), lambda hh, i: (layer, 0, 0))]
        args += [lam_params, subln_g]
    return pl.pallas_call(
        functools.partial(_pair_attn_kernel, diff, tq, tk, lam_init),
        grid=(n_kv, s // tq),
        in_specs=in_specs,
        out_specs=pl.BlockSpec((tq, qw), lambda hh, i: (i, hh)),
        out_shape=jax.ShapeDtypeStruct((s, BRANCH_W), BF16),
        scratch_shapes=[pltpu.VMEM((2 * tq, LANES), BF16),
                        pltpu.VMEM((s // tk, LANES + VT_PAD, tk), BF16),
                        pltpu.VMEM((LANES + VT_PAD, 2 * tq), F32),
                        pltpu.VMEM((2, tk, 2 * tq), F32)],
        compiler_params=_cparams(("arbitrary", "arbitrary")),
        name="diff_attn" if diff else "axial_gqa",
    )(*args)


def _nat_bias(rpb):
    n_layers, n_heads = rpb.shape[:2]
    cols = np.arange(GRID_W)
    c0 = np.clip(cols - NB_COLS // 2, 0, GRID_W - NB_COLS)
    valid = (cols[None, :] >= c0[:, None]) & (cols[None, :] < c0[:, None] + NB_COLS)
    col_sel = (np.arange(2 * NB_COLS - 1)[:, None, None]
               == cols[None, None, :] - cols[None, :, None] + (NB_COLS - 1)).astype(np.float32)
    row_sel = (np.arange(2 * NB_ROWS - 1)[None, None, :]
               == np.arange(NB_ROWS)[:, None, None] + np.arange(NB_ROWS)[None, :, None]).astype(np.float32)
    exact = lax.Precision.HIGHEST
    toep = jnp.einsum('lhdr,rck->lhdck', rpb.astype(F32), col_sel, precision=exact)
    t = jnp.where(valid, toep, NEG_INF)
    bias = jnp.einsum('lhdck,sid->lshcik', t, row_sel, precision=exact)
    return bias.reshape(n_layers, NB_ROWS, n_heads, GRID_W, NB_ROWS * GRID_W)


def _nat_kernel(rows, rps, q_ref, k_ref, v_ref, z_ref, bias_ref, o_ref):
    win = NB_ROWS * GRID_W
    starts, scores = [], []
    for u in range(rps):
        r = pl.program_id(1) * rps + u
        r0 = jnp.clip(r - NB_ROWS // 2, 0, rows - NB_ROWS)
        start = pl.multiple_of(r0 * GRID_W, GRID_W)
        kw = k_ref[pl.ds(start, win), :]
        s = lax.dot_general(q_ref[u * GRID_W:(u + 1) * GRID_W, :], kw, (((1,), (1,)), ((), ())),
                            preferred_element_type=F32)
        scores.append(s * (HEAD_DIM ** -0.5) + bias_ref[r0 - r + (NB_ROWS - 1)])
        starts.append(start)
    probs = []
    for s in scores:
        p = jnp.exp(s - jnp.max(s, axis=-1, keepdims=True))
        probs.append((p.astype(BF16), jnp.sum(p, axis=-1, keepdims=True)))
    for u, (p, l) in enumerate(probs):
        qrows = slice(u * GRID_W, (u + 1) * GRID_W)
        o = jnp.dot(p, v_ref[pl.ds(starts[u], win), :], preferred_element_type=F32) / l
        o_ref[qrows, :] = (o * z_ref[qrows, :].astype(F32)).astype(BF16)


def _nat_attn(hp, bias, layer, rps=32):
    s = hp.shape[1]
    rows = s // GRID_W
    blk = rps * GRID_W
    return pl.pallas_call(
        functools.partial(_nat_kernel, rows, rps),
        grid=(4, rows // rps),
        in_specs=[pl.BlockSpec((None, blk, LANES), lambda hh, r: (CB_BQ + hh, r, 0)),
                  pl.BlockSpec((None, s, LANES), lambda hh, r: (CB_BK + hh, 0, 0)),
                  pl.BlockSpec((None, s, LANES), lambda hh, r: (CB_BV + hh, 0, 0)),
                  pl.BlockSpec((None, blk, LANES), lambda hh, r: (CB_BZ + hh, r, 0)),
                  pl.BlockSpec((None, NB_ROWS, None, GRID_W, NB_ROWS * GRID_W),
                               lambda hh, r: (layer, 0, hh, 0, 0))],
        out_specs=pl.BlockSpec((blk, LANES), lambda hh, r: (r, hh)),
        out_shape=jax.ShapeDtypeStruct((s, BRANCH_W), BF16),
        compiler_params=_cparams(("arbitrary", "arbitrary")),
        name="nat_attn",
    )(hp, hp, hp, hp, bias)


def _dil_masks(tq, rates):
    row = np.arange(tq)[:, None]

    def band(col, rate):
        d = col - row
        return np.where((d >= 0) & (d <= 2 * DIL_SIDE * rate) & (d % rate == 0), 0.0, NEG_INF).astype(np.float32)

    width0 = tq + 2 * DIL_SIDE
    tables = [np.stack([band(np.arange(width0)[None, :] + dlt, 1) for dlt in (DIL_SIDE, 0, -DIL_SIDE)])]
    assert rates[0] == 1
    for rate in rates[1:]:
        reach = DIL_SIDE * rate
        cols = np.arange(-reach, tq + 3 * reach)[None, :]
        full = band(cols, rate)
        tables.append(full.reshape(tq, -1, LANES).transpose(1, 0, 2))
    return [jnp.asarray(t) for t in tables]


def _dil_class_kernel(tq, rate, q_ref, k_ref, v_ref, b_ref, o_ref, lse_ref):
    length = q_ref.shape[0] // rate
    width = tq + 2 * DIL_SIDE
    for a in range(rate):
        cls = pl.ds(a, length, stride=rate)
        qa = q_ref[cls, :].astype(BF16)
        ka = k_ref[cls, :].astype(BF16)
        va = v_ref[cls, :].astype(BF16)
        scored = []
        for i0 in range(0, length, tq):
            ws = min(max(i0 - DIL_SIDE, 0), length - width)
            delta = ws - i0 + DIL_SIDE
            s = lax.dot_general(qa[i0:i0 + tq], ka[ws:ws + width], (((1,), (1,)), ((), ())),
                                preferred_element_type=F32)
            scored.append((i0, ws, s + b_ref[0 if delta > 0 else (2 if delta < 0 else 1)]))
        stats = []
        for i0, ws, s in scored:
            m = jnp.max(s, axis=-1, keepdims=True)
            p = jnp.exp2(s - m)
            stats.append((i0, ws, p.astype(BF16), m, jnp.sum(p, axis=-1, keepdims=True)))
        for i0, ws, p, m, l in stats:
            out_rows = pl.ds(a + rate * i0, tq, stride=rate)
            o_ref[out_rows, :] = jnp.dot(p, va[ws:ws + width], preferred_element_type=F32) / l
            lse_ref[out_rows, :] = jnp.broadcast_to(m + jnp.log2(l), (tq, LANES))


def _dil_kernel(tq, nsub, q0_ref, q1_ref, k0_ref, k1_ref, v0_ref, v1_ref, z_ref, b0_ref, b1_ref,
                o2_ref, lse2_ref, o_ref):
    seq = k0_ref.shape[0]
    q_refs, k_refs, v_refs, b_refs = (q0_ref, q1_ref), (k0_ref, k1_ref), (v0_ref, v1_ref), (b0_ref, b1_ref)
    scores, wins = [], []
    for u in range(nsub):
        t0 = (pl.program_id(1) * nsub + u) * tq
        rows = slice(u * tq, (u + 1) * tq)
        su, wu = [], []
        for rate, q_ref, k_ref, b_ref in zip(DIL_RATES, q_refs, k_refs, b_refs):
            reach = DIL_SIDE * rate
            width = tq + 2 * reach
            start = pl.multiple_of(jnp.clip(t0 - reach, 0, seq - width), DIL_SIDE)
            delta = start - t0 + reach
            s = lax.dot_general(q_ref[rows, :], k_ref[pl.ds(start, width), :], (((1,), (1,)), ((), ())),
                                preferred_element_type=F32)
            if rate == 1:
                s = s + b_ref[jnp.where(delta > 0, 0, jnp.where(delta < 0, 2, 1))]
            else:
                tile0 = (delta + reach) // LANES
                s = jnp.concatenate([s[:, j * LANES:(j + 1) * LANES] + b_ref[tile0 + j]
                                     for j in range(width // LANES)], axis=1)
            su.append(s)
            wu.append((start, width))
        scores.append(su)
        wins.append(wu)
    probs, dens, tops = [], [], []
    for su in scores:
        m = su[0].max(axis=-1, keepdims=True)
        for s in su[1:]:
            m = jnp.maximum(m, s.max(axis=-1, keepdims=True))
        pu = [jnp.exp2(s - m) for s in su]
        l = pu[0].sum(axis=-1, keepdims=True)
        for p in pu[1:]:
            l = l + p.sum(axis=-1, keepdims=True)
        probs.append([p.astype(BF16) for p in pu])
        dens.append(l)
        tops.append(m)
    for u, (pu, wu, l, m) in enumerate(zip(probs, wins, dens, tops)):
        rows = slice(u * tq, (u + 1) * tq)
        o = None
        for p, (start, width), v_ref in zip(pu, wu, v_refs):
            pv = jnp.dot(p, v_ref[pl.ds(start, width), :], preferred_element_type=F32)
            o = pv if o is None else o + pv
        lse2 = lse2_ref[rows, :1]
        top = jnp.maximum(m, lse2)
        w01 = jnp.exp2(m - top)
        w2 = jnp.exp2(lse2 - top)
        mixed = (w01 * o + w2 * o2_ref[rows, :]) / (w01 * l + w2)
        o_ref[rows, :] = (mixed * z_ref[rows, :].astype(F32)).astype(BF16)


def _dil_class_attn(hw, mask, tq=128):
    s = hw.shape[1]
    head = lambda cb: pl.BlockSpec((None, s, LANES), lambda hh: (cb + hh, 0, 0))
    return pl.pallas_call(
        functools.partial(_dil_class_kernel, tq, DIL_RATES[-1]),
        grid=(4,),
        in_specs=[head(CB_WQ), head(CB_WK), head(CB_WV), pl.BlockSpec(mask.shape, lambda hh: (0, 0, 0))],
        out_specs=[head(0), head(0)],
        out_shape=[jax.ShapeDtypeStruct((4, s, LANES), F32), jax.ShapeDtypeStruct((4, s, LANES), F32)],
        compiler_params=_cparams(("arbitrary",)),
        name="dil_class_attn",
    )(hw, hw, hw, mask)


def _dil_attn(hr, hp, hw, tq=128, nsub=16):
    s = hr.shape[1]
    n_dense = len(DIL_RATES) - 1
    masks = _dil_masks(tq, DIL_RATES[:n_dense])
    o2, lse2 = _dil_class_attn(hw, masks[0], tq)
    m_specs = [pl.BlockSpec(m.shape, lambda hh, i: (0, 0, 0)) for m in masks]
    blk = tq * nsub
    qs = [pl.BlockSpec((None, blk, LANES), functools.partial(lambda g, hh, i: (CB_DQ + 4 * g + hh, i, 0), g))
          for g in range(n_dense)]
    ks = [pl.BlockSpec((None, s, LANES), functools.partial(lambda g, hh, i: (CB_DK + 4 * g + hh, 0, 0), g))
          for g in range(n_dense)]
    vs = [pl.BlockSpec((None, s, LANES), functools.partial(lambda g, hh, i: (CB_DV + 4 * g + hh, 0, 0), g))
          for g in range(n_dense)]
    row_blk = pl.BlockSpec((None, blk, LANES), lambda hh, i: (hh, i, 0))
    return pl.pallas_call(
        functools.partial(_dil_kernel, tq, nsub),
        grid=(4, s // blk),
        in_specs=(qs + ks + vs + [pl.BlockSpec((None, blk, LANES), lambda hh, i: (CB_DZ + hh, i, 0))]
                  + m_specs + [row_blk, row_blk]),
        out_specs=pl.BlockSpec((blk, LANES), lambda hh, i: (i, hh)),
        out_shape=jax.ShapeDtypeStruct((s, BRANCH_W), BF16),
        compiler_params=_cparams(("arbitrary", "arbitrary")),
        name="dil_attn",
    )(*([hr] * (2 * n_dense)), *([hp] * (n_dense + 1)), *masks, o2, lse2)


def _merge_kernel(ya_ref, yb_ref, yc_ref, yd_ref, g_ref, wb_ref, o_ref):
    tsub = min(256, o_ref.shape[0])
    for r0 in range(0, o_ref.shape[0], tsub):
        rows = slice(r0, r0 + tsub)
        for c0 in range(0, D_MODEL, TN):
            acc = None
            for n, y_ref in enumerate((ya_ref, yb_ref, yc_ref, yd_ref)):
                proj = jnp.dot(y_ref[rows, :], wb_ref[n, :, c0:c0 + TN], preferred_element_type=F32)
                term = proj * g_ref[rows, n * D_MODEL + c0:n * D_MODEL + c0 + TN].astype(F32)
                acc = term if acc is None else acc + term
            o_ref[rows, c0:c0 + TN] = acc.astype(BF16)


def _merge(ys, hg, wb_bf, layer, tm=512):
    s = hg.shape[0]
    tm = math.gcd(tm, s)
    y_specs = [pl.BlockSpec((tm, BRANCH_W), lambda i: (i, 0)) for _ in range(N_BRANCH)]
    return pl.pallas_call(
        _merge_kernel,
        grid=(s // tm,),
        in_specs=y_specs + [pl.BlockSpec((tm, N_BRANCH * D_MODEL), lambda i: (i, 0)),
                            pl.BlockSpec((None, N_BRANCH, BRANCH_W, D_MODEL), lambda i: (layer, 0, 0, 0))],
        out_specs=pl.BlockSpec((tm, D_MODEL), lambda i: (i, 0)),
        out_shape=jax.ShapeDtypeStruct((s, D_MODEL), BF16),
        compiler_params=_cparams(("arbitrary",)),
        name="branch_merge",
    )(*ys, hg, wb_bf)


def _out_ln_kernel(alpha, tsub, m_ref, w_ref, x_ref, g_ref, b_ref, y_ref, ybf_ref, acc_ref):
    n_sub = m_ref.shape[0] // tsub

    def matmul(u):
        acc_ref[u % 2] = jnp.dot(m_ref[u * tsub:(u + 1) * tsub, :], w_ref[...], preferred_element_type=F32)

    matmul(0)
    for u in range(n_sub):
        if u + 1 < n_sub:
            matmul(u + 1)
        rows = slice(u * tsub, (u + 1) * tsub)
        y = _ln_rows(alpha * x_ref[rows, :] + acc_ref[u % 2], g_ref[...], b_ref[...])
        y_ref[rows, :] = y
        ybf_ref[rows, :] = y.astype(BF16)


def _out_ln(merged, wo_bf, layer, x, g, b, alpha, tm=512, tsub=128):
    s, d = x.shape
    return pl.pallas_call(
        functools.partial(_out_ln_kernel, alpha, tsub),
        grid=(s // tm,),
        in_specs=[pl.BlockSpec((tm, d), lambda i: (i, 0)),
                  pl.BlockSpec((None, d, d), lambda i: (layer, 0, 0)),
                  pl.BlockSpec((tm, d), lambda i: (i, 0)),
                  pl.BlockSpec((None, 1, d), lambda i: (layer, 0, 0)),
                  pl.BlockSpec((None, 1, d), lambda i: (layer, 0, 0))],
        out_specs=[pl.BlockSpec((tm, d), lambda i: (i, 0)),
                   pl.BlockSpec((tm, d), lambda i: (i, 0))],
        out_shape=[jax.ShapeDtypeStruct((s, d), F32), jax.ShapeDtypeStruct((s, d), BF16)],
        scratch_shapes=[pltpu.VMEM((2, tsub, d), F32)],
        compiler_params=_cparams(("arbitrary",)),
        name="out_ln",
    )(merged, wo_bf, x, g, b)


def kernel(x, emb_ln_g, emb_ln_b, w_in, b_gate, diff_lambda, diff_subln_g, nat_rpb,
           gqa_q_norm_g, gqa_k_norm_g, w_branch, w_out, ln_g, ln_b):
    batch, seq, d = x.shape
    assert batch == 1 and d == D_MODEL and w_in.shape[-1] == D_IN
    depth = w_in.shape[0]
    alpha = (2 * depth) ** 0.25
    tabs = _rope_tables(seq)
    nat_bias = _nat_bias(nat_rpb)
    wo_bf = w_out.astype(BF16)
    wb_bf = w_branch.astype(BF16)
    gains = jnp.stack([gqa_q_norm_g, gqa_k_norm_g], axis=1).reshape(depth, 2, 1, LANES)
    b_gate3 = b_gate.reshape(depth, 1, -1)
    subln3 = diff_subln_g.reshape(depth, 1, LANES)
    ln_g3, ln_b3 = ln_g.reshape(depth, 1, d), ln_b.reshape(depth, 1, d)
    xf, xbf = _embed_ln(x[0], emb_ln_g, emb_ln_b)
    for l in range(depth):
        lam_init = 0.8 - 0.6 * math.exp(-0.3 * l)
        hr, hp, hg, hw = _inproj(xbf, w_in, b_gate3, l, tabs, gains)
        ya = _pair_attn(hr, hp, True, l, lam_init, diff_lambda, subln3)
        yb = _nat_attn(hp, nat_bias, l)
        yc = _pair_attn(hr, hp, False)
        yd = _dil_attn(hr, hp, hw)
        merged = _merge((ya, yb, yc, yd), hg, wb_bf, l)
        xf, xbf = _out_ln(merged, wo_bf, l, xf, ln_g3, ln_b3, alpha)
    return xf[None]
```

```python
import functools
import math

import numpy as np
import jax
import jax.numpy as jnp
from jax import lax
from jax.experimental import pallas as pl
from jax.experimental.pallas import tpu as pltpu

F32 = jnp.float32
BF16 = jnp.bfloat16

D_MODEL = 2048
GRID_W = 64
HEAD_DIM = 128
N_BRANCH = 4
BRANCH_W = D_MODEL // 4
DA_QK = HEAD_DIM // 2
NB_ROWS = 8
NB_COLS = 16
AXIAL_THETA = 10000.0
DIL_RATES = (1, 4, 16)
DIL_SIDE = 64
ROPE_THETA = 500000.0
LN_EPS = 1e-5
RMS_EPS = 1e-6
NEG_INF = -1e30
LOG2E = math.log2(math.e)

LANES = 128
VT_PAD = 16
TN = 512
D_IN = 18944

ROT_TILES = (0, 1, 8, 9, 11, 12, 14, 15)
PLAIN_TILES = (2, 3, 4, 5, 6, 7, 10, 17, 18, 20)
WIDE_TILES = (13, 16, 19)
GATE_TILE0, N_GATE_TILES = 21, 16
ROT_VARIANT = (0, 1, 2, 3, 4, 4, 5, 5)
ROT_VARIANT_DQ, ROT_VARIANT_DK = 4, 5
CB_AQ, CB_AK, CB_CQ, CB_CK, CB_CV, CB_DQ, CB_DK = 0, 4, 8, 12, 14, 16, 24
CB_AV, CB_AZ, CB_BQ, CB_BK, CB_BV, CB_BZ, CB_CZ, CB_DV, CB_DZ = 0, 4, 8, 12, 16, 20, 24, 28, 36
CB_WQ, CB_WK, CB_WV = 0, 4, 8
PLAIN_SILU = (1, 5, 6, 9)

VMEM_LIMIT = 56 * 1024 * 1024


def _cparams(sem):
    return pltpu.CompilerParams(dimension_semantics=sem, vmem_limit_bytes=VMEM_LIMIT)


def _select(j, values):
    out = values[-1]
    for t in range(len(values) - 2, -1, -1):
        out = jnp.where(j == t, values[t], out)
    return out


def _any_of(j, tiles):
    c = j == tiles[0]
    for t in tiles[1:]:
        c = c | (j == t)
    return c


def _ln_rows(x, g, b):
    mu = jnp.mean(x, axis=-1, keepdims=True)
    xc = x - mu
    var = jnp.mean(xc * xc, axis=-1, keepdims=True)
    return xc * lax.rsqrt(var + LN_EPS) * g + b


def _embed_ln_kernel(x_ref, g_ref, b_ref, y_ref, ybf_ref):
    y = _ln_rows(x_ref[...], g_ref[...], b_ref[...])
    y_ref[...] = y
    ybf_ref[...] = y.astype(BF16)


def _embed_ln(x, g, b, tm=512):
    s, d = x.shape
    return pl.pallas_call(
        _embed_ln_kernel,
        grid=(s // tm,),
        in_specs=[pl.BlockSpec((tm, d), lambda i: (i, 0)),
                  pl.BlockSpec((1, d), lambda i: (0, 0)),
                  pl.BlockSpec((1, d), lambda i: (0, 0))],
        out_specs=[pl.BlockSpec((tm, d), lambda i: (i, 0)),
                   pl.BlockSpec((tm, d), lambda i: (i, 0))],
        out_shape=[jax.ShapeDtypeStruct((s, d), F32), jax.ShapeDtypeStruct((s, d), BF16)],
        compiler_params=_cparams(("arbitrary",)),
        name="embed_ln",
    )(x, g.reshape(1, d), b.reshape(1, d))


def _rope_tables(seq):
    t = np.arange(seq)

    def cs(pos, dim, theta):
        half = dim // 2
        inv = np.power(float(theta), -np.arange(half, dtype=np.float64) * 2.0 / dim)
        ang = pos.astype(np.float64)[:, None] * inv[None, :]
        return jnp.asarray(np.cos(ang), F32), jnp.asarray(np.sin(ang), F32)

    def group(cos, sin, width):
        half = cos.shape[1]
        pad = width - 2 * half
        one = jnp.ones((seq, pad), F32)
        zero = jnp.zeros((seq, pad), F32)
        zh = jnp.zeros((seq, half), F32)
        return (jnp.concatenate([cos, cos, one], 1),
                jnp.concatenate([zh, sin, zero], 1),
                jnp.concatenate([-sin, zh, zero], 1))

    ca, sa = cs(t, DA_QK // 4, ROPE_THETA)
    va = jnp.stack([jnp.concatenate([p, p], 1) for p in group(ca, sa, DA_QK)])
    cr, sr = cs(t // GRID_W, HEAD_DIM // 2, AXIAL_THETA)
    cc, sc = cs(t % GRID_W, HEAD_DIM // 2, AXIAL_THETA)
    vc = jnp.stack([jnp.concatenate([p, q], 1)
                    for p, q in zip(group(cr, sr, HEAD_DIM // 2), group(cc, sc, HEAD_DIM // 2))])
    cd, sd = cs(t, HEAD_DIM // 4, ROPE_THETA)
    vd = jnp.stack(group(cd, sd, HEAD_DIM))
    qa = DA_QK ** -0.5 * LOG2E
    qh = HEAD_DIM ** -0.5 * LOG2E
    return jnp.stack([va * qa, va, vc * qh, vc, vd * qh, vd])


ROT_HALF_A, ROT_HALF_C, ROT_HALF_D = DA_QK // 8, HEAD_DIM // 4, HEAD_DIM // 8


def _proj_kernel(tsub, n_sub, make_branches, n_extra, x_ref, w_ref, *rest):
    extra = rest[:n_extra]
    o_ref, wbf_ref, acc_ref = rest[n_extra:]
    j = pl.program_id(0)
    n_pass = x_ref.shape[0] // (n_sub * tsub)

    @pl.when(pl.program_id(1) == 0)
    def _():
        wbf_ref[...] = w_ref[...].astype(BF16)

    def pipeline(epilogue):
        def one_pass(h, carry):
            def rows_of(u):
                return pl.ds(pl.multiple_of((h * n_sub + u) * tsub, tsub), tsub)

            def matmul(u):
                acc_ref[u % 2] = jnp.dot(x_ref[rows_of(u), :], wbf_ref[...],
                                         preferred_element_type=F32)

            matmul(0)
            for u in range(n_sub):
                if u + 1 < n_sub:
                    matmul(u + 1)
                for g in range(TN // LANES):
                    lanes = slice(g * LANES, (g + 1) * LANES)
                    y = epilogue(acc_ref[u % 2, :, lanes], rows_of(u), g).astype(o_ref.dtype)
                    if len(o_ref.shape) == 3:
                        o_ref[g, rows_of(u), :] = y
                    else:
                        o_ref[rows_of(u), lanes] = y
            return carry

        lax.fori_loop(0, n_pass, one_pass, 0)

    for cond, epilogue in make_branches(*extra):
        if cond is None:
            pipeline(epilogue)
        else:
            pl.when(cond(j))(functools.partial(pipeline, epilogue))


def _proj(name, xbf, w_in, layer, src_tiles, make_branches, extra, extra_specs, head_major,
          tm=2048, tsub=256, n_sub=4, dtype=BF16):
    s, d = xbf.shape
    tm = math.gcd(tm, s)
    n_sub = min(n_sub, tm // tsub)
    assert tm % (tsub * n_sub) == 0
    n_tiles = len(src_tiles)
    contiguous = src_tiles == tuple(range(src_tiles[0], src_tiles[0] + n_tiles))
    src = (lambda j: src_tiles[0] + j) if contiguous else (lambda j: _select(j, src_tiles))
    if head_major:
        out_spec = pl.BlockSpec((TN // LANES, tm, LANES), lambda j, i: (j, i, 0))
        out_shape = jax.ShapeDtypeStruct((n_tiles * (TN // LANES), s, LANES), dtype)
    else:
        out_spec = pl.BlockSpec((tm, TN), lambda j, i: (i, j))
        out_shape = jax.ShapeDtypeStruct((s, n_tiles * TN), dtype)
    return pl.pallas_call(
        functools.partial(_proj_kernel, tsub, n_sub, make_branches, len(extra)),
        grid=(n_tiles, s // tm),
        in_specs=[pl.BlockSpec((tm, d), lambda j, i: (i, 0)),
                  pl.BlockSpec((None, d, TN), lambda j, i: (layer, 0, src(j)))] + extra_specs(tm),
        out_specs=out_spec,
        out_shape=out_shape,
        scratch_shapes=[pltpu.VMEM((d, TN), BF16), pltpu.VMEM((2, tsub, TN), F32)],
        compiler_params=_cparams(("arbitrary", "arbitrary")),
        name=name,
    )(xbf, w_in, *extra)


def _rot_branches(tab_ref, gn_ref):
    def rot(a, rows, half):
        return (a * tab_ref[0, rows, :] + pltpu.roll(a, half, 1) * tab_ref[1, rows, :]
                + pltpu.roll(a, LANES - half, 1) * tab_ref[2, rows, :])

    def rms(a, gain):
        ms = jnp.mean(a * a, axis=-1, keepdims=True)
        return a * lax.rsqrt(ms + RMS_EPS) * gain

    return [(lambda j: j <= 1, lambda a, rows, g: rot(a, rows, ROT_HALF_A)),
            (lambda j: j == 2, lambda a, rows, g: rot(rms(a, gn_ref[0]), rows, ROT_HALF_C)),
            (lambda j: j == 3, lambda a, rows, g: rot(rms(a, gn_ref[1]), rows, ROT_HALF_C) if g < 2 else a),
            (lambda j: j >= 4, lambda a, rows, g: rot(a, rows, ROT_HALF_D))]


def _plain_branches():
    return [(lambda j: _any_of(j, PLAIN_SILU), lambda a, rows, g: a * jax.nn.sigmoid(a)),
            (lambda j: jnp.logical_not(_any_of(j, PLAIN_SILU)), lambda a, rows, g: a)]


def _gate_branches(bg_ref):
    return [(None, lambda a, rows, g: jax.nn.sigmoid(a + bg_ref[:, g * LANES:(g + 1) * LANES]))]


def _wide_branches(tab_ref):
    def rot(a, rows):
        return (a * tab_ref[0, rows, :] + pltpu.roll(a, ROT_HALF_D, 1) * tab_ref[1, rows, :]
                + pltpu.roll(a, LANES - ROT_HALF_D, 1) * tab_ref[2, rows, :])

    return [(lambda j: j <= 1, lambda a, rows, g: rot(a, rows)),
            (lambda j: j == 2, lambda a, rows, g: a)]


def _inproj(xbf, w_in, b_gate, layer, tabs, gains):
    hr = _proj("inproj_rot", xbf, w_in, layer, ROT_TILES, _rot_branches, (tabs, gains),
               lambda tm: [pl.BlockSpec((None, 3, tm, LANES), lambda j, i: (_select(j, ROT_VARIANT), 0, i, 0)),
                           pl.BlockSpec((None, 2, 1, LANES), lambda j, i: (layer, 0, 0, 0))], True)
    hp = _proj("inproj_plain", xbf, w_in, layer, PLAIN_TILES, _plain_branches, (), lambda tm: [], True,
               n_sub=8)
    hg = _proj("inproj_gate", xbf, w_in, layer, tuple(range(GATE_TILE0, GATE_TILE0 + N_GATE_TILES)),
               _gate_branches, (b_gate,),
               lambda tm: [pl.BlockSpec((None, 1, TN), lambda j, i: (layer, 0, j))], False, n_sub=8)
    hw = _proj("inproj_wide", xbf, w_in, layer, WIDE_TILES, _wide_branches, (tabs,),
               lambda tm: [pl.BlockSpec((None, 3, tm, LANES),
                                        lambda j, i: (jnp.where(j == 0, ROT_VARIANT_DQ, ROT_VARIANT_DK), 0, i, 0))],
               True, n_sub=8, dtype=F32)
    return hr, hp, hg, hw


def _pair_attn_kernel(diff, tq, tk, lam_init, *refs):
    if diff:
        q_ref, k_ref, v_ref, z_ref, lp_ref, sg_ref, o_ref, q2_ref, vt_ref, acc_ref, s_ref = refs
    else:
        q_ref, k_ref, v_ref, z_ref, o_ref, q2_ref, vt_ref, acc_ref, s_ref = refs
    seq = k_ref.shape[0]
    nq = 2 * tq

    @pl.when(pl.program_id(1) == 0)
    def _():
        for c in range(seq // tk):
            vt_ref[c, :LANES, :] = v_ref[c * tk:(c + 1) * tk, :].astype(F32).T.astype(BF16)
            vt_ref[c, LANES:, :] = jnp.ones((VT_PAD, tk), BF16)

    if diff:
        q = q_ref[...]
        lane = lax.broadcasted_iota(jnp.int32, q.shape, 1)
        zero = jnp.zeros_like(q)
        q2_ref[:tq, :] = jnp.where(lane < DA_QK, q, zero)
        q2_ref[tq:, :] = jnp.where(lane >= DA_QK, q, zero)
    else:
        q2_ref[:tq, :] = q_ref[0]
        q2_ref[tq:, :] = q_ref[1]
    acc_ref[...] = jnp.zeros(acc_ref.shape, F32)

    n_chunks = seq // tk

    def scores(c, slot):
        start = pl.multiple_of(c * tk, tk)
        s = lax.dot_general(k_ref[pl.ds(start, tk), :], q2_ref[...], (((1,), (1,)), ((), ())),
                            preferred_element_type=F32)
        s_ref[slot] = s
        return jnp.max(s, axis=0, keepdims=True)

    def fold(c, slot, m_prev, m_chunk):
        m_new = jnp.maximum(m_prev, m_chunk)
        alpha = jnp.exp2(m_prev - m_new)
        p = jnp.exp2(s_ref[slot] - m_new)
        acc_ref[...] = alpha * acc_ref[...] + jnp.dot(vt_ref[c], p.astype(BF16),
                                                      preferred_element_type=F32)
        return m_new

    unroll = 4 if n_chunks % 4 == 0 else 2

    def group(base, carry, last):
        m, m_chunk = carry
        for u in range(unroll):
            m_next = m_chunk if (last and u == unroll - 1) else scores(base + u + 1, (u + 1) % 2)
            m = fold(base + u, u % 2, m, m_chunk)
            m_chunk = m_next
        return m, m_chunk

    carry = (jnp.full((1, nq), NEG_INF, F32), scores(0, 0))
    n_groups = n_chunks // unroll
    if n_groups > 1:
        carry = lax.fori_loop(0, n_groups - 1, lambda j, cc: group(j * unroll, cc, False), carry)
    group((n_groups - 1) * unroll, carry, True)

    acc = acc_ref[...]
    o = (acc[:LANES] / acc[LANES:LANES + 1]).T
    if diff:
        lp = lp_ref[...]
        lam = (jnp.exp(jnp.sum(lp[0:1] * lp[1:2], keepdims=True))
               - jnp.exp(jnp.sum(lp[2:3] * lp[3:4], keepdims=True)) + lam_init)
        dlt = o[:tq] - lam * o[tq:]
        ms = jnp.mean(dlt * dlt, axis=-1, keepdims=True)
        y = dlt * lax.rsqrt(ms + RMS_EPS) * sg_ref[...] * (1.0 - lam_init)
        o_ref[...] = (y * z_ref[...].astype(F32)).astype(BF16)
    else:
        o_ref[:, :LANES] = (o[:tq] * z_ref[0].astype(F32)).astype(BF16)
        o_ref[:, LANES:] = (o[tq:] * z_ref[1].astype(F32)).astype(BF16)


def _pair_attn(hr, hp, diff, layer=0, lam_init=0.0, lam_params=None, subln_g=None, tq=512, tk=512):
    s = hr.shape[1]
    if diff:
        n_kv, qw, qblk = 4, LANES, (None, tq, LANES)
        q_map = lambda hh, i: (CB_AQ + hh, i, 0)
        k_map = lambda hh, i: (CB_AK + hh, 0, 0)
        v_map = lambda hh, i: (CB_AV + hh, 0, 0)
        z_map = lambda hh, i: (CB_AZ + hh, i, 0)
    else:
        n_kv, qw, qblk = 2, 2 * LANES, (2, tq, LANES)
        q_map = lambda hh, i: (CB_CQ // 2 + hh, i, 0)
        k_map = lambda hh, i: (CB_CK + hh, 0, 0)
        v_map = lambda hh, i: (CB_CV + hh, 0, 0)
        z_map = lambda hh, i: (CB_CZ // 2 + hh, i, 0)
    in_specs = [pl.BlockSpec(qblk, q_map),
                pl.BlockSpec((None, s, LANES), k_map),
                pl.BlockSpec((None, s, LANES), v_map),
                pl.BlockSpec(qblk, z_map)]
    args = [hr, hr, hp if diff else hr, hp]
    if diff:
        in_specs += [pl.BlockSpec((None, 4, DA_QK), lambda hh, i: (layer, 0, 0)),
                     pl.BlockSpec((None, 1, LANES), lambda hh, i: (layer, 0, 0))]
        args += [lam_params, subln_g]
    return pl.pallas_call(
        functools.partial(_pair_attn_kernel, diff, tq, tk, lam_init),
        grid=(n_kv, s // tq),
        in_specs=in_specs,
        out_specs=pl.BlockSpec((tq, qw), lambda hh, i: (i, hh)),
        out_shape=jax.ShapeDtypeStruct((s, BRANCH_W), BF16),
        scratch_shapes=[pltpu.VMEM((2 * tq, LANES), BF16),
                        pltpu.VMEM((s // tk, LANES + VT_PAD, tk), BF16),
                        pltpu.VMEM((LANES + VT_PAD, 2 * tq), F32),
                        pltpu.VMEM((2, tk, 2 * tq), F32)],
        compiler_params=_cparams(("arbitrary", "arbitrary")),
        name="diff_attn" if diff else "axial_gqa",
    )(*args)


def _nat_bias(rpb):
    n_layers, n_heads = rpb.shape[:2]
    cols = np.arange(GRID_W)
    c0 = np.clip(cols - NB_COLS // 2, 0, GRID_W - NB_COLS)
    valid = (cols[None, :] >= c0[:, None]) & (cols[None, :] < c0[:, None] + NB_COLS)
    col_sel = (np.arange(2 * NB_COLS - 1)[:, None, None]
               == cols[None, None, :] - cols[None, :, None] + (NB_COLS - 1)).astype(np.float32)
    row_sel = (np.arange(2 * NB_ROWS - 1)[None, None, :]
               == np.arange(NB_ROWS)[:, None, None] + np.arange(NB_ROWS)[None, :, None]).astype(np.float32)
    exact = lax.Precision.HIGHEST
    toep = jnp.einsum('lhdr,rck->lhdck', rpb.astype(F32), col_sel, precision=exact)
    t = jnp.where(valid, toep, NEG_INF)
    bias = jnp.einsum('lhdck,sid->lshcik', t, row_sel, precision=exact)
    return bias.reshape(n_layers, NB_ROWS, n_heads, GRID_W, NB_ROWS * GRID_W)


def _nat_kernel(rows, rps, q_ref, k_ref, v_ref, z_ref, bias_ref, o_ref):
    win = NB_ROWS * GRID_W
    starts, scores = [], []
    for u in range(rps):
        r = pl.program_id(1) * rps + u
        r0 = jnp.clip(r - NB_ROWS // 2, 0, rows - NB_ROWS)
        start = pl.multiple_of(r0 * GRID_W, GRID_W)
        kw = k_ref[pl.ds(start, win), :]
        s = lax.dot_general(q_ref[u * GRID_W:(u + 1) * GRID_W, :], kw, (((1,), (1,)), ((), ())),
                            preferred_element_type=F32)
        scores.append(s * (HEAD_DIM ** -0.5) + bias_ref[r0 - r + (NB_ROWS - 1)])
        starts.append(start)
    probs = []
    for s in scores:
        p = jnp.exp(s - jnp.max(s, axis=-1, keepdims=True))
        probs.append((p.astype(BF16), jnp.sum(p, axis=-1, keepdims=True)))
    for u, (p, l) in enumerate(probs):
        qrows = slice(u * GRID_W, (u + 1) * GRID_W)
        o = jnp.dot(p, v_ref[pl.ds(starts[u], win), :], preferred_element_type=F32) / l
        o_ref[qrows, :] = (o * z_ref[qrows, :].astype(F32)).astype(BF16)


def _nat_attn(hp, bias, layer, rps=64):
    s = hp.shape[1]
    rows = s // GRID_W
    blk = rps * GRID_W
    return pl.pallas_call(
        functools.partial(_nat_kernel, rows, rps),
        grid=(4, rows // rps),
        in_specs=[pl.BlockSpec((None, blk, LANES), lambda hh, r: (CB_BQ + hh, r, 0)),
                  pl.BlockSpec((None, s, LANES), lambda hh, r: (CB_BK + hh, 0, 0)),
                  pl.BlockSpec((None, s, LANES), lambda hh, r: (CB_BV + hh, 0, 0)),
                  pl.BlockSpec((None, blk, LANES), lambda hh, r: (CB_BZ + hh, r, 0)),
                  pl.BlockSpec((None, NB_ROWS, None, GRID_W, NB_ROWS * GRID_W),
                               lambda hh, r: (layer, 0, hh, 0, 0))],
        out_specs=pl.BlockSpec((blk, LANES), lambda hh, r: (r, hh)),
        out_shape=jax.ShapeDtypeStruct((s, BRANCH_W), BF16),
        compiler_params=_cparams(("arbitrary", "arbitrary")),
        name="nat_attn",
    )(hp, hp, hp, hp, bias)


def _dil_masks(tq, rates):
    row = np.arange(tq)[:, None]

    def band(col, rate):
        d = col - row
        return np.where((d >= 0) & (d <= 2 * DIL_SIDE * rate) & (d % rate == 0), 0.0, NEG_INF).astype(np.float32)

    width0 = tq + 2 * DIL_SIDE
    tables = [np.stack([band(np.arange(width0)[None, :] + dlt, 1) for dlt in (DIL_SIDE, 0, -DIL_SIDE)])]
    assert rates[0] == 1
    for rate in rates[1:]:
        reach = DIL_SIDE * rate
        cols = np.arange(-reach, tq + 3 * reach)[None, :]
        full = band(cols, rate)
        tables.append(full.reshape(tq, -1, LANES).transpose(1, 0, 2))
    return [jnp.asarray(t) for t in tables]


def _dil_class_kernel(tq, rate, q_ref, k_ref, v_ref, b_ref, o_ref, lse_ref):
    length = q_ref.shape[0] // rate
    width = tq + 2 * DIL_SIDE
    for a in range(rate):
        cls = pl.ds(a, length, stride=rate)
        qa = q_ref[cls, :].astype(BF16)
        ka = k_ref[cls, :].astype(BF16)
        va = v_ref[cls, :].astype(BF16)
        scored = []
        for i0 in range(0, length, tq):
            ws = min(max(i0 - DIL_SIDE, 0), length - width)
            delta = ws - i0 + DIL_SIDE
            s = lax.dot_general(qa[i0:i0 + tq], ka[ws:ws + width], (((1,), (1,)), ((), ())),
                                preferred_element_type=F32)
            scored.append((i0, ws, s + b_ref[0 if delta > 0 else (2 if delta < 0 else 1)]))
        stats = []
        for i0, ws, s in scored:
            m = jnp.max(s, axis=-1, keepdims=True)
            p = jnp.exp2(s - m)
            stats.append((i0, ws, p.astype(BF16), m, jnp.sum(p, axis=-1, keepdims=True)))
        for i0, ws, p, m, l in stats:
            out_rows = pl.ds(a + rate * i0, tq, stride=rate)
            o_ref[out_rows, :] = jnp.dot(p, va[ws:ws + width], preferred_element_type=F32) / l
            lse_ref[out_rows, :] = jnp.broadcast_to(m + jnp.log2(l), (tq, LANES))


def _dil_kernel(tq, nsub, q0_ref, q1_ref, k0_ref, k1_ref, v0_ref, v1_ref, z_ref, b0_ref, b1_ref,
                o2_ref, lse2_ref, o_ref):
    seq = k0_ref.shape[0]
    q_refs, k_refs, v_refs, b_refs = (q0_ref, q1_ref), (k0_ref, k1_ref), (v0_ref, v1_ref), (b0_ref, b1_ref)
    scores, wins = [], []
    for u in range(nsub):
        t0 = (pl.program_id(1) * nsub + u) * tq
        rows = slice(u * tq, (u + 1) * tq)
        su, wu = [], []
        for rate, q_ref, k_ref, b_ref in zip(DIL_RATES, q_refs, k_refs, b_refs):
            reach = DIL_SIDE * rate
            width = tq + 2 * reach
            start = pl.multiple_of(jnp.clip(t0 - reach, 0, seq - width), DIL_SIDE)
            delta = start - t0 + reach
            s = lax.dot_general(q_ref[rows, :], k_ref[pl.ds(start, width), :], (((1,), (1,)), ((), ())),
                                preferred_element_type=F32)
            if rate == 1:
                s = s + b_ref[jnp.where(delta > 0, 0, jnp.where(delta < 0, 2, 1))]
            else:
                tile0 = (delta + reach) // LANES
                s = jnp.concatenate([s[:, j * LANES:(j + 1) * LANES] + b_ref[tile0 + j]
                                     for j in range(width // LANES)], axis=1)
            su.append(s)
            wu.append((start, width))
        scores.append(su)
        wins.append(wu)
    probs, dens, tops = [], [], []
    for su in scores:
        m = su[0].max(axis=-1, keepdims=True)
        for s in su[1:]:
            m = jnp.maximum(m, s.max(axis=-1, keepdims=True))
        pu = [jnp.exp2(s - m) for s in su]
        l = pu[0].sum(axis=-1, keepdims=True)
        for p in pu[1:]:
            l = l + p.sum(axis=-1, keepdims=True)
        probs.append([p.astype(BF16) for p in pu])
        dens.append(l)
        tops.append(m)
    for u, (pu, wu, l, m) in enumerate(zip(probs, wins, dens, tops)):
        rows = slice(u * tq, (u + 1) * tq)
        o = None
        for p, (start, width), v_ref in zip(pu, wu, v_refs):
            pv = jnp.dot(p, v_ref[pl.ds(start, width), :], preferred_element_type=F32)
            o = pv if o is None else o + pv
        lse2 = lse2_ref[rows, :1]
        top = jnp.maximum(m, lse2)
        w01 = jnp.exp2(m - top)
        w2 = jnp.exp2(lse2 - top)
        mixed = (w01 * o + w2 * o2_ref[rows, :]) / (w01 * l + w2)
        o_ref[rows, :] = (mixed * z_ref[rows, :].astype(F32)).astype(BF16)


def _dil_class_attn(hw, mask, tq=128):
    s = hw.shape[1]
    head = lambda cb: pl.BlockSpec((None, s, LANES), lambda hh: (cb + hh, 0, 0))
    return pl.pallas_call(
        functools.partial(_dil_class_kernel, tq, DIL_RATES[-1]),
        grid=(4,),
        in_specs=[head(CB_WQ), head(CB_WK), head(CB_WV), pl.BlockSpec(mask.shape, lambda hh: (0, 0, 0))],
        out_specs=[head(0), head(0)],
        out_shape=[jax.ShapeDtypeStruct((4, s, LANES), F32), jax.ShapeDtypeStruct((4, s, LANES), F32)],
        compiler_params=_cparams(("arbitrary",)),
        name="dil_class_attn",
    )(hw, hw, hw, mask)


def _dil_attn(hr, hp, hw, tq=128, nsub=32):
    s = hr.shape[1]
    n_dense = len(DIL_RATES) - 1
    masks = _dil_masks(tq, DIL_RATES[:n_dense])
    o2, lse2 = _dil_class_attn(hw, masks[0], tq)
    m_specs = [pl.BlockSpec(m.shape, lambda hh, i: (0, 0, 0)) for m in masks]
    blk = tq * nsub
    qs = [pl.BlockSpec((None, blk, LANES), functools.partial(lambda g, hh, i: (CB_DQ + 4 * g + hh, i, 0), g))
          for g in range(n_dense)]
    ks = [pl.BlockSpec((None, s, LANES), functools.partial(lambda g, hh, i: (CB_DK + 4 * g + hh, 0, 0), g))
          for g in range(n_dense)]
    vs = [pl.BlockSpec((None, s, LANES), functools.partial(lambda g, hh, i: (CB_DV + 4 * g + hh, 0, 0), g))
          for g in range(n_dense)]
    row_blk = pl.BlockSpec((None, blk, LANES), lambda hh, i: (hh, i, 0))
    return pl.pallas_call(
        functools.partial(_dil_kernel, tq, nsub),
        grid=(4, s // blk),
        in_specs=(qs + ks + vs + [pl.BlockSpec((None, blk, LANES), lambda hh, i: (CB_DZ + hh, i, 0))]
                  + m_specs + [row_blk, row_blk]),
        out_specs=pl.BlockSpec((blk, LANES), lambda hh, i: (i, hh)),
        out_shape=jax.ShapeDtypeStruct((s, BRANCH_W), BF16),
        compiler_params=_cparams(("arbitrary", "arbitrary")),
        name="dil_attn",
    )(*([hr] * (2 * n_dense)), *([hp] * (n_dense + 1)), *masks, o2, lse2)


def _merge_kernel(ya_ref, yb_ref, yc_ref, yd_ref, ga_ref, gb_ref, gc_ref, gd_ref, wb_ref, o_ref, wbf_ref):
    @pl.when(pl.program_id(1) == 0)
    def _():
        wbf_ref[...] = wb_ref[...].astype(BF16)

    tsub = min(256, o_ref.shape[0])
    for r0 in range(0, o_ref.shape[0], tsub):
        rows = slice(r0, r0 + tsub)
        acc = None
        for n, (y_ref, g_ref) in enumerate(zip((ya_ref, yb_ref, yc_ref, yd_ref),
                                               (ga_ref, gb_ref, gc_ref, gd_ref))):
            proj = jnp.dot(y_ref[rows, :], wbf_ref[n], preferred_element_type=F32)
            term = proj * g_ref[rows, :].astype(F32)
            acc = term if acc is None else acc + term
        o_ref[rows, :] = acc.astype(BF16)


def _merge(ys, hg, w_branch, layer, tm=2048):
    s = hg.shape[0]
    tm = math.gcd(tm, s)
    n_col = D_MODEL // TN
    y_specs = [pl.BlockSpec((tm, BRANCH_W), lambda j, i: (i, 0)) for _ in range(N_BRANCH)]
    g_specs = [pl.BlockSpec((tm, TN), functools.partial(lambda n, j, i: (i, n_col * n + j), n))
               for n in range(N_BRANCH)]
    return pl.pallas_call(
        _merge_kernel,
        grid=(n_col, s // tm),
        in_specs=y_specs + g_specs + [pl.BlockSpec((None, N_BRANCH, BRANCH_W, TN),
                                                   lambda j, i: (layer, 0, 0, j))],
        out_specs=pl.BlockSpec((tm, TN), lambda j, i: (i, j)),
        out_shape=jax.ShapeDtypeStruct((s, D_MODEL), BF16),
        scratch_shapes=[pltpu.VMEM((N_BRANCH, BRANCH_W, TN), BF16)],
        compiler_params=_cparams(("arbitrary", "arbitrary")),
        name="branch_merge",
    )(*ys, hg, hg, hg, hg, w_branch)


def _out_ln_kernel(alpha, tsub, m_ref, w_ref, x_ref, g_ref, b_ref, y_ref, ybf_ref, acc_ref):
    n_sub = m_ref.shape[0] // tsub

    def matmul(u):
        acc_ref[u % 2] = jnp.dot(m_ref[u * tsub:(u + 1) * tsub, :], w_ref[...], preferred_element_type=F32)

    matmul(0)
    for u in range(n_sub):
        if u + 1 < n_sub:
            matmul(u + 1)
        rows = slice(u * tsub, (u + 1) * tsub)
        y = _ln_rows(alpha * x_ref[rows, :] + acc_ref[u % 2], g_ref[...], b_ref[...])
        y_ref[rows, :] = y
        ybf_ref[rows, :] = y.astype(BF16)


def _out_ln(merged, wo_bf, layer, x, g, b, alpha, tm=512, tsub=128):
    s, d = x.shape
    return pl.pallas_call(
        functools.partial(_out_ln_kernel, alpha, tsub),
        grid=(s // tm,),
        in_specs=[pl.BlockSpec((tm, d), lambda i: (i, 0)),
                  pl.BlockSpec((None, d, d), lambda i: (layer, 0, 0)),
                  pl.BlockSpec((tm, d), lambda i: (i, 0)),
                  pl.BlockSpec((None, 1, d), lambda i: (layer, 0, 0)),
                  pl.BlockSpec((None, 1, d), lambda i: (layer, 0, 0))],
        out_specs=[pl.BlockSpec((tm, d), lambda i: (i, 0)),
                   pl.BlockSpec((tm, d), lambda i: (i, 0))],
        out_shape=[jax.ShapeDtypeStruct((s, d), F32), jax.ShapeDtypeStruct((s, d), BF16)],
        scratch_shapes=[pltpu.VMEM((2, tsub, d), F32)],
        compiler_params=_cparams(("arbitrary",)),
        name="out_ln",
    )(merged, wo_bf, x, g, b)


def kernel(x, emb_ln_g, emb_ln_b, w_in, b_gate, diff_lambda, diff_subln_g, nat_rpb,
           gqa_q_norm_g, gqa_k_norm_g, w_branch, w_out, ln_g, ln_b):
    batch, seq, d = x.shape
    assert batch == 1 and d == D_MODEL and w_in.shape[-1] == D_IN
    depth = w_in.shape[0]
    alpha = (2 * depth) ** 0.25
    tabs = _rope_tables(seq)
    nat_bias = _nat_bias(nat_rpb)
    wo_bf = w_out.astype(BF16)
    gains = jnp.stack([gqa_q_norm_g, gqa_k_norm_g], axis=1).reshape(depth, 2, 1, LANES)
    b_gate3 = b_gate.reshape(depth, 1, -1)
    subln3 = diff_subln_g.reshape(depth, 1, LANES)
    ln_g3, ln_b3 = ln_g.reshape(depth, 1, d), ln_b.reshape(depth, 1, d)
    xf, xbf = _embed_ln(x[0], emb_ln_g, emb_ln_b)
    for l in range(depth):
        lam_init = 0.8 - 0.6 * math.exp(-0.3 * l)
        hr, hp, hg, hw = _inproj(xbf, w_in, b_gate3, l, tabs, gains)
        ya = _pair_attn(hr, hp, True, l, lam_init, diff_lambda, subln3)
        yb = _nat_attn(hp, nat_bias, l)
        yc = _pair_attn(hr, hp, False)
        yd = _dil_attn(hr, hp, hw)
        merged = _merge((ya, yb, yc, yd), hg, w_branch, l)
        xf, xbf = _out_ln(merged, wo_bf, l, xf, ln_g3, ln_b3, alpha)
    return xf[None]
```

```python
import functools
import math

import numpy as np
import jax
import jax.numpy as jnp
from jax import lax
from jax.experimental import pallas as pl
from jax.experimental.pallas import tpu as pltpu

F32 = jnp.float32
BF16 = jnp.bfloat16

D_MODEL = 2048
GRID_W = 64
HEAD_DIM = 128
N_BRANCH = 4
BRANCH_W = D_MODEL // 4
DA_QK = HEAD_DIM // 2
NB_ROWS = 8
NB_COLS = 16
AXIAL_THETA = 10000.0
DIL_RATES = (1, 4, 16)
DIL_SIDE = 64
ROPE_THETA = 500000.0
LN_EPS = 1e-5
RMS_EPS = 1e-6
NEG_INF = -1e30
LOG2E = math.log2(math.e)

LANES = 128
VT_PAD = 16
TN = 512
D_IN = 18944

ROT_TILES = (0, 1, 8, 9, 11, 12, 14, 15)
PLAIN_TILES = (2, 3, 4, 5, 6, 7, 10, 17, 18, 20)
WIDE_TILES = (13, 16, 19)
GATE_TILE0, N_GATE_TILES = 21, 16
ROT_VARIANT = (0, 1, 2, 3, 4, 4, 5, 5)
ROT_VARIANT_DQ, ROT_VARIANT_DK = 4, 5
CB_AQ, CB_AK, CB_CQ, CB_CK, CB_CV, CB_DQ, CB_DK = 0, 4, 8, 12, 14, 16, 24
CB_AV, CB_AZ, CB_BQ, CB_BK, CB_BV, CB_BZ, CB_CZ, CB_DV, CB_DZ = 0, 4, 8, 12, 16, 20, 24, 28, 36
CB_WQ, CB_WK, CB_WV = 0, 4, 8
PLAIN_SILU = (1, 5, 6, 9)

VMEM_LIMIT = 56 * 1024 * 1024


def _cparams(sem):
    return pltpu.CompilerParams(dimension_semantics=sem, vmem_limit_bytes=VMEM_LIMIT)


def _select(j, values):
    out = values[-1]
    for t in range(len(values) - 2, -1, -1):
        out = jnp.where(j == t, values[t], out)
    return out


def _any_of(j, tiles):
    c = j == tiles[0]
    for t in tiles[1:]:
        c = c | (j == t)
    return c


def _ln_rows(x, g, b):
    mu = jnp.mean(x, axis=-1, keepdims=True)
    xc = x - mu
    var = jnp.mean(xc * xc, axis=-1, keepdims=True)
    return xc * lax.rsqrt(var + LN_EPS) * g + b


def _embed_ln_kernel(x_ref, g_ref, b_ref, y_ref, ybf_ref):
    y = _ln_rows(x_ref[...], g_ref[...], b_ref[...])
    y_ref[...] = y
    ybf_ref[...] = y.astype(BF16)


def _embed_ln(x, g, b, tm=512):
    s, d = x.shape
    return pl.pallas_call(
        _embed_ln_kernel,
        grid=(s // tm,),
        in_specs=[pl.BlockSpec((tm, d), lambda i: (i, 0)),
                  pl.BlockSpec((1, d), lambda i: (0, 0)),
                  pl.BlockSpec((1, d), lambda i: (0, 0))],
        out_specs=[pl.BlockSpec((tm, d), lambda i: (i, 0)),
                   pl.BlockSpec((tm, d), lambda i: (i, 0))],
        out_shape=[jax.ShapeDtypeStruct((s, d), F32), jax.ShapeDtypeStruct((s, d), BF16)],
        compiler_params=_cparams(("arbitrary",)),
        name="embed_ln",
    )(x, g.reshape(1, d), b.reshape(1, d))


def _rope_tables(seq):
    t = np.arange(seq)

    def cs(pos, dim, theta):
        half = dim // 2
        inv = np.power(float(theta), -np.arange(half, dtype=np.float64) * 2.0 / dim)
        ang = pos.astype(np.float64)[:, None] * inv[None, :]
        return jnp.asarray(np.cos(ang), F32), jnp.asarray(np.sin(ang), F32)

    def group(cos, sin, width):
        half = cos.shape[1]
        pad = width - 2 * half
        one = jnp.ones((seq, pad), F32)
        zero = jnp.zeros((seq, pad), F32)
        zh = jnp.zeros((seq, half), F32)
        return (jnp.concatenate([cos, cos, one], 1),
                jnp.concatenate([zh, sin, zero], 1),
                jnp.concatenate([-sin, zh, zero], 1))

    ca, sa = cs(t, DA_QK // 4, ROPE_THETA)
    va = jnp.stack([jnp.concatenate([p, p], 1) for p in group(ca, sa, DA_QK)])
    cr, sr = cs(t // GRID_W, HEAD_DIM // 2, AXIAL_THETA)
    cc, sc = cs(t % GRID_W, HEAD_DIM // 2, AXIAL_THETA)
    vc = jnp.stack([jnp.concatenate([p, q], 1)
                    for p, q in zip(group(cr, sr, HEAD_DIM // 2), group(cc, sc, HEAD_DIM // 2))])
    cd, sd = cs(t, HEAD_DIM // 4, ROPE_THETA)
    vd = jnp.stack(group(cd, sd, HEAD_DIM))
    qa = DA_QK ** -0.5 * LOG2E
    qh = HEAD_DIM ** -0.5 * LOG2E
    return jnp.stack([va * qa, va, vc * qh, vc, vd * qh, vd])


ROT_HALF_A, ROT_HALF_C, ROT_HALF_D = DA_QK // 8, HEAD_DIM // 4, HEAD_DIM // 8


def _proj_kernel(tsub, n_sub, make_branches, n_extra, x_ref, w_ref, *rest):
    extra = rest[:n_extra]
    o_ref, wbf_ref, acc_ref = rest[n_extra:]
    j = pl.program_id(0)
    n_pass = x_ref.shape[0] // (n_sub * tsub)

    @pl.when(pl.program_id(1) == 0)
    def _():
        wbf_ref[...] = w_ref[...].astype(BF16)

    def pipeline(epilogue):
        def one_pass(h, carry):
            def rows_of(u):
                return pl.ds(pl.multiple_of((h * n_sub + u) * tsub, tsub), tsub)

            def matmul(u):
                acc_ref[u % 2] = jnp.dot(x_ref[rows_of(u), :], wbf_ref[...],
                                         preferred_element_type=F32)

            matmul(0)
            for u in range(n_sub):
                if u + 1 < n_sub:
                    matmul(u + 1)
                for g in range(TN // LANES):
                    lanes = slice(g * LANES, (g + 1) * LANES)
                    y = epilogue(acc_ref[u % 2, :, lanes], rows_of(u), g).astype(o_ref.dtype)
                    if len(o_ref.shape) == 3:
                        o_ref[g, rows_of(u), :] = y
                    else:
                        o_ref[rows_of(u), lanes] = y
            return carry

        lax.fori_loop(0, n_pass, one_pass, 0)

    for cond, epilogue in make_branches(*extra):
        if cond is None:
            pipeline(epilogue)
        else:
            pl.when(cond(j))(functools.partial(pipeline, epilogue))


def _proj(name, xbf, w_in, layer, src_tiles, make_branches, extra, extra_specs, head_major,
          tm=2048, tsub=256, n_sub=4, dtype=BF16):
    s, d = xbf.shape
    tm = math.gcd(tm, s)
    n_sub = min(n_sub, tm // tsub)
    assert tm % (tsub * n_sub) == 0
    n_tiles = len(src_tiles)
    contiguous = src_tiles == tuple(range(src_tiles[0], src_tiles[0] + n_tiles))
    src = (lambda j: src_tiles[0] + j) if contiguous else (lambda j: _select(j, src_tiles))
    if head_major:
        out_spec = pl.BlockSpec((TN // LANES, tm, LANES), lambda j, i: (j, i, 0))
        out_shape = jax.ShapeDtypeStruct((n_tiles * (TN // LANES), s, LANES), dtype)
    else:
        out_spec = pl.BlockSpec((tm, TN), lambda j, i: (i, j))
        out_shape = jax.ShapeDtypeStruct((s, n_tiles * TN), dtype)
    return pl.pallas_call(
        functools.partial(_proj_kernel, tsub, n_sub, make_branches, len(extra)),
        grid=(n_tiles, s // tm),
        in_specs=[pl.BlockSpec((tm, d), lambda j, i: (i, 0)),
                  pl.BlockSpec((None, d, TN), lambda j, i: (layer, 0, src(j)))] + extra_specs(tm),
        out_specs=out_spec,
        out_shape=out_shape,
        scratch_shapes=[pltpu.VMEM((d, TN), BF16), pltpu.VMEM((2, tsub, TN), F32)],
        compiler_params=_cparams(("arbitrary", "arbitrary")),
        name=name,
    )(xbf, w_in, *extra)


def _rot_branches(tab_ref, gn_ref):
    def rot(a, rows, half):
        return (a * tab_ref[0, rows, :] + pltpu.roll(a, half, 1) * tab_ref[1, rows, :]
                + pltpu.roll(a, LANES - half, 1) * tab_ref[2, rows, :])

    def rms(a, gain):
        ms = jnp.mean(a * a, axis=-1, keepdims=True)
        return a * lax.rsqrt(ms + RMS_EPS) * gain

    return [(lambda j: j <= 1, lambda a, rows, g: rot(a, rows, ROT_HALF_A)),
            (lambda j: j == 2, lambda a, rows, g: rot(rms(a, gn_ref[0]), rows, ROT_HALF_C)),
            (lambda j: j == 3, lambda a, rows, g: rot(rms(a, gn_ref[1]), rows, ROT_HALF_C) if g < 2 else a),
            (lambda j: j >= 4, lambda a, rows, g: rot(a, rows, ROT_HALF_D))]


def _plain_branches():
    return [(lambda j: _any_of(j, PLAIN_SILU), lambda a, rows, g: a * jax.nn.sigmoid(a)),
            (lambda j: jnp.logical_not(_any_of(j, PLAIN_SILU)), lambda a, rows, g: a)]


def _gate_branches(bg_ref):
    return [(None, lambda a, rows, g: jax.nn.sigmoid(a + bg_ref[:, g * LANES:(g + 1) * LANES]))]


def _wide_branches(tab_ref):
    def rot(a, rows):
        return (a * tab_ref[0, rows, :] + pltpu.roll(a, ROT_HALF_D, 1) * tab_ref[1, rows, :]
                + pltpu.roll(a, LANES - ROT_HALF_D, 1) * tab_ref[2, rows, :])

    return [(lambda j: j <= 1, lambda a, rows, g: rot(a, rows)),
            (lambda j: j == 2, lambda a, rows, g: a)]


def _inproj(xbf, w_in, b_gate, layer, tabs, gains):
    hr = _proj("inproj_rot", xbf, w_in, layer, ROT_TILES, _rot_branches, (tabs, gains),
               lambda tm: [pl.BlockSpec((None, 3, tm, LANES), lambda j, i: (_select(j, ROT_VARIANT), 0, i, 0)),
                           pl.BlockSpec((None, 2, 1, LANES), lambda j, i: (layer, 0, 0, 0))], True)
    hp = _proj("inproj_plain", xbf, w_in, layer, PLAIN_TILES, _plain_branches, (), lambda tm: [], True,
               n_sub=8)
    hg = _proj("inproj_gate", xbf, w_in, layer, tuple(range(GATE_TILE0, GATE_TILE0 + N_GATE_TILES)),
               _gate_branches, (b_gate,),
               lambda tm: [pl.BlockSpec((None, 1, TN), lambda j, i: (layer, 0, j))], False, n_sub=8)
    hw = _proj("inproj_wide", xbf, w_in, layer, WIDE_TILES, _wide_branches, (tabs,),
               lambda tm: [pl.BlockSpec((None, 3, tm, LANES),
                                        lambda j, i: (jnp.where(j == 0, ROT_VARIANT_DQ, ROT_VARIANT_DK), 0, i, 0))],
               True, n_sub=8, dtype=F32)
    return hr, hp, hg, hw


def _pair_attn_kernel(diff, tq, tk, lam_init, *refs):
    if diff:
        q_ref, k_ref, v_ref, z_ref, lp_ref, sg_ref, o_ref, q2_ref, vt_ref, acc_ref, s_ref = refs
    else:
        q_ref, k_ref, v_ref, z_ref, o_ref, q2_ref, vt_ref, acc_ref, s_ref = refs
    seq = k_ref.shape[0]
    nq = 2 * tq

    @pl.when(pl.program_id(1) == 0)
    def _():
        for c in range(seq // tk):
            vt_ref[c, :LANES, :] = v_ref[c * tk:(c + 1) * tk, :].astype(F32).T.astype(BF16)
            vt_ref[c, LANES:, :] = jnp.ones((VT_PAD, tk), BF16)

    if diff:
        q = q_ref[...]
        lane = lax.broadcasted_iota(jnp.int32, q.shape, 1)
        zero = jnp.zeros_like(q)
        q2_ref[:tq, :] = jnp.where(lane < DA_QK, q, zero)
        q2_ref[tq:, :] = jnp.where(lane >= DA_QK, q, zero)
    else:
        q2_ref[:tq, :] = q_ref[0]
        q2_ref[tq:, :] = q_ref[1]
    acc_ref[...] = jnp.zeros(acc_ref.shape, F32)

    n_chunks = seq // tk

    def scores(c, slot):
        start = pl.multiple_of(c * tk, tk)
        s = lax.dot_general(k_ref[pl.ds(start, tk), :], q2_ref[...], (((1,), (1,)), ((), ())),
                            preferred_element_type=F32)
        s_ref[slot] = s
        return jnp.max(s, axis=0, keepdims=True)

    def fold(c, slot, m_prev, m_chunk):
        m_new = jnp.maximum(m_prev, m_chunk)
        alpha = jnp.exp2(m_prev - m_new)
        p = jnp.exp2(s_ref[slot] - m_new)
        acc_ref[...] = alpha * acc_ref[...] + jnp.dot(vt_ref[c], p.astype(BF16),
                                                      preferred_element_type=F32)
        return m_new

    unroll = 4 if n_chunks % 4 == 0 else 2

    def group(base, carry, last):
        m, m_chunk = carry
        for u in range(unroll):
            m_next = m_chunk if (last and u == unroll - 1) else scores(base + u + 1, (u + 1) % 2)
            m = fold(base + u, u % 2, m, m_chunk)
            m_chunk = m_next
        return m, m_chunk

    carry = (jnp.full((1, nq), NEG_INF, F32), scores(0, 0))
    n_groups = n_chunks // unroll
    if n_groups > 1:
        carry = lax.fori_loop(0, n_groups - 1, lambda j, cc: group(j * unroll, cc, False), carry)
    group((n_groups - 1) * unroll, carry, True)

    acc = acc_ref[...]
    o = (acc[:LANES] / acc[LANES:LANES + 1]).T
    if diff:
        lp = lp_ref[...]
        lam = (jnp.exp(jnp.sum(lp[0:1] * lp[1:2], keepdims=True))
               - jnp.exp(jnp.sum(lp[2:3] * lp[3:4], keepdims=True)) + lam_init)
        dlt = o[:tq] - lam * o[tq:]
        ms = jnp.mean(dlt * dlt, axis=-1, keepdims=True)
        y = dlt * lax.rsqrt(ms + RMS_EPS) * sg_ref[...] * (1.0 - lam_init)
        o_ref[...] = (y * z_ref[...].astype(F32)).astype(BF16)
    else:
        o_ref[:, :LANES] = (o[:tq] * z_ref[0].astype(F32)).astype(BF16)
        o_ref[:, LANES:] = (o[tq:] * z_ref[1].astype(F32)).astype(BF16)


def _pair_attn(hr, hp, diff, layer=0, lam_init=0.0, lam_params=None, subln_g=None, tq=512, tk=512):
    s = hr.shape[1]
    if diff:
        n_kv, qw, qblk = 4, LANES, (None, tq, LANES)
        q_map = lambda hh, i: (CB_AQ + hh, i, 0)
        k_map = lambda hh, i: (CB_AK + hh, 0, 0)
        v_map = lambda hh, i: (CB_AV + hh, 0, 0)
        z_map = lambda hh, i: (CB_AZ + hh, i, 0)
    else:
        n_kv, qw, qblk = 2, 2 * LANES, (2, tq, LANES)
        q_map = lambda hh, i: (CB_CQ // 2 + hh, i, 0)
        k_map = lambda hh, i: (CB_CK + hh, 0, 0)
        v_map = lambda hh, i: (CB_CV + hh, 0, 0)
        z_map = lambda hh, i: (CB_CZ // 2 + hh, i, 0)
    in_specs = [pl.BlockSpec(qblk, q_map),
                pl.BlockSpec((None, s, LANES), k_map),
                pl.BlockSpec((None, s, LANES), v_map),
                pl.BlockSpec(qblk, z_map)]
    args = [hr, hr, hp if diff else hr, hp]
    if diff:
        in_specs += [pl.BlockSpec((None, 4, DA_QK), lambda hh, i: (layer, 0, 0)),
                     pl.BlockSpec((None, 1, LANES), lambda hh, i: (layer, 0, 0))]
        args += [lam_params, subln_g]
    return pl.pallas_call(
        functools.partial(_pair_attn_kernel, diff, tq, tk, lam_init),
        grid=(n_kv, s // tq),
        in_specs=in_specs,
        out_specs=pl.BlockSpec((tq, qw), lambda hh, i: (i, hh)),
        out_shape=jax.ShapeDtypeStruct((s, BRANCH_W), BF16),
        scratch_shapes=[pltpu.VMEM((2 * tq, LANES), BF16),
                        pltpu.VMEM((s // tk, LANES + VT_PAD, tk), BF16),
                        pltpu.VMEM((LANES + VT_PAD, 2 * tq), F32),
                        pltpu.VMEM((2, tk, 2 * tq), F32)],
        compiler_params=_cparams(("arbitrary", "arbitrary")),
        name="diff_attn" if diff else "axial_gqa",
    )(*args)


def _nat_bias(rpb):
    n_layers, n_heads = rpb.shape[:2]
    cols = np.arange(GRID_W)
    c0 = np.clip(cols - NB_COLS // 2, 0, GRID_W - NB_COLS)
    valid = (cols[None, :] >= c0[:, None]) & (cols[None, :] < c0[:, None] + NB_COLS)
    col_sel = (np.arange(2 * NB_COLS - 1)[:, None, None]
               == cols[None, None, :] - cols[None, :, None] + (NB_COLS - 1)).astype(np.float32)
    row_sel = (np.arange(2 * NB_ROWS - 1)[None, None, :]
               == np.arange(NB_ROWS)[:, None, None] + np.arange(NB_ROWS)[None, :, None]).astype(np.float32)
    exact = lax.Precision.HIGHEST
    toep = jnp.einsum('lhdr,rck->lhdck', rpb.astype(F32), col_sel, precision=exact)
    t = jnp.where(valid, toep, NEG_INF)
    bias = jnp.einsum('lhdck,sid->lshcik', t, row_sel, precision=exact)
    return bias.reshape(n_layers, NB_ROWS, n_heads, GRID_W, NB_ROWS * GRID_W)


def _nat_kernel(rows, rps, q_ref, k_ref, v_ref, z_ref, bias_ref, o_ref):
    win = NB_ROWS * GRID_W
    starts, scores = [], []
    for u in range(rps):
        r = pl.program_id(1) * rps + u
        r0 = jnp.clip(r - NB_ROWS // 2, 0, rows - NB_ROWS)
        start = pl.multiple_of(r0 * GRID_W, GRID_W)
        kw = k_ref[pl.ds(start, win), :]
        s = lax.dot_general(q_ref[u * GRID_W:(u + 1) * GRID_W, :], kw, (((1,), (1,)), ((), ())),
                            preferred_element_type=F32)
        scores.append(s * (HEAD_DIM ** -0.5) + bias_ref[r0 - r + (NB_ROWS - 1)])
        starts.append(start)
    probs = []
    for s in scores:
        p = jnp.exp(s - jnp.max(s, axis=-1, keepdims=True))
        probs.append((p.astype(BF16), jnp.sum(p, axis=-1, keepdims=True)))
    for u, (p, l) in enumerate(probs):
        qrows = slice(u * GRID_W, (u + 1) * GRID_W)
        o = jnp.dot(p, v_ref[pl.ds(starts[u], win), :], preferred_element_type=F32) / l
        o_ref[qrows, :] = (o * z_ref[qrows, :].astype(F32)).astype(BF16)


def _nat_attn(hp, bias, layer, rps=64):
    s = hp.shape[1]
    rows = s // GRID_W
    blk = rps * GRID_W
    return pl.pallas_call(
        functools.partial(_nat_kernel, rows, rps),
        grid=(4, rows // rps),
        in_specs=[pl.BlockSpec((None, blk, LANES), lambda hh, r: (CB_BQ + hh, r, 0)),
                  pl.BlockSpec((None, s, LANES), lambda hh, r: (CB_BK + hh, 0, 0)),
                  pl.BlockSpec((None, s, LANES), lambda hh, r: (CB_BV + hh, 0, 0)),
                  pl.BlockSpec((None, blk, LANES), lambda hh, r: (CB_BZ + hh, r, 0)),
                  pl.BlockSpec((None, NB_ROWS, None, GRID_W, NB_ROWS * GRID_W),
                               lambda hh, r: (layer, 0, hh, 0, 0))],
        out_specs=pl.BlockSpec((blk, LANES), lambda hh, r: (r, hh)),
        out_shape=jax.ShapeDtypeStruct((s, BRANCH_W), BF16),
        compiler_params=_cparams(("arbitrary", "arbitrary")),
        name="nat_attn",
    )(hp, hp, hp, hp, bias)


def _dil_masks(tq, rates):
    row = np.arange(tq)[:, None]

    def band(col, rate):
        d = col - row
        return np.where((d >= 0) & (d <= 2 * DIL_SIDE * rate) & (d % rate == 0), 0.0, NEG_INF).astype(np.float32)

    width0 = tq + 2 * DIL_SIDE
    tables = [np.stack([band(np.arange(width0)[None, :] + dlt, 1) for dlt in (DIL_SIDE, 0, -DIL_SIDE)])]
    assert rates[0] == 1
    for rate in rates[1:]:
        reach = DIL_SIDE * rate
        cols = np.arange(-reach, tq + 3 * reach)[None, :]
        full = band(cols, rate)
        tables.append(full.reshape(tq, -1, LANES).transpose(1, 0, 2))
    return [jnp.asarray(t) for t in tables]


def _dil_class_kernel(tq, rate, q_ref, k_ref, v_ref, b_ref, o_ref, lse_ref):
    length = q_ref.shape[0] // rate
    width = tq + 2 * DIL_SIDE
    for a in range(rate):
        cls = pl.ds(a, length, stride=rate)
        qa = q_ref[cls, :].astype(BF16)
        ka = k_ref[cls, :].astype(BF16)
        va = v_ref[cls, :].astype(BF16)
        scored = []
        for i0 in range(0, length, tq):
            ws = min(max(i0 - DIL_SIDE, 0), length - width)
            delta = ws - i0 + DIL_SIDE
            s = lax.dot_general(qa[i0:i0 + tq], ka[ws:ws + width], (((1,), (1,)), ((), ())),
                                preferred_element_type=F32)
            scored.append((i0, ws, s + b_ref[0 if delta > 0 else (2 if delta < 0 else 1)]))
        stats = []
        for i0, ws, s in scored:
            m = jnp.max(s, axis=-1, keepdims=True)
            p = jnp.exp2(s - m)
            stats.append((i0, ws, p.astype(BF16), m, jnp.sum(p, axis=-1, keepdims=True)))
        for i0, ws, p, m, l in stats:
            out_rows = pl.ds(a + rate * i0, tq, stride=rate)
            o_ref[out_rows, :] = jnp.dot(p, va[ws:ws + width], preferred_element_type=F32) / l
            lse_ref[out_rows, :] = jnp.broadcast_to(m + jnp.log2(l), (tq, LANES))


def _dil_kernel(tq, nsub, q0_ref, q1_ref, k0_ref, k1_ref, v0_ref, v1_ref, z_ref, b0_ref, b1_ref,
                o2_ref, lse2_ref, o_ref):
    seq = k0_ref.shape[0]
    q_refs, k_refs, v_refs, b_refs = (q0_ref, q1_ref), (k0_ref, k1_ref), (v0_ref, v1_ref), (b0_ref, b1_ref)
    scores, wins = [], []
    for u in range(nsub):
        t0 = (pl.program_id(1) * nsub + u) * tq
        rows = slice(u * tq, (u + 1) * tq)
        su, wu = [], []
        for rate, q_ref, k_ref, b_ref in zip(DIL_RATES, q_refs, k_refs, b_refs):
            reach = DIL_SIDE * rate
            width = tq + 2 * reach
            start = pl.multiple_of(jnp.clip(t0 - reach, 0, seq - width), DIL_SIDE)
            delta = start - t0 + reach
            s = lax.dot_general(q_ref[rows, :], k_ref[pl.ds(start, width), :], (((1,), (1,)), ((), ())),
                                preferred_element_type=F32)
            if rate == 1:
                s = s + b_ref[jnp.where(delta > 0, 0, jnp.where(delta < 0, 2, 1))]
            else:
                tile0 = (delta + reach) // LANES
                s = jnp.concatenate([s[:, j * LANES:(j + 1) * LANES] + b_ref[tile0 + j]
                                     for j in range(width // LANES)], axis=1)
            su.append(s)
            wu.append((start, width))
        scores.append(su)
        wins.append(wu)
    probs, dens, tops = [], [], []
    for su in scores:
        m = su[0].max(axis=-1, keepdims=True)
        for s in su[1:]:
            m = jnp.maximum(m, s.max(axis=-1, keepdims=True))
        pu = [jnp.exp2(s - m) for s in su]
        l = pu[0].sum(axis=-1, keepdims=True)
        for p in pu[1:]:
            l = l + p.sum(axis=-1, keepdims=True)
        probs.append([p.astype(BF16) for p in pu])
        dens.append(l)
        tops.append(m)
    for u, (pu, wu, l, m) in enumerate(zip(probs, wins, dens, tops)):
        rows = slice(u * tq, (u + 1) * tq)
        o = None
        for p, (start, width), v_ref in zip(pu, wu, v_refs):
            pv = jnp.dot(p, v_ref[pl.ds(start, width), :], preferred_element_type=F32)
            o = pv if o is None else o + pv
        lse2 = lse2_ref[rows, :1]
        top = jnp.maximum(m, lse2)
        w01 = jnp.exp2(m - top)
        w2 = jnp.exp2(lse2 - top)
        mixed = (w01 * o + w2 * o2_ref[rows, :]) / (w01 * l + w2)
        o_ref[rows, :] = (mixed * z_ref[rows, :].astype(F32)).astype(BF16)


def _dil_class_attn(hw, mask, tq=128):
    s = hw.shape[1]
    head = lambda cb: pl.BlockSpec((None, s, LANES), lambda hh: (cb + hh, 0, 0))
    return pl.pallas_call(
        functools.partial(_dil_class_kernel, tq, DIL_RATES[-1]),
        grid=(4,),
        in_specs=[head(CB_WQ), head(CB_WK), head(CB_WV), pl.BlockSpec(mask.shape, lambda hh: (0, 0, 0))],
        out_specs=[head(0), head(0)],
        out_shape=[jax.ShapeDtypeStruct((4, s, LANES), F32), jax.ShapeDtypeStruct((4, s, LANES), F32)],
        compiler_params=_cparams(("arbitrary",)),
        name="dil_class_attn",
    )(hw, hw, hw, mask)


def _dil_attn(hr, hp, hw, tq=128, nsub=32):
    s = hr.shape[1]
    n_dense = len(DIL_RATES) - 1
    masks = _dil_masks(tq, DIL_RATES[:n_dense])
    o2, lse2 = _dil_class_attn(hw, masks[0], tq)
    m_specs = [pl.BlockSpec(m.shape, lambda hh, i: (0, 0, 0)) for m in masks]
    blk = tq * nsub
    qs = [pl.BlockSpec((None, blk, LANES), functools.partial(lambda g, hh, i: (CB_DQ + 4 * g + hh, i, 0), g))
          for g in range(n_dense)]
    ks = [pl.BlockSpec((None, s, LANES), functools.partial(lambda g, hh, i: (CB_DK + 4 * g + hh, 0, 0), g))
          for g in range(n_dense)]
    vs = [pl.BlockSpec((None, s, LANES), functools.partial(lambda g, hh, i: (CB_DV + 4 * g + hh, 0, 0), g))
          for g in range(n_dense)]
    row_blk = pl.BlockSpec((None, blk, LANES), lambda hh, i: (hh, i, 0))
    return pl.pallas_call(
        functools.partial(_dil_kernel, tq, nsub),
        grid=(4, s // blk),
        in_specs=(qs + ks + vs + [pl.BlockSpec((None, blk, LANES), lambda hh, i: (CB_DZ + hh, i, 0))]
                  + m_specs + [row_blk, row_blk]),
        out_specs=pl.BlockSpec((blk, LANES), lambda hh, i: (i, hh)),
        out_shape=jax.ShapeDtypeStruct((s, BRANCH_W), BF16),
        compiler_params=_cparams(("arbitrary", "arbitrary")),
        name="dil_attn",
    )(*([hr] * (2 * n_dense)), *([hp] * (n_dense + 1)), *masks, o2, lse2)


def _merge_kernel(ya_ref, yb_ref, yc_ref, yd_ref, ga_ref, gb_ref, gc_ref, gd_ref, wb_ref, o_ref, wbf_ref):
    @pl.when(pl.program_id(1) == 0)
    def _():
        wbf_ref[...] = wb_ref[...].astype(BF16)

    tsub = min(256, o_ref.shape[0])
    for r0 in range(0, o_ref.shape[0], tsub):
        rows = slice(r0, r0 + tsub)
        acc = None
        for n, (y_ref, g_ref) in enumerate(zip((ya_ref, yb_ref, yc_ref, yd_ref),
                                               (ga_ref, gb_ref, gc_ref, gd_ref))):
            proj = jnp.dot(y_ref[rows, :], wbf_ref[n], preferred_element_type=F32)
            term = proj * g_ref[rows, :].astype(F32)
            acc = term if acc is None else acc + term
        o_ref[rows, :] = acc.astype(BF16)


def _merge(ys, hg, w_branch, layer, tm=2048):
    s = hg.shape[0]
    tm = math.gcd(tm, s)
    n_col = D_MODEL // TN
    y_specs = [pl.BlockSpec((tm, BRANCH_W), lambda j, i: (i, 0)) for _ in range(N_BRANCH)]
    g_specs = [pl.BlockSpec((tm, TN), functools.partial(lambda n, j, i: (i, n_col * n + j), n))
               for n in range(N_BRANCH)]
    return pl.pallas_call(
        _merge_kernel,
        grid=(n_col, s // tm),
        in_specs=y_specs + g_specs + [pl.BlockSpec((None, N_BRANCH, BRANCH_W, TN),
                                                   lambda j, i: (layer, 0, 0, j))],
        out_specs=pl.BlockSpec((tm, TN), lambda j, i: (i, j)),
        out_shape=jax.ShapeDtypeStruct((s, D_MODEL), BF16),
        scratch_shapes=[pltpu.VMEM((N_BRANCH, BRANCH_W, TN), BF16)],
        compiler_params=_cparams(("arbitrary", "arbitrary")),
        name="branch_merge",
    )(*ys, hg, hg, hg, hg, w_branch)


def _out_ln_kernel(alpha, tsub, m_ref, w_ref, x_ref, g_ref, b_ref, y_ref, ybf_ref, acc_ref):
    n_sub = m_ref.shape[0] // tsub

    def matmul(u):
        acc_ref[u % 2] = jnp.dot(m_ref[u * tsub:(u + 1) * tsub, :], w_ref[...], preferred_element_type=F32)

    matmul(0)
    for u in range(n_sub):
        if u + 1 < n_sub:
            matmul(u + 1)
        rows = slice(u * tsub, (u + 1) * tsub)
        y = _ln_rows(alpha * x_ref[rows, :] + acc_ref[u % 2], g_ref[...], b_ref[...])
        y_ref[rows, :] = y
        ybf_ref[rows, :] = y.astype(BF16)


def _out_ln(merged, wo_bf, layer, x, g, b, alpha, tm=512, tsub=128):
    s, d = x.shape
    return pl.pallas_call(
        functools.partial(_out_ln_kernel, alpha, tsub),
        grid=(s // tm,),
        in_specs=[pl.BlockSpec((tm, d), lambda i: (i, 0)),
                  pl.BlockSpec((None, d, d), lambda i: (layer, 0, 0)),
                  pl.BlockSpec((tm, d), lambda i: (i, 0)),
                  pl.BlockSpec((None, 1, d), lambda i: (layer, 0, 0)),
                  pl.BlockSpec((None, 1, d), lambda i: (layer, 0, 0))],
        out_specs=[pl.BlockSpec((tm, d), lambda i: (i, 0)),
                   pl.BlockSpec((tm, d), lambda i: (i, 0))],
        out_shape=[jax.ShapeDtypeStruct((s, d), F32), jax.ShapeDtypeStruct((s, d), BF16)],
        scratch_shapes=[pltpu.VMEM((2, tsub, d), F32)],
        compiler_params=_cparams(("arbitrary",)),
        name="out_ln",
    )(merged, wo_bf, x, g, b)


def _tail_kernel(alpha, tsub, ya_ref, yb_ref, yc_ref, yd_ref, g_ref, wb_ref, wo_ref, x_ref, lg_ref, lb_ref,
                 y_ref, ybf_ref, m_ref, acc_ref):
    n_sub = x_ref.shape[0] // tsub

    def project(u):
        rows = slice(u * tsub, (u + 1) * tsub)
        for c0 in range(0, D_MODEL, TN):
            acc = None
            for n, y_in in enumerate((ya_ref, yb_ref, yc_ref, yd_ref)):
                proj = jnp.dot(y_in[rows, :], wb_ref[n, :, c0:c0 + TN], preferred_element_type=F32)
                term = proj * g_ref[rows, n * D_MODEL + c0:n * D_MODEL + c0 + TN].astype(F32)
                acc = term if acc is None else acc + term
            m_ref[rows, c0:c0 + TN] = acc.astype(BF16)
        acc_ref[u % 2] = jnp.dot(m_ref[rows, :], wo_ref[...], preferred_element_type=F32)

    project(0)
    for u in range(n_sub):
        if u + 1 < n_sub:
            project(u + 1)
        rows = slice(u * tsub, (u + 1) * tsub)
        y = _ln_rows(alpha * x_ref[rows, :] + acc_ref[u % 2], lg_ref[...], lb_ref[...])
        y_ref[rows, :] = y
        ybf_ref[rows, :] = y.astype(BF16)


def _tail(ys, hg, wb_bf, wo_bf, layer, x, g, b, alpha, tm=256, tsub=128):
    s, d = x.shape
    resident = pl.Buffered(1)
    return pl.pallas_call(
        functools.partial(_tail_kernel, alpha, tsub),
        grid=(s // tm,),
        in_specs=[pl.BlockSpec((tm, BRANCH_W), lambda i: (i, 0)) for _ in range(N_BRANCH)]
        + [pl.BlockSpec((tm, N_BRANCH * d), lambda i: (i, 0)),
           pl.BlockSpec((None, N_BRANCH, BRANCH_W, d), lambda i: (layer, 0, 0, 0), pipeline_mode=resident),
           pl.BlockSpec((None, d, d), lambda i: (layer, 0, 0), pipeline_mode=resident),
           pl.BlockSpec((tm, d), lambda i: (i, 0)),
           pl.BlockSpec((None, 1, d), lambda i: (layer, 0, 0)),
           pl.BlockSpec((None, 1, d), lambda i: (layer, 0, 0))],
        out_specs=[pl.BlockSpec((tm, d), lambda i: (i, 0)),
                   pl.BlockSpec((tm, d), lambda i: (i, 0))],
        out_shape=[jax.ShapeDtypeStruct((s, d), F32), jax.ShapeDtypeStruct((s, d), BF16)],
        scratch_shapes=[pltpu.VMEM((tm, d), BF16), pltpu.VMEM((2, tsub, d), F32)],
        compiler_params=_cparams(("arbitrary",)),
        name="merge_out_ln",
    )(*ys, hg, wb_bf, wo_bf, x, g, b)


def kernel(x, emb_ln_g, emb_ln_b, w_in, b_gate, diff_lambda, diff_subln_g, nat_rpb,
           gqa_q_norm_g, gqa_k_norm_g, w_branch, w_out, ln_g, ln_b):
    batch, seq, d = x.shape
    assert batch == 1 and d == D_MODEL and w_in.shape[-1] == D_IN
    depth = w_in.shape[0]
    alpha = (2 * depth) ** 0.25
    tabs = _rope_tables(seq)
    nat_bias = _nat_bias(nat_rpb)
    wo_bf = w_out.astype(BF16)
    wb_bf = w_branch.astype(BF16)
    gains = jnp.stack([gqa_q_norm_g, gqa_k_norm_g], axis=1).reshape(depth, 2, 1, LANES)
    b_gate3 = b_gate.reshape(depth, 1, -1)
    subln3 = diff_subln_g.reshape(depth, 1, LANES)
    ln_g3, ln_b3 = ln_g.reshape(depth, 1, d), ln_b.reshape(depth, 1, d)
    xf, xbf = _embed_ln(x[0], emb_ln_g, emb_ln_b)
    for l in range(depth):
        lam_init = 0.8 - 0.6 * math.exp(-0.3 * l)
        hr, hp, hg, hw = _inproj(xbf, w_in, b_gate3, l, tabs, gains)
        ya = _pair_attn(hr, hp, True, l, lam_init, diff_lambda, subln3)
        yb = _nat_attn(hp, nat_bias, l)
        yc = _pair_attn(hr, hp, False)
        yd = _dil_attn(hr, hp, hw)
        xf, xbf = _tail((ya, yb, yc, yd), hg, wb_bf, wo_bf, l, xf, ln_g3, ln_b3, alpha)
    return xf[None]
```

```python
import functools
import math

import numpy as np
import jax
import jax.numpy as jnp
from jax import lax
from jax.experimental import pallas as pl
from jax.experimental.pallas import tpu as pltpu

F32 = jnp.float32
BF16 = jnp.bfloat16

D_MODEL = 2048
GRID_W = 64
HEAD_DIM = 128
N_BRANCH = 4
BRANCH_W = D_MODEL // 4
DA_QK = HEAD_DIM // 2
NB_ROWS = 8
NB_COLS = 16
AXIAL_THETA = 10000.0
DIL_RATES = (1, 4, 16)
DIL_SIDE = 64
ROPE_THETA = 500000.0
LN_EPS = 1e-5
RMS_EPS = 1e-6
NEG_INF = -1e30
LOG2E = math.log2(math.e)

LANES = 128
VT_PAD = 16
TN = 512
D_IN = 18944

ROT_TILES = (0, 1, 8, 9, 11, 12, 14, 15)
PLAIN_TILES = (2, 3, 4, 5, 6, 7, 10, 17, 18, 20)
WIDE_TILES = (13, 16, 19)
GATE_TILE0, N_GATE_TILES = 21, 16
ROT_VARIANT = (0, 1, 2, 3, 4, 4, 5, 5)
ROT_VARIANT_DQ, ROT_VARIANT_DK = 4, 5
CB_AQ, CB_AK, CB_CQ, CB_CK, CB_CV, CB_DQ, CB_DK = 0, 4, 8, 12, 14, 16, 24
CB_AV, CB_AZ, CB_BQ, CB_BK, CB_BV, CB_BZ, CB_CZ, CB_DV, CB_DZ = 0, 4, 8, 12, 16, 20, 24, 28, 36
CB_WQ, CB_WK, CB_WV = 0, 4, 8
PLAIN_SILU = (1, 5, 6, 9)

VMEM_LIMIT = 56 * 1024 * 1024


def _cparams(sem):
    return pltpu.CompilerParams(dimension_semantics=sem, vmem_limit_bytes=VMEM_LIMIT)


def _select(j, values):
    out = values[-1]
    for t in range(len(values) - 2, -1, -1):
        out = jnp.where(j == t, values[t], out)
    return out


def _any_of(j, tiles):
    c = j == tiles[0]
    for t in tiles[1:]:
        c = c | (j == t)
    return c


def _ln_rows(x, g, b):
    mu = jnp.mean(x, axis=-1, keepdims=True)
    xc = x - mu
    var = jnp.mean(xc * xc, axis=-1, keepdims=True)
    return xc * lax.rsqrt(var + LN_EPS) * g + b


def _embed_ln_kernel(x_ref, g_ref, b_ref, y_ref, ybf_ref):
    y = _ln_rows(x_ref[...], g_ref[...], b_ref[...])
    y_ref[...] = y
    ybf_ref[...] = y.astype(BF16)


def _embed_ln(x, g, b, tm=512):
    s, d = x.shape
    return pl.pallas_call(
        _embed_ln_kernel,
        grid=(s // tm,),
        in_specs=[pl.BlockSpec((tm, d), lambda i: (i, 0)),
                  pl.BlockSpec((1, d), lambda i: (0, 0)),
                  pl.BlockSpec((1, d), lambda i: (0, 0))],
        out_specs=[pl.BlockSpec((tm, d), lambda i: (i, 0)),
                   pl.BlockSpec((tm, d), lambda i: (i, 0))],
        out_shape=[jax.ShapeDtypeStruct((s, d), F32), jax.ShapeDtypeStruct((s, d), BF16)],
        compiler_params=_cparams(("arbitrary",)),
        name="embed_ln",
    )(x, g.reshape(1, d), b.reshape(1, d))


def _rope_tables(seq):
    t = np.arange(seq)

    def cs(pos, dim, theta):
        half = dim // 2
        inv = np.power(float(theta), -np.arange(half, dtype=np.float64) * 2.0 / dim)
        ang = pos.astype(np.float64)[:, None] * inv[None, :]
        return jnp.asarray(np.cos(ang), F32), jnp.asarray(np.sin(ang), F32)

    def group(cos, sin, width):
        half = cos.shape[1]
        pad = width - 2 * half
        one = jnp.ones((seq, pad), F32)
        zero = jnp.zeros((seq, pad), F32)
        zh = jnp.zeros((seq, half), F32)
        return (jnp.concatenate([cos, cos, one], 1),
                jnp.concatenate([zh, sin, zero], 1),
                jnp.concatenate([-sin, zh, zero], 1))

    ca, sa = cs(t, DA_QK // 4, ROPE_THETA)
    va = jnp.stack([jnp.concatenate([p, p], 1) for p in group(ca, sa, DA_QK)])
    cr, sr = cs(t // GRID_W, HEAD_DIM // 2, AXIAL_THETA)
    cc, sc = cs(t % GRID_W, HEAD_DIM // 2, AXIAL_THETA)
    vc = jnp.stack([jnp.concatenate([p, q], 1)
                    for p, q in zip(group(cr, sr, HEAD_DIM // 2), group(cc, sc, HEAD_DIM // 2))])
    cd, sd = cs(t, HEAD_DIM // 4, ROPE_THETA)
    vd = jnp.stack(group(cd, sd, HEAD_DIM))
    qa = DA_QK ** -0.5 * LOG2E
    qh = HEAD_DIM ** -0.5 * LOG2E
    return jnp.stack([va * qa, va, vc * qh, vc, vd * qh, vd])


ROT_HALF_A, ROT_HALF_C, ROT_HALF_D = DA_QK // 8, HEAD_DIM // 4, HEAD_DIM // 8


def _proj_kernel(tsub, n_sub, make_branches, n_extra, x_ref, w_ref, *rest):
    extra = rest[:n_extra]
    o_ref, wbf_ref, acc_ref = rest[n_extra:]
    j = pl.program_id(0)
    n_pass = x_ref.shape[0] // (n_sub * tsub)

    @pl.when(pl.program_id(1) == 0)
    def _():
        wbf_ref[...] = w_ref[...].astype(BF16)

    def pipeline(epilogue):
        def one_pass(h, carry):
            def rows_of(u):
                return pl.ds(pl.multiple_of((h * n_sub + u) * tsub, tsub), tsub)

            def matmul(u):
                acc_ref[u % 2] = jnp.dot(x_ref[rows_of(u), :], wbf_ref[...],
                                         preferred_element_type=F32)

            matmul(0)
            for u in range(n_sub):
                if u + 1 < n_sub:
                    matmul(u + 1)
                for g in range(TN // LANES):
                    lanes = slice(g * LANES, (g + 1) * LANES)
                    y = epilogue(acc_ref[u % 2, :, lanes], rows_of(u), g).astype(o_ref.dtype)
                    if len(o_ref.shape) == 3:
                        o_ref[g, rows_of(u), :] = y
                    else:
                        o_ref[rows_of(u), lanes] = y
            return carry

        lax.fori_loop(0, n_pass, one_pass, 0)

    for cond, epilogue in make_branches(*extra):
        if cond is None:
            pipeline(epilogue)
        else:
            pl.when(cond(j))(functools.partial(pipeline, epilogue))


def _proj(name, xbf, w_in, layer, src_tiles, make_branches, extra, extra_specs, head_major,
          tm=2048, tsub=256, n_sub=4, dtype=BF16):
    s, d = xbf.shape
    tm = math.gcd(tm, s)
    n_sub = min(n_sub, tm // tsub)
    assert tm % (tsub * n_sub) == 0
    n_tiles = len(src_tiles)
    contiguous = src_tiles == tuple(range(src_tiles[0], src_tiles[0] + n_tiles))
    src = (lambda j: src_tiles[0] + j) if contiguous else (lambda j: _select(j, src_tiles))
    if head_major:
        out_spec = pl.BlockSpec((TN // LANES, tm, LANES), lambda j, i: (j, i, 0))
        out_shape = jax.ShapeDtypeStruct((n_tiles * (TN // LANES), s, LANES), dtype)
    else:
        out_spec = pl.BlockSpec((tm, TN), lambda j, i: (i, j))
        out_shape = jax.ShapeDtypeStruct((s, n_tiles * TN), dtype)
    return pl.pallas_call(
        functools.partial(_proj_kernel, tsub, n_sub, make_branches, len(extra)),
        grid=(n_tiles, s // tm),
        in_specs=[pl.BlockSpec((tm, d), lambda j, i: (i, 0)),
                  pl.BlockSpec((None, d, TN), lambda j, i: (layer, 0, src(j)))] + extra_specs(tm),
        out_specs=out_spec,
        out_shape=out_shape,
        scratch_shapes=[pltpu.VMEM((d, TN), BF16), pltpu.VMEM((2, tsub, TN), F32)],
        compiler_params=_cparams(("arbitrary", "arbitrary")),
        name=name,
    )(xbf, w_in, *extra)


def _rot_branches(tab_ref, gn_ref):
    def rot(a, rows, half):
        return (a * tab_ref[0, rows, :] + pltpu.roll(a, half, 1) * tab_ref[1, rows, :]
                + pltpu.roll(a, LANES - half, 1) * tab_ref[2, rows, :])

    def rms(a, gain):
        ms = jnp.mean(a * a, axis=-1, keepdims=True)
        return a * lax.rsqrt(ms + RMS_EPS) * gain

    return [(lambda j: j <= 1, lambda a, rows, g: rot(a, rows, ROT_HALF_A)),
            (lambda j: j == 2, lambda a, rows, g: rot(rms(a, gn_ref[0]), rows, ROT_HALF_C)),
            (lambda j: j == 3, lambda a, rows, g: rot(rms(a, gn_ref[1]), rows, ROT_HALF_C) if g < 2 else a),
            (lambda j: j >= 4, lambda a, rows, g: rot(a, rows, ROT_HALF_D))]


def _plain_branches():
    return [(lambda j: _any_of(j, PLAIN_SILU), lambda a, rows, g: a * jax.nn.sigmoid(a)),
            (lambda j: jnp.logical_not(_any_of(j, PLAIN_SILU)), lambda a, rows, g: a)]


def _gate_branches(bg_ref):
    return [(None, lambda a, rows, g: jax.nn.sigmoid(a + bg_ref[:, g * LANES:(g + 1) * LANES]))]


def _wide_branches(tab_ref):
    def rot(a, rows):
        return (a * tab_ref[0, rows, :] + pltpu.roll(a, ROT_HALF_D, 1) * tab_ref[1, rows, :]
                + pltpu.roll(a, LANES - ROT_HALF_D, 1) * tab_ref[2, rows, :])

    return [(lambda j: j <= 1, lambda a, rows, g: rot(a, rows)),
            (lambda j: j == 2, lambda a, rows, g: a)]


def _inproj(xbf, w_in, b_gate, layer, tabs, gains):
    hr = _proj("inproj_rot", xbf, w_in, layer, ROT_TILES, _rot_branches, (tabs, gains),
               lambda tm: [pl.BlockSpec((None, 3, tm, LANES), lambda j, i: (_select(j, ROT_VARIANT), 0, i, 0)),
                           pl.BlockSpec((None, 2, 1, LANES), lambda j, i: (layer, 0, 0, 0))], True)
    hp = _proj("inproj_plain", xbf, w_in, layer, PLAIN_TILES, _plain_branches, (), lambda tm: [], True,
               n_sub=8)
    hg = _proj("inproj_gate", xbf, w_in, layer, tuple(range(GATE_TILE0, GATE_TILE0 + N_GATE_TILES)),
               _gate_branches, (b_gate,),
               lambda tm: [pl.BlockSpec((None, 1, TN), lambda j, i: (layer, 0, j))], False, n_sub=8)
    hw = _proj("inproj_wide", xbf, w_in, layer, WIDE_TILES, _wide_branches, (tabs,),
               lambda tm: [pl.BlockSpec((None, 3, tm, LANES),
                                        lambda j, i: (jnp.where(j == 0, ROT_VARIANT_DQ, ROT_VARIANT_DK), 0, i, 0))],
               True, n_sub=8, dtype=F32)
    return hr, hp, hg, hw


def _pair_attn_kernel(diff, tq, tk, lam_init, *refs):
    if diff:
        q_ref, k_ref, v_ref, z_ref, lp_ref, sg_ref, wc_ref, o_ref, wcb_ref, q2_ref, vt_ref, acc_ref, s_ref = refs
    else:
        q_ref, k_ref, v_ref, z_ref, wc_ref, o_ref, wcb_ref, q2_ref, vt_ref, acc_ref, s_ref = refs
    seq = k_ref.shape[0]
    nq = 2 * tq
    wcb_ref[...] = wc_ref[...].astype(BF16)

    @pl.when(pl.program_id(1) == 0)
    def _():
        for c in range(seq // tk):
            vt_ref[c, :LANES, :] = v_ref[c * tk:(c + 1) * tk, :].astype(F32).T.astype(BF16)
            vt_ref[c, LANES:, :] = jnp.ones((VT_PAD, tk), BF16)

    if diff:
        q = q_ref[...]
        lane = lax.broadcasted_iota(jnp.int32, q.shape, 1)
        zero = jnp.zeros_like(q)
        q2_ref[:tq, :] = jnp.where(lane < DA_QK, q, zero)
        q2_ref[tq:, :] = jnp.where(lane >= DA_QK, q, zero)
    else:
        q2_ref[:tq, :] = q_ref[0]
        q2_ref[tq:, :] = q_ref[1]
    acc_ref[...] = jnp.zeros(acc_ref.shape, F32)

    n_chunks = seq // tk

    def scores(c, slot):
        start = pl.multiple_of(c * tk, tk)
        s = lax.dot_general(k_ref[pl.ds(start, tk), :], q2_ref[...], (((1,), (1,)), ((), ())),
                            preferred_element_type=F32)
        s_ref[slot] = s
        return jnp.max(s, axis=0, keepdims=True)

    def fold(c, slot, m_prev, m_chunk):
        m_new = jnp.maximum(m_prev, m_chunk)
        alpha = jnp.exp2(m_prev - m_new)
        p = jnp.exp2(s_ref[slot] - m_new)
        acc_ref[...] = alpha * acc_ref[...] + jnp.dot(vt_ref[c], p.astype(BF16),
                                                      preferred_element_type=F32)
        return m_new

    unroll = 4 if n_chunks % 4 == 0 else 2

    def group(base, carry, last):
        m, m_chunk = carry
        for u in range(unroll):
            m_next = m_chunk if (last and u == unroll - 1) else scores(base + u + 1, (u + 1) % 2)
            m = fold(base + u, u % 2, m, m_chunk)
            m_chunk = m_next
        return m, m_chunk

    carry = (jnp.full((1, nq), NEG_INF, F32), scores(0, 0))
    n_groups = n_chunks // unroll
    if n_groups > 1:
        carry = lax.fori_loop(0, n_groups - 1, lambda j, cc: group(j * unroll, cc, False), carry)
    group((n_groups - 1) * unroll, carry, True)

    acc = acc_ref[...]
    o = (acc[:LANES] / acc[LANES:LANES + 1]).T
    if diff:
        lp = lp_ref[...]
        lam = (jnp.exp(jnp.sum(lp[0:1] * lp[1:2], keepdims=True))
               - jnp.exp(jnp.sum(lp[2:3] * lp[3:4], keepdims=True)) + lam_init)
        dlt = o[:tq] - lam * o[tq:]
        ms = jnp.mean(dlt * dlt, axis=-1, keepdims=True)
        y = dlt * lax.rsqrt(ms + RMS_EPS) * sg_ref[...] * (1.0 - lam_init)
        o_ref[...] = (y * z_ref[...].astype(F32)).astype(BF16)
    else:
        o_ref[:, :LANES] = (o[:tq] * z_ref[0].astype(F32)).astype(BF16)
        o_ref[:, LANES:] = (o[tq:] * z_ref[1].astype(F32)).astype(BF16)


def _pair_attn(hr, hp, diff, cast_w, layer, lam_init=0.0, lam_params=None, subln_g=None, tq=512, tk=512):
    s = hr.shape[1]
    if diff:
        n_kv, qw, qblk = 4, LANES, (None, tq, LANES)
        q_map = lambda hh, i: (CB_AQ + hh, i, 0)
        k_map = lambda hh, i: (CB_AK + hh, 0, 0)
        v_map = lambda hh, i: (CB_AV + hh, 0, 0)
        z_map = lambda hh, i: (CB_AZ + hh, i, 0)
    else:
        n_kv, qw, qblk = 2, 2 * LANES, (2, tq, LANES)
        q_map = lambda hh, i: (CB_CQ // 2 + hh, i, 0)
        k_map = lambda hh, i: (CB_CK + hh, 0, 0)
        v_map = lambda hh, i: (CB_CV + hh, 0, 0)
        z_map = lambda hh, i: (CB_CZ // 2 + hh, i, 0)
    in_specs = [pl.BlockSpec(qblk, q_map),
                pl.BlockSpec((None, s, LANES), k_map),
                pl.BlockSpec((None, s, LANES), v_map),
                pl.BlockSpec(qblk, z_map)]
    args = [hr, hr, hp if diff else hr, hp]
    if diff:
        in_specs += [pl.BlockSpec((None, 4, DA_QK), lambda hh, i: (layer, 0, 0)),
                     pl.BlockSpec((None, 1, LANES), lambda hh, i: (layer, 0, 0))]
        args += [lam_params, subln_g]
    n_q = s // tq
    rb = D_MODEL // (n_kv * n_q)
    in_specs += [pl.BlockSpec((None, rb, D_MODEL), lambda hh, i: (layer, hh * n_q + i, 0))]
    args += [cast_w]
    return pl.pallas_call(
        functools.partial(_pair_attn_kernel, diff, tq, tk, lam_init),
        grid=(n_kv, n_q),
        in_specs=in_specs,
        out_specs=[pl.BlockSpec((tq, qw), lambda hh, i: (i, hh)),
                   pl.BlockSpec((rb, D_MODEL), lambda hh, i: (hh * n_q + i, 0))],
        out_shape=[jax.ShapeDtypeStruct((s, BRANCH_W), BF16),
                   jax.ShapeDtypeStruct((D_MODEL, D_MODEL), BF16)],
        scratch_shapes=[pltpu.VMEM((2 * tq, LANES), BF16),
                        pltpu.VMEM((s // tk, LANES + VT_PAD, tk), BF16),
                        pltpu.VMEM((LANES + VT_PAD, 2 * tq), F32),
                        pltpu.VMEM((2, tk, 2 * tq), F32)],
        compiler_params=_cparams(("arbitrary", "arbitrary")),
        name="diff_attn" if diff else "axial_gqa",
    )(*args)


def _nat_bias(rpb):
    n_layers, n_heads = rpb.shape[:2]
    cols = np.arange(GRID_W)
    c0 = np.clip(cols - NB_COLS // 2, 0, GRID_W - NB_COLS)
    valid = (cols[None, :] >= c0[:, None]) & (cols[None, :] < c0[:, None] + NB_COLS)
    col_sel = (np.arange(2 * NB_COLS - 1)[:, None, None]
               == cols[None, None, :] - cols[None, :, None] + (NB_COLS - 1)).astype(np.float32)
    row_sel = (np.arange(2 * NB_ROWS - 1)[None, None, :]
               == np.arange(NB_ROWS)[:, None, None] + np.arange(NB_ROWS)[None, :, None]).astype(np.float32)
    exact = lax.Precision.HIGHEST
    toep = jnp.einsum('lhdr,rck->lhdck', rpb.astype(F32), col_sel, precision=exact)
    t = jnp.where(valid, toep, NEG_INF)
    bias = jnp.einsum('lhdck,sid->lshcik', t, row_sel, precision=exact)
    return bias.reshape(n_layers, NB_ROWS, n_heads, GRID_W, NB_ROWS * GRID_W)


def _nat_kernel(rows, rps, q_ref, k_ref, v_ref, z_ref, bias_ref, o_ref):
    win = NB_ROWS * GRID_W
    starts, scores = [], []
    for u in range(rps):
        r = pl.program_id(1) * rps + u
        r0 = jnp.clip(r - NB_ROWS // 2, 0, rows - NB_ROWS)
        start = pl.multiple_of(r0 * GRID_W, GRID_W)
        kw = k_ref[pl.ds(start, win), :]
        s = lax.dot_general(q_ref[u * GRID_W:(u + 1) * GRID_W, :], kw, (((1,), (1,)), ((), ())),
                            preferred_element_type=F32)
        scores.append(s * (HEAD_DIM ** -0.5) + bias_ref[r0 - r + (NB_ROWS - 1)])
        starts.append(start)
    probs = []
    for s in scores:
        p = jnp.exp(s - jnp.max(s, axis=-1, keepdims=True))
        probs.append((p.astype(BF16), jnp.sum(p, axis=-1, keepdims=True)))
    for u, (p, l) in enumerate(probs):
        qrows = slice(u * GRID_W, (u + 1) * GRID_W)
        o = jnp.dot(p, v_ref[pl.ds(starts[u], win), :], preferred_element_type=F32) / l
        o_ref[qrows, :] = (o * z_ref[qrows, :].astype(F32)).astype(BF16)


def _nat_attn(hp, bias, layer, rps=64):
    s = hp.shape[1]
    rows = s // GRID_W
    blk = rps * GRID_W
    return pl.pallas_call(
        functools.partial(_nat_kernel, rows, rps),
        grid=(4, rows // rps),
        in_specs=[pl.BlockSpec((None, blk, LANES), lambda hh, r: (CB_BQ + hh, r, 0)),
                  pl.BlockSpec((None, s, LANES), lambda hh, r: (CB_BK + hh, 0, 0)),
                  pl.BlockSpec((None, s, LANES), lambda hh, r: (CB_BV + hh, 0, 0)),
                  pl.BlockSpec((None, blk, LANES), lambda hh, r: (CB_BZ + hh, r, 0)),
                  pl.BlockSpec((None, NB_ROWS, None, GRID_W, NB_ROWS * GRID_W),
                               lambda hh, r: (layer, 0, hh, 0, 0))],
        out_specs=pl.BlockSpec((blk, LANES), lambda hh, r: (r, hh)),
        out_shape=jax.ShapeDtypeStruct((s, BRANCH_W), BF16),
        compiler_params=_cparams(("arbitrary", "arbitrary")),
        name="nat_attn",
    )(hp, hp, hp, hp, bias)


def _dil_masks(tq, rates):
    row = np.arange(tq)[:, None]

    def band(col, rate):
        d = col - row
        return np.where((d >= 0) & (d <= 2 * DIL_SIDE * rate) & (d % rate == 0), 0.0, NEG_INF).astype(np.float32)

    width0 = tq + 2 * DIL_SIDE
    tables = [np.stack([band(np.arange(width0)[None, :] + dlt, 1) for dlt in (DIL_SIDE, 0, -DIL_SIDE)])]
    assert rates[0] == 1
    for rate in rates[1:]:
        reach = DIL_SIDE * rate
        cols = np.arange(-reach, tq + 3 * reach)[None, :]
        full = band(cols, rate)
        tables.append(full.reshape(tq, -1, LANES).transpose(1, 0, 2))
    return [jnp.asarray(t) for t in tables]


def _dil_class_kernel(tq, rate, q_ref, k_ref, v_ref, b_ref, o_ref, lse_ref):
    length = q_ref.shape[0] // rate
    width = tq + 2 * DIL_SIDE
    for a in range(rate):
        cls = pl.ds(a, length, stride=rate)
        qa = q_ref[cls, :].astype(BF16)
        ka = k_ref[cls, :].astype(BF16)
        va = v_ref[cls, :].astype(BF16)
        scored = []
        for i0 in range(0, length, tq):
            ws = min(max(i0 - DIL_SIDE, 0), length - width)
            delta = ws - i0 + DIL_SIDE
            s = lax.dot_general(qa[i0:i0 + tq], ka[ws:ws + width], (((1,), (1,)), ((), ())),
                                preferred_element_type=F32)
            scored.append((i0, ws, s + b_ref[0 if delta > 0 else (2 if delta < 0 else 1)]))
        stats = []
        for i0, ws, s in scored:
            m = jnp.max(s, axis=-1, keepdims=True)
            p = jnp.exp2(s - m)
            stats.append((i0, ws, p.astype(BF16), m, jnp.sum(p, axis=-1, keepdims=True)))
        for i0, ws, p, m, l in stats:
            out_rows = pl.ds(a + rate * i0, tq, stride=rate)
            o_ref[out_rows, :] = jnp.dot(p, va[ws:ws + width], preferred_element_type=F32) / l
            lse_ref[out_rows, :] = jnp.broadcast_to(m + jnp.log2(l), (tq, LANES))


def _dil_kernel(tq, nsub, q0_ref, q1_ref, k0_ref, k1_ref, v0_ref, v1_ref, z_ref, b0_ref, b1_ref,
                o2_ref, lse2_ref, o_ref):
    seq = k0_ref.shape[0]
    q_refs, k_refs, v_refs, b_refs = (q0_ref, q1_ref), (k0_ref, k1_ref), (v0_ref, v1_ref), (b0_ref, b1_ref)
    scores, wins = [], []
    for u in range(nsub):
        t0 = (pl.program_id(1) * nsub + u) * tq
        rows = slice(u * tq, (u + 1) * tq)
        su, wu = [], []
        for rate, q_ref, k_ref, b_ref in zip(DIL_RATES, q_refs, k_refs, b_refs):
            reach = DIL_SIDE * rate
            width = tq + 2 * reach
            start = pl.multiple_of(jnp.clip(t0 - reach, 0, seq - width), DIL_SIDE)
            delta = start - t0 + reach
            s = lax.dot_general(q_ref[rows, :], k_ref[pl.ds(start, width), :], (((1,), (1,)), ((), ())),
                                preferred_element_type=F32)
            if rate == 1:
                s = s + b_ref[jnp.where(delta > 0, 0, jnp.where(delta < 0, 2, 1))]
            else:
                tile0 = (delta + reach) // LANES
                s = jnp.concatenate([s[:, j * LANES:(j + 1) * LANES] + b_ref[tile0 + j]
                                     for j in range(width // LANES)], axis=1)
            su.append(s)
            wu.append((start, width))
        scores.append(su)
        wins.append(wu)
    probs, dens, tops = [], [], []
    for su in scores:
        m = su[0].max(axis=-1, keepdims=True)
        for s in su[1:]:
            m = jnp.maximum(m, s.max(axis=-1, keepdims=True))
        pu = [jnp.exp2(s - m) for s in su]
        l = pu[0].sum(axis=-1, keepdims=True)
        for p in pu[1:]:
            l = l + p.sum(axis=-1, keepdims=True)
        probs.append([p.astype(BF16) for p in pu])
        dens.append(l)
        tops.append(m)
    for u, (pu, wu, l, m) in enumerate(zip(probs, wins, dens, tops)):
        rows = slice(u * tq, (u + 1) * tq)
        o = None
        for p, (start, width), v_ref in zip(pu, wu, v_refs):
            pv = jnp.dot(p, v_ref[pl.ds(start, width), :], preferred_element_type=F32)
            o = pv if o is None else o + pv
        lse2 = lse2_ref[rows, :1]
        top = jnp.maximum(m, lse2)
        w01 = jnp.exp2(m - top)
        w2 = jnp.exp2(lse2 - top)
        mixed = (w01 * o + w2 * o2_ref[rows, :]) / (w01 * l + w2)
        o_ref[rows, :] = (mixed * z_ref[rows, :].astype(F32)).astype(BF16)


def _dil_class_attn(hw, mask, tq=128):
    s = hw.shape[1]
    head = lambda cb: pl.BlockSpec((None, s, LANES), lambda hh: (cb + hh, 0, 0))
    return pl.pallas_call(
        functools.partial(_dil_class_kernel, tq, DIL_RATES[-1]),
        grid=(4,),
        in_specs=[head(CB_WQ), head(CB_WK), head(CB_WV), pl.BlockSpec(mask.shape, lambda hh: (0, 0, 0))],
        out_specs=[head(0), head(0)],
        out_shape=[jax.ShapeDtypeStruct((4, s, LANES), F32), jax.ShapeDtypeStruct((4, s, LANES), F32)],
        compiler_params=_cparams(("arbitrary",)),
        name="dil_class_attn",
    )(hw, hw, hw, mask)


def _dil_attn(hr, hp, hw, tq=128, nsub=32):
    s = hr.shape[1]
    n_dense = len(DIL_RATES) - 1
    masks = _dil_masks(tq, DIL_RATES[:n_dense])
    o2, lse2 = _dil_class_attn(hw, masks[0], tq)
    m_specs = [pl.BlockSpec(m.shape, lambda hh, i: (0, 0, 0)) for m in masks]
    blk = tq * nsub
    qs = [pl.BlockSpec((None, blk, LANES), functools.partial(lambda g, hh, i: (CB_DQ + 4 * g + hh, i, 0), g))
          for g in range(n_dense)]
    ks = [pl.BlockSpec((None, s, LANES), functools.partial(lambda g, hh, i: (CB_DK + 4 * g + hh, 0, 0), g))
          for g in range(n_dense)]
    vs = [pl.BlockSpec((None, s, LANES), functools.partial(lambda g, hh, i: (CB_DV + 4 * g + hh, 0, 0), g))
          for g in range(n_dense)]
    row_blk = pl.BlockSpec((None, blk, LANES), lambda hh, i: (hh, i, 0))
    return pl.pallas_call(
        functools.partial(_dil_kernel, tq, nsub),
        grid=(4, s // blk),
        in_specs=(qs + ks + vs + [pl.BlockSpec((None, blk, LANES), lambda hh, i: (CB_DZ + hh, i, 0))]
                  + m_specs + [row_blk, row_blk]),
        out_specs=pl.BlockSpec((blk, LANES), lambda hh, i: (i, hh)),
        out_shape=jax.ShapeDtypeStruct((s, BRANCH_W), BF16),
        compiler_params=_cparams(("arbitrary", "arbitrary")),
        name="dil_attn",
    )(*([hr] * (2 * n_dense)), *([hp] * (n_dense + 1)), *masks, o2, lse2)


def _merge_kernel(ya_ref, yb_ref, yc_ref, yd_ref, ga_ref, gb_ref, gc_ref, gd_ref, wb_ref, o_ref, wbf_ref):
    @pl.when(pl.program_id(1) == 0)
    def _():
        wbf_ref[...] = wb_ref[...].astype(BF16)

    tsub = min(256, o_ref.shape[0])
    for r0 in range(0, o_ref.shape[0], tsub):
        rows = slice(r0, r0 + tsub)
        acc = None
        for n, (y_ref, g_ref) in enumerate(zip((ya_ref, yb_ref, yc_ref, yd_ref),
                                               (ga_ref, gb_ref, gc_ref, gd_ref))):
            proj = jnp.dot(y_ref[rows, :], wbf_ref[n], preferred_element_type=F32)
            term = proj * g_ref[rows, :].astype(F32)
            acc = term if acc is None else acc + term
        o_ref[rows, :] = acc.astype(BF16)


def _merge(ys, hg, w_branch, layer, tm=2048):
    s = hg.shape[0]
    tm = math.gcd(tm, s)
    n_col = D_MODEL // TN
    y_specs = [pl.BlockSpec((tm, BRANCH_W), lambda j, i: (i, 0)) for _ in range(N_BRANCH)]
    g_specs = [pl.BlockSpec((tm, TN), functools.partial(lambda n, j, i: (i, n_col * n + j), n))
               for n in range(N_BRANCH)]
    return pl.pallas_call(
        _merge_kernel,
        grid=(n_col, s // tm),
        in_specs=y_specs + g_specs + [pl.BlockSpec((None, N_BRANCH, BRANCH_W, TN),
                                                   lambda j, i: (layer, 0, 0, j))],
        out_specs=pl.BlockSpec((tm, TN), lambda j, i: (i, j)),
        out_shape=jax.ShapeDtypeStruct((s, D_MODEL), BF16),
        scratch_shapes=[pltpu.VMEM((N_BRANCH, BRANCH_W, TN), BF16)],
        compiler_params=_cparams(("arbitrary", "arbitrary")),
        name="branch_merge",
    )(*ys, hg, hg, hg, hg, w_branch)


def _out_ln_kernel(alpha, tsub, m_ref, w_ref, x_ref, g_ref, b_ref, y_ref, ybf_ref, acc_ref):
    n_sub = m_ref.shape[0] // tsub

    def matmul(u):
        acc_ref[u % 2] = jnp.dot(m_ref[u * tsub:(u + 1) * tsub, :], w_ref[...], preferred_element_type=F32)

    matmul(0)
    for u in range(n_sub):
        if u + 1 < n_sub:
            matmul(u + 1)
        rows = slice(u * tsub, (u + 1) * tsub)
        y = _ln_rows(alpha * x_ref[rows, :] + acc_ref[u % 2], g_ref[...], b_ref[...])
        y_ref[rows, :] = y
        ybf_ref[rows, :] = y.astype(BF16)


def _out_ln(merged, wo_bf, layer, x, g, b, alpha, tm=512, tsub=128):
    s, d = x.shape
    return pl.pallas_call(
        functools.partial(_out_ln_kernel, alpha, tsub),
        grid=(s // tm,),
        in_specs=[pl.BlockSpec((tm, d), lambda i: (i, 0)),
                  pl.BlockSpec((None, d, d), lambda i: (layer, 0, 0)),
                  pl.BlockSpec((tm, d), lambda i: (i, 0)),
                  pl.BlockSpec((None, 1, d), lambda i: (layer, 0, 0)),
                  pl.BlockSpec((None, 1, d), lambda i: (layer, 0, 0))],
        out_specs=[pl.BlockSpec((tm, d), lambda i: (i, 0)),
                   pl.BlockSpec((tm, d), lambda i: (i, 0))],
        out_shape=[jax.ShapeDtypeStruct((s, d), F32), jax.ShapeDtypeStruct((s, d), BF16)],
        scratch_shapes=[pltpu.VMEM((2, tsub, d), F32)],
        compiler_params=_cparams(("arbitrary",)),
        name="out_ln",
    )(merged, wo_bf, x, g, b)


def _tail_kernel(alpha, tsub, ya_ref, yb_ref, yc_ref, yd_ref, g_ref, wb_ref, wo_ref, x_ref, lg_ref, lb_ref,
                 y_ref, ybf_ref, m_ref, acc_ref):
    n_sub = x_ref.shape[0] // tsub

    def project(u):
        rows = slice(u * tsub, (u + 1) * tsub)
        for c0 in range(0, D_MODEL, TN):
            acc = None
            for n, y_in in enumerate((ya_ref, yb_ref, yc_ref, yd_ref)):
                proj = jnp.dot(y_in[rows, :], wb_ref[n, :, c0:c0 + TN], preferred_element_type=F32)
                term = proj * g_ref[rows, n * D_MODEL + c0:n * D_MODEL + c0 + TN].astype(F32)
                acc = term if acc is None else acc + term
            m_ref[rows, c0:c0 + TN] = acc.astype(BF16)
        acc_ref[u % 2] = jnp.dot(m_ref[rows, :], wo_ref[...], preferred_element_type=F32)

    project(0)
    for u in range(n_sub):
        if u + 1 < n_sub:
            project(u + 1)
        rows = slice(u * tsub, (u + 1) * tsub)
        y = _ln_rows(alpha * x_ref[rows, :] + acc_ref[u % 2], lg_ref[...], lb_ref[...])
        y_ref[rows, :] = y
        ybf_ref[rows, :] = y.astype(BF16)


def _tail(ys, hg, wb_bf, wo_bf, layer, x, g, b, alpha, tm=256, tsub=128):
    s, d = x.shape
    resident = pl.Buffered(1)
    return pl.pallas_call(
        functools.partial(_tail_kernel, alpha, tsub),
        grid=(s // tm,),
        in_specs=[pl.BlockSpec((tm, BRANCH_W), lambda i: (i, 0)) for _ in range(N_BRANCH)]
        + [pl.BlockSpec((tm, N_BRANCH * d), lambda i: (i, 0)),
           pl.BlockSpec((N_BRANCH, BRANCH_W, d), lambda i: (0, 0, 0), pipeline_mode=resident),
           pl.BlockSpec((d, d), lambda i: (0, 0), pipeline_mode=resident),
           pl.BlockSpec((tm, d), lambda i: (i, 0)),
           pl.BlockSpec((None, 1, d), lambda i: (layer, 0, 0)),
           pl.BlockSpec((None, 1, d), lambda i: (layer, 0, 0))],
        out_specs=[pl.BlockSpec((tm, d), lambda i: (i, 0)),
                   pl.BlockSpec((tm, d), lambda i: (i, 0))],
        out_shape=[jax.ShapeDtypeStruct((s, d), F32), jax.ShapeDtypeStruct((s, d), BF16)],
        scratch_shapes=[pltpu.VMEM((tm, d), BF16), pltpu.VMEM((2, tsub, d), F32)],
        compiler_params=_cparams(("arbitrary",)),
        name="merge_out_ln",
    )(*ys, hg, wb_bf, wo_bf, x, g, b)


def kernel(x, emb_ln_g, emb_ln_b, w_in, b_gate, diff_lambda, diff_subln_g, nat_rpb,
           gqa_q_norm_g, gqa_k_norm_g, w_branch, w_out, ln_g, ln_b):
    batch, seq, d = x.shape
    assert batch == 1 and d == D_MODEL and w_in.shape[-1] == D_IN
    depth = w_in.shape[0]
    alpha = (2 * depth) ** 0.25
    tabs = _rope_tables(seq)
    nat_bias = _nat_bias(nat_rpb)
    w_branch2 = w_branch.reshape(depth, N_BRANCH * BRANCH_W, d)
    gains = jnp.stack([gqa_q_norm_g, gqa_k_norm_g], axis=1).reshape(depth, 2, 1, LANES)
    b_gate3 = b_gate.reshape(depth, 1, -1)
    subln3 = diff_subln_g.reshape(depth, 1, LANES)
    ln_g3, ln_b3 = ln_g.reshape(depth, 1, d), ln_b.reshape(depth, 1, d)
    xf, xbf = _embed_ln(x[0], emb_ln_g, emb_ln_b)
    for l in range(depth):
        lam_init = 0.8 - 0.6 * math.exp(-0.3 * l)
        hr, hp, hg, hw = _inproj(xbf, w_in, b_gate3, l, tabs, gains)
        ya, wb_bf = _pair_attn(hr, hp, True, w_branch2, l, lam_init, diff_lambda, subln3)
        yb = _nat_attn(hp, nat_bias, l)
        yc, wo_bf = _pair_attn(hr, hp, False, w_out, l)
        yd = _dil_attn(hr, hp, hw)
        xf, xbf = _tail((ya, yb, yc, yd), hg, wb_bf.reshape(N_BRANCH, BRANCH_W, d), wo_bf, l, xf,
                        ln_g3, ln_b3, alpha)
    return xf[None]
```

```python
import functools
import math

import numpy as np
import jax
import jax.numpy as jnp
from jax import lax
from jax.experimental import pallas as pl
from jax.experimental.pallas import tpu as pltpu

F32 = jnp.float32
BF16 = jnp.bfloat16

D_MODEL = 2048
GRID_W = 64
HEAD_DIM = 128
N_BRANCH = 4
BRANCH_W = D_MODEL // 4
DA_QK = HEAD_DIM // 2
NB_ROWS = 8
NB_COLS = 16
AXIAL_THETA = 10000.0
DIL_RATES = (1, 4, 16)
DIL_SIDE = 64
ROPE_THETA = 500000.0
LN_EPS = 1e-5
RMS_EPS = 1e-6
NEG_INF = -1e30
LOG2E = math.log2(math.e)

LANES = 128
VT_PAD = 16
TN = 512
D_IN = 18944

ROT_TILES = (0, 1, 8, 9, 11, 12, 14, 15)
PLAIN_TILES = (2, 3, 4, 5, 6, 7, 10, 17, 18, 20)
WIDE_TILES = (13, 16, 19)
GATE_TILE0, N_GATE_TILES = 21, 16
ROT_VARIANT = (0, 1, 2, 3, 4, 4, 5, 5)
ROT_VARIANT_DQ, ROT_VARIANT_DK = 4, 5
CB_AQ, CB_AK, CB_CQ, CB_CK, CB_CV, CB_DQ, CB_DK = 0, 4, 8, 12, 14, 16, 24
CB_AV, CB_AZ, CB_BQ, CB_BK, CB_BV, CB_BZ, CB_CZ, CB_DV, CB_DZ = 0, 4, 8, 12, 16, 20, 24, 28, 36
CB_WQ, CB_WK, CB_WV = 0, 4, 8
PLAIN_SILU = (1, 5, 6, 9)

VMEM_LIMIT = 56 * 1024 * 1024


def _cparams(sem):
    return pltpu.CompilerParams(dimension_semantics=sem, vmem_limit_bytes=VMEM_LIMIT)


def _select(j, values):
    out = values[-1]
    for t in range(len(values) - 2, -1, -1):
        out = jnp.where(j == t, values[t], out)
    return out


def _any_of(j, tiles):
    c = j == tiles[0]
    for t in tiles[1:]:
        c = c | (j == t)
    return c


def _ln_rows(x, g, b):
    mu = jnp.mean(x, axis=-1, keepdims=True)
    xc = x - mu
    var = jnp.mean(xc * xc, axis=-1, keepdims=True)
    return xc * lax.rsqrt(var + LN_EPS) * g + b


def _embed_ln_kernel(x_ref, g_ref, b_ref, y_ref, ybf_ref):
    y = _ln_rows(x_ref[...], g_ref[...], b_ref[...])
    y_ref[...] = y
    ybf_ref[...] = y.astype(BF16)


def _embed_ln(x, g, b, tm=512):
    s, d = x.shape
    return pl.pallas_call(
        _embed_ln_kernel,
        grid=(s // tm,),
        in_specs=[pl.BlockSpec((tm, d), lambda i: (i, 0)),
                  pl.BlockSpec((1, d), lambda i: (0, 0)),
                  pl.BlockSpec((1, d), lambda i: (0, 0))],
        out_specs=[pl.BlockSpec((tm, d), lambda i: (i, 0)),
                   pl.BlockSpec((tm, d), lambda i: (i, 0))],
        out_shape=[jax.ShapeDtypeStruct((s, d), F32), jax.ShapeDtypeStruct((s, d), BF16)],
        compiler_params=_cparams(("arbitrary",)),
        name="embed_ln",
    )(x, g.reshape(1, d), b.reshape(1, d))


def _rope_tables(seq):
    t = np.arange(seq)

    def cs(pos, dim, theta):
        half = dim // 2
        inv = np.power(float(theta), -np.arange(half, dtype=np.float64) * 2.0 / dim)
        ang = pos.astype(np.float64)[:, None] * inv[None, :]
        return jnp.asarray(np.cos(ang), F32), jnp.asarray(np.sin(ang), F32)

    def group(cos, sin, width):
        half = cos.shape[1]
        pad = width - 2 * half
        one = jnp.ones((seq, pad), F32)
        zero = jnp.zeros((seq, pad), F32)
        zh = jnp.zeros((seq, half), F32)
        return (jnp.concatenate([cos, cos, one], 1),
                jnp.concatenate([zh, sin, zero], 1),
                jnp.concatenate([-sin, zh, zero], 1))

    ca, sa = cs(t, DA_QK // 4, ROPE_THETA)
    va = jnp.stack([jnp.concatenate([p, p], 1) for p in group(ca, sa, DA_QK)])
    cr, sr = cs(t // GRID_W, HEAD_DIM // 2, AXIAL_THETA)
    cc, sc = cs(t % GRID_W, HEAD_DIM // 2, AXIAL_THETA)
    vc = jnp.stack([jnp.concatenate([p, q], 1)
                    for p, q in zip(group(cr, sr, HEAD_DIM // 2), group(cc, sc, HEAD_DIM // 2))])
    cd, sd = cs(t, HEAD_DIM // 4, ROPE_THETA)
    vd = jnp.stack(group(cd, sd, HEAD_DIM))
    qa = DA_QK ** -0.5 * LOG2E
    qh = HEAD_DIM ** -0.5 * LOG2E
    return jnp.stack([va * qa, va, vc * qh, vc, vd * qh, vd])


ROT_HALF_A, ROT_HALF_C, ROT_HALF_D = DA_QK // 8, HEAD_DIM // 4, HEAD_DIM // 8


def _proj_kernel(tsub, n_sub, make_branches, n_extra, x_ref, w_ref, *rest):
    extra = rest[:n_extra]
    o_ref, wbf_ref, acc_ref = rest[n_extra:]
    j = pl.program_id(0)
    n_pass = x_ref.shape[0] // (n_sub * tsub)

    @pl.when(pl.program_id(1) == 0)
    def _():
        wbf_ref[...] = w_ref[...].astype(BF16)

    def pipeline(epilogue):
        def one_pass(h, carry):
            def rows_of(u):
                return pl.ds(pl.multiple_of((h * n_sub + u) * tsub, tsub), tsub)

            def matmul(u):
                acc_ref[u % 2] = jnp.dot(x_ref[rows_of(u), :], wbf_ref[...],
                                         preferred_element_type=F32)

            matmul(0)
            for u in range(n_sub):
                if u + 1 < n_sub:
                    matmul(u + 1)
                for g in range(TN // LANES):
                    lanes = slice(g * LANES, (g + 1) * LANES)
                    y = epilogue(acc_ref[u % 2, :, lanes], rows_of(u), g).astype(o_ref.dtype)
                    if len(o_ref.shape) == 3:
                        o_ref[g, rows_of(u), :] = y
                    else:
                        o_ref[rows_of(u), lanes] = y
            return carry

        lax.fori_loop(0, n_pass, one_pass, 0)

    for cond, epilogue in make_branches(*extra):
        if cond is None:
            pipeline(epilogue)
        else:
            pl.when(cond(j))(functools.partial(pipeline, epilogue))


def _proj(name, xbf, w_in, layer, src_tiles, make_branches, extra, extra_specs, head_major,
          tm=2048, tsub=256, n_sub=4, dtype=BF16):
    s, d = xbf.shape
    tm = math.gcd(tm, s)
    n_sub = min(n_sub, tm // tsub)
    assert tm % (tsub * n_sub) == 0
    n_tiles = len(src_tiles)
    contiguous = src_tiles == tuple(range(src_tiles[0], src_tiles[0] + n_tiles))
    src = (lambda j: src_tiles[0] + j) if contiguous else (lambda j: _select(j, src_tiles))
    if head_major:
        out_spec = pl.BlockSpec((TN // LANES, tm, LANES), lambda j, i: (j, i, 0))
        out_shape = jax.ShapeDtypeStruct((n_tiles * (TN // LANES), s, LANES), dtype)
    else:
        out_spec = pl.BlockSpec((tm, TN), lambda j, i: (i, j))
        out_shape = jax.ShapeDtypeStruct((s, n_tiles * TN), dtype)
    return pl.pallas_call(
        functools.partial(_proj_kernel, tsub, n_sub, make_branches, len(extra)),
        grid=(n_tiles, s // tm),
        in_specs=[pl.BlockSpec((tm, d), lambda j, i: (i, 0)),
                  pl.BlockSpec((None, d, TN), lambda j, i: (layer, 0, src(j)))] + extra_specs(tm),
        out_specs=out_spec,
        out_shape=out_shape,
        scratch_shapes=[pltpu.VMEM((d, TN), BF16), pltpu.VMEM((2, tsub, TN), F32)],
        compiler_params=_cparams(("arbitrary", "arbitrary")),
        name=name,
    )(xbf, w_in, *extra)


def _rot_branches(tab_ref, gn_ref):
    def rot(a, rows, half):
        return (a * tab_ref[0, rows, :] + pltpu.roll(a, half, 1) * tab_ref[1, rows, :]
                + pltpu.roll(a, LANES - half, 1) * tab_ref[2, rows, :])

    def rms(a, gain):
        ms = jnp.mean(a * a, axis=-1, keepdims=True)
        return a * lax.rsqrt(ms + RMS_EPS) * gain

    return [(lambda j: j <= 1, lambda a, rows, g: rot(a, rows, ROT_HALF_A)),
            (lambda j: j == 2, lambda a, rows, g: rot(rms(a, gn_ref[0]), rows, ROT_HALF_C)),
            (lambda j: j == 3, lambda a, rows, g: rot(rms(a, gn_ref[1]), rows, ROT_HALF_C) if g < 2 else a),
            (lambda j: j >= 4, lambda a, rows, g: rot(a, rows, ROT_HALF_D))]


def _plain_branches():
    return [(lambda j: _any_of(j, PLAIN_SILU), lambda a, rows, g: a * jax.nn.sigmoid(a)),
            (lambda j: jnp.logical_not(_any_of(j, PLAIN_SILU)), lambda a, rows, g: a)]


def _gate_branches(bg_ref):
    return [(None, lambda a, rows, g: jax.nn.sigmoid(a + bg_ref[:, g * LANES:(g + 1) * LANES]))]


def _wide_branches(tab_ref):
    def rot(a, rows):
        return (a * tab_ref[0, rows, :] + pltpu.roll(a, ROT_HALF_D, 1) * tab_ref[1, rows, :]
                + pltpu.roll(a, LANES - ROT_HALF_D, 1) * tab_ref[2, rows, :])

    return [(lambda j: j <= 1, lambda a, rows, g: rot(a, rows)),
            (lambda j: j == 2, lambda a, rows, g: a)]


def _inproj(xbf, w_in, b_gate, layer, tabs, gains):
    hr = _proj("inproj_rot", xbf, w_in, layer, ROT_TILES, _rot_branches, (tabs, gains),
               lambda tm: [pl.BlockSpec((None, 3, tm, LANES), lambda j, i: (_select(j, ROT_VARIANT), 0, i, 0)),
                           pl.BlockSpec((None, 2, 1, LANES), lambda j, i: (layer, 0, 0, 0))], True)
    hp = _proj("inproj_plain", xbf, w_in, layer, PLAIN_TILES, _plain_branches, (), lambda tm: [], True,
               n_sub=8)
    hg = _proj("inproj_gate", xbf, w_in, layer, tuple(range(GATE_TILE0, GATE_TILE0 + N_GATE_TILES)),
               _gate_branches, (b_gate,),
               lambda tm: [pl.BlockSpec((None, 1, TN), lambda j, i: (layer, 0, j))], False, n_sub=8)
    hw = _proj("inproj_wide", xbf, w_in, layer, WIDE_TILES, _wide_branches, (tabs,),
               lambda tm: [pl.BlockSpec((None, 3, tm, LANES),
                                        lambda j, i: (jnp.where(j == 0, ROT_VARIANT_DQ, ROT_VARIANT_DK), 0, i, 0))],
               True, n_sub=8, dtype=F32)
    return hr, hp, hg, hw


def _pair_attn_kernel(diff, tq, tk, lam_init, *refs):
    if diff:
        q_ref, k_ref, v_ref, z_ref, lp_ref, sg_ref, o_ref, q2_ref, vt_ref, acc_ref, s_ref = refs
    else:
        q_ref, k_ref, v_ref, z_ref, o_ref, q2_ref, vt_ref, acc_ref, s_ref = refs
    seq = k_ref.shape[0]
    nq = 2 * tq

    @pl.when(pl.program_id(1) == 0)
    def _():
        for c in range(seq // tk):
            vt_ref[c, :LANES, :] = v_ref[c * tk:(c + 1) * tk, :].astype(F32).T.astype(BF16)
            vt_ref[c, LANES:, :] = jnp.ones((VT_PAD, tk), BF16)

    if diff:
        q = q_ref[...]
        lane = lax.broadcasted_iota(jnp.int32, q.shape, 1)
        zero = jnp.zeros_like(q)
        q2_ref[:tq, :] = jnp.where(lane < DA_QK, q, zero)
        q2_ref[tq:, :] = jnp.where(lane >= DA_QK, q, zero)
    else:
        q2_ref[:tq, :] = q_ref[0]
        q2_ref[tq:, :] = q_ref[1]
    acc_ref[...] = jnp.zeros(acc_ref.shape, F32)

    n_chunks = seq // tk

    def scores(c, slot):
        start = pl.multiple_of(c * tk, tk)
        s = lax.dot_general(k_ref[pl.ds(start, tk), :], q2_ref[...], (((1,), (1,)), ((), ())),
                            preferred_element_type=F32)
        s_ref[slot] = s
        return jnp.max(s, axis=0, keepdims=True)

    def fold(c, slot, m_prev, m_chunk):
        m_new = jnp.maximum(m_prev, m_chunk)
        alpha = jnp.exp2(m_prev - m_new)
        p = jnp.exp2(s_ref[slot] - m_new)
        acc_ref[...] = alpha * acc_ref[...] + jnp.dot(vt_ref[c], p.astype(BF16),
                                                      preferred_element_type=F32)
        return m_new

    unroll = 4 if n_chunks % 4 == 0 else 2

    def group(base, carry, last):
        m, m_chunk = carry
        for u in range(unroll):
            m_next = m_chunk if (last and u == unroll - 1) else scores(base + u + 1, (u + 1) % 2)
            m = fold(base + u, u % 2, m, m_chunk)
            m_chunk = m_next
        return m, m_chunk

    carry = (jnp.full((1, nq), NEG_INF, F32), scores(0, 0))
    n_groups = n_chunks // unroll
    if n_groups > 1:
        carry = lax.fori_loop(0, n_groups - 1, lambda j, cc: group(j * unroll, cc, False), carry)
    group((n_groups - 1) * unroll, carry, True)

    acc = acc_ref[...]
    o = (acc[:LANES] / acc[LANES:LANES + 1]).T
    if diff:
        lp = lp_ref[...]
        lam = (jnp.exp(jnp.sum(lp[0:1] * lp[1:2], keepdims=True))
               - jnp.exp(jnp.sum(lp[2:3] * lp[3:4], keepdims=True)) + lam_init)
        dlt = o[:tq] - lam * o[tq:]
        ms = jnp.mean(dlt * dlt, axis=-1, keepdims=True)
        y = dlt * lax.rsqrt(ms + RMS_EPS) * sg_ref[...] * (1.0 - lam_init)
        o_ref[...] = (y * z_ref[...].astype(F32)).astype(BF16)
    else:
        o_ref[:, :LANES] = (o[:tq] * z_ref[0].astype(F32)).astype(BF16)
        o_ref[:, LANES:] = (o[tq:] * z_ref[1].astype(F32)).astype(BF16)


def _pair_attn(hr, hp, diff, layer=0, lam_init=0.0, lam_params=None, subln_g=None, tq=512, tk=512):
    s = hr.shape[1]
    if diff:
        n_kv, qw, qblk = 4, LANES, (None, tq, LANES)
        q_map = lambda hh, i: (CB_AQ + hh, i, 0)
        k_map = lambda hh, i: (CB_AK + hh, 0, 0)
        v_map = lambda hh, i: (CB_AV + hh, 0, 0)
        z_map = lambda hh, i: (CB_AZ + hh, i, 0)
    else:
        n_kv, qw, qblk = 2, 2 * LANES, (2, tq, LANES)
        q_map = lambda hh, i: (CB_CQ // 2 + hh, i, 0)
        k_map = lambda hh, i: (CB_CK + hh, 0, 0)
        v_map = lambda hh, i: (CB_CV + hh, 0, 0)
        z_map = lambda hh, i: (CB_CZ // 2 + hh, i, 0)
    in_specs = [pl.BlockSpec(qblk, q_map),
                pl.BlockSpec((None, s, LANES), k_map),
                pl.BlockSpec((None, s, LANES), v_map),
                pl.BlockSpec(qblk, z_map)]
    args = [hr, hr, hp if diff else hr, hp]
    if diff:
        in_specs += [pl.BlockSpec((None, 4, DA_QK), lambda hh, i: (layer, 0, 0)),
                     pl.BlockSpec((None, 1, LANES), lambda hh, i: (layer, 0, 0))]
        args += [lam_params, subln_g]
    return pl.pallas_call(
        functools.partial(_pair_attn_kernel, diff, tq, tk, lam_init),
        grid=(n_kv, s // tq),
        in_specs=in_specs,
        out_specs=pl.BlockSpec((tq, qw), lambda hh, i: (i, hh)),
        out_shape=jax.ShapeDtypeStruct((s, BRANCH_W), BF16),
        scratch_shapes=[pltpu.VMEM((2 * tq, LANES), BF16),
                        pltpu.VMEM((s // tk, LANES + VT_PAD, tk), BF16),
                        pltpu.VMEM((LANES + VT_PAD, 2 * tq), F32),
                        pltpu.VMEM((2, tk, 2 * tq), F32)],
        compiler_params=_cparams(("arbitrary", "arbitrary")),
        name="diff_attn" if diff else "axial_gqa",
    )(*args)


def _nat_bias(rpb):
    n_layers, n_heads = rpb.shape[:2]
    cols = np.arange(GRID_W)
    c0 = np.clip(cols - NB_COLS // 2, 0, GRID_W - NB_COLS)
    valid = (cols[None, :] >= c0[:, None]) & (cols[None, :] < c0[:, None] + NB_COLS)
    col_sel = (np.arange(2 * NB_COLS - 1)[:, None, None]
               == cols[None, None, :] - cols[None, :, None] + (NB_COLS - 1)).astype(np.float32)
    row_sel = (np.arange(2 * NB_ROWS - 1)[None, None, :]
               == np.arange(NB_ROWS)[:, None, None] + np.arange(NB_ROWS)[None, :, None]).astype(np.float32)
    exact = lax.Precision.HIGHEST
    toep = jnp.einsum('lhdr,rck->lhdck', rpb.astype(F32), col_sel, precision=exact)
    t = jnp.where(valid, toep, NEG_INF)
    bias = jnp.einsum('lhdck,sid->lshcik', t, row_sel, precision=exact)
    return bias.reshape(n_layers, NB_ROWS, n_heads, GRID_W, NB_ROWS * GRID_W)


def _nat_kernel(rows, rps, q_ref, k_ref, v_ref, z_ref, bias_ref, wc_ref, o_ref, wcb_ref):
    wcb_ref[...] = wc_ref[...].astype(BF16)
    win = NB_ROWS * GRID_W
    starts, scores = [], []
    for u in range(rps):
        r = pl.program_id(1) * rps + u
        r0 = jnp.clip(r - NB_ROWS // 2, 0, rows - NB_ROWS)
        start = pl.multiple_of(r0 * GRID_W, GRID_W)
        kw = k_ref[pl.ds(start, win), :]
        s = lax.dot_general(q_ref[u * GRID_W:(u + 1) * GRID_W, :], kw, (((1,), (1,)), ((), ())),
                            preferred_element_type=F32)
        scores.append(s * (HEAD_DIM ** -0.5) + bias_ref[r0 - r + (NB_ROWS - 1)])
        starts.append(start)
    probs = []
    for s in scores:
        p = jnp.exp(s - jnp.max(s, axis=-1, keepdims=True))
        probs.append((p.astype(BF16), jnp.sum(p, axis=-1, keepdims=True)))
    for u, (p, l) in enumerate(probs):
        qrows = slice(u * GRID_W, (u + 1) * GRID_W)
        o = jnp.dot(p, v_ref[pl.ds(starts[u], win), :], preferred_element_type=F32) / l
        o_ref[qrows, :] = (o * z_ref[qrows, :].astype(F32)).astype(BF16)


def _cast_specs(n_outer, n_inner, layer):
    rb = D_MODEL // (n_outer * n_inner)
    return (pl.BlockSpec((None, rb, D_MODEL), lambda a, b: (layer, a * n_inner + b, 0)),
            pl.BlockSpec((rb, D_MODEL), lambda a, b: (a * n_inner + b, 0)),
            jax.ShapeDtypeStruct((D_MODEL, D_MODEL), BF16))


def _nat_attn(hp, bias, layer, cast_w, rps=64):
    s = hp.shape[1]
    rows = s // GRID_W
    blk = rps * GRID_W
    wc_in, wc_out, wc_shape = _cast_specs(4, rows // rps, layer)
    return pl.pallas_call(
        functools.partial(_nat_kernel, rows, rps),
        grid=(4, rows // rps),
        in_specs=[pl.BlockSpec((None, blk, LANES), lambda hh, r: (CB_BQ + hh, r, 0)),
                  pl.BlockSpec((None, s, LANES), lambda hh, r: (CB_BK + hh, 0, 0)),
                  pl.BlockSpec((None, s, LANES), lambda hh, r: (CB_BV + hh, 0, 0)),
                  pl.BlockSpec((None, blk, LANES), lambda hh, r: (CB_BZ + hh, r, 0)),
                  pl.BlockSpec((None, NB_ROWS, None, GRID_W, NB_ROWS * GRID_W),
                               lambda hh, r: (layer, 0, hh, 0, 0)),
                  wc_in],
        out_specs=[pl.BlockSpec((blk, LANES), lambda hh, r: (r, hh)), wc_out],
        out_shape=[jax.ShapeDtypeStruct((s, BRANCH_W), BF16), wc_shape],
        compiler_params=_cparams(("arbitrary", "arbitrary")),
        name="nat_attn",
    )(hp, hp, hp, hp, bias, cast_w)


def _dil_masks(tq, rates):
    row = np.arange(tq)[:, None]

    def band(col, rate):
        d = col - row
        return np.where((d >= 0) & (d <= 2 * DIL_SIDE * rate) & (d % rate == 0), 0.0, NEG_INF).astype(np.float32)

    width0 = tq + 2 * DIL_SIDE
    tables = [np.stack([band(np.arange(width0)[None, :] + dlt, 1) for dlt in (DIL_SIDE, 0, -DIL_SIDE)])]
    assert rates[0] == 1
    for rate in rates[1:]:
        reach = DIL_SIDE * rate
        cols = np.arange(-reach, tq + 3 * reach)[None, :]
        full = band(cols, rate)
        tables.append(full.reshape(tq, -1, LANES).transpose(1, 0, 2))
    return [jnp.asarray(t) for t in tables]


def _dil_class_kernel(tq, rate, q_ref, k_ref, v_ref, b_ref, o_ref, lse_ref):
    length = q_ref.shape[0] // rate
    width = tq + 2 * DIL_SIDE
    for a in range(rate):
        cls = pl.ds(a, length, stride=rate)
        qa = q_ref[cls, :].astype(BF16)
        ka = k_ref[cls, :].astype(BF16)
        va = v_ref[cls, :].astype(BF16)
        scored = []
        for i0 in range(0, length, tq):
            ws = min(max(i0 - DIL_SIDE, 0), length - width)
            delta = ws - i0 + DIL_SIDE
            s = lax.dot_general(qa[i0:i0 + tq], ka[ws:ws + width], (((1,), (1,)), ((), ())),
                                preferred_element_type=F32)
            scored.append((i0, ws, s + b_ref[0 if delta > 0 else (2 if delta < 0 else 1)]))
        stats = []
        for i0, ws, s in scored:
            m = jnp.max(s, axis=-1, keepdims=True)
            p = jnp.exp2(s - m)
            stats.append((i0, ws, p.astype(BF16), m, jnp.sum(p, axis=-1, keepdims=True)))
        for i0, ws, p, m, l in stats:
            out_rows = pl.ds(a + rate * i0, tq, stride=rate)
            o_ref[out_rows, :] = jnp.dot(p, va[ws:ws + width], preferred_element_type=F32) / l
            lse_ref[out_rows, :] = jnp.broadcast_to(m + jnp.log2(l), (tq, LANES))


def _dil_kernel(tq, nsub, q0_ref, q1_ref, k0_ref, k1_ref, v0_ref, v1_ref, z_ref, b0_ref, b1_ref,
                o2_ref, lse2_ref, wc_ref, o_ref, wcb_ref):
    seq = k0_ref.shape[0]
    wcb_ref[...] = wc_ref[...].astype(BF16)
    q_refs, k_refs, v_refs, b_refs = (q0_ref, q1_ref), (k0_ref, k1_ref), (v0_ref, v1_ref), (b0_ref, b1_ref)
    scores, wins = [], []
    for u in range(nsub):
        t0 = (pl.program_id(1) * nsub + u) * tq
        rows = slice(u * tq, (u + 1) * tq)
        su, wu = [], []
        for rate, q_ref, k_ref, b_ref in zip(DIL_RATES, q_refs, k_refs, b_refs):
            reach = DIL_SIDE * rate
            width = tq + 2 * reach
            start = pl.multiple_of(jnp.clip(t0 - reach, 0, seq - width), DIL_SIDE)
            delta = start - t0 + reach
            s = lax.dot_general(q_ref[rows, :], k_ref[pl.ds(start, width), :], (((1,), (1,)), ((), ())),
                                preferred_element_type=F32)
            if rate == 1:
                s = s + b_ref[jnp.where(delta > 0, 0, jnp.where(delta < 0, 2, 1))]
            else:
                tile0 = (delta + reach) // LANES
                s = jnp.concatenate([s[:, j * LANES:(j + 1) * LANES] + b_ref[tile0 + j]
                                     for j in range(width // LANES)], axis=1)
            su.append(s)
            wu.append((start, width))
        scores.append(su)
        wins.append(wu)
    probs, dens, tops = [], [], []
    for su in scores:
        m = su[0].max(axis=-1, keepdims=True)
        for s in su[1:]:
            m = jnp.maximum(m, s.max(axis=-1, keepdims=True))
        pu = [jnp.exp2(s - m) for s in su]
        l = pu[0].sum(axis=-1, keepdims=True)
        for p in pu[1:]:
            l = l + p.sum(axis=-1, keepdims=True)
        probs.append([p.astype(BF16) for p in pu])
        dens.append(l)
        tops.append(m)
    for u, (pu, wu, l, m) in enumerate(zip(probs, wins, dens, tops)):
        rows = slice(u * tq, (u + 1) * tq)
        o = None
        for p, (start, width), v_ref in zip(pu, wu, v_refs):
            pv = jnp.dot(p, v_ref[pl.ds(start, width), :], preferred_element_type=F32)
            o = pv if o is None else o + pv
        lse2 = lse2_ref[rows, :1]
        top = jnp.maximum(m, lse2)
        w01 = jnp.exp2(m - top)
        w2 = jnp.exp2(lse2 - top)
        mixed = (w01 * o + w2 * o2_ref[rows, :]) / (w01 * l + w2)
        o_ref[rows, :] = (mixed * z_ref[rows, :].astype(F32)).astype(BF16)


def _dil_class_attn(hw, mask, tq=128):
    s = hw.shape[1]
    head = lambda cb: pl.BlockSpec((None, s, LANES), lambda hh: (cb + hh, 0, 0))
    return pl.pallas_call(
        functools.partial(_dil_class_kernel, tq, DIL_RATES[-1]),
        grid=(4,),
        in_specs=[head(CB_WQ), head(CB_WK), head(CB_WV), pl.BlockSpec(mask.shape, lambda hh: (0, 0, 0))],
        out_specs=[head(0), head(0)],
        out_shape=[jax.ShapeDtypeStruct((4, s, LANES), F32), jax.ShapeDtypeStruct((4, s, LANES), F32)],
        compiler_params=_cparams(("arbitrary",)),
        name="dil_class_attn",
    )(hw, hw, hw, mask)


def _dil_attn(hr, hp, hw, layer, cast_w, tq=128, nsub=32):
    s = hr.shape[1]
    n_dense = len(DIL_RATES) - 1
    masks = _dil_masks(tq, DIL_RATES[:n_dense])
    o2, lse2 = _dil_class_attn(hw, masks[0], tq)
    m_specs = [pl.BlockSpec(m.shape, lambda hh, i: (0, 0, 0)) for m in masks]
    blk = tq * nsub
    qs = [pl.BlockSpec((None, blk, LANES), functools.partial(lambda g, hh, i: (CB_DQ + 4 * g + hh, i, 0), g))
          for g in range(n_dense)]
    ks = [pl.BlockSpec((None, s, LANES), functools.partial(lambda g, hh, i: (CB_DK + 4 * g + hh, 0, 0), g))
          for g in range(n_dense)]
    vs = [pl.BlockSpec((None, s, LANES), functools.partial(lambda g, hh, i: (CB_DV + 4 * g + hh, 0, 0), g))
          for g in range(n_dense)]
    row_blk = pl.BlockSpec((None, blk, LANES), lambda hh, i: (hh, i, 0))
    wc_in, wc_out, wc_shape = _cast_specs(4, s // blk, layer)
    return pl.pallas_call(
        functools.partial(_dil_kernel, tq, nsub),
        grid=(4, s // blk),
        in_specs=(qs + ks + vs + [pl.BlockSpec((None, blk, LANES), lambda hh, i: (CB_DZ + hh, i, 0))]
                  + m_specs + [row_blk, row_blk, wc_in]),
        out_specs=[pl.BlockSpec((blk, LANES), lambda hh, i: (i, hh)), wc_out],
        out_shape=[jax.ShapeDtypeStruct((s, BRANCH_W), BF16), wc_shape],
        compiler_params=_cparams(("arbitrary", "arbitrary")),
        name="dil_attn",
    )(*([hr] * (2 * n_dense)), *([hp] * (n_dense + 1)), *masks, o2, lse2, cast_w)


def _merge_kernel(ya_ref, yb_ref, yc_ref, yd_ref, ga_ref, gb_ref, gc_ref, gd_ref, wb_ref, o_ref, wbf_ref):
    @pl.when(pl.program_id(1) == 0)
    def _():
        wbf_ref[...] = wb_ref[...].astype(BF16)

    tsub = min(256, o_ref.shape[0])
    for r0 in range(0, o_ref.shape[0], tsub):
        rows = slice(r0, r0 + tsub)
        acc = None
        for n, (y_ref, g_ref) in enumerate(zip((ya_ref, yb_ref, yc_ref, yd_ref),
                                               (ga_ref, gb_ref, gc_ref, gd_ref))):
            proj = jnp.dot(y_ref[rows, :], wbf_ref[n], preferred_element_type=F32)
            term = proj * g_ref[rows, :].astype(F32)
            acc = term if acc is None else acc + term
        o_ref[rows, :] = acc.astype(BF16)


def _merge(ys, hg, w_branch, layer, tm=2048):
    s = hg.shape[0]
    tm = math.gcd(tm, s)
    n_col = D_MODEL // TN
    y_specs = [pl.BlockSpec((tm, BRANCH_W), lambda j, i: (i, 0)) for _ in range(N_BRANCH)]
    g_specs = [pl.BlockSpec((tm, TN), functools.partial(lambda n, j, i: (i, n_col * n + j), n))
               for n in range(N_BRANCH)]
    return pl.pallas_call(
        _merge_kernel,
        grid=(n_col, s // tm),
        in_specs=y_specs + g_specs + [pl.BlockSpec((None, N_BRANCH, BRANCH_W, TN),
                                                   lambda j, i: (layer, 0, 0, j))],
        out_specs=pl.BlockSpec((tm, TN), lambda j, i: (i, j)),
        out_shape=jax.ShapeDtypeStruct((s, D_MODEL), BF16),
        scratch_shapes=[pltpu.VMEM((N_BRANCH, BRANCH_W, TN), BF16)],
        compiler_params=_cparams(("arbitrary", "arbitrary")),
        name="branch_merge",
    )(*ys, hg, hg, hg, hg, w_branch)


def _out_ln_kernel(alpha, tsub, m_ref, w_ref, x_ref, g_ref, b_ref, y_ref, ybf_ref, acc_ref):
    n_sub = m_ref.shape[0] // tsub

    def matmul(u):
        acc_ref[u % 2] = jnp.dot(m_ref[u * tsub:(u + 1) * tsub, :], w_ref[...], preferred_element_type=F32)

    matmul(0)
    for u in range(n_sub):
        if u + 1 < n_sub:
            matmul(u + 1)
        rows = slice(u * tsub, (u + 1) * tsub)
        y = _ln_rows(alpha * x_ref[rows, :] + acc_ref[u % 2], g_ref[...], b_ref[...])
        y_ref[rows, :] = y
        ybf_ref[rows, :] = y.astype(BF16)


def _out_ln(merged, wo_bf, layer, x, g, b, alpha, tm=512, tsub=128):
    s, d = x.shape
    return pl.pallas_call(
        functools.partial(_out_ln_kernel, alpha, tsub),
        grid=(s // tm,),
        in_specs=[pl.BlockSpec((tm, d), lambda i: (i, 0)),
                  pl.BlockSpec((None, d, d), lambda i: (layer, 0, 0)),
                  pl.BlockSpec((tm, d), lambda i: (i, 0)),
                  pl.BlockSpec((None, 1, d), lambda i: (layer, 0, 0)),
                  pl.BlockSpec((None, 1, d), lambda i: (layer, 0, 0))],
        out_specs=[pl.BlockSpec((tm, d), lambda i: (i, 0)),
                   pl.BlockSpec((tm, d), lambda i: (i, 0))],
        out_shape=[jax.ShapeDtypeStruct((s, d), F32), jax.ShapeDtypeStruct((s, d), BF16)],
        scratch_shapes=[pltpu.VMEM((2, tsub, d), F32)],
        compiler_params=_cparams(("arbitrary",)),
        name="out_ln",
    )(merged, wo_bf, x, g, b)


def _tail_kernel(alpha, tsub, ya_ref, yb_ref, yc_ref, yd_ref, g_ref, wb_ref, wo_ref, x_ref, lg_ref, lb_ref,
                 y_ref, ybf_ref, m_ref, acc_ref):
    n_sub = x_ref.shape[0] // tsub

    def project(u):
        rows = slice(u * tsub, (u + 1) * tsub)
        for c0 in range(0, D_MODEL, TN):
            acc = None
            for n, y_in in enumerate((ya_ref, yb_ref, yc_ref, yd_ref)):
                proj = jnp.dot(y_in[rows, :], wb_ref[n, :, c0:c0 + TN], preferred_element_type=F32)
                term = proj * g_ref[rows, n * D_MODEL + c0:n * D_MODEL + c0 + TN].astype(F32)
                acc = term if acc is None else acc + term
            m_ref[rows, c0:c0 + TN] = acc.astype(BF16)
        acc_ref[u % 2] = jnp.dot(m_ref[rows, :], wo_ref[...], preferred_element_type=F32)

    project(0)
    for u in range(n_sub):
        if u + 1 < n_sub:
            project(u + 1)
        rows = slice(u * tsub, (u + 1) * tsub)
        y = _ln_rows(alpha * x_ref[rows, :] + acc_ref[u % 2], lg_ref[...], lb_ref[...])
        y_ref[rows, :] = y
        ybf_ref[rows, :] = y.astype(BF16)


def _tail(ys, hg, wb_bf, wo_bf, layer, x, g, b, alpha, tm=256, tsub=128):
    s, d = x.shape
    resident = pl.Buffered(1)
    return pl.pallas_call(
        functools.partial(_tail_kernel, alpha, tsub),
        grid=(s // tm,),
        in_specs=[pl.BlockSpec((tm, BRANCH_W), lambda i: (i, 0)) for _ in range(N_BRANCH)]
        + [pl.BlockSpec((tm, N_BRANCH * d), lambda i: (i, 0)),
           pl.BlockSpec((N_BRANCH, BRANCH_W, d), lambda i: (0, 0, 0), pipeline_mode=resident),
           pl.BlockSpec((d, d), lambda i: (0, 0), pipeline_mode=resident),
           pl.BlockSpec((tm, d), lambda i: (i, 0)),
           pl.BlockSpec((None, 1, d), lambda i: (layer, 0, 0)),
           pl.BlockSpec((None, 1, d), lambda i: (layer, 0, 0))],
        out_specs=[pl.BlockSpec((tm, d), lambda i: (i, 0)),
                   pl.BlockSpec((tm, d), lambda i: (i, 0))],
        out_shape=[jax.ShapeDtypeStruct((s, d), F32), jax.ShapeDtypeStruct((s, d), BF16)],
        scratch_shapes=[pltpu.VMEM((tm, d), BF16), pltpu.VMEM((2, tsub, d), F32)],
        compiler_params=_cparams(("arbitrary",)),
        name="merge_out_ln",
    )(*ys, hg, wb_bf, wo_bf, x, g, b)


def kernel(x, emb_ln_g, emb_ln_b, w_in, b_gate, diff_lambda, diff_subln_g, nat_rpb,
           gqa_q_norm_g, gqa_k_norm_g, w_branch, w_out, ln_g, ln_b):
    batch, seq, d = x.shape
    assert batch == 1 and d == D_MODEL and w_in.shape[-1] == D_IN
    depth = w_in.shape[0]
    alpha = (2 * depth) ** 0.25
    tabs = _rope_tables(seq)
    nat_bias = _nat_bias(nat_rpb)
    w_branch2 = w_branch.reshape(depth, N_BRANCH * BRANCH_W, d)
    gains = jnp.stack([gqa_q_norm_g, gqa_k_norm_g], axis=1).reshape(depth, 2, 1, LANES)
    b_gate3 = b_gate.reshape(depth, 1, -1)
    subln3 = diff_subln_g.reshape(depth, 1, LANES)
    ln_g3, ln_b3 = ln_g.reshape(depth, 1, d), ln_b.reshape(depth, 1, d)
    xf, xbf = _embed_ln(x[0], emb_ln_g, emb_ln_b)
    for l in range(depth):
        lam_init = 0.8 - 0.6 * math.exp(-0.3 * l)
        hr, hp, hg, hw = _inproj(xbf, w_in, b_gate3, l, tabs, gains)
        ya = _pair_attn(hr, hp, True, l, lam_init, diff_lambda, subln3)
        yb, wo_bf = _nat_attn(hp, nat_bias, l, w_out)
        yc = _pair_attn(hr, hp, False)
        yd, wb_bf = _dil_attn(hr, hp, hw, l, w_branch2)
        xf, xbf = _tail((ya, yb, yc, yd), hg, wb_bf.reshape(N_BRANCH, BRANCH_W, d), wo_bf, l, xf,
                        ln_g3, ln_b3, alpha)
    return xf[None]
```
